```python
import jax
import jax.numpy as jnp
from jax import lax
import numpy as np


D_MODEL = 1024
BATCH = 8
SEQ = 2048
DEPTH = 4

CHUNK = 128
SGU_HEADS = 8
SGU_HEAD_DIM = D_MODEL // SGU_HEADS
SGU_WIDTH = SGU_HEADS * SGU_HEAD_DIM
POOL_WINDOWS = (2, 4, 8, 16)
POOL_GROUPS = len(POOL_WINDOWS)
POOL_WIDTH = D_MODEL
POOL_GROUP_DIM = POOL_WIDTH // POOL_GROUPS
N_BRANCHES = 2
SPLIT_SGU = 2 * SGU_WIDTH
SPLIT_POOL = SPLIT_SGU + POOL_WIDTH
IN_WIDTH = SPLIT_POOL + N_BRANCHES * D_MODEL
N_EXPERTS = 16
N_EXPERT_GROUPS = 4
EXPERTS_PER_GROUP = N_EXPERTS // N_EXPERT_GROUPS
TOP_K = 2
D_EXPERT = D_MODEL // 2
N_MOD = 6
EPS = 1e-6

kernel_name = 'hybrid_sgu_pool_grouped_moe_adaln'


def rmsnorm(x, g):
    xf = x.astype(jnp.float32)
    y = xf * lax.rsqrt(jnp.mean(xf * xf, axis=-1, keepdims=True) + EPS)
    return (y * g.astype(jnp.float32)).astype(x.dtype)


def layernorm(x, g):
    xf = x.astype(jnp.float32)
    xc = xf - jnp.mean(xf, axis=-1, keepdims=True)
    y = xc * lax.rsqrt(jnp.mean(xc * xc, axis=-1, keepdims=True) + EPS)
    return (y * g.astype(jnp.float32)).astype(x.dtype)


def modulate(h, shift, scale):
    return h * (1 + scale[:, None, :]) + shift[:, None, :]


def spatial_gating(z, v_g, w_s, b_s):
    u, v = jnp.split(z, 2, axis=-1)
    v = layernorm(v, v_g)
    B, S, _ = v.shape
    n_chunks = S // CHUNK
    v = v.reshape(B, n_chunks, CHUNK, SGU_HEADS, SGU_HEAD_DIM)
    causal = jnp.tril(jnp.ones((CHUNK, CHUNK), dtype=bool))
    w = jnp.where(causal[None], w_s, jnp.zeros((), w_s.dtype))
    s = jnp.einsum('hij,bnjhc->bnihc', w, v)
    s = s + jnp.swapaxes(b_s, 0, 1)[None, None, :, :, None]
    return u * s.reshape(B, S, SGU_WIDTH)


def multiscale_pool(p, w_pool, pool_scale):
    B, S, _ = p.shape
    pf = p.astype(jnp.float32)
    cs0 = jnp.pad(jnp.cumsum(pf, axis=1), ((0, 0), (1, 0), (0, 0)))
    pos = jnp.arange(1, S + 1)
    outs = []
    for gi, w in enumerate(POOL_WINDOWS):
        lo, hi = gi * POOL_GROUP_DIM, (gi + 1) * POOL_GROUP_DIM
        c_g = cs0[:, :, lo:hi]
        lagged = jnp.pad(c_g[:, :S - w + 1], ((0, 0), (w - 1, 0), (0, 0)))
        count = jnp.minimum(pos, w).astype(jnp.float32)[None, :, None]
        outs.append((c_g[:, 1:] - lagged) / count - pf[:, :, lo:hi])
    pooled = jnp.stack(outs, axis=2).astype(p.dtype)
    mixed = jnp.einsum('bsgc,gcd->bsgd', pooled, w_pool)
    return mixed.reshape(B, S, POOL_WIDTH) * pool_scale


def grouped_moe(h, router_w, router_bias, w_gate, w_up, w_down):
    B, S, D = h.shape
    t = h.reshape(-1, D)
    logits = jnp.dot(t.astype(jnp.float32), router_w.astype(jnp.float32))
    probs = jax.nn.softmax(logits, axis=-1)
    sel = (probs + router_bias.astype(jnp.float32)).reshape(-1, N_EXPERT_GROUPS, EXPERTS_PER_GROUP)
    group_score = jnp.sum(lax.top_k(sel, TOP_K)[0], axis=-1)
    g_idx = jnp.argmax(group_score, axis=-1)
    in_group = jnp.take_along_axis(sel, g_idx[:, None, None], axis=1)[:, 0]
    _, loc = lax.top_k(in_group, TOP_K)
    e_idx = g_idx[:, None] * EXPERTS_PER_GROUP + loc
    wts = jnp.take_along_axis(probs, e_idx, axis=1)
    wts = wts / jnp.sum(wts, axis=-1, keepdims=True)
    combine = jnp.einsum('tk,tke->te', wts,
                         jax.nn.one_hot(e_idx, N_EXPERTS, dtype=jnp.float32)).astype(h.dtype)
    g = jnp.einsum('td,edf->tef', t, w_gate)
    u = jnp.einsum('td,edf->tef', t, w_up)
    a = jax.nn.silu(g) * u * combine[:, :, None]
    y = jnp.einsum('tef,efd->td', a, w_down)
    return y.reshape(B, S, D)


def setup_inputs(seed: int = 0) -> dict:
    key = jax.random.key(seed)
    ks = jax.random.split(key, 24)
    f32 = jnp.float32
    nrm = lambda k, shape, s: jax.random.normal(k, shape, f32) * s
    L, D = DEPTH, D_MODEL
    return {
        'x': nrm(ks[0], (BATCH, SEQ, D), 1.0),
        'c': nrm(ks[1], (BATCH, D), 1.0),
        'w_ada': nrm(ks[2], (L, D, N_MOD * D), 0.5 * D ** -0.5),
        'b_ada': nrm(ks[3], (L, N_MOD * D), 0.02),
        'norm_mix_g': 1.0 + nrm(ks[4], (L, D), 0.02),
        'w_in': nrm(ks[5], (L, D, IN_WIDTH), D ** -0.5),
        'v_norm_g': 1.0 + nrm(ks[6], (L, SGU_WIDTH), 0.02),
        'sgu_w': nrm(ks[7], (L, SGU_HEADS, CHUNK, CHUNK), CHUNK ** -0.5),
        'sgu_b': 1.0 + nrm(ks[8], (L, SGU_HEADS, CHUNK), 0.1),
        'pool_w': nrm(ks[9], (L, POOL_GROUPS, POOL_GROUP_DIM, POOL_GROUP_DIM), POOL_GROUP_DIM ** -0.5),
        'pool_scale': 1.0 + nrm(ks[10], (L, POOL_WIDTH), 0.1),
        'w_branch_a': nrm(ks[11], (L, SGU_WIDTH, D), SGU_WIDTH ** -0.5),
        'w_branch_b': nrm(ks[12], (L, POOL_WIDTH, D), POOL_WIDTH ** -0.5),
        'w_out': nrm(ks[13], (L, D, D), D ** -0.5),
        'norm_ffn_g': 1.0 + nrm(ks[14], (L, D), 0.02),
        'router_w': nrm(ks[15], (D, N_EXPERTS), D ** -0.5),
        'router_bias': nrm(ks[16], (N_EXPERTS,), 0.01),
        'w_exp_gate': nrm(ks[17], (L, N_EXPERTS, D, D_EXPERT), D ** -0.5),
        'w_exp_up': nrm(ks[18], (L, N_EXPERTS, D, D_EXPERT), D ** -0.5),
        'w_exp_down': nrm(ks[19], (L, N_EXPERTS, D_EXPERT, D), D_EXPERT ** -0.5),
        'final_norm_g': 1.0 + nrm(ks[20], (D,), 0.02),
    }


def reference(x, c, w_ada, b_ada, norm_mix_g, w_in, v_norm_g, sgu_w, sgu_b, pool_w, pool_scale,
              w_branch_a, w_branch_b, w_out, norm_ffn_g, router_w, router_bias,
              w_exp_gate, w_exp_up, w_exp_down, final_norm_g):
    c_act = jax.nn.silu(c)
    for l in range(DEPTH):
        mod = jnp.dot(c_act, w_ada[l]) + b_ada[l]
        sh_m, sc_m, g_m, sh_f, sc_f, g_f = jnp.split(mod, N_MOD, axis=-1)
        h = modulate(rmsnorm(x, norm_mix_g[l]), sh_m, sc_m)
        z = jnp.einsum('bsd,de->bse', h, w_in[l])
        z_sgu, z_pool, z_gate = jnp.split(z, [SPLIT_SGU, SPLIT_POOL], axis=-1)
        y_a = spatial_gating(jax.nn.gelu(z_sgu), v_norm_g[l], sgu_w[l], sgu_b[l])
        y_b = multiscale_pool(z_pool, pool_w[l], pool_scale[l])
        gates = jax.nn.sigmoid(z_gate.astype(jnp.float32)).astype(x.dtype)
        gate_a, gate_b = jnp.split(gates, N_BRANCHES, axis=-1)
        merged = (gate_a * jnp.einsum('bsc,cd->bsd', y_a, w_branch_a[l])
                  + gate_b * jnp.einsum('bsc,cd->bsd', y_b, w_branch_b[l]))
        x = x + g_m[:, None, :] * jnp.einsum('bsd,de->bse', merged, w_out[l])
        hf = modulate(rmsnorm(x, norm_ffn_g[l]), sh_f, sc_f)
        x = x + g_f[:, None, :] * grouped_moe(hf, router_w, router_bias,
                                              w_exp_gate[l], w_exp_up[l], w_exp_down[l])
    return rmsnorm(x, final_norm_g)
```

```python
import functools

import jax
import jax.numpy as jnp
from jax import lax
from jax.experimental import pallas as pl
from jax.experimental.pallas import tpu as pltpu

D_MODEL = 1024
CHUNK = 128
SGU_HEADS = 8
SGU_HEAD_DIM = 128
SGU_WIDTH = 1024
POOL_WINDOWS = (2, 4, 8, 16)
POOL_GROUP_DIM = 256
POOL_WIDTH = 1024
HALO = 16
N_EXPERTS = 16
N_EXPERT_GROUPS = 4
EXPERTS_PER_GROUP = 4
PAIRS_PER_GROUP = 6
N_BUCKETS = N_EXPERT_GROUPS * PAIRS_PER_GROUP
BUCKET_ROWS = 32
D_EXPERT = 512
N_MOD = 6
EPS = 1e-6

SEQ_TILE = 512
ROW_TILE = 512
EXPERT_TILE = 256
VMEM_LIMIT_BYTES = 58 * 1024 * 1024

_PAIR_LO = (0, 0, 0, 1, 1, 2)
_PAIR_HI = (1, 2, 3, 2, 3, 3)

_bf16 = jnp.bfloat16
_f32 = jnp.float32


def _dot(a, b):
    return jnp.dot(a, b, preferred_element_type=_f32)


def _rms_modulate(x, g, shift, scale):
    y = x * lax.rsqrt(jnp.mean(x * x, axis=-1, keepdims=True) + EPS)
    return (y * g) * (1.0 + scale) + shift


def _ada_kernel(c_ref, w_ref, b_ref, o_ref):
    c = c_ref[...]
    c_act = (c * jax.nn.sigmoid(c)).astype(_bf16)
    o_ref[...] = _dot(c_act, w_ref[...].astype(_bf16)) + b_ref[...]


def _ada(c, w_ada, b_ada):
    depth, d, n = w_ada.shape
    batch = c.shape[0]
    tn = 2048
    return pl.pallas_call(
        _ada_kernel,
        grid=(depth, n // tn),
        in_specs=[
            pl.BlockSpec((batch, d), lambda l, j: (0, 0)),
            pl.BlockSpec((None, d, tn), lambda l, j: (l, 0, j)),
            pl.BlockSpec((None, 1, tn), lambda l, j: (l, 0, j)),
        ],
        out_specs=pl.BlockSpec((None, batch, tn), lambda l, j: (l, 0, j)),
        out_shape=jax.ShapeDtypeStruct((depth, batch, n), _f32),
        compiler_params=pltpu.CompilerParams(
            dimension_semantics=("arbitrary", "arbitrary"), vmem_limit_bytes=VMEM_LIMIT_BYTES),
        name="ada_modulation",
    )(c, w_ada, b_ada.reshape(depth, 1, n))


def _route(hf, rw_ref, rb_ref, carry_ref):
    ts = hf.shape[0]
    logits = jnp.dot(hf, rw_ref[...], preferred_element_type=_f32,
                     precision=lax.Precision.HIGHEST)
    lt = logits.T
    rows = [lt[e:e + 1, :] for e in range(N_EXPERTS)]
    m = functools.reduce(jnp.maximum, rows)
    ex = [jnp.exp(r - m) for r in rows]
    den = functools.reduce(lambda a, b: a + b, ex)
    probs = [e / den for e in ex]
    sel = [probs[e] + rb_ref[e] for e in range(N_EXPERTS)]

    def top2_sum(v):
        pairs = [v[i] + v[j] for i, j in zip(_PAIR_LO, _PAIR_HI)]
        return functools.reduce(jnp.maximum, pairs)

    gscore = [top2_sum(sel[4 * g:4 * g + 4]) for g in range(N_EXPERT_GROUPS)]
    best = gscore[0]
    gidx = jnp.zeros_like(best, dtype=jnp.int32)
    for g in range(1, N_EXPERT_GROUPS):
        better = gscore[g] > best
        best = jnp.where(better, gscore[g], best)
        gidx = jnp.where(better, g, gidx)
    ing = []
    for k in range(EXPERTS_PER_GROUP):
        v = sel[k]
        for g in range(1, N_EXPERT_GROUPS):
            v = jnp.where(gidx == g, sel[4 * g + k], v)
        ing.append(v)
    chosen = []
    for k in range(EXPERTS_PER_GROUP):
        r = jnp.zeros_like(gidx)
        for j in range(EXPERTS_PER_GROUP):
            if j == k:
                continue
            beats = (ing[j] >= ing[k]) if j < k else (ing[j] > ing[k])
            r = r + beats.astype(jnp.int32)
        chosen.append(r < 2)
    lo = jnp.where(chosen[0], 0, jnp.where(chosen[1], 1, 2))
    hi = jnp.where(chosen[3], 3, jnp.where(chosen[2], 2, 1))
    base = jnp.where(lo == 0, 0, jnp.where(lo == 1, 3, 5))
    bucket = gidx * PAIRS_PER_GROUP + base + hi - lo - 1

    brow = lax.broadcasted_iota(jnp.int32, (BUCKET_ROWS, ts), 0)
    onehot = (brow == bucket).astype(_f32)
    jj = lax.broadcasted_iota(jnp.int32, (ts, ts), 0)
    tt = lax.broadcasted_iota(jnp.int32, (ts, ts), 1)
    upper = (jj <= tt).astype(_bf16)
    cum = _dot(onehot.astype(_bf16), upper)
    carry = carry_ref[...][:, 0:1]
    rank = jnp.sum(onehot * (cum - 1.0 + carry), axis=0, keepdims=True)
    carry_ref[...] = carry_ref[...] + jnp.sum(onehot, axis=1, keepdims=True)
    return bucket.astype(_f32), rank


def _mixer_kernel(x_ref, mod_ref, gmix_ref, gffn_ref, win_ref, vg_ref, sw_ref, sb_ref, pw_ref,
                  ps_ref, wa_ref, wb_ref, wo_ref, rw_ref, rb_ref,
                  xo_ref, hf_ref, route_ref, counts_ref,
                  ext_ref, carry_ref):
    b = pl.program_id(0)
    s = pl.program_id(1)
    ts = x_ref.shape[0]
    d = D_MODEL

    @pl.when((b == 0) & (s == 0))
    def _():
        carry_ref[...] = jnp.zeros_like(carry_ref)

    @pl.when(s == 0)
    def _():
        ext_ref[0:HALO, :] = jnp.zeros((HALO, POOL_WIDTH), _f32)

    x = x_ref[...]
    mod = mod_ref[...]
    sh_m, sc_m, g_m = mod[:, 0:d], mod[:, d:2 * d], mod[:, 2 * d:3 * d]
    sh_f, sc_f, g_f = mod[:, 3 * d:4 * d], mod[:, 4 * d:5 * d], mod[:, 5 * d:6 * d]
    del g_f

    hb = _rms_modulate(x, gmix_ref[...], sh_m, sc_m).astype(_bf16)

    v = jax.nn.gelu(_dot(hb, win_ref[:, SGU_WIDTH:2 * SGU_WIDTH]))
    vc = v - jnp.mean(v, axis=-1, keepdims=True)
    vn = (vc * lax.rsqrt(jnp.mean(vc * vc, axis=-1, keepdims=True) + EPS) * vg_ref[...]).astype(_bf16)
    u = jax.nn.gelu(_dot(hb, win_ref[:, 0:SGU_WIDTH]))
    n_chunks = ts // CHUNK
    ci = lax.broadcasted_iota(jnp.int32, (CHUNK, CHUNK), 0)
    cj = lax.broadcasted_iota(jnp.int32, (CHUNK, CHUNK), 1)
    causal = ci >= cj
    ya_cols = []
    for h in range(SGU_HEADS):
        cols = slice(h * SGU_HEAD_DIM, (h + 1) * SGU_HEAD_DIM)
        w_h = jnp.where(causal, sw_ref[h], jnp.zeros((), _bf16))
        rhs = jnp.concatenate([vn[n * CHUNK:(n + 1) * CHUNK, cols] for n in range(n_chunks)], axis=1)
        sg = _dot(w_h, rhs) + sb_ref[:, h:h + 1]
        s_h = jnp.concatenate([sg[:, n * SGU_HEAD_DIM:(n + 1) * SGU_HEAD_DIM] for n in range(n_chunks)],
                              axis=0)
        ya_cols.append((u[:, cols] * s_h).astype(_bf16))
    ya = jnp.concatenate(ya_cols, axis=1)
    gate_a = jax.nn.sigmoid(_dot(hb, win_ref[:, 3 * d:4 * d]))
    merged = gate_a * _dot(ya, wa_ref[...])

    p = _dot(hb, win_ref[:, 2 * SGU_WIDTH:2 * SGU_WIDTH + POOL_WIDTH])
    ext_ref[HALO:HALO + ts, :] = p
    pos1 = (s * ts + 1 + lax.broadcasted_iota(jnp.int32, (ts, 1), 0)).astype(_f32)
    yb_cols = []
    for gi, w in enumerate(POOL_WINDOWS):
        cols = slice(gi * POOL_GROUP_DIM, (gi + 1) * POOL_GROUP_DIM)
        acc = p[:, cols]
        for k in range(1, w):
            acc = acc + ext_ref[HALO - k:HALO - k + ts, cols]
        count = jnp.minimum(pos1, float(w))
        pooled = (acc / count - p[:, cols]).astype(_bf16)
        yb_cols.append(_dot(pooled, pw_ref[gi]))
    ext_ref[0:HALO, :] = p[ts - HALO:ts, :]
    yb = (jnp.concatenate(yb_cols, axis=1) * ps_ref[...]).astype(_bf16)
    gate_b = jax.nn.sigmoid(_dot(hb, win_ref[:, 4 * d:5 * d]))
    merged = merged + gate_b * _dot(yb, wb_ref[...])

    x_new = x + g_m * _dot(merged.astype(_bf16), wo_ref[...])
    xo_ref[...] = x_new

    hf = _rms_modulate(x_new, gffn_ref[...], sh_f, sc_f)
    hf_ref[...] = hf
    bucket, rank = _route(hf, rw_ref, rb_ref, carry_ref)
    route_ref[0:1, :] = bucket
    route_ref[1:2, :] = rank
    route_ref[2:8, :] = jnp.zeros((6, ts), _f32)
    counts_ref[...] = carry_ref[...]


def _mixer(x, mod_l, gmix, gffn, w_in, v_g, sgu_w, sgu_bt, pool_w, pool_scale, w_a, w_b, w_o,
           router_w_pad, router_bias):
    batch, seq, d = x.shape
    ts = SEQ_TILE
    n_tiles = batch * (seq // ts)
    tiles_per_seq = seq // ts
    const = lambda *shape: pl.BlockSpec(shape, lambda b, s: (0,) * len(shape),
                                        pipeline_mode=pl.Buffered(1))
    return pl.pallas_call(
        _mixer_kernel,
        grid=(batch, tiles_per_seq),
        in_specs=[
            pl.BlockSpec((None, ts, d), lambda b, s: (b, s, 0)),
            pl.BlockSpec((None, 1, N_MOD * d), lambda b, s: (b, 0, 0)),
            const(1, d), const(1, d),
            const(*w_in.shape),
            const(1, SGU_WIDTH),
            const(*sgu_w.shape),
            const(*sgu_bt.shape),
            const(*pool_w.shape),
            const(1, POOL_WIDTH),
            const(d, d), const(d, d), const(d, d),
            const(*router_w_pad.shape),
            pl.BlockSpec(memory_space=pltpu.SMEM),
        ],
        out_specs=[
            pl.BlockSpec((None, ts, d), lambda b, s: (b, s, 0)),
            pl.BlockSpec((None, ts, d), lambda b, s: (b, s, 0)),
            pl.BlockSpec((None, 8, ts), lambda b, s: (b * tiles_per_seq + s, 0, 0)),
            pl.BlockSpec((BUCKET_ROWS, 128), lambda b, s: (0, 0)),
        ],
        out_shape=[
            jax.ShapeDtypeStruct((batch, seq, d), _f32),
            jax.ShapeDtypeStruct((batch, seq, d), _f32),
            jax.ShapeDtypeStruct((n_tiles, 8, ts), _f32),
            jax.ShapeDtypeStruct((BUCKET_ROWS, 128), _f32),
        ],
        scratch_shapes=[
            pltpu.VMEM((HALO + ts, POOL_WIDTH), _f32),
            pltpu.VMEM((BUCKET_ROWS, 128), _f32),
        ],
        compiler_params=pltpu.CompilerParams(
            dimension_semantics=("arbitrary", "arbitrary"), vmem_limit_bytes=VMEM_LIMIT_BYTES),
        name="mixer_router",
    )(x, mod_l, gmix, gffn, w_in, v_g, sgu_w, sgu_bt, pool_w, pool_scale, w_a, w_b, w_o,
      router_w_pad, router_bias)


def _row_copy(src_ref, dst_ref, sem, src_row, dst_row):
    return pltpu.make_async_copy(src_ref.at[pl.ds(src_row, 1)], dst_ref.at[pl.ds(dst_row, 1)], sem)


def _scatter_kernel(pos_ref, hf_ref, init_ref, xs_ref, sem):
    del init_ref
    i = pl.program_id(0)
    rows = hf_ref.shape[0]

    def issue(r, c):
        _row_copy(hf_ref, xs_ref, sem, r, pos_ref[i * rows + r]).start()
        return c

    lax.fori_loop(0, rows, issue, 0, unroll=8)

    def drain(r, c):
        _row_copy(hf_ref, xs_ref, sem, r, 0).wait()
        return c

    lax.fori_loop(0, rows, drain, 0, unroll=8)


def _scatter_rows(pos, hf, n_slots):
    t, d = hf.shape
    init = jnp.zeros((n_slots, d), _f32)
    grid_spec = pltpu.PrefetchScalarGridSpec(
        num_scalar_prefetch=1,
        grid=(t // ROW_TILE,),
        in_specs=[
            pl.BlockSpec((ROW_TILE, d), lambda i, pos: (i, 0)),
            pl.BlockSpec(memory_space=pl.ANY),
        ],
        out_specs=pl.BlockSpec(memory_space=pl.ANY),
        scratch_shapes=[pltpu.SemaphoreType.DMA(())],
    )
    return pl.pallas_call(
        _scatter_kernel,
        grid_spec=grid_spec,
        out_shape=jax.ShapeDtypeStruct((n_slots, d), _f32),
        input_output_aliases={2: 0},
        compiler_params=pltpu.CompilerParams(dimension_semantics=("arbitrary",)),
        name="scatter_rows",
    )(pos, hf, init)


def _combine_kernel(pos_ref, x_ref, gf_ref, ys_ref, fg_ref, o_ref, buf_ref, sem, *, final_norm):
    i = pl.program_id(0)
    rows = x_ref.shape[0]

    def issue(r, c):
        _row_copy(ys_ref, buf_ref, sem, pos_ref[i * rows + r], r).start()
        return c

    lax.fori_loop(0, rows, issue, 0, unroll=8)

    def drain(r, c):
        _row_copy(ys_ref, buf_ref, sem, 0, r).wait()
        return c

    lax.fori_loop(0, rows, drain, 0, unroll=8)
    out = x_ref[...] + gf_ref[...] * buf_ref[...]
    if final_norm:
        out = out * lax.rsqrt(jnp.mean(out * out, axis=-1, keepdims=True) + EPS) * fg_ref[...]
    o_ref[...] = out


def _combine_rows(pos, x2d, gate_f, ys, final_g, seq, final_norm):
    t, d = x2d.shape
    tiles_per_seq = seq // ROW_TILE
    grid_spec = pltpu.PrefetchScalarGridSpec(
        num_scalar_prefetch=1,
        grid=(t // ROW_TILE,),
        in_specs=[
            pl.BlockSpec((ROW_TILE, d), lambda i, pos: (i, 0)),
            pl.BlockSpec((None, 1, d), lambda i, pos: (i // tiles_per_seq, 0, 0)),
            pl.BlockSpec(memory_space=pl.ANY),
            pl.BlockSpec((1, d), lambda i, pos: (0, 0)),
        ],
        out_specs=pl.BlockSpec((ROW_TILE, d), lambda i, pos: (i, 0)),
        scratch_shapes=[pltpu.VMEM((ROW_TILE, d), _f32), pltpu.SemaphoreType.DMA(())],
    )
    return pl.pallas_call(
        functools.partial(_combine_kernel, final_norm=final_norm),
        grid_spec=grid_spec,
        out_shape=jax.ShapeDtypeStruct((t, d), _f32),
        compiler_params=pltpu.CompilerParams(dimension_semantics=("arbitrary",)),
        name="combine_rows",
    )(pos, x2d, gate_f, ys, final_g)


def _experts_kernel(ea_ref, eb_ref, nused_ref, xs_ref, rwa_ref, rwb_ref,
                    wga_ref, wgb_ref, wua_ref, wub_ref, wda_ref, wdb_ref, ys_ref):
    i = pl.program_id(0)

    @pl.when(i < nused_ref[0])
    def _():
        x = xs_ref[...]
        la = jnp.sum(x * rwa_ref[...], axis=-1, keepdims=True)
        lb = jnp.sum(x * rwb_ref[...], axis=-1, keepdims=True)
        wa = jax.nn.sigmoid(la - lb)
        wb = jax.nn.sigmoid(lb - la)
        xb = x.astype(_bf16)
        act_a = (jax.nn.silu(_dot(xb, wga_ref[...])) * _dot(xb, wua_ref[...]) * wa).astype(_bf16)
        act_b = (jax.nn.silu(_dot(xb, wgb_ref[...])) * _dot(xb, wub_ref[...]) * wb).astype(_bf16)
        ys_ref[...] = _dot(act_a, wda_ref[...]) + _dot(act_b, wdb_ref[...])

    @pl.when(i >= nused_ref[0])
    def _():
        ys_ref[...] = jnp.zeros_like(ys_ref)


def _experts(tile_ea, tile_eb, n_used, xs, router_wt, w_gate, w_up, w_down):
    n_slots, d = xs.shape
    n_tiles = n_slots // EXPERT_TILE
    f = D_EXPERT

    def row(i, ea, eb, nu):
        return (jnp.minimum(i, nu[0] - 1), 0)

    grid_spec = pltpu.PrefetchScalarGridSpec(
        num_scalar_prefetch=3,
        grid=(n_tiles,),
        in_specs=[
            pl.BlockSpec((EXPERT_TILE, d), row),
            pl.BlockSpec((None, 1, d), lambda i, ea, eb, nu: (ea[i], 0, 0)),
            pl.BlockSpec((None, 1, d), lambda i, ea, eb, nu: (eb[i], 0, 0)),
            pl.BlockSpec((None, d, f), lambda i, ea, eb, nu: (ea[i], 0, 0)),
            pl.BlockSpec((None, d, f), lambda i, ea, eb, nu: (eb[i], 0, 0)),
            pl.BlockSpec((None, d, f), lambda i, ea, eb, nu: (ea[i], 0, 0)),
            pl.BlockSpec((None, d, f), lambda i, ea, eb, nu: (eb[i], 0, 0)),
            pl.BlockSpec((None, f, d), lambda i, ea, eb, nu: (ea[i], 0, 0)),
            pl.BlockSpec((None, f, d), lambda i, ea, eb, nu: (eb[i], 0, 0)),
        ],
        out_specs=pl.BlockSpec((EXPERT_TILE, d), lambda i, ea, eb, nu: (i, 0)),
    )
    return pl.pallas_call(
        _experts_kernel,
        grid_spec=grid_spec,
        out_shape=jax.ShapeDtypeStruct((n_slots, d), _f32),
        compiler_params=pltpu.CompilerParams(
            dimension_semantics=("arbitrary",), vmem_limit_bytes=VMEM_LIMIT_BYTES),
        name="grouped_experts",
    )(tile_ea, tile_eb, n_used, xs, router_wt, router_wt, w_gate, w_gate, w_up, w_up, w_down, w_down)


def _routing_tables(route, counts, n_tiles_max):
    bucket = route[:, 0, :].reshape(-1).astype(jnp.int32)
    rank = route[:, 1, :].reshape(-1).astype(jnp.int32)
    cnt = counts[:N_BUCKETS, 0].astype(jnp.int32)
    tiles_b = (cnt + EXPERT_TILE - 1) // EXPERT_TILE
    tile_end = jnp.cumsum(tiles_b)
    tile_start = tile_end - tiles_b
    n_used = tile_end[-1]
    onehot = bucket[:, None] == jnp.arange(N_BUCKETS, dtype=jnp.int32)[None, :]
    pos = jnp.sum(jnp.where(onehot, (tile_start * EXPERT_TILE)[None, :], 0), axis=1) + rank
    tile_ids = jnp.minimum(jnp.arange(n_tiles_max, dtype=jnp.int32), n_used - 1)
    tile_bucket = jnp.sum((tile_ids[:, None] >= tile_end[None, :]).astype(jnp.int32), axis=1)
    group = tile_bucket // PAIRS_PER_GROUP
    pair = tile_bucket % PAIRS_PER_GROUP
    lo = jnp.asarray(_PAIR_LO, jnp.int32)[pair]
    hi = jnp.asarray(_PAIR_HI, jnp.int32)[pair]
    tile_ea = group * EXPERTS_PER_GROUP + lo
    tile_eb = group * EXPERTS_PER_GROUP + hi
    return pos, tile_ea, tile_eb, n_used.reshape(1)


def kernel(x, c, w_ada, b_ada, norm_mix_g, w_in, v_norm_g, sgu_w, sgu_b, pool_w, pool_scale,
           w_branch_a, w_branch_b, w_out, norm_ffn_g, router_w, router_bias,
           w_exp_gate, w_exp_up, w_exp_down, final_norm_g):
    batch, seq, d = x.shape
    depth = w_ada.shape[0]
    t = batch * seq
    n_tiles_max = t // EXPERT_TILE + N_BUCKETS
    n_slots = n_tiles_max * EXPERT_TILE

    mod = _ada(c, w_ada, b_ada)
    router_w_pad = jnp.pad(router_w, ((0, 0), (0, 128 - N_EXPERTS)))
    router_wt = router_w.T.reshape(N_EXPERTS, 1, d)
    final_g = final_norm_g.reshape(1, d)

    for l in range(depth):
        mod_l = mod[l].reshape(batch, 1, N_MOD * d)
        x, hf, route, counts = _mixer(
            x, mod_l, norm_mix_g[l].reshape(1, d), norm_ffn_g[l].reshape(1, d),
            w_in[l].astype(_bf16), v_norm_g[l].reshape(1, SGU_WIDTH),
            sgu_w[l].astype(_bf16), sgu_b[l].T,
            pool_w[l].astype(_bf16), pool_scale[l].reshape(1, POOL_WIDTH),
            w_branch_a[l].astype(_bf16), w_branch_b[l].astype(_bf16), w_out[l].astype(_bf16),
            router_w_pad, router_bias)
        pos, tile_ea, tile_eb, n_used = _routing_tables(route, counts, n_tiles_max)
        xs = _scatter_rows(pos, hf.reshape(t, d), n_slots)
        ys = _experts(tile_ea, tile_eb, n_used, xs, router_wt,
                      w_exp_gate[l].astype(_bf16), w_exp_up[l].astype(_bf16),
                      w_exp_down[l].astype(_bf16))
        gate_f = mod_l[:, :, 5 * d:6 * d]
        x = _combine_rows(pos, x.reshape(t, d), gate_f, ys, final_g, seq,
                          final_norm=(l == depth - 1)).reshape(batch, seq, d)
    return x
```

```python
import functools

import jax
import jax.numpy as jnp
from jax import lax
from jax.experimental import pallas as pl
from jax.experimental.pallas import tpu as pltpu

D_MODEL = 1024
CHUNK = 128
SGU_HEADS = 8
SGU_HEAD_DIM = 128
SGU_WIDTH = 1024
POOL_WINDOWS = (2, 4, 8, 16)
POOL_GROUP_DIM = 256
POOL_WIDTH = 1024
HALO = 16
N_EXPERTS = 16
N_EXPERT_GROUPS = 4
EXPERTS_PER_GROUP = 4
PAIRS_PER_GROUP = 6
N_BUCKETS = N_EXPERT_GROUPS * PAIRS_PER_GROUP
BUCKET_ROWS = 32
D_EXPERT = 512
N_MOD = 6
EPS = 1e-6

SEQ_TILE = 512
SUB_TILE = 256
ROW_TILE = 512
EXPERT_TILE = 256
VMEM_LIMIT_BYTES = 58 * 1024 * 1024

_PAIR_LO = (0, 0, 0, 1, 1, 2)
_PAIR_HI = (1, 2, 3, 2, 3, 3)

_bf16 = jnp.bfloat16
_f32 = jnp.float32


def _dot(a, b):
    return jnp.dot(a, b, preferred_element_type=_f32)


def _rms_modulate(x, g, shift, scale):
    y = x * lax.rsqrt(jnp.mean(x * x, axis=-1, keepdims=True) + EPS)
    return (y * g) * (1.0 + scale) + shift


def _ada_kernel(c_ref, w_ref, b_ref, o_ref):
    c = c_ref[...]
    c_act = (c * jax.nn.sigmoid(c)).astype(_bf16)
    o_ref[...] = _dot(c_act, w_ref[...].astype(_bf16)) + b_ref[...]


def _ada(c, w_ada, b_ada):
    depth, d, n = w_ada.shape
    batch = c.shape[0]
    tn = 2048
    return pl.pallas_call(
        _ada_kernel,
        grid=(depth, n // tn),
        in_specs=[
            pl.BlockSpec((batch, d), lambda l, j: (0, 0)),
            pl.BlockSpec((None, d, tn), lambda l, j: (l, 0, j)),
            pl.BlockSpec((None, 1, tn), lambda l, j: (l, 0, j)),
        ],
        out_specs=pl.BlockSpec((None, batch, tn), lambda l, j: (l, 0, j)),
        out_shape=jax.ShapeDtypeStruct((depth, batch, n), _f32),
        compiler_params=pltpu.CompilerParams(
            dimension_semantics=("arbitrary", "arbitrary"), vmem_limit_bytes=VMEM_LIMIT_BYTES),
        name="ada_modulation",
    )(c, w_ada, b_ada.reshape(depth, 1, n))


def _route(hf, rw_ref, rb_ref, carry_ref):
    ts = hf.shape[0]
    hf_hi = hf.astype(_bf16)
    hf_lo = (hf - hf_hi.astype(_f32)).astype(_bf16)
    both = _dot(hf_hi, rw_ref[...])
    logits = both[:, 0:128] + both[:, 128:256] + _dot(hf_lo, rw_ref[:, 0:128])
    lt = logits.T
    rows = [lt[e:e + 1, :] for e in range(N_EXPERTS)]
    m = functools.reduce(jnp.maximum, rows)
    ex = [jnp.exp(r - m) for r in rows]
    den = functools.reduce(lambda a, b: a + b, ex)
    probs = [e / den for e in ex]
    sel = [probs[e] + rb_ref[e] for e in range(N_EXPERTS)]

    def top2_sum(v):
        pairs = [v[i] + v[j] for i, j in zip(_PAIR_LO, _PAIR_HI)]
        return functools.reduce(jnp.maximum, pairs)

    gscore = [top2_sum(sel[4 * g:4 * g + 4]) for g in range(N_EXPERT_GROUPS)]
    best = gscore[0]
    gidx = jnp.zeros_like(best, dtype=jnp.int32)
    for g in range(1, N_EXPERT_GROUPS):
        better = gscore[g] > best
        best = jnp.where(better, gscore[g], best)
        gidx = jnp.where(better, g, gidx)
    ing = []
    for k in range(EXPERTS_PER_GROUP):
        v = sel[k]
        for g in range(1, N_EXPERT_GROUPS):
            v = jnp.where(gidx == g, sel[4 * g + k], v)
        ing.append(v)
    chosen = []
    for k in range(EXPERTS_PER_GROUP):
        r = jnp.zeros_like(gidx)
        for j in range(EXPERTS_PER_GROUP):
            if j == k:
                continue
            beats = (ing[j] >= ing[k]) if j < k else (ing[j] > ing[k])
            r = r + beats.astype(jnp.int32)
        chosen.append(r < 2)
    lo = jnp.where(chosen[0], 0, jnp.where(chosen[1], 1, 2))
    hi = jnp.where(chosen[3], 3, jnp.where(chosen[2], 2, 1))
    base = jnp.where(lo == 0, 0, jnp.where(lo == 1, 3, 5))
    bucket = gidx * PAIRS_PER_GROUP + base + hi - lo - 1

    brow = lax.broadcasted_iota(jnp.int32, (BUCKET_ROWS, ts), 0)
    onehot = (brow == bucket).astype(_f32)
    jj = lax.broadcasted_iota(jnp.int32, (ts, ts), 0)
    tt = lax.broadcasted_iota(jnp.int32, (ts, ts), 1)
    upper = (jj <= tt).astype(_bf16)
    cum = _dot(onehot.astype(_bf16), upper)
    carry = carry_ref[...][:, 0:1]
    rank = jnp.sum(onehot * (cum - 1.0 + carry), axis=0, keepdims=True)
    carry_ref[...] = carry_ref[...] + jnp.sum(onehot, axis=1, keepdims=True)
    return bucket.astype(_f32), rank


def _mixer_kernel(x_ref, mod_ref, gmix_ref, gffn_ref, win_ref, vg_ref, sw_ref, sb_ref, pw_ref,
                  ps_ref, wa_ref, wb_ref, wo_ref, rw_ref, rb_ref,
                  xo_ref, hf_ref, route_ref, counts_ref,
                  ext_ref, carry_ref):
    b = pl.program_id(0)
    s = pl.program_id(1)
    ts = x_ref.shape[0]
    d = D_MODEL

    @pl.when((b == 0) & (s == 0))
    def _():
        carry_ref[...] = jnp.zeros_like(carry_ref)

    @pl.when(s == 0)
    def _():
        ext_ref[0:HALO, :] = jnp.zeros((HALO, POOL_WIDTH), _f32)

    mod = mod_ref[...]
    ci = lax.broadcasted_iota(jnp.int32, (CHUNK, CHUNK), 0)
    cj = lax.broadcasted_iota(jnp.int32, (CHUNK, CHUNK), 1)
    sgu_w = [jnp.where(ci >= cj, sw_ref[h], jnp.zeros((), _bf16)) for h in range(SGU_HEADS)]
    for r in range(ts // SUB_TILE):
        _mixer_rows(r * SUB_TILE, s * ts + r * SUB_TILE, mod, sgu_w,
                    x_ref, gmix_ref, gffn_ref, win_ref, vg_ref, sb_ref, pw_ref, ps_ref, wa_ref,
                    wb_ref, wo_ref, rw_ref, rb_ref, xo_ref, hf_ref, route_ref, ext_ref, carry_ref)
    ext_ref[0:HALO, :] = ext_ref[ts:ts + HALO, :]
    route_ref[2:8, :] = jnp.zeros((6, ts), _f32)
    counts_ref[...] = carry_ref[...]


def _mixer_rows(row0, seq_pos0, mod, sgu_w, x_ref, gmix_ref, gffn_ref, win_ref, vg_ref, sb_ref,
                pw_ref, ps_ref, wa_ref, wb_ref, wo_ref, rw_ref, rb_ref, xo_ref, hf_ref, route_ref,
                ext_ref, carry_ref):
    d = D_MODEL
    ts = SUB_TILE
    rows = slice(row0, row0 + ts)
    sh_m, sc_m, g_m = mod[:, 0:d], mod[:, d:2 * d], mod[:, 2 * d:3 * d]
    sh_f, sc_f = mod[:, 3 * d:4 * d], mod[:, 4 * d:5 * d]
    x = x_ref[rows, :]

    hb = _rms_modulate(x, gmix_ref[...], sh_m, sc_m).astype(_bf16)

    v = jax.nn.gelu(_dot(hb, win_ref[:, SGU_WIDTH:2 * SGU_WIDTH]))
    vc = v - jnp.mean(v, axis=-1, keepdims=True)
    vn = (vc * lax.rsqrt(jnp.mean(vc * vc, axis=-1, keepdims=True) + EPS) * vg_ref[...]).astype(_bf16)
    u = jax.nn.gelu(_dot(hb, win_ref[:, 0:SGU_WIDTH]))
    n_chunks = ts // CHUNK
    ya_cols = []
    for h in range(SGU_HEADS):
        cols = slice(h * SGU_HEAD_DIM, (h + 1) * SGU_HEAD_DIM)
        rhs = jnp.concatenate([vn[n * CHUNK:(n + 1) * CHUNK, cols] for n in range(n_chunks)], axis=1)
        sg = _dot(sgu_w[h], rhs) + sb_ref[:, h:h + 1]
        s_h = jnp.concatenate([sg[:, n * SGU_HEAD_DIM:(n + 1) * SGU_HEAD_DIM] for n in range(n_chunks)],
                              axis=0)
        ya_cols.append((u[:, cols] * s_h).astype(_bf16))
    ya = jnp.concatenate(ya_cols, axis=1)
    gate_a = jax.nn.sigmoid(_dot(hb, win_ref[:, 3 * d:4 * d]))
    merged = gate_a * _dot(ya, wa_ref[...])

    p = _dot(hb, win_ref[:, 2 * SGU_WIDTH:2 * SGU_WIDTH + POOL_WIDTH])
    e0 = HALO + row0
    ext_ref[e0:e0 + ts, :] = p
    pos1 = (seq_pos0 + 1 + lax.broadcasted_iota(jnp.int32, (ts, 1), 0)).astype(_f32)
    yb_cols = []
    for gi, w in enumerate(POOL_WINDOWS):
        cols = slice(gi * POOL_GROUP_DIM, (gi + 1) * POOL_GROUP_DIM)
        acc = p[:, cols]
        for k in range(1, w):
            acc = acc + ext_ref[e0 - k:e0 - k + ts, cols]
        count = jnp.minimum(pos1, float(w))
        pooled = (acc / count - p[:, cols]).astype(_bf16)
        yb_cols.append(_dot(pooled, pw_ref[gi]))
    yb = (jnp.concatenate(yb_cols, axis=1) * ps_ref[...]).astype(_bf16)
    gate_b = jax.nn.sigmoid(_dot(hb, win_ref[:, 4 * d:5 * d]))
    merged = merged + gate_b * _dot(yb, wb_ref[...])

    x_new = x + g_m * _dot(merged.astype(_bf16), wo_ref[...])
    xo_ref[rows, :] = x_new

    hf = _rms_modulate(x_new, gffn_ref[...], sh_f, sc_f)
    hf_ref[rows, :] = hf
    bucket, rank = _route(hf, rw_ref, rb_ref, carry_ref)
    route_ref[0:1, rows] = bucket
    route_ref[1:2, rows] = rank


def _mixer(x, mod_l, gmix, gffn, w_in, v_g, sgu_w, sgu_bt, pool_w, pool_scale, w_a, w_b, w_o,
           router_w_pad, router_bias):
    batch, seq, d = x.shape
    ts = SEQ_TILE
    n_tiles = batch * (seq // ts)
    tiles_per_seq = seq // ts
    const = lambda *shape: pl.BlockSpec(shape, lambda b, s: (0,) * len(shape),
                                        pipeline_mode=pl.Buffered(1))
    return pl.pallas_call(
        _mixer_kernel,
        grid=(batch, tiles_per_seq),
        in_specs=[
            pl.BlockSpec((None, ts, d), lambda b, s: (b, s, 0)),
            pl.BlockSpec((None, 1, N_MOD * d), lambda b, s: (b, 0, 0)),
            const(1, d), const(1, d),
            const(*w_in.shape),
            const(1, SGU_WIDTH),
            const(*sgu_w.shape),
            const(*sgu_bt.shape),
            const(*pool_w.shape),
            const(1, POOL_WIDTH),
            const(d, d), const(d, d), const(d, d),
            const(*router_w_pad.shape),
            pl.BlockSpec(memory_space=pltpu.SMEM),
        ],
        out_specs=[
            pl.BlockSpec((None, ts, d), lambda b, s: (b, s, 0)),
            pl.BlockSpec((None, ts, d), lambda b, s: (b, s, 0)),
            pl.BlockSpec((None, 8, ts), lambda b, s: (b * tiles_per_seq + s, 0, 0)),
            pl.BlockSpec((BUCKET_ROWS, 128), lambda b, s: (0, 0)),
        ],
        out_shape=[
            jax.ShapeDtypeStruct((batch, seq, d), _f32),
            jax.ShapeDtypeStruct((batch, seq, d), _f32),
            jax.ShapeDtypeStruct((n_tiles, 8, ts), _f32),
            jax.ShapeDtypeStruct((BUCKET_ROWS, 128), _f32),
        ],
        scratch_shapes=[
            pltpu.VMEM((HALO + ts, POOL_WIDTH), _f32),
            pltpu.VMEM((BUCKET_ROWS, 128), _f32),
        ],
        compiler_params=pltpu.CompilerParams(
            dimension_semantics=("arbitrary", "arbitrary"), vmem_limit_bytes=VMEM_LIMIT_BYTES),
        name="mixer_router",
    )(x, mod_l, gmix, gffn, w_in, v_g, sgu_w, sgu_bt, pool_w, pool_scale, w_a, w_b, w_o,
      router_w_pad, router_bias)


def _row_copy(src_ref, dst_ref, sem, src_row, dst_row):
    return pltpu.make_async_copy(src_ref.at[pl.ds(src_row, 1)], dst_ref.at[pl.ds(dst_row, 1)], sem)


def _scatter_kernel(pos_ref, hf_ref, init_ref, xs_ref, sem):
    del init_ref
    i = pl.program_id(0)
    rows = hf_ref.shape[0]

    def issue(r, c):
        _row_copy(hf_ref, xs_ref, sem, r, pos_ref[i * rows + r]).start()
        return c

    lax.fori_loop(0, rows, issue, 0, unroll=8)

    def drain(r, c):
        _row_copy(hf_ref, xs_ref, sem, r, 0).wait()
        return c

    lax.fori_loop(0, rows, drain, 0, unroll=8)


def _scatter_rows(pos, hf, n_slots):
    t, d = hf.shape
    init = jnp.zeros((n_slots, d), _f32)
    grid_spec = pltpu.PrefetchScalarGridSpec(
        num_scalar_prefetch=1,
        grid=(t // ROW_TILE,),
        in_specs=[
            pl.BlockSpec((ROW_TILE, d), lambda i, pos: (i, 0)),
            pl.BlockSpec(memory_space=pl.ANY),
        ],
        out_specs=pl.BlockSpec(memory_space=pl.ANY),
        scratch_shapes=[pltpu.SemaphoreType.DMA(())],
    )
    return pl.pallas_call(
        _scatter_kernel,
        grid_spec=grid_spec,
        out_shape=jax.ShapeDtypeStruct((n_slots, d), _f32),
        input_output_aliases={2: 0},
        compiler_params=pltpu.CompilerParams(dimension_semantics=("arbitrary",)),
        name="scatter_rows",
    )(pos, hf, init)


def _combine_kernel(pos_ref, x_ref, gf_ref, ys_ref, fg_ref, o_ref, buf_ref, sem, *, final_norm):
    i = pl.program_id(0)
    rows = x_ref.shape[0]

    def issue(r, c):
        _row_copy(ys_ref, buf_ref, sem, pos_ref[i * rows + r], r).start()
        return c

    lax.fori_loop(0, rows, issue, 0, unroll=8)

    def drain(r, c):
        _row_copy(ys_ref, buf_ref, sem, 0, r).wait()
        return c

    lax.fori_loop(0, rows, drain, 0, unroll=8)
    out = x_ref[...] + gf_ref[...] * buf_ref[...]
    if final_norm:
        out = out * lax.rsqrt(jnp.mean(out * out, axis=-1, keepdims=True) + EPS) * fg_ref[...]
    o_ref[...] = out


def _combine_rows(pos, x2d, gate_f, ys, final_g, seq, final_norm):
    t, d = x2d.shape
    tiles_per_seq = seq // ROW_TILE
    grid_spec = pltpu.PrefetchScalarGridSpec(
        num_scalar_prefetch=1,
        grid=(t // ROW_TILE,),
        in_specs=[
            pl.BlockSpec((ROW_TILE, d), lambda i, pos: (i, 0)),
            pl.BlockSpec((None, 1, d), lambda i, pos: (i // tiles_per_seq, 0, 0)),
            pl.BlockSpec(memory_space=pl.ANY),
            pl.BlockSpec((1, d), lambda i, pos: (0, 0)),
        ],
        out_specs=pl.BlockSpec((ROW_TILE, d), lambda i, pos: (i, 0)),
        scratch_shapes=[pltpu.VMEM((ROW_TILE, d), _f32), pltpu.SemaphoreType.DMA(())],
    )
    return pl.pallas_call(
        functools.partial(_combine_kernel, final_norm=final_norm),
        grid_spec=grid_spec,
        out_shape=jax.ShapeDtypeStruct((t, d), _f32),
        compiler_params=pltpu.CompilerParams(dimension_semantics=("arbitrary",)),
        name="combine_rows",
    )(pos, x2d, gate_f, ys, final_g)


def _experts_kernel(ea_ref, eb_ref, nused_ref, xs_ref, rwa_ref, rwb_ref,
                    wga_ref, wgb_ref, wua_ref, wub_ref, wda_ref, wdb_ref, ys_ref):
    i = pl.program_id(0)

    @pl.when(i < nused_ref[0])
    def _():
        x = xs_ref[...]
        la = jnp.sum(x * rwa_ref[...], axis=-1, keepdims=True)
        lb = jnp.sum(x * rwb_ref[...], axis=-1, keepdims=True)
        wa = jax.nn.sigmoid(la - lb)
        wb = jax.nn.sigmoid(lb - la)
        xb = x.astype(_bf16)
        act_a = (jax.nn.silu(_dot(xb, wga_ref[...])) * _dot(xb, wua_ref[...]) * wa).astype(_bf16)
        act_b = (jax.nn.silu(_dot(xb, wgb_ref[...])) * _dot(xb, wub_ref[...]) * wb).astype(_bf16)
        ys_ref[...] = _dot(act_a, wda_ref[...]) + _dot(act_b, wdb_ref[...])

    @pl.when(i >= nused_ref[0])
    def _():
        ys_ref[...] = jnp.zeros_like(ys_ref)


def _experts(tile_ea, tile_eb, n_used, xs, router_wt, w_gate, w_up, w_down):
    n_slots, d = xs.shape
    n_tiles = n_slots // EXPERT_TILE
    f = D_EXPERT

    def row(i, ea, eb, nu):
        return (jnp.minimum(i, nu[0] - 1), 0)

    grid_spec = pltpu.PrefetchScalarGridSpec(
        num_scalar_prefetch=3,
        grid=(n_tiles,),
        in_specs=[
            pl.BlockSpec((EXPERT_TILE, d), row),
            pl.BlockSpec((None, 1, d), lambda i, ea, eb, nu: (ea[i], 0, 0)),
            pl.BlockSpec((None, 1, d), lambda i, ea, eb, nu: (eb[i], 0, 0)),
            pl.BlockSpec((None, d, f), lambda i, ea, eb, nu: (ea[i], 0, 0)),
            pl.BlockSpec((None, d, f), lambda i, ea, eb, nu: (eb[i], 0, 0)),
            pl.BlockSpec((None, d, f), lambda i, ea, eb, nu: (ea[i], 0, 0)),
            pl.BlockSpec((None, d, f), lambda i, ea, eb, nu: (eb[i], 0, 0)),
            pl.BlockSpec((None, f, d), lambda i, ea, eb, nu: (ea[i], 0, 0)),
            pl.BlockSpec((None, f, d), lambda i, ea, eb, nu: (eb[i], 0, 0)),
        ],
        out_specs=pl.BlockSpec((EXPERT_TILE, d), lambda i, ea, eb, nu: (i, 0)),
    )
    return pl.pallas_call(
        _experts_kernel,
        grid_spec=grid_spec,
        out_shape=jax.ShapeDtypeStruct((n_slots, d), _f32),
        compiler_params=pltpu.CompilerParams(
            dimension_semantics=("arbitrary",), vmem_limit_bytes=VMEM_LIMIT_BYTES),
        name="grouped_experts",
    )(tile_ea, tile_eb, n_used, xs, router_wt, router_wt, w_gate, w_gate, w_up, w_up, w_down, w_down)


def _routing_tables(route, counts, n_tiles_max):
    bucket = route[:, 0, :].reshape(-1).astype(jnp.int32)
    rank = route[:, 1, :].reshape(-1).astype(jnp.int32)
    cnt = counts[:N_BUCKETS, 0].astype(jnp.int32)
    tiles_b = (cnt + EXPERT_TILE - 1) // EXPERT_TILE
    tile_end = jnp.cumsum(tiles_b)
    tile_start = tile_end - tiles_b
    n_used = tile_end[-1]
    onehot = bucket[:, None] == jnp.arange(N_BUCKETS, dtype=jnp.int32)[None, :]
    pos = jnp.sum(jnp.where(onehot, (tile_start * EXPERT_TILE)[None, :], 0), axis=1) + rank
    tile_ids = jnp.minimum(jnp.arange(n_tiles_max, dtype=jnp.int32), n_used - 1)
    tile_bucket = jnp.sum((tile_ids[:, None] >= tile_end[None, :]).astype(jnp.int32), axis=1)
    group = tile_bucket // PAIRS_PER_GROUP
    pair = tile_bucket % PAIRS_PER_GROUP
    lo = jnp.asarray(_PAIR_LO, jnp.int32)[pair]
    hi = jnp.asarray(_PAIR_HI, jnp.int32)[pair]
    tile_ea = group * EXPERTS_PER_GROUP + lo
    tile_eb = group * EXPERTS_PER_GROUP + hi
    return pos, tile_ea, tile_eb, n_used.reshape(1)


def kernel(x, c, w_ada, b_ada, norm_mix_g, w_in, v_norm_g, sgu_w, sgu_b, pool_w, pool_scale,
           w_branch_a, w_branch_b, w_out, norm_ffn_g, router_w, router_bias,
           w_exp_gate, w_exp_up, w_exp_down, final_norm_g):
    batch, seq, d = x.shape
    depth = w_ada.shape[0]
    t = batch * seq
    n_tiles_max = t // EXPERT_TILE + N_BUCKETS
    n_slots = n_tiles_max * EXPERT_TILE

    mod = _ada(c, w_ada, b_ada)
    rw_pad = jnp.pad(router_w, ((0, 0), (0, 128 - N_EXPERTS)))
    rw_hi = rw_pad.astype(_bf16)
    rw_lo = (rw_pad - rw_hi.astype(_f32)).astype(_bf16)
    router_w_pad = jnp.concatenate([rw_hi, rw_lo], axis=1)
    router_wt = router_w.T.reshape(N_EXPERTS, 1, d)
    final_g = final_norm_g.reshape(1, d)

    for l in range(depth):
        mod_l = mod[l].reshape(batch, 1, N_MOD * d)
        x, hf, route, counts = _mixer(
            x, mod_l, norm_mix_g[l].reshape(1, d), norm_ffn_g[l].reshape(1, d),
            w_in[l].astype(_bf16), v_norm_g[l].reshape(1, SGU_WIDTH),
            sgu_w[l].astype(_bf16), sgu_b[l].T,
            pool_w[l].astype(_bf16), pool_scale[l].reshape(1, POOL_WIDTH),
            w_branch_a[l].astype(_bf16), w_branch_b[l].astype(_bf16), w_out[l].astype(_bf16),
            router_w_pad, router_bias)
        pos, tile_ea, tile_eb, n_used = _routing_tables(route, counts, n_tiles_max)
        xs = _scatter_rows(pos, hf.reshape(t, d), n_slots)
        ys = _experts(tile_ea, tile_eb, n_used, xs, router_wt,
                      w_exp_gate[l].astype(_bf16), w_exp_up[l].astype(_bf16),
                      w_exp_down[l].astype(_bf16))
        gate_f = mod_l[:, :, 5 * d:6 * d]
        x = _combine_rows(pos, x.reshape(t, d), gate_f, ys, final_g, seq,
                          final_norm=(l == depth - 1)).reshape(batch, seq, d)
    return x
```

```python
import functools

import jax
import jax.numpy as jnp
from jax import lax
from jax.experimental import pallas as pl
from jax.experimental.pallas import tpu as pltpu

D_MODEL = 1024
CHUNK = 128
SGU_HEADS = 8
SGU_HEAD_DIM = 128
SGU_WIDTH = 1024
POOL_WINDOWS = (2, 4, 8, 16)
POOL_GROUP_DIM = 256
POOL_WIDTH = 1024
HALO = 16
N_EXPERTS = 16
N_EXPERT_GROUPS = 4
EXPERTS_PER_GROUP = 4
PAIRS_PER_GROUP = 6
N_BUCKETS = N_EXPERT_GROUPS * PAIRS_PER_GROUP
BUCKET_ROWS = 32
D_EXPERT = 512
N_MOD = 6
EPS = 1e-6

SEQ_TILE = 512
SUB_TILE = 256
ROW_TILE = 512
EXPERT_TILE = 256
VMEM_LIMIT_BYTES = 58 * 1024 * 1024

_PAIR_LO = (0, 0, 0, 1, 1, 2)
_PAIR_HI = (1, 2, 3, 2, 3, 3)

_bf16 = jnp.bfloat16
_f32 = jnp.float32


def _dot(a, b):
    return jnp.dot(a, b, preferred_element_type=_f32)


def _rms_modulate(x, g, shift, scale):
    y = x * lax.rsqrt(jnp.mean(x * x, axis=-1, keepdims=True) + EPS)
    return (y * g) * (1.0 + scale) + shift


def _ada_kernel(c_ref, w_ref, b_ref, o_ref):
    c = c_ref[...]
    c_act = (c * jax.nn.sigmoid(c)).astype(_bf16)
    o_ref[...] = _dot(c_act, w_ref[...].astype(_bf16)) + b_ref[...]


def _ada(c, w_ada, b_ada):
    depth, d, n = w_ada.shape
    batch = c.shape[0]
    tn = 2048
    return pl.pallas_call(
        _ada_kernel,
        grid=(depth, n // tn),
        in_specs=[
            pl.BlockSpec((batch, d), lambda l, j: (0, 0)),
            pl.BlockSpec((None, d, tn), lambda l, j: (l, 0, j)),
            pl.BlockSpec((None, 1, tn), lambda l, j: (l, 0, j)),
        ],
        out_specs=pl.BlockSpec((None, batch, tn), lambda l, j: (l, 0, j)),
        out_shape=jax.ShapeDtypeStruct((depth, batch, n), _f32),
        compiler_params=pltpu.CompilerParams(
            dimension_semantics=("arbitrary", "arbitrary"), vmem_limit_bytes=VMEM_LIMIT_BYTES),
        name="ada_modulation",
    )(c, w_ada, b_ada.reshape(depth, 1, n))


def _route(hf, rw_ref, rb_ref, carry_ref):
    ts = hf.shape[0]
    hf_hi = hf.astype(_bf16)
    hf_lo = (hf - hf_hi.astype(_f32)).astype(_bf16)
    both = _dot(hf_hi, rw_ref[...])
    logits = both[:, 0:128] + both[:, 128:256] + _dot(hf_lo, rw_ref[:, 0:128])
    lt = logits.T
    rows = [lt[e:e + 1, :] for e in range(N_EXPERTS)]
    m = functools.reduce(jnp.maximum, rows)
    ex = [jnp.exp(r - m) for r in rows]
    den = functools.reduce(lambda a, b: a + b, ex)
    probs = [e / den for e in ex]
    sel = [probs[e] + rb_ref[e] for e in range(N_EXPERTS)]

    def top2_sum(v):
        pairs = [v[i] + v[j] for i, j in zip(_PAIR_LO, _PAIR_HI)]
        return functools.reduce(jnp.maximum, pairs)

    gscore = [top2_sum(sel[4 * g:4 * g + 4]) for g in range(N_EXPERT_GROUPS)]
    best = gscore[0]
    gidx = jnp.zeros_like(best, dtype=jnp.int32)
    for g in range(1, N_EXPERT_GROUPS):
        better = gscore[g] > best
        best = jnp.where(better, gscore[g], best)
        gidx = jnp.where(better, g, gidx)
    ing = []
    for k in range(EXPERTS_PER_GROUP):
        v = sel[k]
        for g in range(1, N_EXPERT_GROUPS):
            v = jnp.where(gidx == g, sel[4 * g + k], v)
        ing.append(v)
    chosen = []
    for k in range(EXPERTS_PER_GROUP):
        r = jnp.zeros_like(gidx)
        for j in range(EXPERTS_PER_GROUP):
            if j == k:
                continue
            beats = (ing[j] >= ing[k]) if j < k else (ing[j] > ing[k])
            r = r + beats.astype(jnp.int32)
        chosen.append(r < 2)
    lo = jnp.where(chosen[0], 0, jnp.where(chosen[1], 1, 2))
    hi = jnp.where(chosen[3], 3, jnp.where(chosen[2], 2, 1))
    base = jnp.where(lo == 0, 0, jnp.where(lo == 1, 3, 5))
    bucket = gidx * PAIRS_PER_GROUP + base + hi - lo - 1

    brow = lax.broadcasted_iota(jnp.int32, (BUCKET_ROWS, ts), 0)
    onehot = (brow == bucket).astype(_f32)
    jj = lax.broadcasted_iota(jnp.int32, (ts, ts), 0)
    tt = lax.broadcasted_iota(jnp.int32, (ts, ts), 1)
    upper = (jj <= tt).astype(_bf16)
    cum = _dot(onehot.astype(_bf16), upper)
    carry = carry_ref[...][:, 0:1]
    rank = jnp.sum(onehot * (cum - 1.0 + carry), axis=0, keepdims=True)
    carry_ref[...] = carry_ref[...] + jnp.sum(onehot, axis=1, keepdims=True)
    return bucket.astype(_f32), rank


def _mixer_kernel(*refs, has_prev):
    if has_prev:
        x_ref, y_ref, gprev_ref = refs[:3]
        refs = refs[3:]
        x_in = lambda rows: x_ref[rows, :] + gprev_ref[...] * y_ref[rows, :]
    else:
        x_ref = refs[0]
        refs = refs[1:]
        x_in = lambda rows: x_ref[rows, :]
    (mod_ref, gmix_ref, gffn_ref, win_ref, vg_ref, sw_ref, sb_ref, pw_ref, ps_ref, wa_ref, wb_ref,
     wo_ref, rw_ref, rb_ref, xo_ref, hf_ref, route_ref, counts_ref, ext_ref, carry_ref) = refs
    b = pl.program_id(0)
    s = pl.program_id(1)
    ts = x_ref.shape[0]
    d = D_MODEL

    @pl.when((b == 0) & (s == 0))
    def _():
        carry_ref[...] = jnp.zeros_like(carry_ref)

    @pl.when(s == 0)
    def _():
        ext_ref[0:HALO, :] = jnp.zeros((HALO, POOL_WIDTH), _f32)

    mod = mod_ref[...]
    ci = lax.broadcasted_iota(jnp.int32, (CHUNK, CHUNK), 0)
    cj = lax.broadcasted_iota(jnp.int32, (CHUNK, CHUNK), 1)
    sgu_w = [jnp.where(ci >= cj, sw_ref[h], jnp.zeros((), _bf16)) for h in range(SGU_HEADS)]
    for r in range(ts // SUB_TILE):
        _mixer_rows(r * SUB_TILE, s * ts + r * SUB_TILE, mod, sgu_w,
                    x_in, gmix_ref, gffn_ref, win_ref, vg_ref, sb_ref, pw_ref, ps_ref, wa_ref,
                    wb_ref, wo_ref, rw_ref, rb_ref, xo_ref, hf_ref, route_ref, ext_ref, carry_ref)
    ext_ref[0:HALO, :] = ext_ref[ts:ts + HALO, :]
    route_ref[2:8, :] = jnp.zeros((6, ts), _f32)
    counts_ref[...] = carry_ref[...]


def _mixer_rows(row0, seq_pos0, mod, sgu_w, x_in, gmix_ref, gffn_ref, win_ref, vg_ref, sb_ref,
                pw_ref, ps_ref, wa_ref, wb_ref, wo_ref, rw_ref, rb_ref, xo_ref, hf_ref, route_ref,
                ext_ref, carry_ref):
    d = D_MODEL
    ts = SUB_TILE
    rows = slice(row0, row0 + ts)
    sh_m, sc_m, g_m = mod[:, 0:d], mod[:, d:2 * d], mod[:, 2 * d:3 * d]
    sh_f, sc_f = mod[:, 3 * d:4 * d], mod[:, 4 * d:5 * d]
    x = x_in(rows)

    hb = _rms_modulate(x, gmix_ref[...], sh_m, sc_m).astype(_bf16)

    v = jax.nn.gelu(_dot(hb, win_ref[:, SGU_WIDTH:2 * SGU_WIDTH]))
    vc = v - jnp.mean(v, axis=-1, keepdims=True)
    vn = (vc * lax.rsqrt(jnp.mean(vc * vc, axis=-1, keepdims=True) + EPS) * vg_ref[...]).astype(_bf16)
    u = jax.nn.gelu(_dot(hb, win_ref[:, 0:SGU_WIDTH]))
    n_chunks = ts // CHUNK
    ya_cols = []
    for h in range(SGU_HEADS):
        cols = slice(h * SGU_HEAD_DIM, (h + 1) * SGU_HEAD_DIM)
        rhs = jnp.concatenate([vn[n * CHUNK:(n + 1) * CHUNK, cols] for n in range(n_chunks)], axis=1)
        sg = _dot(sgu_w[h], rhs) + sb_ref[:, h:h + 1]
        s_h = jnp.concatenate([sg[:, n * SGU_HEAD_DIM:(n + 1) * SGU_HEAD_DIM] for n in range(n_chunks)],
                              axis=0)
        ya_cols.append((u[:, cols] * s_h).astype(_bf16))
    ya = jnp.concatenate(ya_cols, axis=1)
    gate_a = jax.nn.sigmoid(_dot(hb, win_ref[:, 3 * d:4 * d]))
    merged = gate_a * _dot(ya, wa_ref[...])

    p = _dot(hb, win_ref[:, 2 * SGU_WIDTH:2 * SGU_WIDTH + POOL_WIDTH])
    e0 = HALO + row0
    ext_ref[e0:e0 + ts, :] = p
    pos1 = (seq_pos0 + 1 + lax.broadcasted_iota(jnp.int32, (ts, 1), 0)).astype(_f32)
    yb_cols = []
    for gi, w in enumerate(POOL_WINDOWS):
        cols = slice(gi * POOL_GROUP_DIM, (gi + 1) * POOL_GROUP_DIM)
        acc = p[:, cols]
        for k in range(1, w):
            acc = acc + ext_ref[e0 - k:e0 - k + ts, cols]
        count = jnp.minimum(pos1, float(w))
        pooled = (acc / count - p[:, cols]).astype(_bf16)
        yb_cols.append(_dot(pooled, pw_ref[gi]))
    yb = (jnp.concatenate(yb_cols, axis=1) * ps_ref[...]).astype(_bf16)
    gate_b = jax.nn.sigmoid(_dot(hb, win_ref[:, 4 * d:5 * d]))
    merged = merged + gate_b * _dot(yb, wb_ref[...])

    x_new = x + g_m * _dot(merged.astype(_bf16), wo_ref[...])
    xo_ref[rows, :] = x_new

    hf = _rms_modulate(x_new, gffn_ref[...], sh_f, sc_f)
    hf_ref[rows, :] = hf
    bucket, rank = _route(hf, rw_ref, rb_ref, carry_ref)
    route_ref[0:1, rows] = bucket
    route_ref[1:2, rows] = rank


def _mixer(x, y_prev, gate_prev, mod_l, gmix, gffn, w_in, v_g, sgu_w, sgu_bt, pool_w, pool_scale,
           w_a, w_b, w_o, router_w_pad, router_bias):
    batch, seq, d = x.shape
    has_prev = y_prev is not None
    ts = SEQ_TILE
    n_tiles = batch * (seq // ts)
    tiles_per_seq = seq // ts
    const = lambda *shape: pl.BlockSpec(shape, lambda b, s: (0,) * len(shape),
                                        pipeline_mode=pl.Buffered(1))
    row_spec = pl.BlockSpec((None, ts, d), lambda b, s: (b, s, 0))
    prev_specs = [row_spec, pl.BlockSpec((None, 1, d), lambda b, s: (b, 0, 0))] if has_prev else []
    prev_args = (y_prev, gate_prev) if has_prev else ()
    return pl.pallas_call(
        functools.partial(_mixer_kernel, has_prev=has_prev),
        grid=(batch, tiles_per_seq),
        in_specs=[row_spec] + prev_specs + [
            pl.BlockSpec((None, 1, N_MOD * d), lambda b, s: (b, 0, 0)),
            const(1, d), const(1, d),
            const(*w_in.shape),
            const(1, SGU_WIDTH),
            const(*sgu_w.shape),
            const(*sgu_bt.shape),
            const(*pool_w.shape),
            const(1, POOL_WIDTH),
            const(d, d), const(d, d), const(d, d),
            const(*router_w_pad.shape),
            pl.BlockSpec(memory_space=pltpu.SMEM),
        ],
        out_specs=[
            pl.BlockSpec((None, ts, d), lambda b, s: (b, s, 0)),
            pl.BlockSpec((None, ts, d), lambda b, s: (b, s, 0)),
            pl.BlockSpec((None, 8, ts), lambda b, s: (b * tiles_per_seq + s, 0, 0)),
            pl.BlockSpec((BUCKET_ROWS, 128), lambda b, s: (0, 0)),
        ],
        out_shape=[
            jax.ShapeDtypeStruct((batch, seq, d), _f32),
            jax.ShapeDtypeStruct((batch, seq, d), _f32),
            jax.ShapeDtypeStruct((n_tiles, 8, ts), _f32),
            jax.ShapeDtypeStruct((BUCKET_ROWS, 128), _f32),
        ],
        scratch_shapes=[
            pltpu.VMEM((HALO + ts, POOL_WIDTH), _f32),
            pltpu.VMEM((BUCKET_ROWS, 128), _f32),
        ],
        compiler_params=pltpu.CompilerParams(
            dimension_semantics=("arbitrary", "arbitrary"), vmem_limit_bytes=VMEM_LIMIT_BYTES),
        name="mixer_router",
    )(x, *prev_args, mod_l, gmix, gffn, w_in, v_g, sgu_w, sgu_bt, pool_w, pool_scale, w_a, w_b, w_o,
      router_w_pad, router_bias)


def _row_move_kernel(idx_ref, src_ref, *rest, rows, gather):
    dst_ref, sem = rest[-2], rest[-1]
    i = pl.program_id(0)

    def issue(r, c):
        t = i * rows + r
        src_row, dst_row = (idx_ref[t], t) if gather else (t, idx_ref[t])
        pltpu.make_async_copy(src_ref.at[pl.ds(src_row, 1)], dst_ref.at[pl.ds(dst_row, 1)], sem).start()
        return c

    lax.fori_loop(0, rows, issue, 0, unroll=16)

    def wait_one_step():
        pltpu.make_async_copy(src_ref.at[pl.ds(0, rows)], dst_ref.at[pl.ds(0, rows)], sem).wait()

    @pl.when(i > 0)
    def _():
        wait_one_step()

    @pl.when(i == pl.num_programs(0) - 1)
    def _():
        wait_one_step()


def _scatter_rows(pos, hf, n_slots):
    t, d = hf.shape
    init = jnp.zeros((n_slots, d), _f32)
    grid_spec = pltpu.PrefetchScalarGridSpec(
        num_scalar_prefetch=1,
        grid=(t // ROW_TILE,),
        in_specs=[pl.BlockSpec(memory_space=pl.ANY), pl.BlockSpec(memory_space=pl.ANY)],
        out_specs=pl.BlockSpec(memory_space=pl.ANY),
        scratch_shapes=[pltpu.SemaphoreType.DMA(())],
    )
    return pl.pallas_call(
        functools.partial(_row_move_kernel, rows=ROW_TILE, gather=False),
        grid_spec=grid_spec,
        out_shape=jax.ShapeDtypeStruct((n_slots, d), _f32),
        input_output_aliases={2: 0},
        compiler_params=pltpu.CompilerParams(dimension_semantics=("arbitrary",)),
        name="scatter_rows",
    )(pos, hf, init)


def _gather_rows(pos, ys):
    t = pos.shape[0]
    d = ys.shape[1]
    grid_spec = pltpu.PrefetchScalarGridSpec(
        num_scalar_prefetch=1,
        grid=(t // ROW_TILE,),
        in_specs=[pl.BlockSpec(memory_space=pl.ANY)],
        out_specs=pl.BlockSpec(memory_space=pl.ANY),
        scratch_shapes=[pltpu.SemaphoreType.DMA(())],
    )
    return pl.pallas_call(
        functools.partial(_row_move_kernel, rows=ROW_TILE, gather=True),
        grid_spec=grid_spec,
        out_shape=jax.ShapeDtypeStruct((t, d), _f32),
        compiler_params=pltpu.CompilerParams(dimension_semantics=("arbitrary",)),
        name="gather_rows",
    )(pos, ys)


def _final_kernel(x_ref, y_ref, g_ref, fg_ref, o_ref):
    out = x_ref[...] + g_ref[...] * y_ref[...]
    o_ref[...] = out * lax.rsqrt(jnp.mean(out * out, axis=-1, keepdims=True) + EPS) * fg_ref[...]


def _final_norm(x, y_tok, gate_f, final_g):
    batch, seq, d = x.shape
    row_spec = pl.BlockSpec((None, ROW_TILE, d), lambda b, s: (b, s, 0))
    return pl.pallas_call(
        _final_kernel,
        grid=(batch, seq // ROW_TILE),
        in_specs=[row_spec, row_spec,
                  pl.BlockSpec((None, 1, d), lambda b, s: (b, 0, 0)),
                  pl.BlockSpec((1, d), lambda b, s: (0, 0))],
        out_specs=row_spec,
        out_shape=jax.ShapeDtypeStruct((batch, seq, d), _f32),
        compiler_params=pltpu.CompilerParams(dimension_semantics=("arbitrary", "arbitrary")),
        name="residual_final_norm",
    )(x, y_tok, gate_f, final_g)


def _experts_kernel(ea_ref, eb_ref, nused_ref, xs_ref, rwa_ref, rwb_ref,
                    wga_ref, wgb_ref, wua_ref, wub_ref, wda_ref, wdb_ref, ys_ref):
    i = pl.program_id(0)

    @pl.when(i < nused_ref[0])
    def _():
        x = xs_ref[...]
        la = jnp.sum(x * rwa_ref[...], axis=-1, keepdims=True)
        lb = jnp.sum(x * rwb_ref[...], axis=-1, keepdims=True)
        wa = jax.nn.sigmoid(la - lb)
        wb = jax.nn.sigmoid(lb - la)
        xb = x.astype(_bf16)
        act_a = (jax.nn.silu(_dot(xb, wga_ref[...])) * _dot(xb, wua_ref[...]) * wa).astype(_bf16)
        act_b = (jax.nn.silu(_dot(xb, wgb_ref[...])) * _dot(xb, wub_ref[...]) * wb).astype(_bf16)
        ys_ref[...] = _dot(act_a, wda_ref[...]) + _dot(act_b, wdb_ref[...])

    @pl.when(i >= nused_ref[0])
    def _():
        ys_ref[...] = jnp.zeros_like(ys_ref)


def _experts(tile_ea, tile_eb, n_used, xs, router_wt, w_gate, w_up, w_down):
    n_slots, d = xs.shape
    n_tiles = n_slots // EXPERT_TILE
    f = D_EXPERT

    def row(i, ea, eb, nu):
        return (jnp.maximum(jnp.minimum(i, nu[0] - 1), 0), 0)

    grid_spec = pltpu.PrefetchScalarGridSpec(
        num_scalar_prefetch=3,
        grid=(n_tiles,),
        in_specs=[
            pl.BlockSpec((EXPERT_TILE, d), row),
            pl.BlockSpec((None, 1, d), lambda i, ea, eb, nu: (ea[i], 0, 0)),
            pl.BlockSpec((None, 1, d), lambda i, ea, eb, nu: (eb[i], 0, 0)),
            pl.BlockSpec((None, d, f), lambda i, ea, eb, nu: (ea[i], 0, 0)),
            pl.BlockSpec((None, d, f), lambda i, ea, eb, nu: (eb[i], 0, 0)),
            pl.BlockSpec((None, d, f), lambda i, ea, eb, nu: (ea[i], 0, 0)),
            pl.BlockSpec((None, d, f), lambda i, ea, eb, nu: (eb[i], 0, 0)),
            pl.BlockSpec((None, f, d), lambda i, ea, eb, nu: (ea[i], 0, 0)),
            pl.BlockSpec((None, f, d), lambda i, ea, eb, nu: (eb[i], 0, 0)),
        ],
        out_specs=pl.BlockSpec((EXPERT_TILE, d), lambda i, ea, eb, nu: (i, 0)),
    )
    return pl.pallas_call(
        _experts_kernel,
        grid_spec=grid_spec,
        out_shape=jax.ShapeDtypeStruct((n_slots, d), _f32),
        compiler_params=pltpu.CompilerParams(
            dimension_semantics=("arbitrary",), vmem_limit_bytes=VMEM_LIMIT_BYTES),
        name="grouped_experts",
    )(tile_ea, tile_eb, n_used, xs, router_wt, router_wt, w_gate, w_gate, w_up, w_up, w_down, w_down)


def _routing_tables(route, counts, n_tiles_max):
    bucket = route[:, 0, :].reshape(-1).astype(jnp.int32)
    rank = route[:, 1, :].reshape(-1).astype(jnp.int32)
    cnt = counts[:N_BUCKETS, 0].astype(jnp.int32)
    tiles_b = (cnt + EXPERT_TILE - 1) // EXPERT_TILE
    tile_end = jnp.cumsum(tiles_b)
    tile_start = tile_end - tiles_b
    n_used = tile_end[-1]
    onehot = bucket[:, None] == jnp.arange(N_BUCKETS, dtype=jnp.int32)[None, :]
    pos = jnp.sum(jnp.where(onehot, (tile_start * EXPERT_TILE)[None, :], 0), axis=1) + rank
    tile_ids = jnp.minimum(jnp.arange(n_tiles_max, dtype=jnp.int32), n_used - 1)
    tile_bucket = jnp.sum((tile_ids[:, None] >= tile_end[None, :]).astype(jnp.int32), axis=1)
    group = tile_bucket // PAIRS_PER_GROUP
    pair = tile_bucket % PAIRS_PER_GROUP
    lo = jnp.asarray(_PAIR_LO, jnp.int32)[pair]
    hi = jnp.asarray(_PAIR_HI, jnp.int32)[pair]
    tile_ea = group * EXPERTS_PER_GROUP + lo
    tile_eb = group * EXPERTS_PER_GROUP + hi
    return pos, tile_ea, tile_eb, n_used.reshape(1)


def kernel(x, c, w_ada, b_ada, norm_mix_g, w_in, v_norm_g, sgu_w, sgu_b, pool_w, pool_scale,
           w_branch_a, w_branch_b, w_out, norm_ffn_g, router_w, router_bias,
           w_exp_gate, w_exp_up, w_exp_down, final_norm_g):
    batch, seq, d = x.shape
    depth = w_ada.shape[0]
    t = batch * seq
    n_tiles_max = t // EXPERT_TILE + N_BUCKETS
    n_slots = n_tiles_max * EXPERT_TILE

    mod = _ada(c, w_ada, b_ada)
    rw_pad = jnp.pad(router_w, ((0, 0), (0, 128 - N_EXPERTS)))
    rw_hi = rw_pad.astype(_bf16)
    rw_lo = (rw_pad - rw_hi.astype(_f32)).astype(_bf16)
    router_w_pad = jnp.concatenate([rw_hi, rw_lo], axis=1)
    router_wt = router_w.T.reshape(N_EXPERTS, 1, d)
    final_g = final_norm_g.reshape(1, d)

    y_tok, gate_prev = None, None
    for l in range(depth):
        mod_l = mod[l].reshape(batch, 1, N_MOD * d)
        x, hf, route, counts = _mixer(
            x, y_tok, gate_prev, mod_l, norm_mix_g[l].reshape(1, d), norm_ffn_g[l].reshape(1, d),
            w_in[l].astype(_bf16), v_norm_g[l].reshape(1, SGU_WIDTH),
            sgu_w[l].astype(_bf16), sgu_b[l].T,
            pool_w[l].astype(_bf16), pool_scale[l].reshape(1, POOL_WIDTH),
            w_branch_a[l].astype(_bf16), w_branch_b[l].astype(_bf16), w_out[l].astype(_bf16),
            router_w_pad, router_bias)
        pos, tile_ea, tile_eb, n_used = _routing_tables(route, counts, n_tiles_max)
        xs = _scatter_rows(pos, hf.reshape(t, d), n_slots)
        ys = _experts(tile_ea, tile_eb, n_used, xs, router_wt,
                      w_exp_gate[l].astype(_bf16), w_exp_up[l].astype(_bf16),
                      w_exp_down[l].astype(_bf16))
        y_tok = _gather_rows(pos, ys).reshape(batch, seq, d)
        gate_prev = mod_l[:, :, 5 * d:6 * d]
    return _final_norm(x, y_tok, gate_prev, final_g)
```

```python
import functools

import jax
import jax.numpy as jnp
from jax import lax
from jax.experimental import pallas as pl
from jax.experimental.pallas import tpu as pltpu

D_MODEL = 1024
CHUNK = 128
SGU_HEADS = 8
SGU_HEAD_DIM = 128
SGU_WIDTH = 1024
POOL_WINDOWS = (2, 4, 8, 16)
POOL_GROUP_DIM = 256
POOL_WIDTH = 1024
HALO = 16
N_EXPERTS = 16
N_EXPERT_GROUPS = 4
EXPERTS_PER_GROUP = 4
PAIRS_PER_GROUP = 6
N_BUCKETS = N_EXPERT_GROUPS * PAIRS_PER_GROUP
BUCKET_ROWS = 32
D_EXPERT = 512
N_MOD = 6
EPS = 1e-6

SEQ_TILE = 512
SUB_TILE = 256
ROW_TILE = 512
EXPERT_TILE = 256
VMEM_LIMIT_BYTES = 58 * 1024 * 1024

_PAIR_LO = (0, 0, 0, 1, 1, 2)
_PAIR_HI = (1, 2, 3, 2, 3, 3)

_bf16 = jnp.bfloat16
_f32 = jnp.float32


def _dot(a, b):
    return jnp.dot(a, b, preferred_element_type=_f32)


def _rms_modulate(x, g, shift, scale):
    y = x * lax.rsqrt(jnp.mean(x * x, axis=-1, keepdims=True) + EPS)
    return (y * g) * (1.0 + scale) + shift


def _ada_kernel(c_ref, w_ref, b_ref, o_ref):
    c = c_ref[...]
    c_act = (c * jax.nn.sigmoid(c)).astype(_bf16)
    o_ref[...] = _dot(c_act, w_ref[...].astype(_bf16)) + b_ref[...]


def _ada(c, w_ada, b_ada):
    depth, d, n = w_ada.shape
    batch = c.shape[0]
    tn = 2048
    return pl.pallas_call(
        _ada_kernel,
        grid=(depth, n // tn),
        in_specs=[
            pl.BlockSpec((batch, d), lambda l, j: (0, 0)),
            pl.BlockSpec((None, d, tn), lambda l, j: (l, 0, j)),
            pl.BlockSpec((None, 1, tn), lambda l, j: (l, 0, j)),
        ],
        out_specs=pl.BlockSpec((None, batch, tn), lambda l, j: (l, 0, j)),
        out_shape=jax.ShapeDtypeStruct((depth, batch, n), _f32),
        compiler_params=pltpu.CompilerParams(
            dimension_semantics=("arbitrary", "arbitrary"), vmem_limit_bytes=VMEM_LIMIT_BYTES),
        name="ada_modulation",
    )(c, w_ada, b_ada.reshape(depth, 1, n))


def _route(hf, rw_ref, rb_ref, carry_ref):
    ts = hf.shape[0]
    hf_hi = hf.astype(_bf16)
    hf_lo = (hf - hf_hi.astype(_f32)).astype(_bf16)
    both = _dot(hf_hi, rw_ref[...])
    logits = both[:, 0:128] + both[:, 128:256] + _dot(hf_lo, rw_ref[:, 0:128])
    lt = logits.T
    rows = [lt[e:e + 1, :] for e in range(N_EXPERTS)]
    m = functools.reduce(jnp.maximum, rows)
    ex = [jnp.exp(r - m) for r in rows]
    den = functools.reduce(lambda a, b: a + b, ex)
    probs = [e / den for e in ex]
    sel = [probs[e] + rb_ref[e] for e in range(N_EXPERTS)]

    def top2_sum(v):
        pairs = [v[i] + v[j] for i, j in zip(_PAIR_LO, _PAIR_HI)]
        return functools.reduce(jnp.maximum, pairs)

    gscore = [top2_sum(sel[4 * g:4 * g + 4]) for g in range(N_EXPERT_GROUPS)]
    best = gscore[0]
    gidx = jnp.zeros_like(best, dtype=jnp.int32)
    for g in range(1, N_EXPERT_GROUPS):
        better = gscore[g] > best
        best = jnp.where(better, gscore[g], best)
        gidx = jnp.where(better, g, gidx)
    ing = []
    for k in range(EXPERTS_PER_GROUP):
        v = sel[k]
        for g in range(1, N_EXPERT_GROUPS):
            v = jnp.where(gidx == g, sel[4 * g + k], v)
        ing.append(v)
    chosen = []
    for k in range(EXPERTS_PER_GROUP):
        r = jnp.zeros_like(gidx)
        for j in range(EXPERTS_PER_GROUP):
            if j == k:
                continue
            beats = (ing[j] >= ing[k]) if j < k else (ing[j] > ing[k])
            r = r + beats.astype(jnp.int32)
        chosen.append(r < 2)
    lo = jnp.where(chosen[0], 0, jnp.where(chosen[1], 1, 2))
    hi = jnp.where(chosen[3], 3, jnp.where(chosen[2], 2, 1))
    base = jnp.where(lo == 0, 0, jnp.where(lo == 1, 3, 5))
    bucket = gidx * PAIRS_PER_GROUP + base + hi - lo - 1

    brow = lax.broadcasted_iota(jnp.int32, (BUCKET_ROWS, ts), 0)
    onehot = (brow == bucket).astype(_f32)
    jj = lax.broadcasted_iota(jnp.int32, (ts, ts), 0)
    tt = lax.broadcasted_iota(jnp.int32, (ts, ts), 1)
    upper = (jj <= tt).astype(_bf16)
    cum = _dot(onehot.astype(_bf16), upper)
    carry = carry_ref[...][:, 0:1]
    rank = jnp.sum(onehot * (cum - 1.0 + carry), axis=0, keepdims=True)
    carry_ref[...] = carry_ref[...] + jnp.sum(onehot, axis=1, keepdims=True)
    return bucket.astype(_f32), rank


def _mixer_kernel(*refs, has_prev):
    if has_prev:
        x_ref, y_ref, gprev_ref = refs[:3]
        refs = refs[3:]
        x_in = lambda rows: x_ref[rows, :] + gprev_ref[...] * y_ref[rows, :]
    else:
        x_ref = refs[0]
        refs = refs[1:]
        x_in = lambda rows: x_ref[rows, :]
    (mod_ref, gmix_ref, gffn_ref, win_ref, vg_ref, sw_ref, sb_ref, pw_ref, ps_ref, wa_ref, wb_ref,
     wo_ref, rw_ref, rb_ref, xo_ref, hf_ref, route_ref, counts_ref, ext_ref, carry_ref) = refs
    b = pl.program_id(0)
    s = pl.program_id(1)
    ts = x_ref.shape[0]
    d = D_MODEL

    @pl.when((b == 0) & (s == 0))
    def _():
        carry_ref[...] = jnp.zeros_like(carry_ref)

    @pl.when(s == 0)
    def _():
        ext_ref[0:HALO, :] = jnp.zeros((HALO, POOL_WIDTH), _f32)

    mod = mod_ref[...]
    ci = lax.broadcasted_iota(jnp.int32, (CHUNK, CHUNK), 0)
    cj = lax.broadcasted_iota(jnp.int32, (CHUNK, CHUNK), 1)
    sgu_w = [jnp.where(ci >= cj, sw_ref[h], jnp.zeros((), _bf16)) for h in range(SGU_HEADS)]
    for r in range(ts // SUB_TILE):
        _mixer_rows(r * SUB_TILE, s * ts + r * SUB_TILE, mod, sgu_w,
                    x_in, gmix_ref, gffn_ref, win_ref, vg_ref, sb_ref, pw_ref, ps_ref, wa_ref,
                    wb_ref, wo_ref, rw_ref, rb_ref, xo_ref, hf_ref, route_ref, ext_ref, carry_ref)
    ext_ref[0:HALO, :] = ext_ref[ts:ts + HALO, :]
    route_ref[2:8, :] = jnp.zeros((6, ts), _f32)
    counts_ref[...] = carry_ref[...]


def _mixer_rows(row0, seq_pos0, mod, sgu_w, x_in, gmix_ref, gffn_ref, win_ref, vg_ref, sb_ref,
                pw_ref, ps_ref, wa_ref, wb_ref, wo_ref, rw_ref, rb_ref, xo_ref, hf_ref, route_ref,
                ext_ref, carry_ref):
    d = D_MODEL
    ts = SUB_TILE
    rows = slice(row0, row0 + ts)
    sh_m, sc_m, g_m = mod[:, 0:d], mod[:, d:2 * d], mod[:, 2 * d:3 * d]
    sh_f, sc_f = mod[:, 3 * d:4 * d], mod[:, 4 * d:5 * d]
    x = x_in(rows)

    hb = _rms_modulate(x, gmix_ref[...], sh_m, sc_m).astype(_bf16)

    v = jax.nn.gelu(_dot(hb, win_ref[:, SGU_WIDTH:2 * SGU_WIDTH]))
    vc = v - jnp.mean(v, axis=-1, keepdims=True)
    vn = (vc * lax.rsqrt(jnp.mean(vc * vc, axis=-1, keepdims=True) + EPS) * vg_ref[...]).astype(_bf16)
    u = jax.nn.gelu(_dot(hb, win_ref[:, 0:SGU_WIDTH]))
    n_chunks = ts // CHUNK
    ya_cols = []
    for h in range(SGU_HEADS):
        cols = slice(h * SGU_HEAD_DIM, (h + 1) * SGU_HEAD_DIM)
        rhs = jnp.concatenate([vn[n * CHUNK:(n + 1) * CHUNK, cols] for n in range(n_chunks)], axis=1)
        sg = _dot(sgu_w[h], rhs) + sb_ref[:, h:h + 1]
        s_h = jnp.concatenate([sg[:, n * SGU_HEAD_DIM:(n + 1) * SGU_HEAD_DIM] for n in range(n_chunks)],
                              axis=0)
        ya_cols.append((u[:, cols] * s_h).astype(_bf16))
    ya = jnp.concatenate(ya_cols, axis=1)
    gate_a = jax.nn.sigmoid(_dot(hb, win_ref[:, 3 * d:4 * d]))
    merged = gate_a * _dot(ya, wa_ref[...])

    p = _dot(hb, win_ref[:, 2 * SGU_WIDTH:2 * SGU_WIDTH + POOL_WIDTH])
    e0 = HALO + row0
    ext_ref[e0:e0 + ts, :] = p
    pos1 = (seq_pos0 + 1 + lax.broadcasted_iota(jnp.int32, (ts, 1), 0)).astype(_f32)
    yb_cols = []
    for gi, w in enumerate(POOL_WINDOWS):
        cols = slice(gi * POOL_GROUP_DIM, (gi + 1) * POOL_GROUP_DIM)
        acc = p[:, cols]
        for k in range(1, w):
            acc = acc + ext_ref[e0 - k:e0 - k + ts, cols]
        count = jnp.minimum(pos1, float(w))
        pooled = (acc / count - p[:, cols]).astype(_bf16)
        yb_cols.append(_dot(pooled, pw_ref[gi]))
    yb = (jnp.concatenate(yb_cols, axis=1) * ps_ref[...]).astype(_bf16)
    gate_b = jax.nn.sigmoid(_dot(hb, win_ref[:, 4 * d:5 * d]))
    merged = merged + gate_b * _dot(yb, wb_ref[...])

    x_new = x + g_m * _dot(merged.astype(_bf16), wo_ref[...])
    xo_ref[rows, :] = x_new

    hf = _rms_modulate(x_new, gffn_ref[...], sh_f, sc_f)
    hf_ref[rows, :] = hf
    bucket, rank = _route(hf, rw_ref, rb_ref, carry_ref)
    route_ref[0:1, rows] = bucket
    route_ref[1:2, rows] = rank


def _mixer(x, y_prev, gate_prev, mod_l, gmix, gffn, w_in, v_g, sgu_w, sgu_bt, pool_w, pool_scale,
           w_a, w_b, w_o, router_w_pad, router_bias):
    batch, seq, d = x.shape
    has_prev = y_prev is not None
    ts = SEQ_TILE
    n_tiles = batch * (seq // ts)
    tiles_per_seq = seq // ts
    const = lambda *shape: pl.BlockSpec(shape, lambda b, s: (0,) * len(shape),
                                        pipeline_mode=pl.Buffered(1))
    row_spec = pl.BlockSpec((None, ts, d), lambda b, s: (b, s, 0))
    prev_specs = [row_spec, pl.BlockSpec((None, 1, d), lambda b, s: (b, 0, 0))] if has_prev else []
    prev_args = (y_prev, gate_prev) if has_prev else ()
    return pl.pallas_call(
        functools.partial(_mixer_kernel, has_prev=has_prev),
        grid=(batch, tiles_per_seq),
        in_specs=[row_spec] + prev_specs + [
            pl.BlockSpec((None, 1, N_MOD * d), lambda b, s: (b, 0, 0)),
            const(1, d), const(1, d),
            const(*w_in.shape),
            const(1, SGU_WIDTH),
            const(*sgu_w.shape),
            const(*sgu_bt.shape),
            const(*pool_w.shape),
            const(1, POOL_WIDTH),
            const(d, d), const(d, d), const(d, d),
            const(*router_w_pad.shape),
            pl.BlockSpec(memory_space=pltpu.SMEM),
        ],
        out_specs=[
            pl.BlockSpec((None, ts, d), lambda b, s: (b, s, 0)),
            pl.BlockSpec((None, ts, d), lambda b, s: (b, s, 0)),
            pl.BlockSpec((None, 8, ts), lambda b, s: (b * tiles_per_seq + s, 0, 0)),
            pl.BlockSpec((BUCKET_ROWS, 128), lambda b, s: (0, 0)),
        ],
        out_shape=[
            jax.ShapeDtypeStruct((batch, seq, d), _f32),
            jax.ShapeDtypeStruct((batch, seq, d), _f32),
            jax.ShapeDtypeStruct((n_tiles, 8, ts), _f32),
            jax.ShapeDtypeStruct((BUCKET_ROWS, 128), _f32),
        ],
        scratch_shapes=[
            pltpu.VMEM((HALO + ts, POOL_WIDTH), _f32),
            pltpu.VMEM((BUCKET_ROWS, 128), _f32),
        ],
        compiler_params=pltpu.CompilerParams(
            dimension_semantics=("arbitrary", "arbitrary"), vmem_limit_bytes=VMEM_LIMIT_BYTES),
        name="mixer_router",
    )(x, *prev_args, mod_l, gmix, gffn, w_in, v_g, sgu_w, sgu_bt, pool_w, pool_scale, w_a, w_b, w_o,
      router_w_pad, router_bias)


def _row_move_kernel(idx_ref, src_ref, *rest, rows, gather):
    dst_ref, sem = rest[-2], rest[-1]
    i = pl.program_id(0)

    def issue(r, c):
        far = idx_ref[i * rows + r]
        src_row, dst_row = (far, r) if gather else (r, far)
        pltpu.make_async_copy(src_ref.at[pl.ds(src_row, 1)], dst_ref.at[pl.ds(dst_row, 1)], sem).start()
        return c

    lax.fori_loop(0, rows, issue, 0, unroll=16)
    if gather:
        pltpu.make_async_copy(src_ref.at[pl.ds(0, rows)], dst_ref, sem).wait()
    else:
        pltpu.make_async_copy(src_ref, dst_ref.at[pl.ds(0, rows)], sem).wait()


def _scatter_rows(pos, hf, n_slots):
    t, d = hf.shape
    init = jnp.zeros((n_slots, d), _f32)
    grid_spec = pltpu.PrefetchScalarGridSpec(
        num_scalar_prefetch=1,
        grid=(t // ROW_TILE,),
        in_specs=[pl.BlockSpec((ROW_TILE, d), lambda i, pos: (i, 0)), pl.BlockSpec(memory_space=pl.ANY)],
        out_specs=pl.BlockSpec(memory_space=pl.ANY),
        scratch_shapes=[pltpu.SemaphoreType.DMA(())],
    )
    return pl.pallas_call(
        functools.partial(_row_move_kernel, rows=ROW_TILE, gather=False),
        grid_spec=grid_spec,
        out_shape=jax.ShapeDtypeStruct((n_slots, d), _f32),
        input_output_aliases={2: 0},
        compiler_params=pltpu.CompilerParams(dimension_semantics=("arbitrary",)),
        name="scatter_rows",
    )(pos, hf, init)


def _gather_rows(pos, ys):
    t = pos.shape[0]
    d = ys.shape[1]
    grid_spec = pltpu.PrefetchScalarGridSpec(
        num_scalar_prefetch=1,
        grid=(t // ROW_TILE,),
        in_specs=[pl.BlockSpec(memory_space=pl.ANY)],
        out_specs=pl.BlockSpec((ROW_TILE, d), lambda i, pos: (i, 0)),
        scratch_shapes=[pltpu.SemaphoreType.DMA(())],
    )
    return pl.pallas_call(
        functools.partial(_row_move_kernel, rows=ROW_TILE, gather=True),
        grid_spec=grid_spec,
        out_shape=jax.ShapeDtypeStruct((t, d), _f32),
        compiler_params=pltpu.CompilerParams(dimension_semantics=("arbitrary",)),
        name="gather_rows",
    )(pos, ys)


def _final_kernel(x_ref, y_ref, g_ref, fg_ref, o_ref):
    out = x_ref[...] + g_ref[...] * y_ref[...]
    o_ref[...] = out * lax.rsqrt(jnp.mean(out * out, axis=-1, keepdims=True) + EPS) * fg_ref[...]


def _final_norm(x, y_tok, gate_f, final_g):
    batch, seq, d = x.shape
    row_spec = pl.BlockSpec((None, ROW_TILE, d), lambda b, s: (b, s, 0))
    return pl.pallas_call(
        _final_kernel,
        grid=(batch, seq // ROW_TILE),
        in_specs=[row_spec, row_spec,
                  pl.BlockSpec((None, 1, d), lambda b, s: (b, 0, 0)),
                  pl.BlockSpec((1, d), lambda b, s: (0, 0))],
        out_specs=row_spec,
        out_shape=jax.ShapeDtypeStruct((batch, seq, d), _f32),
        compiler_params=pltpu.CompilerParams(dimension_semantics=("arbitrary", "arbitrary")),
        name="residual_final_norm",
    )(x, y_tok, gate_f, final_g)


def _experts_kernel(ea_ref, eb_ref, nused_ref, xs_ref, rwa_ref, rwb_ref,
                    wga_ref, wgb_ref, wua_ref, wub_ref, wda_ref, wdb_ref, ys_ref):
    i = pl.program_id(0)

    @pl.when(i < nused_ref[0])
    def _():
        x = xs_ref[...]
        la = jnp.sum(x * rwa_ref[...], axis=-1, keepdims=True)
        lb = jnp.sum(x * rwb_ref[...], axis=-1, keepdims=True)
        wa = jax.nn.sigmoid(la - lb)
        wb = jax.nn.sigmoid(lb - la)
        xb = x.astype(_bf16)
        act_a = (jax.nn.silu(_dot(xb, wga_ref[...])) * _dot(xb, wua_ref[...]) * wa).astype(_bf16)
        act_b = (jax.nn.silu(_dot(xb, wgb_ref[...])) * _dot(xb, wub_ref[...]) * wb).astype(_bf16)
        ys_ref[...] = _dot(act_a, wda_ref[...]) + _dot(act_b, wdb_ref[...])

    @pl.when(i >= nused_ref[0])
    def _():
        ys_ref[...] = jnp.zeros_like(ys_ref)


def _experts(tile_ea, tile_eb, n_used, xs, router_wt, w_gate, w_up, w_down):
    n_slots, d = xs.shape
    n_tiles = n_slots // EXPERT_TILE
    f = D_EXPERT

    def row(i, ea, eb, nu):
        return (jnp.maximum(jnp.minimum(i, nu[0] - 1), 0), 0)

    grid_spec = pltpu.PrefetchScalarGridSpec(
        num_scalar_prefetch=3,
        grid=(n_tiles,),
        in_specs=[
            pl.BlockSpec((EXPERT_TILE, d), row),
            pl.BlockSpec((None, 1, d), lambda i, ea, eb, nu: (ea[i], 0, 0)),
            pl.BlockSpec((None, 1, d), lambda i, ea, eb, nu: (eb[i], 0, 0)),
            pl.BlockSpec((None, d, f), lambda i, ea, eb, nu: (ea[i], 0, 0)),
            pl.BlockSpec((None, d, f), lambda i, ea, eb, nu: (eb[i], 0, 0)),
            pl.BlockSpec((None, d, f), lambda i, ea, eb, nu: (ea[i], 0, 0)),
            pl.BlockSpec((None, d, f), lambda i, ea, eb, nu: (eb[i], 0, 0)),
            pl.BlockSpec((None, f, d), lambda i, ea, eb, nu: (ea[i], 0, 0)),
            pl.BlockSpec((None, f, d), lambda i, ea, eb, nu: (eb[i], 0, 0)),
        ],
        out_specs=pl.BlockSpec((EXPERT_TILE, d), lambda i, ea, eb, nu: (i, 0)),
    )
    return pl.pallas_call(
        _experts_kernel,
        grid_spec=grid_spec,
        out_shape=jax.ShapeDtypeStruct((n_slots, d), _f32),
        compiler_params=pltpu.CompilerParams(
            dimension_semantics=("arbitrary",), vmem_limit_bytes=VMEM_LIMIT_BYTES),
        name="grouped_experts",
    )(tile_ea, tile_eb, n_used, xs, router_wt, router_wt, w_gate, w_gate, w_up, w_up, w_down, w_down)


def _routing_tables(route, counts, n_tiles_max):
    bucket = route[:, 0, :].reshape(-1).astype(jnp.int32)
    rank = route[:, 1, :].reshape(-1).astype(jnp.int32)
    cnt = counts[:N_BUCKETS, 0].astype(jnp.int32)
    tiles_b = (cnt + EXPERT_TILE - 1) // EXPERT_TILE
    tile_end = jnp.cumsum(tiles_b)
    tile_start = tile_end - tiles_b
    n_used = tile_end[-1]
    onehot = bucket[:, None] == jnp.arange(N_BUCKETS, dtype=jnp.int32)[None, :]
    pos = jnp.sum(jnp.where(onehot, (tile_start * EXPERT_TILE)[None, :], 0), axis=1) + rank
    tile_ids = jnp.minimum(jnp.arange(n_tiles_max, dtype=jnp.int32), n_used - 1)
    tile_bucket = jnp.sum((tile_ids[:, None] >= tile_end[None, :]).astype(jnp.int32), axis=1)
    group = tile_bucket // PAIRS_PER_GROUP
    pair = tile_bucket % PAIRS_PER_GROUP
    lo = jnp.asarray(_PAIR_LO, jnp.int32)[pair]
    hi = jnp.asarray(_PAIR_HI, jnp.int32)[pair]
    tile_ea = group * EXPERTS_PER_GROUP + lo
    tile_eb = group * EXPERTS_PER_GROUP + hi
    return pos, tile_ea, tile_eb, n_used.reshape(1)


def kernel(x, c, w_ada, b_ada, norm_mix_g, w_in, v_norm_g, sgu_w, sgu_b, pool_w, pool_scale,
           w_branch_a, w_branch_b, w_out, norm_ffn_g, router_w, router_bias,
           w_exp_gate, w_exp_up, w_exp_down, final_norm_g):
    batch, seq, d = x.shape
    depth = w_ada.shape[0]
    t = batch * seq
    n_tiles_max = t // EXPERT_TILE + N_BUCKETS
    n_slots = n_tiles_max * EXPERT_TILE

    mod = _ada(c, w_ada, b_ada)
    rw_pad = jnp.pad(router_w, ((0, 0), (0, 128 - N_EXPERTS)))
    rw_hi = rw_pad.astype(_bf16)
    rw_lo = (rw_pad - rw_hi.astype(_f32)).astype(_bf16)
    router_w_pad = jnp.concatenate([rw_hi, rw_lo], axis=1)
    router_wt = router_w.T.reshape(N_EXPERTS, 1, d)
    final_g = final_norm_g.reshape(1, d)

    y_tok, gate_prev = None, None
    for l in range(depth):
        mod_l = mod[l].reshape(batch, 1, N_MOD * d)
        x, hf, route, counts = _mixer(
            x, y_tok, gate_prev, mod_l, norm_mix_g[l].reshape(1, d), norm_ffn_g[l].reshape(1, d),
            w_in[l].astype(_bf16), v_norm_g[l].reshape(1, SGU_WIDTH),
            sgu_w[l].astype(_bf16), sgu_b[l].T,
            pool_w[l].astype(_bf16), pool_scale[l].reshape(1, POOL_WIDTH),
            w_branch_a[l].astype(_bf16), w_branch_b[l].astype(_bf16), w_out[l].astype(_bf16),
            router_w_pad, router_bias)
        pos, tile_ea, tile_eb, n_used = _routing_tables(route, counts, n_tiles_max)
        xs = _scatter_rows(pos, hf.reshape(t, d), n_slots)
        ys = _experts(tile_ea, tile_eb, n_used, xs, router_wt,
                      w_exp_gate[l].astype(_bf16), w_exp_up[l].astype(_bf16),
                      w_exp_down[l].astype(_bf16))
        y_tok = _gather_rows(pos, ys).reshape(batch, seq, d)
        gate_prev = mod_l[:, :, 5 * d:6 * d]
    return _final_norm(x, y_tok, gate_prev, final_g)
```

```python
import functools

import jax
import jax.numpy as jnp
from jax import lax
from jax.experimental import pallas as pl
from jax.experimental.pallas import tpu as pltpu
from jax.experimental.pallas import tpu_sc as plsc

D_MODEL = 1024
CHUNK = 128
SGU_HEADS = 8
SGU_HEAD_DIM = 128
SGU_WIDTH = 1024
POOL_WINDOWS = (2, 4, 8, 16)
POOL_GROUP_DIM = 256
POOL_WIDTH = 1024
HALO = 16
N_EXPERTS = 16
N_EXPERT_GROUPS = 4
EXPERTS_PER_GROUP = 4
PAIRS_PER_GROUP = 6
N_BUCKETS = N_EXPERT_GROUPS * PAIRS_PER_GROUP
BUCKET_ROWS = 32
D_EXPERT = 512
N_MOD = 6
EPS = 1e-6

SEQ_TILE = 512
SUB_TILE = 256
ROW_TILE = 512
EXPERT_TILE = 256
VMEM_LIMIT_BYTES = 58 * 1024 * 1024
TOKEN_SLAB = (8, 128)
SC_CORES = 2
SC_SUBCORES = 16
SC_CHUNK = 32

_PAIR_LO = (0, 0, 0, 1, 1, 2)
_PAIR_HI = (1, 2, 3, 2, 3, 3)

_bf16 = jnp.bfloat16
_f32 = jnp.float32


def _dot(a, b):
    return jnp.dot(a, b, preferred_element_type=_f32)


def _rms_modulate(x, g, shift, scale):
    y = x * lax.rsqrt(jnp.mean(x * x, axis=-1, keepdims=True) + EPS)
    return (y * g) * (1.0 + scale) + shift


def _ada_kernel(c_ref, w_ref, b_ref, o_ref):
    c = c_ref[...]
    c_act = (c * jax.nn.sigmoid(c)).astype(_bf16)
    o_ref[...] = _dot(c_act, w_ref[...].astype(_bf16)) + b_ref[...]


def _ada(c, w_ada, b_ada):
    depth, d, n = w_ada.shape
    batch = c.shape[0]
    tn = 2048
    return pl.pallas_call(
        _ada_kernel,
        grid=(depth, n // tn),
        in_specs=[
            pl.BlockSpec((batch, d), lambda l, j: (0, 0)),
            pl.BlockSpec((None, d, tn), lambda l, j: (l, 0, j)),
            pl.BlockSpec((None, 1, tn), lambda l, j: (l, 0, j)),
        ],
        out_specs=pl.BlockSpec((None, batch, tn), lambda l, j: (l, 0, j)),
        out_shape=jax.ShapeDtypeStruct((depth, batch, n), _f32),
        compiler_params=pltpu.CompilerParams(
            dimension_semantics=("arbitrary", "arbitrary"), vmem_limit_bytes=VMEM_LIMIT_BYTES),
        name="ada_modulation",
    )(c, w_ada, b_ada.reshape(depth, 1, n))


def _route(hf, rw_ref, rb_ref, carry_ref):
    ts = hf.shape[0]
    hf_hi = hf.astype(_bf16)
    hf_lo = (hf - hf_hi.astype(_f32)).astype(_bf16)
    both = _dot(hf_hi, rw_ref[...])
    logits = both[:, 0:128] + both[:, 128:256] + _dot(hf_lo, rw_ref[:, 0:128])
    lt = logits.T
    rows = [lt[e:e + 1, :] for e in range(N_EXPERTS)]
    m = functools.reduce(jnp.maximum, rows)
    ex = [jnp.exp(r - m) for r in rows]
    den = functools.reduce(lambda a, b: a + b, ex)
    probs = [e / den for e in ex]
    sel = [probs[e] + rb_ref[e] for e in range(N_EXPERTS)]

    def top2_sum(v):
        pairs = [v[i] + v[j] for i, j in zip(_PAIR_LO, _PAIR_HI)]
        return functools.reduce(jnp.maximum, pairs)

    gscore = [top2_sum(sel[4 * g:4 * g + 4]) for g in range(N_EXPERT_GROUPS)]
    best = gscore[0]
    gidx = jnp.zeros_like(best, dtype=jnp.int32)
    for g in range(1, N_EXPERT_GROUPS):
        better = gscore[g] > best
        best = jnp.where(better, gscore[g], best)
        gidx = jnp.where(better, g, gidx)
    ing = []
    for k in range(EXPERTS_PER_GROUP):
        v = sel[k]
        for g in range(1, N_EXPERT_GROUPS):
            v = jnp.where(gidx == g, sel[4 * g + k], v)
        ing.append(v)
    chosen = []
    for k in range(EXPERTS_PER_GROUP):
        r = jnp.zeros_like(gidx)
        for j in range(EXPERTS_PER_GROUP):
            if j == k:
                continue
            beats = (ing[j] >= ing[k]) if j < k else (ing[j] > ing[k])
            r = r + beats.astype(jnp.int32)
        chosen.append(r < 2)
    lo = jnp.where(chosen[0], 0, jnp.where(chosen[1], 1, 2))
    hi = jnp.where(chosen[3], 3, jnp.where(chosen[2], 2, 1))
    base = jnp.where(lo == 0, 0, jnp.where(lo == 1, 3, 5))
    bucket = gidx * PAIRS_PER_GROUP + base + hi - lo - 1

    brow = lax.broadcasted_iota(jnp.int32, (BUCKET_ROWS, ts), 0)
    onehot = (brow == bucket).astype(_f32)
    jj = lax.broadcasted_iota(jnp.int32, (ts, ts), 0)
    tt = lax.broadcasted_iota(jnp.int32, (ts, ts), 1)
    upper = (jj <= tt).astype(_bf16)
    cum = _dot(onehot.astype(_bf16), upper)
    carry = carry_ref[...][:, 0:1]
    rank = jnp.sum(onehot * (cum - 1.0 + carry), axis=0, keepdims=True)
    carry_ref[...] = carry_ref[...] + jnp.sum(onehot, axis=1, keepdims=True)
    return bucket.astype(_f32), rank


def _mixer_kernel(*refs, has_prev):
    if has_prev:
        x_ref, y_ref, gprev_ref = refs[:3]
        refs = refs[3:]
        x_in = lambda rows: x_ref[rows, :] + gprev_ref[...] * y_ref[rows].reshape(SUB_TILE, D_MODEL)
    else:
        x_ref = refs[0]
        refs = refs[1:]
        x_in = lambda rows: x_ref[rows, :]
    (mod_ref, gmix_ref, gffn_ref, win_ref, vg_ref, sw_ref, sb_ref, pw_ref, ps_ref, wa_ref, wb_ref,
     wo_ref, rw_ref, rb_ref, xo_ref, hf_ref, route_ref, counts_ref, ext_ref, carry_ref) = refs
    b = pl.program_id(0)
    s = pl.program_id(1)
    ts = x_ref.shape[0]
    d = D_MODEL

    @pl.when((b == 0) & (s == 0))
    def _():
        carry_ref[...] = jnp.zeros_like(carry_ref)

    @pl.when(s == 0)
    def _():
        ext_ref[0:HALO, :] = jnp.zeros((HALO, POOL_WIDTH), _f32)

    mod = mod_ref[...]
    ci = lax.broadcasted_iota(jnp.int32, (CHUNK, CHUNK), 0)
    cj = lax.broadcasted_iota(jnp.int32, (CHUNK, CHUNK), 1)
    sgu_w = [jnp.where(ci >= cj, sw_ref[h], jnp.zeros((), _bf16)) for h in range(SGU_HEADS)]
    for r in range(ts // SUB_TILE):
        _mixer_rows(r * SUB_TILE, s * ts + r * SUB_TILE, mod, sgu_w,
                    x_in, gmix_ref, gffn_ref, win_ref, vg_ref, sb_ref, pw_ref, ps_ref, wa_ref,
                    wb_ref, wo_ref, rw_ref, rb_ref, xo_ref, hf_ref, route_ref, ext_ref, carry_ref)
    ext_ref[0:HALO, :] = ext_ref[ts:ts + HALO, :]
    route_ref[2:8, :] = jnp.zeros((6, ts), _f32)
    counts_ref[...] = carry_ref[...]


def _mixer_rows(row0, seq_pos0, mod, sgu_w, x_in, gmix_ref, gffn_ref, win_ref, vg_ref, sb_ref,
                pw_ref, ps_ref, wa_ref, wb_ref, wo_ref, rw_ref, rb_ref, xo_ref, hf_ref, route_ref,
                ext_ref, carry_ref):
    d = D_MODEL
    ts = SUB_TILE
    rows = slice(row0, row0 + ts)
    sh_m, sc_m, g_m = mod[:, 0:d], mod[:, d:2 * d], mod[:, 2 * d:3 * d]
    sh_f, sc_f = mod[:, 3 * d:4 * d], mod[:, 4 * d:5 * d]
    x = x_in(rows)

    hb = _rms_modulate(x, gmix_ref[...], sh_m, sc_m).astype(_bf16)

    v = jax.nn.gelu(_dot(hb, win_ref[:, SGU_WIDTH:2 * SGU_WIDTH]))
    vc = v - jnp.mean(v, axis=-1, keepdims=True)
    vn = (vc * lax.rsqrt(jnp.mean(vc * vc, axis=-1, keepdims=True) + EPS) * vg_ref[...]).astype(_bf16)
    u = jax.nn.gelu(_dot(hb, win_ref[:, 0:SGU_WIDTH]))
    n_chunks = ts // CHUNK
    ya_cols = []
    for h in range(SGU_HEADS):
        cols = slice(h * SGU_HEAD_DIM, (h + 1) * SGU_HEAD_DIM)
        rhs = jnp.concatenate([vn[n * CHUNK:(n + 1) * CHUNK, cols] for n in range(n_chunks)], axis=1)
        sg = _dot(sgu_w[h], rhs) + sb_ref[:, h:h + 1]
        s_h = jnp.concatenate([sg[:, n * SGU_HEAD_DIM:(n + 1) * SGU_HEAD_DIM] for n in range(n_chunks)],
                              axis=0)
        ya_cols.append((u[:, cols] * s_h).astype(_bf16))
    ya = jnp.concatenate(ya_cols, axis=1)
    gate_a = jax.nn.sigmoid(_dot(hb, win_ref[:, 3 * d:4 * d]))
    merged = gate_a * _dot(ya, wa_ref[...])

    p = _dot(hb, win_ref[:, 2 * SGU_WIDTH:2 * SGU_WIDTH + POOL_WIDTH])
    e0 = HALO + row0
    ext_ref[e0:e0 + ts, :] = p
    pos1 = (seq_pos0 + 1 + lax.broadcasted_iota(jnp.int32, (ts, 1), 0)).astype(_f32)
    yb_cols = []
    for gi, w in enumerate(POOL_WINDOWS):
        cols = slice(gi * POOL_GROUP_DIM, (gi + 1) * POOL_GROUP_DIM)
        acc = p[:, cols]
        for k in range(1, w):
            acc = acc + ext_ref[e0 - k:e0 - k + ts, cols]
        count = jnp.minimum(pos1, float(w))
        pooled = (acc / count - p[:, cols]).astype(_bf16)
        yb_cols.append(_dot(pooled, pw_ref[gi]))
    yb = (jnp.concatenate(yb_cols, axis=1) * ps_ref[...]).astype(_bf16)
    gate_b = jax.nn.sigmoid(_dot(hb, win_ref[:, 4 * d:5 * d]))
    merged = merged + gate_b * _dot(yb, wb_ref[...])

    x_new = x + g_m * _dot(merged.astype(_bf16), wo_ref[...])
    xo_ref[rows, :] = x_new

    hf = _rms_modulate(x_new, gffn_ref[...], sh_f, sc_f)
    hf_ref[rows, :] = hf
    bucket, rank = _route(hf, rw_ref, rb_ref, carry_ref)
    route_ref[0:1, rows] = bucket
    route_ref[1:2, rows] = rank


def _mixer(x, y_prev, gate_prev, mod_l, gmix, gffn, w_in, v_g, sgu_w, sgu_bt, pool_w, pool_scale,
           w_a, w_b, w_o, router_w_pad, router_bias):
    batch, seq, d = x.shape
    has_prev = y_prev is not None
    ts = SEQ_TILE
    n_tiles = batch * (seq // ts)
    tiles_per_seq = seq // ts
    const = lambda *shape: pl.BlockSpec(shape, lambda b, s: (0,) * len(shape),
                                        pipeline_mode=pl.Buffered(1))
    row_spec = pl.BlockSpec((None, ts, d), lambda b, s: (b, s, 0))
    slab_spec = pl.BlockSpec((None, ts) + TOKEN_SLAB, lambda b, s: (b, s, 0, 0))
    prev_specs = [slab_spec, pl.BlockSpec((None, 1, d), lambda b, s: (b, 0, 0))] if has_prev else []
    prev_args = (y_prev, gate_prev) if has_prev else ()
    return pl.pallas_call(
        functools.partial(_mixer_kernel, has_prev=has_prev),
        grid=(batch, tiles_per_seq),
        in_specs=[row_spec] + prev_specs + [
            pl.BlockSpec((None, 1, N_MOD * d), lambda b, s: (b, 0, 0)),
            const(1, d), const(1, d),
            const(*w_in.shape),
            const(1, SGU_WIDTH),
            const(*sgu_w.shape),
            const(*sgu_bt.shape),
            const(*pool_w.shape),
            const(1, POOL_WIDTH),
            const(d, d), const(d, d), const(d, d),
            const(*router_w_pad.shape),
            pl.BlockSpec(memory_space=pltpu.SMEM),
        ],
        out_specs=[
            pl.BlockSpec((None, ts, d), lambda b, s: (b, s, 0)),
            pl.BlockSpec((None, ts, d), lambda b, s: (b, s, 0)),
            pl.BlockSpec((None, 8, ts), lambda b, s: (b * tiles_per_seq + s, 0, 0)),
            pl.BlockSpec((BUCKET_ROWS, 128), lambda b, s: (0, 0)),
        ],
        out_shape=[
            jax.ShapeDtypeStruct((batch, seq, d), _f32),
            jax.ShapeDtypeStruct((batch, seq, d), _f32),
            jax.ShapeDtypeStruct((n_tiles, 8, ts), _f32),
            jax.ShapeDtypeStruct((BUCKET_ROWS, 128), _f32),
        ],
        scratch_shapes=[
            pltpu.VMEM((HALO + ts, POOL_WIDTH), _f32),
            pltpu.VMEM((BUCKET_ROWS, 128), _f32),
        ],
        compiler_params=pltpu.CompilerParams(
            dimension_semantics=("arbitrary", "arbitrary"), vmem_limit_bytes=VMEM_LIMIT_BYTES),
        name="mixer_router",
    )(x, *prev_args, mod_l, gmix, gffn, w_in, v_g, sgu_w, sgu_bt, pool_w, pool_scale, w_a, w_b, w_o,
      router_w_pad, router_bias)


def _row_move_kernel(idx_ref, src_ref, *rest, rows, gather):
    dst_ref, sem = rest[-2], rest[-1]
    i = pl.program_id(0)

    def issue(r, c):
        far = idx_ref[i * rows + r]
        src_row, dst_row = (far, r) if gather else (r, far)
        pltpu.make_async_copy(src_ref.at[pl.ds(src_row, 1)], dst_ref.at[pl.ds(dst_row, 1)], sem).start()
        return c

    lax.fori_loop(0, rows, issue, 0, unroll=16)
    if gather:
        pltpu.make_async_copy(src_ref.at[pl.ds(0, rows)], dst_ref, sem).wait()
    else:
        pltpu.make_async_copy(src_ref, dst_ref.at[pl.ds(0, rows)], sem).wait()


def _scatter_rows(pos, hf, n_slots):
    t, d = hf.shape
    init = jnp.zeros((n_slots, d), _f32)
    grid_spec = pltpu.PrefetchScalarGridSpec(
        num_scalar_prefetch=1,
        grid=(t // ROW_TILE,),
        in_specs=[pl.BlockSpec((ROW_TILE, d), lambda i, pos: (i, 0)), pl.BlockSpec(memory_space=pl.ANY)],
        out_specs=pl.BlockSpec(memory_space=pl.ANY),
        scratch_shapes=[pltpu.SemaphoreType.DMA(())],
    )
    return pl.pallas_call(
        functools.partial(_row_move_kernel, rows=ROW_TILE, gather=False),
        grid_spec=grid_spec,
        out_shape=jax.ShapeDtypeStruct((n_slots, d), _f32),
        input_output_aliases={2: 0},
        compiler_params=pltpu.CompilerParams(dimension_semantics=("arbitrary",)),
        name="scatter_rows",
    )(pos, hf, init)


def _sc_gather_rows(idx, table):
    n_out = idx.shape[0]
    n_chunks = n_out // (SC_CORES * SC_SUBCORES * SC_CHUNK)
    assert n_chunks % 2 == 0 and n_chunks * SC_CORES * SC_SUBCORES * SC_CHUNK == n_out
    mesh = plsc.VectorSubcoreMesh(core_axis_name="c", subcore_axis_name="s")
    buf = pltpu.VMEM((SC_CHUNK,) + TOKEN_SLAB, _f32)

    @functools.partial(
        pl.kernel, mesh=mesh,
        out_type=jax.ShapeDtypeStruct((n_out,) + TOKEN_SLAB, _f32),
        scratch_types=[pltpu.VMEM((n_chunks, SC_CHUNK), jnp.int32), buf, buf,
                       pltpu.SemaphoreType.DMA, pltpu.SemaphoreType.DMA],
    )
    def gather_kernel(table_hbm, idx_hbm, out_hbm, idx_v, buf0, buf1, sem0, sem1):
        worker = lax.axis_index("s") * SC_CORES + lax.axis_index("c")
        chunk0 = worker * n_chunks
        pltpu.sync_copy(idx_hbm.at[pl.ds(chunk0, n_chunks)], idx_v)

        def gather(j, dst, sem):
            return pltpu.make_async_copy(table_hbm.at[idx_v.at[j]], dst, sem)

        def write_out(j, src):
            pltpu.sync_copy(src, out_hbm.at[pl.ds((chunk0 + j) * SC_CHUNK, SC_CHUNK)])

        gather(0, buf0, sem0).start()

        @pl.loop(0, n_chunks, step=2)
        def _(j):
            gather(j + 1, buf1, sem1).start()
            gather(j, buf0, sem0).wait()
            write_out(j, buf0)

            @pl.when(j + 2 < n_chunks)
            def _():
                gather(j + 2, buf0, sem0).start()

            gather(j + 1, buf1, sem1).wait()
            write_out(j + 1, buf1)

    return gather_kernel(table, idx.reshape(n_out // SC_CHUNK, SC_CHUNK))


def _final_kernel(x_ref, y_ref, g_ref, fg_ref, o_ref):
    out = x_ref[...] + g_ref[...] * y_ref[...].reshape(x_ref.shape)
    o_ref[...] = out * lax.rsqrt(jnp.mean(out * out, axis=-1, keepdims=True) + EPS) * fg_ref[...]


def _final_norm(x, y_tok, gate_f, final_g):
    batch, seq, d = x.shape
    row_spec = pl.BlockSpec((None, ROW_TILE, d), lambda b, s: (b, s, 0))
    return pl.pallas_call(
        _final_kernel,
        grid=(batch, seq // ROW_TILE),
        in_specs=[row_spec, pl.BlockSpec((None, ROW_TILE) + TOKEN_SLAB, lambda b, s: (b, s, 0, 0)),
                  pl.BlockSpec((None, 1, d), lambda b, s: (b, 0, 0)),
                  pl.BlockSpec((1, d), lambda b, s: (0, 0))],
        out_specs=row_spec,
        out_shape=jax.ShapeDtypeStruct((batch, seq, d), _f32),
        compiler_params=pltpu.CompilerParams(dimension_semantics=("arbitrary", "arbitrary")),
        name="residual_final_norm",
    )(x, y_tok, gate_f, final_g)


def _experts_kernel(ea_ref, eb_ref, nused_ref, xs_ref, rwa_ref, rwb_ref,
                    wga_ref, wgb_ref, wua_ref, wub_ref, wda_ref, wdb_ref, ys_ref):
    i = pl.program_id(0)

    @pl.when(i < nused_ref[0])
    def _():
        x = xs_ref[...]
        la = jnp.sum(x * rwa_ref[...], axis=-1, keepdims=True)
        lb = jnp.sum(x * rwb_ref[...], axis=-1, keepdims=True)
        wa = jax.nn.sigmoid(la - lb)
        wb = jax.nn.sigmoid(lb - la)
        xb = x.astype(_bf16)
        act_a = (jax.nn.silu(_dot(xb, wga_ref[...])) * _dot(xb, wua_ref[...]) * wa).astype(_bf16)
        act_b = (jax.nn.silu(_dot(xb, wgb_ref[...])) * _dot(xb, wub_ref[...]) * wb).astype(_bf16)
        y = _dot(act_a, wda_ref[...]) + _dot(act_b, wdb_ref[...])
        ys_ref[...] = y.reshape(ys_ref.shape)

    @pl.when(i >= nused_ref[0])
    def _():
        ys_ref[...] = jnp.zeros_like(ys_ref)


def _experts(tile_ea, tile_eb, n_used, xs, router_wt, w_gate, w_up, w_down):
    n_slots, d = xs.shape
    n_tiles = n_slots // EXPERT_TILE
    f = D_EXPERT

    def row(i, ea, eb, nu):
        return (jnp.maximum(jnp.minimum(i, nu[0] - 1), 0), 0)

    grid_spec = pltpu.PrefetchScalarGridSpec(
        num_scalar_prefetch=3,
        grid=(n_tiles,),
        in_specs=[
            pl.BlockSpec((EXPERT_TILE, d), row),
            pl.BlockSpec((None, 1, d), lambda i, ea, eb, nu: (ea[i], 0, 0)),
            pl.BlockSpec((None, 1, d), lambda i, ea, eb, nu: (eb[i], 0, 0)),
            pl.BlockSpec((None, d, f), lambda i, ea, eb, nu: (ea[i], 0, 0)),
            pl.BlockSpec((None, d, f), lambda i, ea, eb, nu: (eb[i], 0, 0)),
            pl.BlockSpec((None, d, f), lambda i, ea, eb, nu: (ea[i], 0, 0)),
            pl.BlockSpec((None, d, f), lambda i, ea, eb, nu: (eb[i], 0, 0)),
            pl.BlockSpec((None, f, d), lambda i, ea, eb, nu: (ea[i], 0, 0)),
            pl.BlockSpec((None, f, d), lambda i, ea, eb, nu: (eb[i], 0, 0)),
        ],
        out_specs=pl.BlockSpec((EXPERT_TILE,) + TOKEN_SLAB, lambda i, ea, eb, nu: (i, 0, 0)),
    )
    return pl.pallas_call(
        _experts_kernel,
        grid_spec=grid_spec,
        out_shape=jax.ShapeDtypeStruct((n_slots,) + TOKEN_SLAB, _f32),
        compiler_params=pltpu.CompilerParams(
            dimension_semantics=("arbitrary",), vmem_limit_bytes=VMEM_LIMIT_BYTES),
        name="grouped_experts",
    )(tile_ea, tile_eb, n_used, xs, router_wt, router_wt, w_gate, w_gate, w_up, w_up, w_down, w_down)


def _routing_tables(route, counts, n_tiles_max):
    bucket = route[:, 0, :].reshape(-1).astype(jnp.int32)
    rank = route[:, 1, :].reshape(-1).astype(jnp.int32)
    cnt = counts[:N_BUCKETS, 0].astype(jnp.int32)
    tiles_b = (cnt + EXPERT_TILE - 1) // EXPERT_TILE
    tile_end = jnp.cumsum(tiles_b)
    tile_start = tile_end - tiles_b
    n_used = tile_end[-1]
    onehot = bucket[:, None] == jnp.arange(N_BUCKETS, dtype=jnp.int32)[None, :]
    pos = jnp.sum(jnp.where(onehot, (tile_start * EXPERT_TILE)[None, :], 0), axis=1) + rank
    tile_ids = jnp.minimum(jnp.arange(n_tiles_max, dtype=jnp.int32), n_used - 1)
    tile_bucket = jnp.sum((tile_ids[:, None] >= tile_end[None, :]).astype(jnp.int32), axis=1)
    group = tile_bucket // PAIRS_PER_GROUP
    pair = tile_bucket % PAIRS_PER_GROUP
    lo = jnp.asarray(_PAIR_LO, jnp.int32)[pair]
    hi = jnp.asarray(_PAIR_HI, jnp.int32)[pair]
    tile_ea = group * EXPERTS_PER_GROUP + lo
    tile_eb = group * EXPERTS_PER_GROUP + hi
    return pos, tile_ea, tile_eb, n_used.reshape(1)


def kernel(x, c, w_ada, b_ada, norm_mix_g, w_in, v_norm_g, sgu_w, sgu_b, pool_w, pool_scale,
           w_branch_a, w_branch_b, w_out, norm_ffn_g, router_w, router_bias,
           w_exp_gate, w_exp_up, w_exp_down, final_norm_g):
    batch, seq, d = x.shape
    depth = w_ada.shape[0]
    t = batch * seq
    n_tiles_max = t // EXPERT_TILE + N_BUCKETS
    n_slots = n_tiles_max * EXPERT_TILE

    mod = _ada(c, w_ada, b_ada)
    rw_pad = jnp.pad(router_w, ((0, 0), (0, 128 - N_EXPERTS)))
    rw_hi = rw_pad.astype(_bf16)
    rw_lo = (rw_pad - rw_hi.astype(_f32)).astype(_bf16)
    router_w_pad = jnp.concatenate([rw_hi, rw_lo], axis=1)
    router_wt = router_w.T.reshape(N_EXPERTS, 1, d)
    final_g = final_norm_g.reshape(1, d)

    y_tok, gate_prev = None, None
    for l in range(depth):
        mod_l = mod[l].reshape(batch, 1, N_MOD * d)
        x, hf, route, counts = _mixer(
            x, y_tok, gate_prev, mod_l, norm_mix_g[l].reshape(1, d), norm_ffn_g[l].reshape(1, d),
            w_in[l].astype(_bf16), v_norm_g[l].reshape(1, SGU_WIDTH),
            sgu_w[l].astype(_bf16), sgu_b[l].T,
            pool_w[l].astype(_bf16), pool_scale[l].reshape(1, POOL_WIDTH),
            w_branch_a[l].astype(_bf16), w_branch_b[l].astype(_bf16), w_out[l].astype(_bf16),
            router_w_pad, router_bias)
        pos, tile_ea, tile_eb, n_used = _routing_tables(route, counts, n_tiles_max)
        xs = _scatter_rows(pos, hf.reshape(t, d), n_slots)
        ys = _experts(tile_ea, tile_eb, n_used, xs, router_wt,
                      w_exp_gate[l].astype(_bf16), w_exp_up[l].astype(_bf16),
                      w_exp_down[l].astype(_bf16))
        y_tok = _sc_gather_rows(pos, ys).reshape((batch, seq) + TOKEN_SLAB)
        gate_prev = mod_l[:, :, 5 * d:6 * d]
    return _final_norm(x, y_tok, gate_prev, final_g)
```

```python
import functools

import jax
import jax.numpy as jnp
from jax import lax
from jax.experimental import pallas as pl
from jax.experimental.pallas import tpu as pltpu
from jax.experimental.pallas import tpu_sc as plsc

D_MODEL = 1024
CHUNK = 128
SGU_HEADS = 8
SGU_HEAD_DIM = 128
SGU_WIDTH = 1024
POOL_WINDOWS = (2, 4, 8, 16)
POOL_GROUP_DIM = 256
POOL_WIDTH = 1024
HALO = 16
N_EXPERTS = 16
N_EXPERT_GROUPS = 4
EXPERTS_PER_GROUP = 4
PAIRS_PER_GROUP = 6
N_BUCKETS = N_EXPERT_GROUPS * PAIRS_PER_GROUP
BUCKET_ROWS = 32
D_EXPERT = 512
N_MOD = 6
EPS = 1e-6

SEQ_TILE = 512
SUB_TILE = 256
ROW_TILE = 512
EXPERT_TILE = 256
VMEM_LIMIT_BYTES = 58 * 1024 * 1024
TOKEN_SLAB = (8, 128)
SC_CORES = 2
SC_SUBCORES = 16
SC_CHUNK = 32
SC_LANES = 16

_PAIR_LO = (0, 0, 0, 1, 1, 2)
_PAIR_HI = (1, 2, 3, 2, 3, 3)

_bf16 = jnp.bfloat16
_f32 = jnp.float32


def _dot(a, b):
    return jnp.dot(a, b, preferred_element_type=_f32)


def _rms_modulate(x, g, shift, scale):
    y = x * lax.rsqrt(jnp.mean(x * x, axis=-1, keepdims=True) + EPS)
    return (y * g) * (1.0 + scale) + shift


def _ada_kernel(c_ref, w_ref, b_ref, o_ref):
    c = c_ref[...]
    c_act = (c * jax.nn.sigmoid(c)).astype(_bf16)
    o_ref[...] = _dot(c_act, w_ref[...].astype(_bf16)) + b_ref[...]


def _ada(c, w_ada, b_ada):
    depth, d, n = w_ada.shape
    batch = c.shape[0]
    tn = 2048
    return pl.pallas_call(
        _ada_kernel,
        grid=(depth, n // tn),
        in_specs=[
            pl.BlockSpec((batch, d), lambda l, j: (0, 0)),
            pl.BlockSpec((None, d, tn), lambda l, j: (l, 0, j)),
            pl.BlockSpec((None, 1, tn), lambda l, j: (l, 0, j)),
        ],
        out_specs=pl.BlockSpec((None, batch, tn), lambda l, j: (l, 0, j)),
        out_shape=jax.ShapeDtypeStruct((depth, batch, n), _f32),
        compiler_params=pltpu.CompilerParams(
            dimension_semantics=("arbitrary", "arbitrary"), vmem_limit_bytes=VMEM_LIMIT_BYTES),
        name="ada_modulation",
    )(c, w_ada, b_ada.reshape(depth, 1, n))


def _route(hf, rw_ref, rb_ref, carry_ref):
    ts = hf.shape[0]
    hf_hi = hf.astype(_bf16)
    hf_lo = (hf - hf_hi.astype(_f32)).astype(_bf16)
    both = _dot(hf_hi, rw_ref[...])
    logits = both[:, 0:128] + both[:, 128:256] + _dot(hf_lo, rw_ref[:, 0:128])
    lt = logits.T
    rows = [lt[e:e + 1, :] for e in range(N_EXPERTS)]
    m = functools.reduce(jnp.maximum, rows)
    ex = [jnp.exp(r - m) for r in rows]
    den = functools.reduce(lambda a, b: a + b, ex)
    probs = [e / den for e in ex]
    sel = [probs[e] + rb_ref[e] for e in range(N_EXPERTS)]

    def top2_sum(v):
        pairs = [v[i] + v[j] for i, j in zip(_PAIR_LO, _PAIR_HI)]
        return functools.reduce(jnp.maximum, pairs)

    gscore = [top2_sum(sel[4 * g:4 * g + 4]) for g in range(N_EXPERT_GROUPS)]
    best = gscore[0]
    gidx = jnp.zeros_like(best, dtype=jnp.int32)
    for g in range(1, N_EXPERT_GROUPS):
        better = gscore[g] > best
        best = jnp.where(better, gscore[g], best)
        gidx = jnp.where(better, g, gidx)
    ing = []
    for k in range(EXPERTS_PER_GROUP):
        v = sel[k]
        for g in range(1, N_EXPERT_GROUPS):
            v = jnp.where(gidx == g, sel[4 * g + k], v)
        ing.append(v)
    chosen = []
    for k in range(EXPERTS_PER_GROUP):
        r = jnp.zeros_like(gidx)
        for j in range(EXPERTS_PER_GROUP):
            if j == k:
                continue
            beats = (ing[j] >= ing[k]) if j < k else (ing[j] > ing[k])
            r = r + beats.astype(jnp.int32)
        chosen.append(r < 2)
    lo = jnp.where(chosen[0], 0, jnp.where(chosen[1], 1, 2))
    hi = jnp.where(chosen[3], 3, jnp.where(chosen[2], 2, 1))
    base = jnp.where(lo == 0, 0, jnp.where(lo == 1, 3, 5))
    bucket = gidx * PAIRS_PER_GROUP + base + hi - lo - 1

    brow = lax.broadcasted_iota(jnp.int32, (BUCKET_ROWS, ts), 0)
    onehot = (brow == bucket).astype(_f32)
    jj = lax.broadcasted_iota(jnp.int32, (ts, ts), 0)
    tt = lax.broadcasted_iota(jnp.int32, (ts, ts), 1)
    upper = (jj <= tt).astype(_bf16)
    cum = _dot(onehot.astype(_bf16), upper)
    carry = carry_ref[...][:, 0:1]
    rank = jnp.sum(onehot * (cum - 1.0 + carry), axis=0, keepdims=True)
    carry_ref[...] = carry_ref[...] + jnp.sum(onehot, axis=1, keepdims=True)
    return bucket.astype(_f32), rank


def _mixer_kernel(*refs, has_prev):
    if has_prev:
        x_ref, y_ref, gprev_ref = refs[:3]
        refs = refs[3:]
        x_in = lambda rows: x_ref[rows, :] + gprev_ref[...] * y_ref[rows].reshape(SUB_TILE, D_MODEL)
    else:
        x_ref = refs[0]
        refs = refs[1:]
        x_in = lambda rows: x_ref[rows, :]
    (mod_ref, gmix_ref, gffn_ref, win_ref, vg_ref, sw_ref, sb_ref, pw_ref, ps_ref, wa_ref, wb_ref,
     wo_ref, rw_ref, rb_ref, xo_ref, hf_ref, route_ref, counts_ref, ext_ref, carry_ref) = refs
    b = pl.program_id(0)
    s = pl.program_id(1)
    ts = x_ref.shape[0]
    d = D_MODEL

    @pl.when((b == 0) & (s == 0))
    def _():
        carry_ref[...] = jnp.zeros_like(carry_ref)

    @pl.when(s == 0)
    def _():
        ext_ref[0:HALO, :] = jnp.zeros((HALO, POOL_WIDTH), _f32)

    mod = mod_ref[...]
    ci = lax.broadcasted_iota(jnp.int32, (CHUNK, CHUNK), 0)
    cj = lax.broadcasted_iota(jnp.int32, (CHUNK, CHUNK), 1)
    sgu_w = [jnp.where(ci >= cj, sw_ref[h], jnp.zeros((), _bf16)) for h in range(SGU_HEADS)]
    for r in range(ts // SUB_TILE):
        _mixer_rows(r * SUB_TILE, s * ts + r * SUB_TILE, mod, sgu_w,
                    x_in, gmix_ref, gffn_ref, win_ref, vg_ref, sb_ref, pw_ref, ps_ref, wa_ref,
                    wb_ref, wo_ref, rw_ref, rb_ref, xo_ref, hf_ref, route_ref, ext_ref, carry_ref)
    ext_ref[0:HALO, :] = ext_ref[ts:ts + HALO, :]
    route_ref[2:8, :] = jnp.zeros((6, ts), _f32)
    counts_ref[...] = carry_ref[...]


def _mixer_rows(row0, seq_pos0, mod, sgu_w, x_in, gmix_ref, gffn_ref, win_ref, vg_ref, sb_ref,
                pw_ref, ps_ref, wa_ref, wb_ref, wo_ref, rw_ref, rb_ref, xo_ref, hf_ref, route_ref,
                ext_ref, carry_ref):
    d = D_MODEL
    ts = SUB_TILE
    rows = slice(row0, row0 + ts)
    sh_m, sc_m, g_m = mod[:, 0:d], mod[:, d:2 * d], mod[:, 2 * d:3 * d]
    sh_f, sc_f = mod[:, 3 * d:4 * d], mod[:, 4 * d:5 * d]
    x = x_in(rows)

    hb = _rms_modulate(x, gmix_ref[...], sh_m, sc_m).astype(_bf16)

    v = jax.nn.gelu(_dot(hb, win_ref[:, SGU_WIDTH:2 * SGU_WIDTH]))
    vc = v - jnp.mean(v, axis=-1, keepdims=True)
    vn = (vc * lax.rsqrt(jnp.mean(vc * vc, axis=-1, keepdims=True) + EPS) * vg_ref[...]).astype(_bf16)
    u = jax.nn.gelu(_dot(hb, win_ref[:, 0:SGU_WIDTH]))
    n_chunks = ts // CHUNK
    ya_cols = []
    for h in range(SGU_HEADS):
        cols = slice(h * SGU_HEAD_DIM, (h + 1) * SGU_HEAD_DIM)
        rhs = jnp.concatenate([vn[n * CHUNK:(n + 1) * CHUNK, cols] for n in range(n_chunks)], axis=1)
        sg = _dot(sgu_w[h], rhs) + sb_ref[:, h:h + 1]
        s_h = jnp.concatenate([sg[:, n * SGU_HEAD_DIM:(n + 1) * SGU_HEAD_DIM] for n in range(n_chunks)],
                              axis=0)
        ya_cols.append((u[:, cols] * s_h).astype(_bf16))
    ya = jnp.concatenate(ya_cols, axis=1)
    gate_a = jax.nn.sigmoid(_dot(hb, win_ref[:, 3 * d:4 * d]))
    merged = gate_a * _dot(ya, wa_ref[...])

    p = _dot(hb, win_ref[:, 2 * SGU_WIDTH:2 * SGU_WIDTH + POOL_WIDTH])
    e0 = HALO + row0
    ext_ref[e0:e0 + ts, :] = p
    pos1 = (seq_pos0 + 1 + lax.broadcasted_iota(jnp.int32, (ts, 1), 0)).astype(_f32)
    yb_cols = []
    for gi, w in enumerate(POOL_WINDOWS):
        cols = slice(gi * POOL_GROUP_DIM, (gi + 1) * POOL_GROUP_DIM)
        acc = p[:, cols]
        for k in range(1, w):
            acc = acc + ext_ref[e0 - k:e0 - k + ts, cols]
        count = jnp.minimum(pos1, float(w))
        pooled = (acc / count - p[:, cols]).astype(_bf16)
        yb_cols.append(_dot(pooled, pw_ref[gi]))
    yb = (jnp.concatenate(yb_cols, axis=1) * ps_ref[...]).astype(_bf16)
    gate_b = jax.nn.sigmoid(_dot(hb, win_ref[:, 4 * d:5 * d]))
    merged = merged + gate_b * _dot(yb, wb_ref[...])

    x_new = x + g_m * _dot(merged.astype(_bf16), wo_ref[...])
    xo_ref[rows, :] = x_new

    hf = _rms_modulate(x_new, gffn_ref[...], sh_f, sc_f)
    hf_ref[rows] = hf.reshape((ts,) + TOKEN_SLAB)
    bucket, rank = _route(hf, rw_ref, rb_ref, carry_ref)
    route_ref[0:1, rows] = bucket
    route_ref[1:2, rows] = rank


def _mixer(x, y_prev, gate_prev, mod_l, gmix, gffn, w_in, v_g, sgu_w, sgu_bt, pool_w, pool_scale,
           w_a, w_b, w_o, router_w_pad, router_bias):
    batch, seq, d = x.shape
    has_prev = y_prev is not None
    ts = SEQ_TILE
    n_tiles = batch * (seq // ts)
    tiles_per_seq = seq // ts
    const = lambda *shape: pl.BlockSpec(shape, lambda b, s: (0,) * len(shape),
                                        pipeline_mode=pl.Buffered(1))
    row_spec = pl.BlockSpec((None, ts, d), lambda b, s: (b, s, 0))
    slab_spec = pl.BlockSpec((None, ts) + TOKEN_SLAB, lambda b, s: (b, s, 0, 0))
    prev_specs = [slab_spec, pl.BlockSpec((None, 1, d), lambda b, s: (b, 0, 0))] if has_prev else []
    prev_args = (y_prev, gate_prev) if has_prev else ()
    return pl.pallas_call(
        functools.partial(_mixer_kernel, has_prev=has_prev),
        grid=(batch, tiles_per_seq),
        in_specs=[row_spec] + prev_specs + [
            pl.BlockSpec((None, 1, N_MOD * d), lambda b, s: (b, 0, 0)),
            const(1, d), const(1, d),
            const(*w_in.shape),
            const(1, SGU_WIDTH),
            const(*sgu_w.shape),
            const(*sgu_bt.shape),
            const(*pool_w.shape),
            const(1, POOL_WIDTH),
            const(d, d), const(d, d), const(d, d),
            const(*router_w_pad.shape),
            pl.BlockSpec(memory_space=pltpu.SMEM),
        ],
        out_specs=[
            pl.BlockSpec((None, ts, d), lambda b, s: (b, s, 0)),
            pl.BlockSpec((None, ts) + TOKEN_SLAB, lambda b, s: (b, s, 0, 0)),
            pl.BlockSpec((None, 8, ts), lambda b, s: (b * tiles_per_seq + s, 0, 0)),
            pl.BlockSpec((BUCKET_ROWS, 128), lambda b, s: (0, 0)),
        ],
        out_shape=[
            jax.ShapeDtypeStruct((batch, seq, d), _f32),
            jax.ShapeDtypeStruct((batch, seq) + TOKEN_SLAB, _f32),
            jax.ShapeDtypeStruct((n_tiles, 8, ts), _f32),
            jax.ShapeDtypeStruct((BUCKET_ROWS, 128), _f32),
        ],
        scratch_shapes=[
            pltpu.VMEM((HALO + ts, POOL_WIDTH), _f32),
            pltpu.VMEM((BUCKET_ROWS, 128), _f32),
        ],
        compiler_params=pltpu.CompilerParams(
            dimension_semantics=("arbitrary", "arbitrary"), vmem_limit_bytes=VMEM_LIMIT_BYTES),
        name="mixer_router",
    )(x, *prev_args, mod_l, gmix, gffn, w_in, v_g, sgu_w, sgu_bt, pool_w, pool_scale, w_a, w_b, w_o,
      router_w_pad, router_bias)


def _sc_mesh():
    return plsc.VectorSubcoreMesh(core_axis_name="c", subcore_axis_name="s")


def _sc_worker():
    return lax.axis_index("s") * SC_CORES + lax.axis_index("c")


def _sc_gather_chunks(table_hbm, idx_v, out_hbm, out_row0, n_chunks, buf0, buf1, sem0, sem1):
    def gather(j, dst, sem):
        return pltpu.make_async_copy(table_hbm.at[idx_v.at[j]], dst, sem)

    def write_out(j, src):
        pltpu.sync_copy(src, out_hbm.at[pl.ds(out_row0 + j * SC_CHUNK, SC_CHUNK)])

    gather(0, buf0, sem0).start()

    @pl.loop(0, n_chunks, step=2)
    def _(j):
        gather(j + 1, buf1, sem1).start()
        gather(j, buf0, sem0).wait()
        write_out(j, buf0)

        @pl.when(j + 2 < n_chunks)
        def _():
            gather(j + 2, buf0, sem0).start()

        gather(j + 1, buf1, sem1).wait()
        write_out(j + 1, buf1)


def _sc_chunks_per_worker(n_rows):
    n_chunks = n_rows // (SC_CORES * SC_SUBCORES * SC_CHUNK)
    assert n_chunks % 2 == 0 and n_chunks * SC_CORES * SC_SUBCORES * SC_CHUNK == n_rows
    return n_chunks


def _sc_gather_rows(idx, table):
    n_out = idx.shape[0]
    n_chunks = _sc_chunks_per_worker(n_out)
    buf = pltpu.VMEM((SC_CHUNK,) + TOKEN_SLAB, _f32)

    @functools.partial(
        pl.kernel, mesh=_sc_mesh(),
        out_type=jax.ShapeDtypeStruct((n_out,) + TOKEN_SLAB, _f32),
        scratch_types=[pltpu.VMEM((n_chunks, SC_CHUNK), jnp.int32), buf, buf,
                       pltpu.SemaphoreType.DMA, pltpu.SemaphoreType.DMA],
    )
    def gather_kernel(table_hbm, idx_hbm, out_hbm, idx_v, buf0, buf1, sem0, sem1):
        chunk0 = _sc_worker() * n_chunks
        pltpu.sync_copy(idx_hbm.at[pl.ds(chunk0, n_chunks)], idx_v)
        _sc_gather_chunks(table_hbm, idx_v, out_hbm, chunk0 * SC_CHUNK, n_chunks, buf0, buf1, sem0, sem1)

    return gather_kernel(table, idx.reshape(n_out // SC_CHUNK, SC_CHUNK))


def _sc_gather_by_slot(pos, table, n_slots):
    n_tok = pos.shape[0]
    per_worker = n_slots // (SC_CORES * SC_SUBCORES)
    n_chunks = _sc_chunks_per_worker(n_slots)
    buf = pltpu.VMEM((SC_CHUNK,) + TOKEN_SLAB, _f32)

    @functools.partial(
        pl.kernel, mesh=_sc_mesh(),
        compiler_params=pltpu.CompilerParams(needs_layout_passes=False),
        out_type=jax.ShapeDtypeStruct((n_slots,) + TOKEN_SLAB, _f32),
        scratch_types=[pltpu.VMEM((n_tok,), jnp.int32), pltpu.VMEM((n_chunks, SC_CHUNK), jnp.int32),
                       buf, buf, pltpu.SemaphoreType.DMA, pltpu.SemaphoreType.DMA],
    )
    def slot_kernel(table_hbm, pos_hbm, out_hbm, pos_v, inv_v, buf0, buf1, sem0, sem1):
        slot0 = _sc_worker() * per_worker
        pltpu.sync_copy(pos_hbm, pos_v)
        zeros = jnp.zeros((SC_LANES,), jnp.int32)

        @pl.loop(0, n_chunks)
        def _(j):
            for h in range(SC_CHUNK // SC_LANES):
                inv_v[j, pl.ds(h * SC_LANES, SC_LANES)] = zeros

        lane = lax.iota(jnp.int32, SC_LANES)

        @pl.loop(0, n_tok // SC_LANES)
        def _(i):
            local = pos_v[pl.ds(i * SC_LANES, SC_LANES)] - slot0
            mine = (local >= 0) & (local < per_worker)
            local = jnp.where(mine, local, 0)
            plsc.store_scatter(inv_v, [local // SC_CHUNK, local % SC_CHUNK], i * SC_LANES + lane, mask=mine)

        _sc_gather_chunks(table_hbm, inv_v, out_hbm, slot0, n_chunks, buf0, buf1, sem0, sem1)

    return slot_kernel(table, pos)


def _final_kernel(x_ref, y_ref, g_ref, fg_ref, o_ref):
    out = x_ref[...] + g_ref[...] * y_ref[...].reshape(x_ref.shape)
    o_ref[...] = out * lax.rsqrt(jnp.mean(out * out, axis=-1, keepdims=True) + EPS) * fg_ref[...]


def _final_norm(x, y_tok, gate_f, final_g):
    batch, seq, d = x.shape
    row_spec = pl.BlockSpec((None, ROW_TILE, d), lambda b, s: (b, s, 0))
    return pl.pallas_call(
        _final_kernel,
        grid=(batch, seq // ROW_TILE),
        in_specs=[row_spec, pl.BlockSpec((None, ROW_TILE) + TOKEN_SLAB, lambda b, s: (b, s, 0, 0)),
                  pl.BlockSpec((None, 1, d), lambda b, s: (b, 0, 0)),
                  pl.BlockSpec((1, d), lambda b, s: (0, 0))],
        out_specs=row_spec,
        out_shape=jax.ShapeDtypeStruct((batch, seq, d), _f32),
        compiler_params=pltpu.CompilerParams(dimension_semantics=("arbitrary", "arbitrary")),
        name="residual_final_norm",
    )(x, y_tok, gate_f, final_g)


def _experts_kernel(ea_ref, eb_ref, nused_ref, xs_ref, rwa_ref, rwb_ref,
                    wga_ref, wgb_ref, wua_ref, wub_ref, wda_ref, wdb_ref, ys_ref):
    i = pl.program_id(0)

    @pl.when(i < nused_ref[0])
    def _():
        x = xs_ref[...].reshape(EXPERT_TILE, D_MODEL)
        la = jnp.sum(x * rwa_ref[...], axis=-1, keepdims=True)
        lb = jnp.sum(x * rwb_ref[...], axis=-1, keepdims=True)
        wa = jax.nn.sigmoid(la - lb)
        wb = jax.nn.sigmoid(lb - la)
        xb = x.astype(_bf16)
        act_a = (jax.nn.silu(_dot(xb, wga_ref[...])) * _dot(xb, wua_ref[...]) * wa).astype(_bf16)
        act_b = (jax.nn.silu(_dot(xb, wgb_ref[...])) * _dot(xb, wub_ref[...]) * wb).astype(_bf16)
        y = _dot(act_a, wda_ref[...]) + _dot(act_b, wdb_ref[...])
        ys_ref[...] = y.reshape(ys_ref.shape)

    @pl.when(i >= nused_ref[0])
    def _():
        ys_ref[...] = jnp.zeros_like(ys_ref)


def _experts(tile_ea, tile_eb, n_used, xs, router_wt, w_gate, w_up, w_down):
    n_slots = xs.shape[0]
    d = D_MODEL
    n_tiles = n_slots // EXPERT_TILE
    f = D_EXPERT

    def row(i, ea, eb, nu):
        return (jnp.maximum(jnp.minimum(i, nu[0] - 1), 0), 0, 0)

    grid_spec = pltpu.PrefetchScalarGridSpec(
        num_scalar_prefetch=3,
        grid=(n_tiles,),
        in_specs=[
            pl.BlockSpec((EXPERT_TILE,) + TOKEN_SLAB, row),
            pl.BlockSpec((None, 1, d), lambda i, ea, eb, nu: (ea[i], 0, 0)),
            pl.BlockSpec((None, 1, d), lambda i, ea, eb, nu: (eb[i], 0, 0)),
            pl.BlockSpec((None, d, f), lambda i, ea, eb, nu: (ea[i], 0, 0)),
            pl.BlockSpec((None, d, f), lambda i, ea, eb, nu: (eb[i], 0, 0)),
            pl.BlockSpec((None, d, f), lambda i, ea, eb, nu: (ea[i], 0, 0)),
            pl.BlockSpec((None, d, f), lambda i, ea, eb, nu: (eb[i], 0, 0)),
            pl.BlockSpec((None, f, d), lambda i, ea, eb, nu: (ea[i], 0, 0)),
            pl.BlockSpec((None, f, d), lambda i, ea, eb, nu: (eb[i], 0, 0)),
        ],
        out_specs=pl.BlockSpec((EXPERT_TILE,) + TOKEN_SLAB, lambda i, ea, eb, nu: (i, 0, 0)),
    )
    return pl.pallas_call(
        _experts_kernel,
        grid_spec=grid_spec,
        out_shape=jax.ShapeDtypeStruct((n_slots,) + TOKEN_SLAB, _f32),
        compiler_params=pltpu.CompilerParams(
            dimension_semantics=("arbitrary",), vmem_limit_bytes=VMEM_LIMIT_BYTES),
        name="grouped_experts",
    )(tile_ea, tile_eb, n_used, xs, router_wt, router_wt, w_gate, w_gate, w_up, w_up, w_down, w_down)


def _routing_tables(route, counts, n_tiles_max):
    bucket = route[:, 0, :].reshape(-1).astype(jnp.int32)
    rank = route[:, 1, :].reshape(-1).astype(jnp.int32)
    cnt = counts[:N_BUCKETS, 0].astype(jnp.int32)
    tiles_b = (cnt + EXPERT_TILE - 1) // EXPERT_TILE
    tile_end = jnp.cumsum(tiles_b)
    tile_start = tile_end - tiles_b
    n_used = tile_end[-1]
    onehot = bucket[:, None] == jnp.arange(N_BUCKETS, dtype=jnp.int32)[None, :]
    pos = jnp.sum(jnp.where(onehot, (tile_start * EXPERT_TILE)[None, :], 0), axis=1) + rank
    tile_ids = jnp.minimum(jnp.arange(n_tiles_max, dtype=jnp.int32), n_used - 1)
    tile_bucket = jnp.sum((tile_ids[:, None] >= tile_end[None, :]).astype(jnp.int32), axis=1)
    group = tile_bucket // PAIRS_PER_GROUP
    pair = tile_bucket % PAIRS_PER_GROUP
    lo = jnp.asarray(_PAIR_LO, jnp.int32)[pair]
    hi = jnp.asarray(_PAIR_HI, jnp.int32)[pair]
    tile_ea = group * EXPERTS_PER_GROUP + lo
    tile_eb = group * EXPERTS_PER_GROUP + hi
    return pos, tile_ea, tile_eb, n_used.reshape(1)


def kernel(x, c, w_ada, b_ada, norm_mix_g, w_in, v_norm_g, sgu_w, sgu_b, pool_w, pool_scale,
           w_branch_a, w_branch_b, w_out, norm_ffn_g, router_w, router_bias,
           w_exp_gate, w_exp_up, w_exp_down, final_norm_g):
    batch, seq, d = x.shape
    depth = w_ada.shape[0]
    t = batch * seq
    n_tiles_max = t // EXPERT_TILE + N_BUCKETS
    n_slots = n_tiles_max * EXPERT_TILE

    mod = _ada(c, w_ada, b_ada)
    rw_pad = jnp.pad(router_w, ((0, 0), (0, 128 - N_EXPERTS)))
    rw_hi = rw_pad.astype(_bf16)
    rw_lo = (rw_pad - rw_hi.astype(_f32)).astype(_bf16)
    router_w_pad = jnp.concatenate([rw_hi, rw_lo], axis=1)
    router_wt = router_w.T.reshape(N_EXPERTS, 1, d)
    final_g = final_norm_g.reshape(1, d)

    y_tok, gate_prev = None, None
    for l in range(depth):
        mod_l = mod[l].reshape(batch, 1, N_MOD * d)
        x, hf, route, counts = _mixer(
            x, y_tok, gate_prev, mod_l, norm_mix_g[l].reshape(1, d), norm_ffn_g[l].reshape(1, d),
            w_in[l].astype(_bf16), v_norm_g[l].reshape(1, SGU_WIDTH),
            sgu_w[l].astype(_bf16), sgu_b[l].T,
            pool_w[l].astype(_bf16), pool_scale[l].reshape(1, POOL_WIDTH),
            w_branch_a[l].astype(_bf16), w_branch_b[l].astype(_bf16), w_out[l].astype(_bf16),
            router_w_pad, router_bias)
        pos, tile_ea, tile_eb, n_used = _routing_tables(route, counts, n_tiles_max)
        xs = _sc_gather_by_slot(pos, hf.reshape((t,) + TOKEN_SLAB), n_slots)
        ys = _experts(tile_ea, tile_eb, n_used, xs, router_wt,
                      w_exp_gate[l].astype(_bf16), w_exp_up[l].astype(_bf16),
                      w_exp_down[l].astype(_bf16))
        y_tok = _sc_gather_rows(pos, ys).reshape((batch, seq) + TOKEN_SLAB)
        gate_prev = mod_l[:, :, 5 * d:6 * d]
    return _final_norm(x, y_tok, gate_prev, final_g)
```

```python
import functools

import jax
import jax.numpy as jnp
from jax import lax
from jax.experimental import pallas as pl
from jax.experimental.pallas import tpu as pltpu
from jax.experimental.pallas import tpu_sc as plsc

D_MODEL = 1024
CHUNK = 128
SGU_HEADS = 8
SGU_HEAD_DIM = 128
SGU_WIDTH = 1024
POOL_WINDOWS = (2, 4, 8, 16)
POOL_GROUP_DIM = 256
POOL_WIDTH = 1024
HALO = 16
N_EXPERTS = 16
N_EXPERT_GROUPS = 4
EXPERTS_PER_GROUP = 4
PAIRS_PER_GROUP = 6
N_BUCKETS = N_EXPERT_GROUPS * PAIRS_PER_GROUP
BUCKET_ROWS = 32
D_EXPERT = 512
N_MOD = 6
EPS = 1e-6

SEQ_TILE = 512
SUB_TILE = 256
ROW_TILE = 512
EXPERT_TILE = 256
VMEM_LIMIT_BYTES = 58 * 1024 * 1024
TOKEN_SLAB = (8, 128)
SC_CORES = 2
SC_SUBCORES = 16
SC_CHUNK = 32
SC_LANES = 16

_PAIR_LO = (0, 0, 0, 1, 1, 2)
_PAIR_HI = (1, 2, 3, 2, 3, 3)

_bf16 = jnp.bfloat16
_f32 = jnp.float32


def _dot(a, b):
    return jnp.dot(a, b, preferred_element_type=_f32)


def _rms_modulate(x, g, shift, scale):
    y = x * lax.rsqrt(jnp.mean(x * x, axis=-1, keepdims=True) + EPS)
    return (y * g) * (1.0 + scale) + shift


def _ada_kernel(c_ref, w_ref, b_ref, o_ref):
    c = c_ref[...]
    c_act = (c * jax.nn.sigmoid(c)).astype(_bf16)
    o_ref[...] = _dot(c_act, w_ref[...].astype(_bf16)) + b_ref[...]


def _ada(c, w_ada, b_ada):
    depth, d, n = w_ada.shape
    batch = c.shape[0]
    tn = 2048
    return pl.pallas_call(
        _ada_kernel,
        grid=(depth, n // tn),
        in_specs=[
            pl.BlockSpec((batch, d), lambda l, j: (0, 0)),
            pl.BlockSpec((None, d, tn), lambda l, j: (l, 0, j)),
            pl.BlockSpec((None, 1, tn), lambda l, j: (l, 0, j)),
        ],
        out_specs=pl.BlockSpec((None, batch, tn), lambda l, j: (l, 0, j)),
        out_shape=jax.ShapeDtypeStruct((depth, batch, n), _f32),
        compiler_params=pltpu.CompilerParams(
            dimension_semantics=("arbitrary", "arbitrary"), vmem_limit_bytes=VMEM_LIMIT_BYTES),
        name="ada_modulation",
    )(c, w_ada, b_ada.reshape(depth, 1, n))


def _route(hf, rw_ref, rb_ref, carry_ref):
    ts = hf.shape[0]
    hf_hi = hf.astype(_bf16)
    hf_lo = (hf - hf_hi.astype(_f32)).astype(_bf16)
    both = _dot(hf_hi, rw_ref[...])
    logits = both[:, 0:128] + both[:, 128:256] + _dot(hf_lo, rw_ref[:, 0:128])
    lt = logits.T
    rows = [lt[e:e + 1, :] for e in range(N_EXPERTS)]
    m = functools.reduce(jnp.maximum, rows)
    ex = [jnp.exp(r - m) for r in rows]
    den = functools.reduce(lambda a, b: a + b, ex)
    probs = [e / den for e in ex]
    sel = [probs[e] + rb_ref[e] for e in range(N_EXPERTS)]

    def top2_sum(v):
        pairs = [v[i] + v[j] for i, j in zip(_PAIR_LO, _PAIR_HI)]
        return functools.reduce(jnp.maximum, pairs)

    gscore = [top2_sum(sel[4 * g:4 * g + 4]) for g in range(N_EXPERT_GROUPS)]
    best = gscore[0]
    gidx = jnp.zeros_like(best, dtype=jnp.int32)
    for g in range(1, N_EXPERT_GROUPS):
        better = gscore[g] > best
        best = jnp.where(better, gscore[g], best)
        gidx = jnp.where(better, g, gidx)
    ing = []
    for k in range(EXPERTS_PER_GROUP):
        v = sel[k]
        for g in range(1, N_EXPERT_GROUPS):
            v = jnp.where(gidx == g, sel[4 * g + k], v)
        ing.append(v)
    chosen = []
    for k in range(EXPERTS_PER_GROUP):
        r = jnp.zeros_like(gidx)
        for j in range(EXPERTS_PER_GROUP):
            if j == k:
                continue
            beats = (ing[j] >= ing[k]) if j < k else (ing[j] > ing[k])
            r = r + beats.astype(jnp.int32)
        chosen.append(r < 2)
    lo = jnp.where(chosen[0], 0, jnp.where(chosen[1], 1, 2))
    hi = jnp.where(chosen[3], 3, jnp.where(chosen[2], 2, 1))
    base = jnp.where(lo == 0, 0, jnp.where(lo == 1, 3, 5))
    bucket = gidx * PAIRS_PER_GROUP + base + hi - lo - 1

    brow = lax.broadcasted_iota(jnp.int32, (BUCKET_ROWS, ts), 0)
    onehot = (brow == bucket).astype(_f32)
    jj = lax.broadcasted_iota(jnp.int32, (ts, ts), 0)
    tt = lax.broadcasted_iota(jnp.int32, (ts, ts), 1)
    upper = (jj <= tt).astype(_bf16)
    cum = _dot(onehot.astype(_bf16), upper)
    carry = carry_ref[...][:, 0:1]
    rank = jnp.sum(onehot * (cum - 1.0 + carry), axis=0, keepdims=True)
    carry_ref[...] = carry_ref[...] + jnp.sum(onehot, axis=1, keepdims=True)
    return bucket.astype(_f32), rank


def _mixer_kernel(*refs, has_prev):
    if has_prev:
        x_ref, y_ref, gprev_ref = refs[:3]
        refs = refs[3:]
        x_in = lambda rows: x_ref[rows, :] + gprev_ref[...] * y_ref[rows].reshape(SUB_TILE, D_MODEL)
    else:
        x_ref = refs[0]
        refs = refs[1:]
        x_in = lambda rows: x_ref[rows, :]
    (mod_ref, gmix_ref, gffn_ref, win_ref, vg_ref, sw_ref, sb_ref, pw_ref, ps_ref, wa_ref, wb_ref,
     wo_ref, rw_ref, rb_ref, xo_ref, hf_ref, route_ref, counts_ref, ext_ref, carry_ref) = refs
    b = pl.program_id(0)
    s = pl.program_id(1)
    ts = x_ref.shape[0]
    d = D_MODEL

    @pl.when((b == 0) & (s == 0))
    def _():
        carry_ref[...] = jnp.zeros_like(carry_ref)

    @pl.when(s == 0)
    def _():
        ext_ref[0:HALO, :] = jnp.zeros((HALO, POOL_WIDTH), _f32)

    mod = mod_ref[...]
    ci = lax.broadcasted_iota(jnp.int32, (CHUNK, CHUNK), 0)
    cj = lax.broadcasted_iota(jnp.int32, (CHUNK, CHUNK), 1)
    sgu_w = [jnp.where(ci >= cj, sw_ref[h], jnp.zeros((), _bf16)) for h in range(SGU_HEADS)]
    for r in range(ts // SUB_TILE):
        _mixer_rows(r * SUB_TILE, s * ts + r * SUB_TILE, mod, sgu_w,
                    x_in, gmix_ref, gffn_ref, win_ref, vg_ref, sb_ref, pw_ref, ps_ref, wa_ref,
                    wb_ref, wo_ref, rw_ref, rb_ref, xo_ref, hf_ref, route_ref, ext_ref, carry_ref)
    ext_ref[0:HALO, :] = ext_ref[ts:ts + HALO, :]
    route_ref[2:8, :] = jnp.zeros((6, ts), _f32)
    counts_ref[...] = carry_ref[...]


def _mixer_rows(row0, seq_pos0, mod, sgu_w, x_in, gmix_ref, gffn_ref, win_ref, vg_ref, sb_ref,
                pw_ref, ps_ref, wa_ref, wb_ref, wo_ref, rw_ref, rb_ref, xo_ref, hf_ref, route_ref,
                ext_ref, carry_ref):
    d = D_MODEL
    ts = SUB_TILE
    rows = slice(row0, row0 + ts)
    sh_m, sc_m, g_m = mod[:, 0:d], mod[:, d:2 * d], mod[:, 2 * d:3 * d]
    sh_f, sc_f = mod[:, 3 * d:4 * d], mod[:, 4 * d:5 * d]
    x = x_in(rows)

    hb = _rms_modulate(x, gmix_ref[...], sh_m, sc_m).astype(_bf16)

    v = jax.nn.gelu(_dot(hb, win_ref[:, SGU_WIDTH:2 * SGU_WIDTH]))
    vc = v - jnp.mean(v, axis=-1, keepdims=True)
    vn = (vc * lax.rsqrt(jnp.mean(vc * vc, axis=-1, keepdims=True) + EPS) * vg_ref[...]).astype(_bf16)
    u = jax.nn.gelu(_dot(hb, win_ref[:, 0:SGU_WIDTH]))
    n_chunks = ts // CHUNK
    ya_cols = []
    for h in range(SGU_HEADS):
        cols = slice(h * SGU_HEAD_DIM, (h + 1) * SGU_HEAD_DIM)
        rhs = jnp.concatenate([vn[n * CHUNK:(n + 1) * CHUNK, cols] for n in range(n_chunks)], axis=1)
        sg = _dot(sgu_w[h], rhs) + sb_ref[:, h:h + 1]
        s_h = jnp.concatenate([sg[:, n * SGU_HEAD_DIM:(n + 1) * SGU_HEAD_DIM] for n in range(n_chunks)],
                              axis=0)
        ya_cols.append((u[:, cols] * s_h).astype(_bf16))
    ya = jnp.concatenate(ya_cols, axis=1)
    gate_a = jax.nn.sigmoid(_dot(hb, win_ref[:, 3 * d:4 * d]))
    merged = gate_a * _dot(ya, wa_ref[...])

    p = _dot(hb, win_ref[:, 2 * SGU_WIDTH:2 * SGU_WIDTH + POOL_WIDTH])
    e0 = HALO + row0
    ext_ref[e0:e0 + ts, :] = p
    pos1 = (seq_pos0 + 1 + lax.broadcasted_iota(jnp.int32, (ts, 1), 0)).astype(_f32)
    yb_cols = []
    for gi, w in enumerate(POOL_WINDOWS):
        cols = slice(gi * POOL_GROUP_DIM, (gi + 1) * POOL_GROUP_DIM)
        acc = p[:, cols]
        for k in range(1, w):
            acc = acc + ext_ref[e0 - k:e0 - k + ts, cols]
        count = jnp.minimum(pos1, float(w))
        pooled = (acc / count - p[:, cols]).astype(_bf16)
        yb_cols.append(_dot(pooled, pw_ref[gi]))
    yb = (jnp.concatenate(yb_cols, axis=1) * ps_ref[...]).astype(_bf16)
    gate_b = jax.nn.sigmoid(_dot(hb, win_ref[:, 4 * d:5 * d]))
    merged = merged + gate_b * _dot(yb, wb_ref[...])

    x_new = x + g_m * _dot(merged.astype(_bf16), wo_ref[...])
    xo_ref[rows, :] = x_new

    hf = _rms_modulate(x_new, gffn_ref[...], sh_f, sc_f)
    hf_ref[rows] = hf.reshape((ts,) + TOKEN_SLAB)
    bucket, rank = _route(hf, rw_ref, rb_ref, carry_ref)
    route_ref[0:1, rows] = bucket
    route_ref[1:2, rows] = rank


def _mixer(x, y_prev, gate_prev, mod_l, gmix, gffn, w_in, v_g, sgu_w, sgu_bt, pool_w, pool_scale,
           w_a, w_b, w_o, router_w_pad, router_bias):
    batch, seq, d = x.shape
    has_prev = y_prev is not None
    ts = SEQ_TILE
    n_tiles = batch * (seq // ts)
    tiles_per_seq = seq // ts
    const = lambda *shape: pl.BlockSpec(shape, lambda b, s: (0,) * len(shape),
                                        pipeline_mode=pl.Buffered(1))
    row_spec = pl.BlockSpec((None, ts, d), lambda b, s: (b, s, 0))
    slab_spec = pl.BlockSpec((None, ts) + TOKEN_SLAB, lambda b, s: (b, s, 0, 0))
    prev_specs = [slab_spec, pl.BlockSpec((None, 1, d), lambda b, s: (b, 0, 0))] if has_prev else []
    prev_args = (y_prev, gate_prev) if has_prev else ()
    return pl.pallas_call(
        functools.partial(_mixer_kernel, has_prev=has_prev),
        grid=(batch, tiles_per_seq),
        in_specs=[row_spec] + prev_specs + [
            pl.BlockSpec((None, 1, N_MOD * d), lambda b, s: (b, 0, 0)),
            const(1, d), const(1, d),
            const(*w_in.shape),
            const(1, SGU_WIDTH),
            const(*sgu_w.shape),
            const(*sgu_bt.shape),
            const(*pool_w.shape),
            const(1, POOL_WIDTH),
            const(d, d), const(d, d), const(d, d),
            const(*router_w_pad.shape),
            pl.BlockSpec(memory_space=pltpu.SMEM),
        ],
        out_specs=[
            pl.BlockSpec((None, ts, d), lambda b, s: (b, s, 0)),
            pl.BlockSpec((None, ts) + TOKEN_SLAB, lambda b, s: (b, s, 0, 0)),
            pl.BlockSpec((None, 8, ts), lambda b, s: (b * tiles_per_seq + s, 0, 0)),
            pl.BlockSpec((BUCKET_ROWS, 128), lambda b, s: (0, 0)),
        ],
        out_shape=[
            jax.ShapeDtypeStruct((batch, seq, d), _f32),
            jax.ShapeDtypeStruct((batch, seq) + TOKEN_SLAB, _f32),
            jax.ShapeDtypeStruct((n_tiles, 8, ts), _f32),
            jax.ShapeDtypeStruct((BUCKET_ROWS, 128), _f32),
        ],
        scratch_shapes=[
            pltpu.VMEM((HALO + ts, POOL_WIDTH), _f32),
            pltpu.VMEM((BUCKET_ROWS, 128), _f32),
        ],
        compiler_params=pltpu.CompilerParams(
            dimension_semantics=("arbitrary", "arbitrary"), vmem_limit_bytes=VMEM_LIMIT_BYTES),
        name="mixer_router",
    )(x, *prev_args, mod_l, gmix, gffn, w_in, v_g, sgu_w, sgu_bt, pool_w, pool_scale, w_a, w_b, w_o,
      router_w_pad, router_bias)


def _sc_mesh():
    return plsc.VectorSubcoreMesh(core_axis_name="c", subcore_axis_name="s")


def _sc_worker():
    return lax.axis_index("s") * SC_CORES + lax.axis_index("c")


def _sc_gather_chunks(table_hbm, idx_v, out_hbm, out_row0, n_chunks, buf0, buf1, sem0, sem1):
    def gather(j, dst, sem):
        return pltpu.make_async_copy(table_hbm.at[idx_v.at[j]], dst, sem)

    def write_out(j, src):
        pltpu.sync_copy(src, out_hbm.at[pl.ds(out_row0 + j * SC_CHUNK, SC_CHUNK)])

    gather(0, buf0, sem0).start()

    @pl.loop(0, n_chunks, step=2)
    def _(j):
        gather(j + 1, buf1, sem1).start()
        gather(j, buf0, sem0).wait()
        write_out(j, buf0)

        @pl.when(j + 2 < n_chunks)
        def _():
            gather(j + 2, buf0, sem0).start()

        gather(j + 1, buf1, sem1).wait()
        write_out(j + 1, buf1)


def _sc_chunks_per_worker(n_rows):
    n_chunks = n_rows // (SC_CORES * SC_SUBCORES * SC_CHUNK)
    assert n_chunks % 2 == 0 and n_chunks * SC_CORES * SC_SUBCORES * SC_CHUNK == n_rows
    return n_chunks


def _sc_gather_rows(idx, table):
    n_out = idx.shape[0]
    n_chunks = _sc_chunks_per_worker(n_out)
    buf = pltpu.VMEM((SC_CHUNK,) + TOKEN_SLAB, _f32)

    @functools.partial(
        pl.kernel, mesh=_sc_mesh(),
        out_type=jax.ShapeDtypeStruct((n_out,) + TOKEN_SLAB, _f32),
        scratch_types=[pltpu.VMEM((n_chunks, SC_CHUNK), jnp.int32), buf, buf,
                       pltpu.SemaphoreType.DMA, pltpu.SemaphoreType.DMA],
    )
    def gather_kernel(table_hbm, idx_hbm, out_hbm, idx_v, buf0, buf1, sem0, sem1):
        chunk0 = _sc_worker() * n_chunks
        pltpu.sync_copy(idx_hbm.at[pl.ds(chunk0, n_chunks)], idx_v)
        _sc_gather_chunks(table_hbm, idx_v, out_hbm, chunk0 * SC_CHUNK, n_chunks, buf0, buf1, sem0, sem1)

    return gather_kernel(table, idx.reshape(n_out // SC_CHUNK, SC_CHUNK))


def _sc_gather_by_slot(pos, table, n_slots):
    n_tok = pos.shape[0]
    per_worker = n_slots // (SC_CORES * SC_SUBCORES)
    n_chunks = _sc_chunks_per_worker(n_slots)
    buf = pltpu.VMEM((SC_CHUNK,) + TOKEN_SLAB, _f32)

    @functools.partial(
        pl.kernel, mesh=_sc_mesh(),
        compiler_params=pltpu.CompilerParams(needs_layout_passes=False),
        out_type=jax.ShapeDtypeStruct((n_slots,) + TOKEN_SLAB, _f32),
        scratch_types=[pltpu.VMEM((n_tok,), jnp.int32), pltpu.VMEM((n_chunks, SC_CHUNK), jnp.int32),
                       buf, buf, pltpu.SemaphoreType.DMA, pltpu.SemaphoreType.DMA],
    )
    def slot_kernel(table_hbm, pos_hbm, out_hbm, pos_v, inv_v, buf0, buf1, sem0, sem1):
        slot0 = _sc_worker() * per_worker
        pltpu.sync_copy(pos_hbm, pos_v)
        lane = lax.iota(jnp.int32, SC_LANES)

        @pl.loop(0, n_chunks)
        def _(j):
            for h in range(SC_CHUNK // SC_LANES):
                s = slot0 + j * SC_CHUNK + h * SC_LANES + lane
                inv_v[j, pl.ds(h * SC_LANES, SC_LANES)] = jnp.where(s >= n_tok, s - n_tok, s)

        @pl.loop(0, n_tok // SC_LANES)
        def _(i):
            local = pos_v[pl.ds(i * SC_LANES, SC_LANES)] - slot0
            mine = (local >= 0) & (local < per_worker)
            local = jnp.where(mine, local, 0)
            plsc.store_scatter(inv_v, [local // SC_CHUNK, local % SC_CHUNK], i * SC_LANES + lane, mask=mine)

        _sc_gather_chunks(table_hbm, inv_v, out_hbm, slot0, n_chunks, buf0, buf1, sem0, sem1)

    return slot_kernel(table, pos)


def _final_kernel(x_ref, y_ref, g_ref, fg_ref, o_ref):
    out = x_ref[...] + g_ref[...] * y_ref[...].reshape(x_ref.shape)
    o_ref[...] = out * lax.rsqrt(jnp.mean(out * out, axis=-1, keepdims=True) + EPS) * fg_ref[...]


def _final_norm(x, y_tok, gate_f, final_g):
    batch, seq, d = x.shape
    row_spec = pl.BlockSpec((None, ROW_TILE, d), lambda b, s: (b, s, 0))
    return pl.pallas_call(
        _final_kernel,
        grid=(batch, seq // ROW_TILE),
        in_specs=[row_spec, pl.BlockSpec((None, ROW_TILE) + TOKEN_SLAB, lambda b, s: (b, s, 0, 0)),
                  pl.BlockSpec((None, 1, d), lambda b, s: (b, 0, 0)),
                  pl.BlockSpec((1, d), lambda b, s: (0, 0))],
        out_specs=row_spec,
        out_shape=jax.ShapeDtypeStruct((batch, seq, d), _f32),
        compiler_params=pltpu.CompilerParams(dimension_semantics=("arbitrary", "arbitrary")),
        name="residual_final_norm",
    )(x, y_tok, gate_f, final_g)


def _experts_kernel(ea_ref, eb_ref, nused_ref, xs_ref, rwa_ref, rwb_ref,
                    wga_ref, wgb_ref, wua_ref, wub_ref, wda_ref, wdb_ref, ys_ref):
    i = pl.program_id(0)

    @pl.when(i < nused_ref[0])
    def _():
        x = xs_ref[...].reshape(EXPERT_TILE, D_MODEL)
        la = jnp.sum(x * rwa_ref[...], axis=-1, keepdims=True)
        lb = jnp.sum(x * rwb_ref[...], axis=-1, keepdims=True)
        wa = jax.nn.sigmoid(la - lb)
        wb = jax.nn.sigmoid(lb - la)
        xb = x.astype(_bf16)
        act_a = (jax.nn.silu(_dot(xb, wga_ref[...])) * _dot(xb, wua_ref[...]) * wa).astype(_bf16)
        act_b = (jax.nn.silu(_dot(xb, wgb_ref[...])) * _dot(xb, wub_ref[...]) * wb).astype(_bf16)
        y = _dot(act_a, wda_ref[...]) + _dot(act_b, wdb_ref[...])
        ys_ref[...] = y.reshape(ys_ref.shape)

    @pl.when(i >= nused_ref[0])
    def _():
        ys_ref[...] = jnp.zeros_like(ys_ref)


def _experts(tile_ea, tile_eb, n_used, xs, router_wt, w_gate, w_up, w_down):
    n_slots = xs.shape[0]
    d = D_MODEL
    n_tiles = n_slots // EXPERT_TILE
    f = D_EXPERT

    def row(i, ea, eb, nu):
        return (jnp.maximum(jnp.minimum(i, nu[0] - 1), 0), 0, 0)

    grid_spec = pltpu.PrefetchScalarGridSpec(
        num_scalar_prefetch=3,
        grid=(n_tiles,),
        in_specs=[
            pl.BlockSpec((EXPERT_TILE,) + TOKEN_SLAB, row),
            pl.BlockSpec((None, 1, d), lambda i, ea, eb, nu: (ea[i], 0, 0)),
            pl.BlockSpec((None, 1, d), lambda i, ea, eb, nu: (eb[i], 0, 0)),
            pl.BlockSpec((None, d, f), lambda i, ea, eb, nu: (ea[i], 0, 0)),
            pl.BlockSpec((None, d, f), lambda i, ea, eb, nu: (eb[i], 0, 0)),
            pl.BlockSpec((None, d, f), lambda i, ea, eb, nu: (ea[i], 0, 0)),
            pl.BlockSpec((None, d, f), lambda i, ea, eb, nu: (eb[i], 0, 0)),
            pl.BlockSpec((None, f, d), lambda i, ea, eb, nu: (ea[i], 0, 0)),
            pl.BlockSpec((None, f, d), lambda i, ea, eb, nu: (eb[i], 0, 0)),
        ],
        out_specs=pl.BlockSpec((EXPERT_TILE,) + TOKEN_SLAB, lambda i, ea, eb, nu: (i, 0, 0)),
    )
    return pl.pallas_call(
        _experts_kernel,
        grid_spec=grid_spec,
        out_shape=jax.ShapeDtypeStruct((n_slots,) + TOKEN_SLAB, _f32),
        compiler_params=pltpu.CompilerParams(
            dimension_semantics=("arbitrary",), vmem_limit_bytes=VMEM_LIMIT_BYTES),
        name="grouped_experts",
    )(tile_ea, tile_eb, n_used, xs, router_wt, router_wt, w_gate, w_gate, w_up, w_up, w_down, w_down)


def _routing_tables(route, counts, n_tiles_max):
    bucket = route[:, 0, :].reshape(-1).astype(jnp.int32)
    rank = route[:, 1, :].reshape(-1).astype(jnp.int32)
    cnt = counts[:N_BUCKETS, 0].astype(jnp.int32)
    tiles_b = (cnt + EXPERT_TILE - 1) // EXPERT_TILE
    tile_end = jnp.cumsum(tiles_b)
    tile_start = tile_end - tiles_b
    n_used = tile_end[-1]
    onehot = bucket[:, None] == jnp.arange(N_BUCKETS, dtype=jnp.int32)[None, :]
    pos = jnp.sum(jnp.where(onehot, (tile_start * EXPERT_TILE)[None, :], 0), axis=1) + rank
    tile_ids = jnp.minimum(jnp.arange(n_tiles_max, dtype=jnp.int32), n_used - 1)
    tile_bucket = jnp.sum((tile_ids[:, None] >= tile_end[None, :]).astype(jnp.int32), axis=1)
    group = tile_bucket // PAIRS_PER_GROUP
    pair = tile_bucket % PAIRS_PER_GROUP
    lo = jnp.asarray(_PAIR_LO, jnp.int32)[pair]
    hi = jnp.asarray(_PAIR_HI, jnp.int32)[pair]
    tile_ea = group * EXPERTS_PER_GROUP + lo
    tile_eb = group * EXPERTS_PER_GROUP + hi
    return pos, tile_ea, tile_eb, n_used.reshape(1)


def kernel(x, c, w_ada, b_ada, norm_mix_g, w_in, v_norm_g, sgu_w, sgu_b, pool_w, pool_scale,
           w_branch_a, w_branch_b, w_out, norm_ffn_g, router_w, router_bias,
           w_exp_gate, w_exp_up, w_exp_down, final_norm_g):
    batch, seq, d = x.shape
    depth = w_ada.shape[0]
    t = batch * seq
    n_tiles_max = t // EXPERT_TILE + N_BUCKETS
    n_slots = n_tiles_max * EXPERT_TILE

    mod = _ada(c, w_ada, b_ada)
    rw_pad = jnp.pad(router_w, ((0, 0), (0, 128 - N_EXPERTS)))
    rw_hi = rw_pad.astype(_bf16)
    rw_lo = (rw_pad - rw_hi.astype(_f32)).astype(_bf16)
    router_w_pad = jnp.concatenate([rw_hi, rw_lo], axis=1)
    router_wt = router_w.T.reshape(N_EXPERTS, 1, d)
    final_g = final_norm_g.reshape(1, d)

    y_tok, gate_prev = None, None
    for l in range(depth):
        mod_l = mod[l].reshape(batch, 1, N_MOD * d)
        x, hf, route, counts = _mixer(
            x, y_tok, gate_prev, mod_l, norm_mix_g[l].reshape(1, d), norm_ffn_g[l].reshape(1, d),
            w_in[l].astype(_bf16), v_norm_g[l].reshape(1, SGU_WIDTH),
            sgu_w[l].astype(_bf16), sgu_b[l].T,
            pool_w[l].astype(_bf16), pool_scale[l].reshape(1, POOL_WIDTH),
            w_branch_a[l].astype(_bf16), w_branch_b[l].astype(_bf16), w_out[l].astype(_bf16),
            router_w_pad, router_bias)
        pos, tile_ea, tile_eb, n_used = _routing_tables(route, counts, n_tiles_max)
        xs = _sc_gather_by_slot(pos, hf.reshape((t,) + TOKEN_SLAB), n_slots)
        ys = _experts(tile_ea, tile_eb, n_used, xs, router_wt,
                      w_exp_gate[l].astype(_bf16), w_exp_up[l].astype(_bf16),
                      w_exp_down[l].astype(_bf16))
        y_tok = _sc_gather_rows(pos, ys).reshape((batch, seq) + TOKEN_SLAB)
        gate_prev = mod_l[:, :, 5 * d:6 * d]
    return _final_norm(x, y_tok, gate_prev, final_g)
```

```python
import functools

import jax
import jax.numpy as jnp
from jax import lax
from jax.experimental import pallas as pl
from jax.experimental.pallas import tpu as pltpu
from jax.experimental.pallas import tpu_sc as plsc

D_MODEL = 1024
CHUNK = 128
SGU_HEADS = 8
SGU_HEAD_DIM = 128
SGU_WIDTH = 1024
POOL_WINDOWS = (2, 4, 8, 16)
POOL_GROUP_DIM = 256
POOL_WIDTH = 1024
HALO = 16
N_EXPERTS = 16
N_EXPERT_GROUPS = 4
EXPERTS_PER_GROUP = 4
PAIRS_PER_GROUP = 6
N_BUCKETS = N_EXPERT_GROUPS * PAIRS_PER_GROUP
BUCKET_ROWS = 32
D_EXPERT = 512
N_MOD = 6
EPS = 1e-6

SEQ_TILE = 512
SUB_TILE = 256
MIXER_STAGES = 2
ROW_TILE = 512
EXPERT_TILE = 256
VMEM_LIMIT_BYTES = 58 * 1024 * 1024
TOKEN_SLAB = (8, 128)
SC_CORES = 2
SC_SUBCORES = 16
SC_CHUNK = 32
SC_LANES = 16
SC_RING = 3

_PAIR_A = (0, 0, 0, 1, 1, 3)
_PAIR_B = (1, 2, 3, 3, 2, 2)

_bf16 = jnp.bfloat16
_f32 = jnp.float32


def _dot(a, b):
    return jnp.dot(a, b, preferred_element_type=_f32)


def _rms_modulate(x, g, shift, scale):
    y = x * lax.rsqrt(jnp.mean(x * x, axis=-1, keepdims=True) + EPS)
    return (y * g) * (1.0 + scale) + shift


def _ada_kernel(c_ref, w_ref, b_ref, o_ref):
    c = c_ref[...]
    c_act = (c * jax.nn.sigmoid(c)).astype(_bf16)
    o_ref[...] = _dot(c_act, w_ref[...].astype(_bf16)) + b_ref[...]


def _ada(c, w_ada, b_ada):
    depth, d, n = w_ada.shape
    batch = c.shape[0]
    tn = 2048
    return pl.pallas_call(
        _ada_kernel,
        grid=(depth, n // tn),
        in_specs=[
            pl.BlockSpec((batch, d), lambda l, j: (0, 0)),
            pl.BlockSpec((None, d, tn), lambda l, j: (l, 0, j)),
            pl.BlockSpec((None, 1, tn), lambda l, j: (l, 0, j)),
        ],
        out_specs=pl.BlockSpec((None, batch, tn), lambda l, j: (l, 0, j)),
        out_shape=jax.ShapeDtypeStruct((depth, batch, n), _f32),
        compiler_params=pltpu.CompilerParams(
            dimension_semantics=("arbitrary", "arbitrary"), vmem_limit_bytes=VMEM_LIMIT_BYTES),
        name="ada_modulation",
    )(c, w_ada, b_ada.reshape(depth, 1, n))


def _route(hf, rw_ref, rb_ref, carry_ref):
    ts = hf.shape[0]
    hf_hi = hf.astype(_bf16)
    hf_lo = (hf - hf_hi.astype(_f32)).astype(_bf16)
    both = _dot(hf_hi, rw_ref[...])
    logits = both[:, 0:128] + both[:, 128:256] + _dot(hf_lo, rw_ref[:, 0:128])
    lt = logits.T
    rows = [lt[e:e + 1, :] for e in range(N_EXPERTS)]
    m = functools.reduce(jnp.maximum, rows)
    ex = [jnp.exp(r - m) for r in rows]
    den = functools.reduce(lambda a, b: a + b, ex)
    probs = [e / den for e in ex]
    sel = [probs[e] + rb_ref[e] for e in range(N_EXPERTS)]

    def top2_sum(v):
        pairs = [v[i] + v[j] for i, j in zip(_PAIR_A, _PAIR_B)]
        return functools.reduce(jnp.maximum, pairs)

    gscore = [top2_sum(sel[4 * g:4 * g + 4]) for g in range(N_EXPERT_GROUPS)]
    best = gscore[0]
    gidx = jnp.zeros_like(best, dtype=jnp.int32)
    for g in range(1, N_EXPERT_GROUPS):
        better = gscore[g] > best
        best = jnp.where(better, gscore[g], best)
        gidx = jnp.where(better, g, gidx)
    ing = []
    for k in range(EXPERTS_PER_GROUP):
        v = sel[k]
        for g in range(1, N_EXPERT_GROUPS):
            v = jnp.where(gidx == g, sel[4 * g + k], v)
        ing.append(v)
    chosen = []
    for k in range(EXPERTS_PER_GROUP):
        r = jnp.zeros_like(gidx)
        for j in range(EXPERTS_PER_GROUP):
            if j == k:
                continue
            beats = (ing[j] >= ing[k]) if j < k else (ing[j] > ing[k])
            r = r + beats.astype(jnp.int32)
        chosen.append(r < 2)
    lo = jnp.where(chosen[0], 0, jnp.where(chosen[1], 1, 2))
    hi = jnp.where(chosen[3], 3, jnp.where(chosen[2], 2, 1))
    pair = jnp.where(lo == 0, hi - 1, jnp.where(lo == 2, 5, jnp.where(hi == 3, 3, 4)))
    bucket = gidx * PAIRS_PER_GROUP + pair

    brow = lax.broadcasted_iota(jnp.int32, (BUCKET_ROWS, ts), 0)
    onehot = (brow == bucket).astype(_f32)
    jj = lax.broadcasted_iota(jnp.int32, (ts, ts), 0)
    tt = lax.broadcasted_iota(jnp.int32, (ts, ts), 1)
    upper = (jj <= tt).astype(_bf16)
    cum = _dot(onehot.astype(_bf16), upper)
    carry = carry_ref[...][:, 0:1]
    rank = jnp.sum(onehot * (cum - 1.0 + carry), axis=0, keepdims=True)
    carry_ref[...] = carry_ref[...] + jnp.sum(onehot, axis=1, keepdims=True)
    return bucket.astype(_f32), rank


def _mixer_kernel(*refs, has_prev):
    if has_prev:
        x_ref, y_ref, gprev_ref = refs[:3]
        refs = refs[3:]
        x_in = lambda rows: x_ref[rows, :] + gprev_ref[...] * y_ref[rows].reshape(SUB_TILE, D_MODEL)
    else:
        x_ref = refs[0]
        refs = refs[1:]
        x_in = lambda rows: x_ref[rows, :]
    (mod_ref, gmix_ref, gffn_ref, win_ref, vg_ref, sw_ref, sb_ref, pw_ref, ps_ref, wa_ref, wb_ref,
     wo_ref, rw_ref, rb_ref, xo_ref, hf_ref, route_ref, counts_ref, ext_ref, carry_ref) = refs
    b = pl.program_id(0)
    s = pl.program_id(1)
    ts = x_ref.shape[0]
    d = D_MODEL

    @pl.when((b == 0) & (s == 0))
    def _():
        carry_ref[...] = jnp.zeros_like(carry_ref)

    @pl.when(s == 0)
    def _():
        ext_ref[0:HALO, :] = jnp.zeros((HALO, POOL_WIDTH), _f32)

    mod = mod_ref[...]
    ci = lax.broadcasted_iota(jnp.int32, (CHUNK, CHUNK), 0)
    cj = lax.broadcasted_iota(jnp.int32, (CHUNK, CHUNK), 1)
    sgu_w = [jnp.where(ci >= cj, sw_ref[h], jnp.zeros((), _bf16)) for h in range(SGU_HEADS)]
    sub_tiles = [_mixer_rows(r * SUB_TILE, s * ts + r * SUB_TILE, mod, sgu_w,
                             x_in, gmix_ref, gffn_ref, win_ref, vg_ref, sb_ref, pw_ref, ps_ref, wa_ref,
                             wb_ref, wo_ref, rw_ref, rb_ref, xo_ref, hf_ref, route_ref, ext_ref, carry_ref)
                 for r in range(ts // SUB_TILE)]
    for _ in range(MIXER_STAGES):
        for sub_tile in sub_tiles:
            next(sub_tile, None)
    ext_ref[0:HALO, :] = ext_ref[ts:ts + HALO, :]
    route_ref[2:8, :] = jnp.zeros((6, ts), _f32)
    counts_ref[...] = carry_ref[...]


def _mixer_rows(row0, seq_pos0, mod, sgu_w, x_in, gmix_ref, gffn_ref, win_ref, vg_ref, sb_ref,
                pw_ref, ps_ref, wa_ref, wb_ref, wo_ref, rw_ref, rb_ref, xo_ref, hf_ref, route_ref,
                ext_ref, carry_ref):
    d = D_MODEL
    ts = SUB_TILE
    rows = slice(row0, row0 + ts)
    sh_m, sc_m, g_m = mod[:, 0:d], mod[:, d:2 * d], mod[:, 2 * d:3 * d]
    sh_f, sc_f = mod[:, 3 * d:4 * d], mod[:, 4 * d:5 * d]
    x = x_in(rows)

    hb = _rms_modulate(x, gmix_ref[...], sh_m, sc_m).astype(_bf16)

    v = jax.nn.gelu(_dot(hb, win_ref[:, SGU_WIDTH:2 * SGU_WIDTH]))
    p = _dot(hb, win_ref[:, 2 * SGU_WIDTH:2 * SGU_WIDTH + POOL_WIDTH])
    u = jax.nn.gelu(_dot(hb, win_ref[:, 0:SGU_WIDTH]))
    gate_a = jax.nn.sigmoid(_dot(hb, win_ref[:, 3 * d:4 * d]))
    gate_b = jax.nn.sigmoid(_dot(hb, win_ref[:, 4 * d:5 * d]))
    e0 = HALO + row0
    ext_ref[e0:e0 + ts, :] = p
    yield

    vc = v - jnp.mean(v, axis=-1, keepdims=True)
    vn = (vc * lax.rsqrt(jnp.mean(vc * vc, axis=-1, keepdims=True) + EPS) * vg_ref[...]).astype(_bf16)
    n_chunks = ts // CHUNK
    ya_cols = []
    for h in range(SGU_HEADS):
        cols = slice(h * SGU_HEAD_DIM, (h + 1) * SGU_HEAD_DIM)
        rhs = jnp.concatenate([vn[n * CHUNK:(n + 1) * CHUNK, cols] for n in range(n_chunks)], axis=1)
        sg = _dot(sgu_w[h], rhs) + sb_ref[:, h:h + 1]
        s_h = jnp.concatenate([sg[:, n * SGU_HEAD_DIM:(n + 1) * SGU_HEAD_DIM] for n in range(n_chunks)],
                              axis=0)
        ya_cols.append((u[:, cols] * s_h).astype(_bf16))
    ya = jnp.concatenate(ya_cols, axis=1)
    merged = gate_a * _dot(ya, wa_ref[...])

    pos1 = (seq_pos0 + 1 + lax.broadcasted_iota(jnp.int32, (ts, 1), 0)).astype(_f32)
    yb_cols = []
    for gi, w in enumerate(POOL_WINDOWS):
        cols = slice(gi * POOL_GROUP_DIM, (gi + 1) * POOL_GROUP_DIM)
        acc = p[:, cols]
        for k in range(1, w):
            acc = acc + ext_ref[e0 - k:e0 - k + ts, cols]
        count = jnp.minimum(pos1, float(w))
        pooled = (acc / count - p[:, cols]).astype(_bf16)
        yb_cols.append(_dot(pooled, pw_ref[gi]))
    yb = (jnp.concatenate(yb_cols, axis=1) * ps_ref[...]).astype(_bf16)
    merged = merged + gate_b * _dot(yb, wb_ref[...])

    x_new = x + g_m * _dot(merged.astype(_bf16), wo_ref[...])
    xo_ref[rows, :] = x_new

    hf = _rms_modulate(x_new, gffn_ref[...], sh_f, sc_f)
    hf_ref[rows] = hf.reshape((ts,) + TOKEN_SLAB)
    bucket, rank = _route(hf, rw_ref, rb_ref, carry_ref)
    route_ref[0:1, rows] = bucket
    route_ref[1:2, rows] = rank
    yield


def _mixer(x, y_prev, gate_prev, mod_l, gmix, gffn, w_in, v_g, sgu_w, sgu_bt, pool_w, pool_scale,
           w_a, w_b, w_o, router_w_pad, router_bias):
    batch, seq, d = x.shape
    has_prev = y_prev is not None
    ts = SEQ_TILE
    n_tiles = batch * (seq // ts)
    tiles_per_seq = seq // ts
    const = lambda *shape: pl.BlockSpec(shape, lambda b, s: (0,) * len(shape),
                                        pipeline_mode=pl.Buffered(1))
    row_spec = pl.BlockSpec((None, ts, d), lambda b, s: (b, s, 0))
    slab_spec = pl.BlockSpec((None, ts) + TOKEN_SLAB, lambda b, s: (b, s, 0, 0))
    prev_specs = [slab_spec, pl.BlockSpec((None, 1, d), lambda b, s: (b, 0, 0))] if has_prev else []
    prev_args = (y_prev, gate_prev) if has_prev else ()
    return pl.pallas_call(
        functools.partial(_mixer_kernel, has_prev=has_prev),
        grid=(batch, tiles_per_seq),
        in_specs=[row_spec] + prev_specs + [
            pl.BlockSpec((None, 1, N_MOD * d), lambda b, s: (b, 0, 0)),
            const(1, d), const(1, d),
            const(*w_in.shape),
            const(1, SGU_WIDTH),
            const(*sgu_w.shape),
            const(*sgu_bt.shape),
            const(*pool_w.shape),
            const(1, POOL_WIDTH),
            const(d, d), const(d, d), const(d, d),
            const(*router_w_pad.shape),
            pl.BlockSpec(memory_space=pltpu.SMEM),
        ],
        out_specs=[
            pl.BlockSpec((None, ts, d), lambda b, s: (b, s, 0)),
            pl.BlockSpec((None, ts) + TOKEN_SLAB, lambda b, s: (b, s, 0, 0)),
            pl.BlockSpec((None, 8, ts), lambda b, s: (b * tiles_per_seq + s, 0, 0)),
            pl.BlockSpec((BUCKET_ROWS, 128), lambda b, s: (0, 0)),
        ],
        out_shape=[
            jax.ShapeDtypeStruct((batch, seq, d), _f32),
            jax.ShapeDtypeStruct((batch, seq) + TOKEN_SLAB, _f32),
            jax.ShapeDtypeStruct((n_tiles, 8, ts), _f32),
            jax.ShapeDtypeStruct((BUCKET_ROWS, 128), _f32),
        ],
        scratch_shapes=[
            pltpu.VMEM((HALO + ts, POOL_WIDTH), _f32),
            pltpu.VMEM((BUCKET_ROWS, 128), _f32),
        ],
        compiler_params=pltpu.CompilerParams(
            dimension_semantics=("arbitrary", "arbitrary"), vmem_limit_bytes=VMEM_LIMIT_BYTES),
        name="mixer_router",
    )(x, *prev_args, mod_l, gmix, gffn, w_in, v_g, sgu_w, sgu_bt, pool_w, pool_scale, w_a, w_b, w_o,
      router_w_pad, router_bias)


def _sc_mesh():
    return plsc.VectorSubcoreMesh(core_axis_name="c", subcore_axis_name="s")


def _sc_worker():
    return lax.axis_index("s") * SC_CORES + lax.axis_index("c")


def _sc_gather_chunks(table_hbm, idx_v, out_hbm, out_row0, n_chunks, bufs, sems):
    depth = len(bufs)

    def gather(j, slot):
        return pltpu.make_async_copy(table_hbm.at[idx_v.at[j]], bufs[slot], sems[slot])

    for j in range(min(depth - 1, n_chunks)):
        gather(j, j).start()

    @pl.loop(0, n_chunks, step=depth)
    def _(j0):
        for slot in range(depth):
            j = j0 + slot

            @pl.when(j < n_chunks)
            def _():
                @pl.when(j + depth - 1 < n_chunks)
                def _():
                    gather(j + depth - 1, (slot + depth - 1) % depth).start()

                gather(j, slot).wait()
                pltpu.sync_copy(bufs[slot], out_hbm.at[pl.ds(out_row0 + j * SC_CHUNK, SC_CHUNK)])


def _sc_chunks_per_worker(n_rows):
    n_chunks = n_rows // (SC_CORES * SC_SUBCORES * SC_CHUNK)
    assert n_chunks * SC_CORES * SC_SUBCORES * SC_CHUNK == n_rows
    return n_chunks


def _sc_gather_rows(idx, table):
    n_out = idx.shape[0]
    n_chunks = _sc_chunks_per_worker(n_out)
    buf = pltpu.VMEM((SC_CHUNK,) + TOKEN_SLAB, _f32)

    @functools.partial(
        pl.kernel, mesh=_sc_mesh(),
        out_type=jax.ShapeDtypeStruct((n_out,) + TOKEN_SLAB, _f32),
        scratch_types=[pltpu.VMEM((n_chunks, SC_CHUNK), jnp.int32)]
        + [buf] * SC_RING + [pltpu.SemaphoreType.DMA] * SC_RING,
    )
    def gather_kernel(table_hbm, idx_hbm, out_hbm, idx_v, *ring):
        chunk0 = _sc_worker() * n_chunks
        pltpu.sync_copy(idx_hbm.at[pl.ds(chunk0, n_chunks)], idx_v)
        _sc_gather_chunks(table_hbm, idx_v, out_hbm, chunk0 * SC_CHUNK, n_chunks,
                          ring[:SC_RING], ring[SC_RING:])

    return gather_kernel(table, idx.reshape(n_out // SC_CHUNK, SC_CHUNK))


def _sc_gather_by_slot(pos, table, n_slots):
    n_tok = pos.shape[0]
    per_worker = n_slots // (SC_CORES * SC_SUBCORES)
    n_chunks = _sc_chunks_per_worker(n_slots)
    buf = pltpu.VMEM((SC_CHUNK,) + TOKEN_SLAB, _f32)

    @functools.partial(
        pl.kernel, mesh=_sc_mesh(),
        compiler_params=pltpu.CompilerParams(needs_layout_passes=False),
        out_type=jax.ShapeDtypeStruct((n_slots,) + TOKEN_SLAB, _f32),
        scratch_types=[pltpu.VMEM((n_tok,), jnp.int32), pltpu.VMEM((n_chunks, SC_CHUNK), jnp.int32)]
        + [buf] * SC_RING + [pltpu.SemaphoreType.DMA] * SC_RING,
    )
    def slot_kernel(table_hbm, pos_hbm, out_hbm, pos_v, inv_v, *ring):
        slot0 = _sc_worker() * per_worker
        pltpu.sync_copy(pos_hbm, pos_v)
        lane = lax.iota(jnp.int32, SC_LANES)

        @pl.loop(0, n_chunks)
        def _(j):
            for h in range(SC_CHUNK // SC_LANES):
                s = slot0 + j * SC_CHUNK + h * SC_LANES + lane
                inv_v[j, pl.ds(h * SC_LANES, SC_LANES)] = jnp.where(s >= n_tok, s - n_tok, s)

        @pl.loop(0, n_tok // SC_LANES)
        def _(i):
            local = pos_v[pl.ds(i * SC_LANES, SC_LANES)] - slot0
            mine = (local >= 0) & (local < per_worker)
            local = jnp.where(mine, local, 0)
            plsc.store_scatter(inv_v, [local // SC_CHUNK, local % SC_CHUNK], i * SC_LANES + lane, mask=mine)

        _sc_gather_chunks(table_hbm, inv_v, out_hbm, slot0, n_chunks, ring[:SC_RING], ring[SC_RING:])

    return slot_kernel(table, pos)


def _final_kernel(x_ref, y_ref, g_ref, fg_ref, o_ref):
    out = x_ref[...] + g_ref[...] * y_ref[...].reshape(x_ref.shape)
    o_ref[...] = out * lax.rsqrt(jnp.mean(out * out, axis=-1, keepdims=True) + EPS) * fg_ref[...]


def _final_norm(x, y_tok, gate_f, final_g):
    batch, seq, d = x.shape
    row_spec = pl.BlockSpec((None, ROW_TILE, d), lambda b, s: (b, s, 0))
    return pl.pallas_call(
        _final_kernel,
        grid=(batch, seq // ROW_TILE),
        in_specs=[row_spec, pl.BlockSpec((None, ROW_TILE) + TOKEN_SLAB, lambda b, s: (b, s, 0, 0)),
                  pl.BlockSpec((None, 1, d), lambda b, s: (b, 0, 0)),
                  pl.BlockSpec((1, d), lambda b, s: (0, 0))],
        out_specs=row_spec,
        out_shape=jax.ShapeDtypeStruct((batch, seq, d), _f32),
        compiler_params=pltpu.CompilerParams(dimension_semantics=("arbitrary", "arbitrary")),
        name="residual_final_norm",
    )(x, y_tok, gate_f, final_g)


def _experts_kernel(ea_ref, eb_ref, nused_ref, xs_ref, rwa_ref, rwb_ref,
                    wga_ref, wgb_ref, wua_ref, wub_ref, wda_ref, wdb_ref, ys_ref):
    i = pl.program_id(0)

    @pl.when(i < nused_ref[0])
    def _():
        x = xs_ref[...].reshape(EXPERT_TILE, D_MODEL)
        la = jnp.sum(x * rwa_ref[...], axis=-1, keepdims=True)
        lb = jnp.sum(x * rwb_ref[...], axis=-1, keepdims=True)
        wa = jax.nn.sigmoid(la - lb)
        wb = jax.nn.sigmoid(lb - la)
        xb = x.astype(_bf16)
        act_a = (jax.nn.silu(_dot(xb, wga_ref[...])) * _dot(xb, wua_ref[...]) * wa).astype(_bf16)
        act_b = (jax.nn.silu(_dot(xb, wgb_ref[...])) * _dot(xb, wub_ref[...]) * wb).astype(_bf16)
        y = _dot(act_a, wda_ref[...]) + _dot(act_b, wdb_ref[...])
        ys_ref[...] = y.reshape(ys_ref.shape)

    @pl.when(i >= nused_ref[0])
    def _():
        ys_ref[...] = jnp.zeros_like(ys_ref)


def _experts(tile_ea, tile_eb, n_used, xs, router_wt, w_gate, w_up, w_down):
    n_slots = xs.shape[0]
    d = D_MODEL
    n_tiles = n_slots // EXPERT_TILE
    f = D_EXPERT

    def row(i, ea, eb, nu):
        return (jnp.maximum(jnp.minimum(i, nu[0] - 1), 0), 0, 0)

    grid_spec = pltpu.PrefetchScalarGridSpec(
        num_scalar_prefetch=3,
        grid=(n_tiles,),
        in_specs=[
            pl.BlockSpec((EXPERT_TILE,) + TOKEN_SLAB, row),
            pl.BlockSpec((None, 1, d), lambda i, ea, eb, nu: (ea[i], 0, 0)),
            pl.BlockSpec((None, 1, d), lambda i, ea, eb, nu: (eb[i], 0, 0)),
            pl.BlockSpec((None, d, f), lambda i, ea, eb, nu: (ea[i], 0, 0)),
            pl.BlockSpec((None, d, f), lambda i, ea, eb, nu: (eb[i], 0, 0)),
            pl.BlockSpec((None, d, f), lambda i, ea, eb, nu: (ea[i], 0, 0)),
            pl.BlockSpec((None, d, f), lambda i, ea, eb, nu: (eb[i], 0, 0)),
            pl.BlockSpec((None, f, d), lambda i, ea, eb, nu: (ea[i], 0, 0)),
            pl.BlockSpec((None, f, d), lambda i, ea, eb, nu: (eb[i], 0, 0)),
        ],
        out_specs=pl.BlockSpec((EXPERT_TILE,) + TOKEN_SLAB, lambda i, ea, eb, nu: (i, 0, 0)),
    )
    return pl.pallas_call(
        _experts_kernel,
        grid_spec=grid_spec,
        out_shape=jax.ShapeDtypeStruct((n_slots,) + TOKEN_SLAB, _f32),
        compiler_params=pltpu.CompilerParams(
            dimension_semantics=("arbitrary",), vmem_limit_bytes=VMEM_LIMIT_BYTES),
        name="grouped_experts",
    )(tile_ea, tile_eb, n_used, xs, router_wt, router_wt, w_gate, w_gate, w_up, w_up, w_down, w_down)


def _routing_tables(route, counts, n_tiles_max):
    bucket = route[:, 0, :].reshape(-1).astype(jnp.int32)
    rank = route[:, 1, :].reshape(-1).astype(jnp.int32)
    cnt = counts[:N_BUCKETS, 0].astype(jnp.int32)
    tiles_b = (cnt + EXPERT_TILE - 1) // EXPERT_TILE
    tile_end = jnp.cumsum(tiles_b)
    tile_start = tile_end - tiles_b
    n_used = tile_end[-1]
    onehot = bucket[:, None] == jnp.arange(N_BUCKETS, dtype=jnp.int32)[None, :]
    pos = jnp.sum(jnp.where(onehot, (tile_start * EXPERT_TILE)[None, :], 0), axis=1) + rank
    tile_ids = jnp.minimum(jnp.arange(n_tiles_max, dtype=jnp.int32), n_used - 1)
    tile_bucket = jnp.sum((tile_ids[:, None] >= tile_end[None, :]).astype(jnp.int32), axis=1)
    group = tile_bucket // PAIRS_PER_GROUP
    pair = tile_bucket % PAIRS_PER_GROUP
    tile_ea = group * EXPERTS_PER_GROUP + jnp.asarray(_PAIR_A, jnp.int32)[pair]
    tile_eb = group * EXPERTS_PER_GROUP + jnp.asarray(_PAIR_B, jnp.int32)[pair]
    return pos, tile_ea, tile_eb, n_used.reshape(1)


def kernel(x, c, w_ada, b_ada, norm_mix_g, w_in, v_norm_g, sgu_w, sgu_b, pool_w, pool_scale,
           w_branch_a, w_branch_b, w_out, norm_ffn_g, router_w, router_bias,
           w_exp_gate, w_exp_up, w_exp_down, final_norm_g):
    batch, seq, d = x.shape
    depth = w_ada.shape[0]
    t = batch * seq
    n_tiles_max = t // EXPERT_TILE + N_BUCKETS
    n_slots = n_tiles_max * EXPERT_TILE

    mod = _ada(c, w_ada, b_ada)
    rw_pad = jnp.pad(router_w, ((0, 0), (0, 128 - N_EXPERTS)))
    rw_hi = rw_pad.astype(_bf16)
    rw_lo = (rw_pad - rw_hi.astype(_f32)).astype(_bf16)
    router_w_pad = jnp.concatenate([rw_hi, rw_lo], axis=1)
    router_wt = router_w.T.reshape(N_EXPERTS, 1, d)
    final_g = final_norm_g.reshape(1, d)

    y_tok, gate_prev = None, None
    for l in range(depth):
        mod_l = mod[l].reshape(batch, 1, N_MOD * d)
        x, hf, route, counts = _mixer(
            x, y_tok, gate_prev, mod_l, norm_mix_g[l].reshape(1, d), norm_ffn_g[l].reshape(1, d),
            w_in[l].astype(_bf16), v_norm_g[l].reshape(1, SGU_WIDTH),
            sgu_w[l].astype(_bf16), sgu_b[l].T,
            pool_w[l].astype(_bf16), pool_scale[l].reshape(1, POOL_WIDTH),
            w_branch_a[l].astype(_bf16), w_branch_b[l].astype(_bf16), w_out[l].astype(_bf16),
            router_w_pad, router_bias)
        pos, tile_ea, tile_eb, n_used = _routing_tables(route, counts, n_tiles_max)
        xs = _sc_gather_by_slot(pos, hf.reshape((t,) + TOKEN_SLAB), n_slots)
        ys = _experts(tile_ea, tile_eb, n_used, xs, router_wt,
                      w_exp_gate[l].astype(_bf16), w_exp_up[l].astype(_bf16),
                      w_exp_down[l].astype(_bf16))
        y_tok = _sc_gather_rows(pos, ys).reshape((batch, seq) + TOKEN_SLAB)
        gate_prev = mod_l[:, :, 5 * d:6 * d]
    return _final_norm(x, y_tok, gate_prev, final_g)
```

```python
import functools

import jax
import jax.numpy as jnp
from jax import lax
from jax.experimental import pallas as pl
from jax.experimental.pallas import tpu as pltpu
from jax.experimental.pallas import tpu_sc as plsc

D_MODEL = 1024
CHUNK = 128
SGU_HEADS = 8
SGU_HEAD_DIM = 128
SGU_WIDTH = 1024
POOL_WINDOWS = (2, 4, 8, 16)
POOL_GROUP_DIM = 256
POOL_WIDTH = 1024
HALO = 16
N_EXPERTS = 16
N_EXPERT_GROUPS = 4
EXPERTS_PER_GROUP = 4
PAIRS_PER_GROUP = 6
N_BUCKETS = N_EXPERT_GROUPS * PAIRS_PER_GROUP
BUCKET_ROWS = 32
D_EXPERT = 512
N_MOD = 6
EPS = 1e-6

SEQ_TILE = 512
SUB_TILE = 256
MIXER_STAGES = 2
ROW_TILE = 512
EXPERT_TILE = 256
VMEM_LIMIT_BYTES = 58 * 1024 * 1024
TOKEN_SLAB = (8, 128)
SC_CORES = 2
SC_SUBCORES = 16
SC_CHUNK = 32
SC_LANES = 16
SC_RING = 3

_PAIR_A = (0, 0, 0, 1, 1, 3)
_PAIR_B = (1, 2, 3, 3, 2, 2)

_bf16 = jnp.bfloat16
_f32 = jnp.float32


def _dot(a, b):
    return jnp.dot(a, b, preferred_element_type=_f32)


def _rms_modulate(x, g, shift, scale):
    y = x * lax.rsqrt(jnp.mean(x * x, axis=-1, keepdims=True) + EPS)
    return (y * g) * (1.0 + scale) + shift


def _ada_kernel(c_ref, w_ref, b_ref, o_ref):
    c = c_ref[...]
    c_act = (c * jax.nn.sigmoid(c)).astype(_bf16)
    o_ref[...] = _dot(c_act, w_ref[...].astype(_bf16)) + b_ref[...]


def _ada(c, w_ada, b_ada):
    depth, d, n = w_ada.shape
    batch = c.shape[0]
    tn = 2048
    return pl.pallas_call(
        _ada_kernel,
        grid=(depth, n // tn),
        in_specs=[
            pl.BlockSpec((batch, d), lambda l, j: (0, 0)),
            pl.BlockSpec((None, d, tn), lambda l, j: (l, 0, j)),
            pl.BlockSpec((None, 1, tn), lambda l, j: (l, 0, j)),
        ],
        out_specs=pl.BlockSpec((None, batch, tn), lambda l, j: (l, 0, j)),
        out_shape=jax.ShapeDtypeStruct((depth, batch, n), _f32),
        compiler_params=pltpu.CompilerParams(
            dimension_semantics=("arbitrary", "arbitrary"), vmem_limit_bytes=VMEM_LIMIT_BYTES),
        name="ada_modulation",
    )(c, w_ada, b_ada.reshape(depth, 1, n))


def _route(hf, rw_ref, rb_ref, carry_ref):
    ts = hf.shape[0]
    hf_hi = hf.astype(_bf16)
    hf_lo = (hf - hf_hi.astype(_f32)).astype(_bf16)
    both = _dot(hf_hi, rw_ref[...])
    logits = both[:, 0:128] + both[:, 128:256] + _dot(hf_lo, rw_ref[:, 0:128])
    lt = logits.T
    rows = [lt[e:e + 1, :] for e in range(N_EXPERTS)]
    m = functools.reduce(jnp.maximum, rows)
    ex = [jnp.exp(r - m) for r in rows]
    den = functools.reduce(lambda a, b: a + b, ex)
    probs = [e / den for e in ex]
    sel = [probs[e] + rb_ref[e] for e in range(N_EXPERTS)]

    def top2_sum(v):
        pairs = [v[i] + v[j] for i, j in zip(_PAIR_A, _PAIR_B)]
        return functools.reduce(jnp.maximum, pairs)

    gscore = [top2_sum(sel[4 * g:4 * g + 4]) for g in range(N_EXPERT_GROUPS)]
    best = gscore[0]
    gidx = jnp.zeros_like(best, dtype=jnp.int32)
    for g in range(1, N_EXPERT_GROUPS):
        better = gscore[g] > best
        best = jnp.where(better, gscore[g], best)
        gidx = jnp.where(better, g, gidx)
    ing = []
    for k in range(EXPERTS_PER_GROUP):
        v = sel[k]
        for g in range(1, N_EXPERT_GROUPS):
            v = jnp.where(gidx == g, sel[4 * g + k], v)
        ing.append(v)
    chosen = []
    for k in range(EXPERTS_PER_GROUP):
        r = jnp.zeros_like(gidx)
        for j in range(EXPERTS_PER_GROUP):
            if j == k:
                continue
            beats = (ing[j] >= ing[k]) if j < k else (ing[j] > ing[k])
            r = r + beats.astype(jnp.int32)
        chosen.append(r < 2)
    lo = jnp.where(chosen[0], 0, jnp.where(chosen[1], 1, 2))
    hi = jnp.where(chosen[3], 3, jnp.where(chosen[2], 2, 1))
    pair = jnp.where(lo == 0, hi - 1, jnp.where(lo == 2, 5, jnp.where(hi == 3, 3, 4)))
    bucket = gidx * PAIRS_PER_GROUP + pair

    brow = lax.broadcasted_iota(jnp.int32, (BUCKET_ROWS, ts), 0)
    onehot = (brow == bucket).astype(_f32)
    jj = lax.broadcasted_iota(jnp.int32, (ts, ts), 0)
    tt = lax.broadcasted_iota(jnp.int32, (ts, ts), 1)
    upper = (jj <= tt).astype(_bf16)
    cum = _dot(onehot.astype(_bf16), upper)
    carry = carry_ref[...][:, 0:1]
    rank = jnp.sum(onehot * (cum - 1.0 + carry), axis=0, keepdims=True)
    carry_ref[...] = carry_ref[...] + jnp.sum(onehot, axis=1, keepdims=True)
    return bucket.astype(_f32), rank


def _mixer_kernel(*refs, has_prev):
    if has_prev:
        x_ref, y_ref, gprev_ref = refs[:3]
        refs = refs[3:]
        x_in = lambda rows: x_ref[rows, :] + gprev_ref[...] * y_ref[rows].reshape(SUB_TILE, D_MODEL)
    else:
        x_ref = refs[0]
        refs = refs[1:]
        x_in = lambda rows: x_ref[rows, :]
    (mod_ref, gmix_ref, gffn_ref, win_ref, vg_ref, sw_ref, sb_ref, pw_ref, ps_ref, wa_ref, wb_ref,
     wo_ref, rw_ref, rb_ref, xo_ref, hf_ref, route_ref, counts_ref, ext_ref, carry_ref) = refs
    b = pl.program_id(0)
    s = pl.program_id(1)
    ts = x_ref.shape[0]
    d = D_MODEL

    @pl.when((b == 0) & (s == 0))
    def _():
        carry_ref[...] = jnp.zeros_like(carry_ref)

    @pl.when(s == 0)
    def _():
        ext_ref[0:HALO, :] = jnp.zeros((HALO, POOL_WIDTH), _f32)

    mod = mod_ref[...]
    ci = lax.broadcasted_iota(jnp.int32, (CHUNK, CHUNK), 0)
    cj = lax.broadcasted_iota(jnp.int32, (CHUNK, CHUNK), 1)
    sgu_w = [jnp.where(ci >= cj, sw_ref[h], jnp.zeros((), _bf16)) for h in range(SGU_HEADS)]
    sub_tiles = [_mixer_rows(r * SUB_TILE, s * ts + r * SUB_TILE, mod, sgu_w,
                             x_in, gmix_ref, gffn_ref, win_ref, vg_ref, sb_ref, pw_ref, ps_ref, wa_ref,
                             wb_ref, wo_ref, rw_ref, rb_ref, xo_ref, hf_ref, route_ref, ext_ref, carry_ref)
                 for r in range(ts // SUB_TILE)]
    for _ in range(MIXER_STAGES):
        for sub_tile in sub_tiles:
            next(sub_tile, None)
    ext_ref[0:HALO, :] = ext_ref[ts:ts + HALO, :]
    route_ref[2:8, :] = jnp.zeros((6, ts), _f32)
    counts_ref[...] = carry_ref[...]


def _mixer_rows(row0, seq_pos0, mod, sgu_w, x_in, gmix_ref, gffn_ref, win_ref, vg_ref, sb_ref,
                pw_ref, ps_ref, wa_ref, wb_ref, wo_ref, rw_ref, rb_ref, xo_ref, hf_ref, route_ref,
                ext_ref, carry_ref):
    d = D_MODEL
    ts = SUB_TILE
    rows = slice(row0, row0 + ts)
    sh_m, sc_m, g_m = mod[:, 0:d], mod[:, d:2 * d], mod[:, 2 * d:3 * d]
    sh_f, sc_f = mod[:, 3 * d:4 * d], mod[:, 4 * d:5 * d]
    x = x_in(rows)

    hb = _rms_modulate(x, gmix_ref[...], sh_m, sc_m).astype(_bf16)

    v = jax.nn.gelu(_dot(hb, win_ref[:, SGU_WIDTH:2 * SGU_WIDTH]))
    p = _dot(hb, win_ref[:, 2 * SGU_WIDTH:2 * SGU_WIDTH + POOL_WIDTH])
    u = jax.nn.gelu(_dot(hb, win_ref[:, 0:SGU_WIDTH]))
    gate_a = jax.nn.sigmoid(_dot(hb, win_ref[:, 3 * d:4 * d]))
    gate_b = jax.nn.sigmoid(_dot(hb, win_ref[:, 4 * d:5 * d]))
    e0 = HALO + row0
    ext_ref[e0:e0 + ts, :] = p
    yield

    vc = v - jnp.mean(v, axis=-1, keepdims=True)
    vn = (vc * lax.rsqrt(jnp.mean(vc * vc, axis=-1, keepdims=True) + EPS) * vg_ref[...]).astype(_bf16)
    n_chunks = ts // CHUNK
    ya_cols = []
    for h in range(SGU_HEADS):
        cols = slice(h * SGU_HEAD_DIM, (h + 1) * SGU_HEAD_DIM)
        rhs = jnp.concatenate([vn[n * CHUNK:(n + 1) * CHUNK, cols] for n in range(n_chunks)], axis=1)
        sg = _dot(sgu_w[h], rhs) + sb_ref[:, h:h + 1]
        s_h = jnp.concatenate([sg[:, n * SGU_HEAD_DIM:(n + 1) * SGU_HEAD_DIM] for n in range(n_chunks)],
                              axis=0)
        ya_cols.append((u[:, cols] * s_h).astype(_bf16))
    ya = jnp.concatenate(ya_cols, axis=1)
    merged = gate_a * _dot(ya, wa_ref[...])

    pos1 = (seq_pos0 + 1 + lax.broadcasted_iota(jnp.int32, (ts, 1), 0)).astype(_f32)
    yb_cols = []
    for gi, w in enumerate(POOL_WINDOWS):
        cols = slice(gi * POOL_GROUP_DIM, (gi + 1) * POOL_GROUP_DIM)
        acc = p[:, cols]
        for k in range(1, w):
            acc = acc + ext_ref[e0 - k:e0 - k + ts, cols]
        count = jnp.minimum(pos1, float(w))
        pooled = (acc / count - p[:, cols]).astype(_bf16)
        yb_cols.append(_dot(pooled, pw_ref[gi]))
    yb = (jnp.concatenate(yb_cols, axis=1) * ps_ref[...]).astype(_bf16)
    merged = merged + gate_b * _dot(yb, wb_ref[...])

    x_new = x + g_m * _dot(merged.astype(_bf16), wo_ref[...])
    xo_ref[rows, :] = x_new

    hf = _rms_modulate(x_new, gffn_ref[...], sh_f, sc_f)
    hf_ref[rows] = hf.reshape((ts,) + TOKEN_SLAB)
    bucket, rank = _route(hf, rw_ref, rb_ref, carry_ref)
    route_ref[0:1, rows] = bucket
    route_ref[1:2, rows] = rank
    yield


def _mixer(x, y_prev, gate_prev, mod_l, gmix, gffn, w_in, v_g, sgu_w, sgu_bt, pool_w, pool_scale,
           w_a, w_b, w_o, router_w_pad, router_bias):
    batch, seq, d = x.shape
    has_prev = y_prev is not None
    ts = SEQ_TILE
    n_tiles = batch * (seq // ts)
    tiles_per_seq = seq // ts
    const = lambda *shape: pl.BlockSpec(shape, lambda b, s: (0,) * len(shape),
                                        pipeline_mode=pl.Buffered(1))
    row_spec = pl.BlockSpec((None, ts, d), lambda b, s: (b, s, 0))
    slab_spec = pl.BlockSpec((None, ts) + TOKEN_SLAB, lambda b, s: (b, s, 0, 0))
    prev_specs = [slab_spec, pl.BlockSpec((None, 1, d), lambda b, s: (b, 0, 0))] if has_prev else []
    prev_args = (y_prev, gate_prev) if has_prev else ()
    return pl.pallas_call(
        functools.partial(_mixer_kernel, has_prev=has_prev),
        grid=(batch, tiles_per_seq),
        in_specs=[row_spec] + prev_specs + [
            pl.BlockSpec((None, 1, N_MOD * d), lambda b, s: (b, 0, 0)),
            const(1, d), const(1, d),
            const(*w_in.shape),
            const(1, SGU_WIDTH),
            const(*sgu_w.shape),
            const(*sgu_bt.shape),
            const(*pool_w.shape),
            const(1, POOL_WIDTH),
            const(d, d), const(d, d), const(d, d),
            const(*router_w_pad.shape),
            pl.BlockSpec(memory_space=pltpu.SMEM),
        ],
        out_specs=[
            pl.BlockSpec((None, ts, d), lambda b, s: (b, s, 0)),
            pl.BlockSpec((None, ts) + TOKEN_SLAB, lambda b, s: (b, s, 0, 0)),
            pl.BlockSpec((None, 8, ts), lambda b, s: (b * tiles_per_seq + s, 0, 0)),
            pl.BlockSpec((BUCKET_ROWS, 128), lambda b, s: (0, 0)),
        ],
        out_shape=[
            jax.ShapeDtypeStruct((batch, seq, d), _f32),
            jax.ShapeDtypeStruct((batch, seq) + TOKEN_SLAB, _f32),
            jax.ShapeDtypeStruct((n_tiles, 8, ts), _f32),
            jax.ShapeDtypeStruct((BUCKET_ROWS, 128), _f32),
        ],
        scratch_shapes=[
            pltpu.VMEM((HALO + ts, POOL_WIDTH), _f32),
            pltpu.VMEM((BUCKET_ROWS, 128), _f32),
        ],
        compiler_params=pltpu.CompilerParams(
            dimension_semantics=("arbitrary", "arbitrary"), vmem_limit_bytes=VMEM_LIMIT_BYTES),
        name="mixer_router",
    )(x, *prev_args, mod_l, gmix, gffn, w_in, v_g, sgu_w, sgu_bt, pool_w, pool_scale, w_a, w_b, w_o,
      router_w_pad, router_bias)


def _sc_mesh():
    return plsc.VectorSubcoreMesh(core_axis_name="c", subcore_axis_name="s")


def _sc_worker():
    return lax.axis_index("s") * SC_CORES + lax.axis_index("c")


def _sc_gather_chunks(table_hbm, idx_v, out_hbm, out_row0, n_chunks, bufs, sems):
    depth = len(bufs)

    def gather(j, slot):
        return pltpu.make_async_copy(table_hbm.at[idx_v.at[j]], bufs[slot], sems[slot])

    for j in range(min(depth - 1, n_chunks)):
        gather(j, j).start()

    @pl.loop(0, n_chunks, step=depth)
    def _(j0):
        for slot in range(depth):
            j = j0 + slot

            @pl.when(j < n_chunks)
            def _():
                @pl.when(j + depth - 1 < n_chunks)
                def _():
                    gather(j + depth - 1, (slot + depth - 1) % depth).start()

                gather(j, slot).wait()
                pltpu.sync_copy(bufs[slot], out_hbm.at[pl.ds(out_row0 + j * SC_CHUNK, SC_CHUNK)])


def _sc_chunks_per_worker(n_rows):
    n_chunks = n_rows // (SC_CORES * SC_SUBCORES * SC_CHUNK)
    assert n_chunks * SC_CORES * SC_SUBCORES * SC_CHUNK == n_rows
    return n_chunks


def _sc_gather_rows(idx, table):
    n_out = idx.shape[0]
    n_chunks = _sc_chunks_per_worker(n_out)
    buf = pltpu.VMEM((SC_CHUNK,) + TOKEN_SLAB, _f32)

    @functools.partial(
        pl.kernel, mesh=_sc_mesh(),
        out_type=jax.ShapeDtypeStruct((n_out,) + TOKEN_SLAB, _f32),
        scratch_types=[pltpu.VMEM((n_chunks, SC_CHUNK), jnp.int32)]
        + [buf] * SC_RING + [pltpu.SemaphoreType.DMA] * SC_RING,
    )
    def gather_kernel(table_hbm, idx_hbm, out_hbm, idx_v, *ring):
        chunk0 = _sc_worker() * n_chunks
        pltpu.sync_copy(idx_hbm.at[pl.ds(chunk0, n_chunks)], idx_v)
        _sc_gather_chunks(table_hbm, idx_v, out_hbm, chunk0 * SC_CHUNK, n_chunks,
                          ring[:SC_RING], ring[SC_RING:])

    return gather_kernel(table, idx.reshape(n_out // SC_CHUNK, SC_CHUNK))


def _sc_gather_by_slot(pos, table, n_slots):
    n_tok = pos.shape[0]
    per_worker = n_slots // (SC_CORES * SC_SUBCORES)
    n_chunks = _sc_chunks_per_worker(n_slots)
    buf = pltpu.VMEM((SC_CHUNK,) + TOKEN_SLAB, _f32)

    @functools.partial(
        pl.kernel, mesh=_sc_mesh(),
        compiler_params=pltpu.CompilerParams(needs_layout_passes=False),
        out_type=jax.ShapeDtypeStruct((n_slots,) + TOKEN_SLAB, _f32),
        scratch_types=[pltpu.VMEM((n_tok,), jnp.int32), pltpu.VMEM((n_chunks, SC_CHUNK), jnp.int32)]
        + [buf] * SC_RING + [pltpu.SemaphoreType.DMA] * SC_RING,
    )
    def slot_kernel(table_hbm, pos_hbm, out_hbm, pos_v, inv_v, *ring):
        slot0 = _sc_worker() * per_worker
        pltpu.sync_copy(pos_hbm, pos_v)
        lane = lax.iota(jnp.int32, SC_LANES)

        @pl.loop(0, n_chunks)
        def _(j):
            for h in range(SC_CHUNK // SC_LANES):
                s = slot0 + j * SC_CHUNK + h * SC_LANES + lane
                inv_v[j, pl.ds(h * SC_LANES, SC_LANES)] = jnp.where(s >= n_tok, s - n_tok, s)

        @pl.loop(0, n_tok // SC_LANES)
        def _(i):
            local = pos_v[pl.ds(i * SC_LANES, SC_LANES)] - slot0
            mine = (local >= 0) & (local < per_worker)
            local = jnp.where(mine, local, 0)
            plsc.store_scatter(inv_v, [local // SC_CHUNK, local % SC_CHUNK], i * SC_LANES + lane, mask=mine)

        _sc_gather_chunks(table_hbm, inv_v, out_hbm, slot0, n_chunks, ring[:SC_RING], ring[SC_RING:])

    return slot_kernel(table, pos)


def _final_kernel(x_ref, y_ref, g_ref, fg_ref, o_ref):
    out = x_ref[...] + g_ref[...] * y_ref[...].reshape(x_ref.shape)
    o_ref[...] = out * lax.rsqrt(jnp.mean(out * out, axis=-1, keepdims=True) + EPS) * fg_ref[...]


def _final_norm(x, y_tok, gate_f, final_g):
    batch, seq, d = x.shape
    row_spec = pl.BlockSpec((None, ROW_TILE, d), lambda b, s: (b, s, 0))
    return pl.pallas_call(
        _final_kernel,
        grid=(batch, seq // ROW_TILE),
        in_specs=[row_spec, pl.BlockSpec((None, ROW_TILE) + TOKEN_SLAB, lambda b, s: (b, s, 0, 0)),
                  pl.BlockSpec((None, 1, d), lambda b, s: (b, 0, 0)),
                  pl.BlockSpec((1, d), lambda b, s: (0, 0))],
        out_specs=row_spec,
        out_shape=jax.ShapeDtypeStruct((batch, seq, d), _f32),
        compiler_params=pltpu.CompilerParams(dimension_semantics=("arbitrary", "arbitrary")),
        name="residual_final_norm",
    )(x, y_tok, gate_f, final_g)


def _experts_kernel(ea_ref, eb_ref, nused_ref, xs_ref, rwa_ref, rwb_ref,
                    wga_ref, wgb_ref, wua_ref, wub_ref, wda_ref, wdb_ref, ys_ref):
    i = pl.program_id(0)

    @pl.when(i < nused_ref[0])
    def _():
        x = xs_ref[...].reshape(EXPERT_TILE, D_MODEL)
        la = jnp.sum(x * rwa_ref[...], axis=-1, keepdims=True)
        lb = jnp.sum(x * rwb_ref[...], axis=-1, keepdims=True)
        wa = jax.nn.sigmoid(la - lb)
        wb = jax.nn.sigmoid(lb - la)
        xb = x.astype(_bf16)
        act_a = (jax.nn.silu(_dot(xb, wga_ref[...])) * _dot(xb, wua_ref[...]) * wa).astype(_bf16)
        act_b = (jax.nn.silu(_dot(xb, wgb_ref[...])) * _dot(xb, wub_ref[...]) * wb).astype(_bf16)
        y = _dot(act_a, wda_ref[...]) + _dot(act_b, wdb_ref[...])
        ys_ref[...] = y.reshape(ys_ref.shape)

    @pl.when(i >= nused_ref[0])
    def _():
        ys_ref[...] = jnp.zeros_like(ys_ref)


def _experts(tile_ea, tile_eb, n_used, xs, router_wt, w_gate, w_up, w_down):
    n_slots = xs.shape[0]
    d = D_MODEL
    n_tiles = n_slots // EXPERT_TILE
    f = D_EXPERT

    def row(i, ea, eb, nu):
        return (jnp.maximum(jnp.minimum(i, nu[0] - 1), 0), 0, 0)

    grid_spec = pltpu.PrefetchScalarGridSpec(
        num_scalar_prefetch=3,
        grid=(n_tiles,),
        in_specs=[
            pl.BlockSpec((EXPERT_TILE,) + TOKEN_SLAB, row),
            pl.BlockSpec((None, 1, d), lambda i, ea, eb, nu: (ea[i], 0, 0)),
            pl.BlockSpec((None, 1, d), lambda i, ea, eb, nu: (eb[i], 0, 0)),
            pl.BlockSpec((None, d, f), lambda i, ea, eb, nu: (ea[i], 0, 0)),
            pl.BlockSpec((None, d, f), lambda i, ea, eb, nu: (eb[i], 0, 0)),
            pl.BlockSpec((None, d, f), lambda i, ea, eb, nu: (ea[i], 0, 0)),
            pl.BlockSpec((None, d, f), lambda i, ea, eb, nu: (eb[i], 0, 0)),
            pl.BlockSpec((None, f, d), lambda i, ea, eb, nu: (ea[i], 0, 0)),
            pl.BlockSpec((None, f, d), lambda i, ea, eb, nu: (eb[i], 0, 0)),
        ],
        out_specs=pl.BlockSpec((EXPERT_TILE,) + TOKEN_SLAB, lambda i, ea, eb, nu: (i, 0, 0)),
    )
    return pl.pallas_call(
        _experts_kernel,
        grid_spec=grid_spec,
        out_shape=jax.ShapeDtypeStruct((n_slots,) + TOKEN_SLAB, _f32),
        compiler_params=pltpu.CompilerParams(
            dimension_semantics=("arbitrary",), vmem_limit_bytes=VMEM_LIMIT_BYTES),
        name="grouped_experts",
    )(tile_ea, tile_eb, n_used, xs, router_wt, router_wt, w_gate, w_gate, w_up, w_up, w_down, w_down)


def _routing_tables(route, counts, n_tiles_max):
    bucket = route[:, 0, :].reshape(-1).astype(jnp.int32)
    rank = route[:, 1, :].reshape(-1).astype(jnp.int32)
    cnt = counts[:N_BUCKETS, 0].astype(jnp.int32)
    tiles_b = (cnt + EXPERT_TILE - 1) // EXPERT_TILE
    tile_end = jnp.cumsum(tiles_b)
    tile_start = tile_end - tiles_b
    n_used = tile_end[-1]
    onehot = bucket[:, None] == jnp.arange(N_BUCKETS, dtype=jnp.int32)[None, :]
    pos = jnp.sum(jnp.where(onehot, (tile_start * EXPERT_TILE)[None, :], 0), axis=1) + rank
    tile_ids = jnp.minimum(jnp.arange(n_tiles_max, dtype=jnp.int32), n_used - 1)
    tile_bucket = jnp.sum((tile_ids[:, None] >= tile_end[None, :]).astype(jnp.int32), axis=1)
    group = tile_bucket // PAIRS_PER_GROUP
    pair = tile_bucket % PAIRS_PER_GROUP
    tile_ea = group * EXPERTS_PER_GROUP + jnp.asarray(_PAIR_A, jnp.int32)[pair]
    tile_eb = group * EXPERTS_PER_GROUP + jnp.asarray(_PAIR_B, jnp.int32)[pair]
    return pos, tile_ea, tile_eb, n_used.reshape(1)


def kernel(x, c, w_ada, b_ada, norm_mix_g, w_in, v_norm_g, sgu_w, sgu_b, pool_w, pool_scale,
           w_branch_a, w_branch_b, w_out, norm_ffn_g, router_w, router_bias,
           w_exp_gate, w_exp_up, w_exp_down, final_norm_g):
    batch, seq, d = x.shape
    depth = w_ada.shape[0]
    t = batch * seq
    n_tiles_max = t // EXPERT_TILE + N_BUCKETS
    n_slots = n_tiles_max * EXPERT_TILE

    mod = _ada(c, w_ada, b_ada)
    rw_pad = jnp.pad(router_w, ((0, 0), (0, 128 - N_EXPERTS)))
    rw_hi = rw_pad.astype(_bf16)
    rw_lo = (rw_pad - rw_hi.astype(_f32)).astype(_bf16)
    router_w_pad = jnp.concatenate([rw_hi, rw_lo], axis=1)
    router_wt = router_w.T.reshape(N_EXPERTS, 1, d)
    final_g = final_norm_g.reshape(1, d)

    def to_bf16(weights, after):
        if after is not None:
            weights, _ = lax.optimization_barrier((weights, after))
        return tuple(w.astype(_bf16) for w in weights)

    def mixer_weights(l, after):
        return to_bf16((w_in[l], sgu_w[l], pool_w[l], w_branch_a[l], w_branch_b[l], w_out[l]), after)

    def expert_weights(l, after):
        return to_bf16((w_exp_gate[l], w_exp_up[l], w_exp_down[l]), after)

    y_tok, gate_prev = None, None
    mixer_w, expert_w = mixer_weights(0, None), expert_weights(0, None)
    for l in range(depth):
        mod_l = mod[l].reshape(batch, 1, N_MOD * d)
        w_in_l, sgu_w_l, pool_w_l, w_a_l, w_b_l, w_o_l = mixer_w
        x, hf, route, counts = _mixer(
            x, y_tok, gate_prev, mod_l, norm_mix_g[l].reshape(1, d), norm_ffn_g[l].reshape(1, d),
            w_in_l, v_norm_g[l].reshape(1, SGU_WIDTH), sgu_w_l, sgu_b[l].T,
            pool_w_l, pool_scale[l].reshape(1, POOL_WIDTH), w_a_l, w_b_l, w_o_l,
            router_w_pad, router_bias)
        pos, tile_ea, tile_eb, n_used = _routing_tables(route, counts, n_tiles_max)
        xs = _sc_gather_by_slot(pos, hf.reshape((t,) + TOKEN_SLAB), n_slots)
        if l + 1 < depth:
            mixer_w = mixer_weights(l + 1, pos)
        ys = _experts(tile_ea, tile_eb, n_used, xs, router_wt, *expert_w)
        y_tok = _sc_gather_rows(pos, ys).reshape((batch, seq) + TOKEN_SLAB)
        if l + 1 < depth:
            expert_w = expert_weights(l + 1, ys)
        gate_prev = mod_l[:, :, 5 * d:6 * d]
    return _final_norm(x, y_tok, gate_prev, final_g)
```

```python
import functools

import jax
import jax.numpy as jnp
from jax import lax
from jax.experimental import pallas as pl
from jax.experimental.pallas import tpu as pltpu
from jax.experimental.pallas import tpu_sc as plsc

D_MODEL = 1024
CHUNK = 128
SGU_HEADS = 8
SGU_HEAD_DIM = 128
SGU_WIDTH = 1024
POOL_WINDOWS = (2, 4, 8, 16)
POOL_GROUP_DIM = 256
POOL_WIDTH = 1024
HALO = 16
N_EXPERTS = 16
N_EXPERT_GROUPS = 4
EXPERTS_PER_GROUP = 4
PAIRS_PER_GROUP = 6
N_BUCKETS = N_EXPERT_GROUPS * PAIRS_PER_GROUP
BUCKET_ROWS = 32
D_EXPERT = 512
N_MOD = 6
EPS = 1e-6

SEQ_TILE = 512
SUB_TILE = 256
MIXER_STAGES = 2
ROW_TILE = 512
EXPERT_TILE = 256
VMEM_LIMIT_BYTES = 58 * 1024 * 1024
TOKEN_SLAB = (8, 128)
SC_CORES = 2
SC_SUBCORES = 16
SC_CHUNK = 32
SC_LANES = 16
SC_RING = 3

_PAIR_A = (0, 0, 0, 1, 1, 3)
_PAIR_B = (1, 2, 3, 3, 2, 2)

_bf16 = jnp.bfloat16
_f32 = jnp.float32


def _dot(a, b):
    return jnp.dot(a, b, preferred_element_type=_f32)


def _rms_modulate(x, g, shift, scale):
    y = x * lax.rsqrt(jnp.mean(x * x, axis=-1, keepdims=True) + EPS)
    return (y * g) * (1.0 + scale) + shift


def _ada_kernel(c_ref, w_ref, b_ref, o_ref):
    c = c_ref[...]
    c_act = (c * jax.nn.sigmoid(c)).astype(_bf16)
    o_ref[...] = _dot(c_act, w_ref[...].astype(_bf16)) + b_ref[...]


def _ada(c, w_ada, b_ada):
    depth, d, n = w_ada.shape
    batch = c.shape[0]
    tn = 2048
    return pl.pallas_call(
        _ada_kernel,
        grid=(depth, n // tn),
        in_specs=[
            pl.BlockSpec((batch, d), lambda l, j: (0, 0)),
            pl.BlockSpec((None, d, tn), lambda l, j: (l, 0, j)),
            pl.BlockSpec((None, 1, tn), lambda l, j: (l, 0, j)),
        ],
        out_specs=pl.BlockSpec((None, batch, tn), lambda l, j: (l, 0, j)),
        out_shape=jax.ShapeDtypeStruct((depth, batch, n), _f32),
        compiler_params=pltpu.CompilerParams(
            dimension_semantics=("arbitrary", "arbitrary"), vmem_limit_bytes=VMEM_LIMIT_BYTES),
        name="ada_modulation",
    )(c, w_ada, b_ada.reshape(depth, 1, n))


def _route(hf, rw_ref, rb_ref, carry_ref):
    ts = hf.shape[0]
    hf_hi = hf.astype(_bf16)
    hf_lo = (hf - hf_hi.astype(_f32)).astype(_bf16)
    both = _dot(hf_hi, rw_ref[...])
    logits = both[:, 0:128] + both[:, 128:256] + _dot(hf_lo, rw_ref[:, 0:128])
    lt = logits.T
    rows = [lt[e:e + 1, :] for e in range(N_EXPERTS)]
    m = functools.reduce(jnp.maximum, rows)
    ex = [jnp.exp(r - m) for r in rows]
    den = functools.reduce(lambda a, b: a + b, ex)
    probs = [e / den for e in ex]
    sel = [probs[e] + rb_ref[e] for e in range(N_EXPERTS)]

    def top2_sum(v):
        pairs = [v[i] + v[j] for i, j in zip(_PAIR_A, _PAIR_B)]
        return functools.reduce(jnp.maximum, pairs)

    gscore = [top2_sum(sel[4 * g:4 * g + 4]) for g in range(N_EXPERT_GROUPS)]
    best = gscore[0]
    gidx = jnp.zeros_like(best, dtype=jnp.int32)
    for g in range(1, N_EXPERT_GROUPS):
        better = gscore[g] > best
        best = jnp.where(better, gscore[g], best)
        gidx = jnp.where(better, g, gidx)
    ing = []
    for k in range(EXPERTS_PER_GROUP):
        v = sel[k]
        for g in range(1, N_EXPERT_GROUPS):
            v = jnp.where(gidx == g, sel[4 * g + k], v)
        ing.append(v)
    chosen = []
    for k in range(EXPERTS_PER_GROUP):
        r = jnp.zeros_like(gidx)
        for j in range(EXPERTS_PER_GROUP):
            if j == k:
                continue
            beats = (ing[j] >= ing[k]) if j < k else (ing[j] > ing[k])
            r = r + beats.astype(jnp.int32)
        chosen.append(r < 2)
    lo = jnp.where(chosen[0], 0, jnp.where(chosen[1], 1, 2))
    hi = jnp.where(chosen[3], 3, jnp.where(chosen[2], 2, 1))
    pair = jnp.where(lo == 0, hi - 1, jnp.where(lo == 2, 5, jnp.where(hi == 3, 3, 4)))
    bucket = gidx * PAIRS_PER_GROUP + pair

    brow = lax.broadcasted_iota(jnp.int32, (BUCKET_ROWS, ts), 0)
    onehot = (brow == bucket).astype(_f32)
    jj = lax.broadcasted_iota(jnp.int32, (ts, ts), 0)
    tt = lax.broadcasted_iota(jnp.int32, (ts, ts), 1)
    upper = (jj <= tt).astype(_bf16)
    cum = _dot(onehot.astype(_bf16), upper)
    carry = carry_ref[...][:, 0:1]
    rank = jnp.sum(onehot * (cum - 1.0 + carry), axis=0, keepdims=True)
    carry_ref[...] = carry_ref[...] + jnp.sum(onehot, axis=1, keepdims=True)
    return bucket.astype(_f32), rank


def _mixer_kernel(*refs, has_prev):
    if has_prev:
        x_ref, y_ref, gprev_ref = refs[:3]
        refs = refs[3:]
        x_in = lambda rows: x_ref[rows, :] + gprev_ref[...] * y_ref[rows].reshape(SUB_TILE, D_MODEL)
    else:
        x_ref = refs[0]
        refs = refs[1:]
        x_in = lambda rows: x_ref[rows, :]
    (mod_ref, gmix_ref, gffn_ref, win_ref, vg_ref, sw_ref, sb_ref, pw_ref, ps_ref, wa_ref, wb_ref,
     wo_ref, rw_ref, rb_ref, xo_ref, hf_ref, route_ref, counts_ref, ext_ref, carry_ref) = refs
    b = pl.program_id(0)
    s = pl.program_id(1)
    ts = x_ref.shape[0]
    d = D_MODEL

    @pl.when((b == 0) & (s == 0))
    def _():
        carry_ref[...] = jnp.zeros_like(carry_ref)

    @pl.when(s == 0)
    def _():
        ext_ref[0:HALO, :] = jnp.zeros((HALO, POOL_WIDTH), _f32)

    mod = mod_ref[...]
    ci = lax.broadcasted_iota(jnp.int32, (CHUNK, CHUNK), 0)
    cj = lax.broadcasted_iota(jnp.int32, (CHUNK, CHUNK), 1)
    sgu_w = [jnp.where(ci >= cj, sw_ref[h], jnp.zeros((), _bf16)) for h in range(SGU_HEADS)]
    sub_tiles = [_mixer_rows(r * SUB_TILE, s * ts + r * SUB_TILE, mod, sgu_w,
                             x_in, gmix_ref, gffn_ref, win_ref, vg_ref, sb_ref, pw_ref, ps_ref, wa_ref,
                             wb_ref, wo_ref, rw_ref, rb_ref, xo_ref, hf_ref, route_ref, ext_ref, carry_ref)
                 for r in range(ts // SUB_TILE)]
    for _ in range(MIXER_STAGES):
        for sub_tile in sub_tiles:
            next(sub_tile, None)
    ext_ref[0:HALO, :] = ext_ref[ts:ts + HALO, :]
    route_ref[2:8, :] = jnp.zeros((6, ts), _f32)
    counts_ref[...] = carry_ref[...]


def _mixer_rows(row0, seq_pos0, mod, sgu_w, x_in, gmix_ref, gffn_ref, win_ref, vg_ref, sb_ref,
                pw_ref, ps_ref, wa_ref, wb_ref, wo_ref, rw_ref, rb_ref, xo_ref, hf_ref, route_ref,
                ext_ref, carry_ref):
    d = D_MODEL
    ts = SUB_TILE
    rows = slice(row0, row0 + ts)
    sh_m, sc_m, g_m = mod[:, 0:d], mod[:, d:2 * d], mod[:, 2 * d:3 * d]
    sh_f, sc_f = mod[:, 3 * d:4 * d], mod[:, 4 * d:5 * d]
    x = x_in(rows)

    hb = _rms_modulate(x, gmix_ref[...], sh_m, sc_m).astype(_bf16)

    v = jax.nn.gelu(_dot(hb, win_ref[:, SGU_WIDTH:2 * SGU_WIDTH]))
    p = _dot(hb, win_ref[:, 2 * SGU_WIDTH:2 * SGU_WIDTH + POOL_WIDTH])
    u = jax.nn.gelu(_dot(hb, win_ref[:, 0:SGU_WIDTH]))
    gate_a = jax.nn.sigmoid(_dot(hb, win_ref[:, 3 * d:4 * d]))
    gate_b = jax.nn.sigmoid(_dot(hb, win_ref[:, 4 * d:5 * d]))
    e0 = HALO + row0
    ext_ref[e0:e0 + ts, :] = p
    yield

    vc = v - jnp.mean(v, axis=-1, keepdims=True)
    vn = (vc * lax.rsqrt(jnp.mean(vc * vc, axis=-1, keepdims=True) + EPS) * vg_ref[...]).astype(_bf16)
    n_chunks = ts // CHUNK
    ya_cols = []
    for h in range(SGU_HEADS):
        cols = slice(h * SGU_HEAD_DIM, (h + 1) * SGU_HEAD_DIM)
        rhs = jnp.concatenate([vn[n * CHUNK:(n + 1) * CHUNK, cols] for n in range(n_chunks)], axis=1)
        sg = _dot(sgu_w[h], rhs) + sb_ref[:, h:h + 1]
        s_h = jnp.concatenate([sg[:, n * SGU_HEAD_DIM:(n + 1) * SGU_HEAD_DIM] for n in range(n_chunks)],
                              axis=0)
        ya_cols.append((u[:, cols] * s_h).astype(_bf16))
    ya = jnp.concatenate(ya_cols, axis=1)
    merged = gate_a * _dot(ya, wa_ref[...])

    pos1 = (seq_pos0 + 1 + lax.broadcasted_iota(jnp.int32, (ts, 1), 0)).astype(_f32)
    yb_cols = []
    for gi, w in enumerate(POOL_WINDOWS):
        cols = slice(gi * POOL_GROUP_DIM, (gi + 1) * POOL_GROUP_DIM)
        acc = p[:, cols]
        for k in range(1, w):
            acc = acc + ext_ref[e0 - k:e0 - k + ts, cols]
        count = jnp.minimum(pos1, float(w))
        pooled = (acc / count - p[:, cols]).astype(_bf16)
        yb_cols.append(_dot(pooled, pw_ref[gi]))
    yb = (jnp.concatenate(yb_cols, axis=1) * ps_ref[...]).astype(_bf16)
    merged = merged + gate_b * _dot(yb, wb_ref[...])

    x_new = x + g_m * _dot(merged.astype(_bf16), wo_ref[...])
    xo_ref[rows, :] = x_new

    hf = _rms_modulate(x_new, gffn_ref[...], sh_f, sc_f)
    hf_ref[rows] = hf.reshape((ts,) + TOKEN_SLAB)
    bucket, rank = _route(hf, rw_ref, rb_ref, carry_ref)
    route_ref[0:1, rows] = bucket
    route_ref[1:2, rows] = rank
    yield


def _mixer(x, y_prev, gate_prev, mod_l, gmix, gffn, w_in, v_g, sgu_w, sgu_bt, pool_w, pool_scale,
           w_a, w_b, w_o, router_w_pad, router_bias):
    batch, seq, d = x.shape
    has_prev = y_prev is not None
    ts = SEQ_TILE
    n_tiles = batch * (seq // ts)
    tiles_per_seq = seq // ts
    const = lambda *shape: pl.BlockSpec(shape, lambda b, s: (0,) * len(shape),
                                        pipeline_mode=pl.Buffered(1))
    row_spec = pl.BlockSpec((None, ts, d), lambda b, s: (b, s, 0))
    slab_spec = pl.BlockSpec((None, ts) + TOKEN_SLAB, lambda b, s: (b, s, 0, 0))
    prev_specs = [slab_spec, pl.BlockSpec((None, 1, d), lambda b, s: (b, 0, 0))] if has_prev else []
    prev_args = (y_prev, gate_prev) if has_prev else ()
    return pl.pallas_call(
        functools.partial(_mixer_kernel, has_prev=has_prev),
        grid=(batch, tiles_per_seq),
        in_specs=[row_spec] + prev_specs + [
            pl.BlockSpec((None, 1, N_MOD * d), lambda b, s: (b, 0, 0)),
            const(1, d), const(1, d),
            const(*w_in.shape),
            const(1, SGU_WIDTH),
            const(*sgu_w.shape),
            const(*sgu_bt.shape),
            const(*pool_w.shape),
            const(1, POOL_WIDTH),
            const(d, d), const(d, d), const(d, d),
            const(*router_w_pad.shape),
            pl.BlockSpec(memory_space=pltpu.SMEM),
        ],
        out_specs=[
            pl.BlockSpec((None, ts, d), lambda b, s: (b, s, 0)),
            pl.BlockSpec((None, ts) + TOKEN_SLAB, lambda b, s: (b, s, 0, 0)),
            pl.BlockSpec((None, 8, ts), lambda b, s: (b * tiles_per_seq + s, 0, 0)),
            pl.BlockSpec((BUCKET_ROWS, 128), lambda b, s: (0, 0)),
        ],
        out_shape=[
            jax.ShapeDtypeStruct((batch, seq, d), _f32),
            jax.ShapeDtypeStruct((batch, seq) + TOKEN_SLAB, _f32),
            jax.ShapeDtypeStruct((n_tiles, 8, ts), _f32),
            jax.ShapeDtypeStruct((BUCKET_ROWS, 128), _f32),
        ],
        scratch_shapes=[
            pltpu.VMEM((HALO + ts, POOL_WIDTH), _f32),
            pltpu.VMEM((BUCKET_ROWS, 128), _f32),
        ],
        compiler_params=pltpu.CompilerParams(
            dimension_semantics=("arbitrary", "arbitrary"), vmem_limit_bytes=VMEM_LIMIT_BYTES),
        name="mixer_router",
    )(x, *prev_args, mod_l, gmix, gffn, w_in, v_g, sgu_w, sgu_bt, pool_w, pool_scale, w_a, w_b, w_o,
      router_w_pad, router_bias)


def _sc_mesh():
    return plsc.VectorSubcoreMesh(core_axis_name="c", subcore_axis_name="s")


def _sc_worker():
    return lax.axis_index("s") * SC_CORES + lax.axis_index("c")


def _sc_gather_chunks(table_hbm, idx_v, out_hbm, out_row0, n_chunks, bufs, sems):
    depth = len(bufs)

    def gather(j, slot):
        return pltpu.make_async_copy(table_hbm.at[idx_v.at[j]], bufs[slot], sems[slot])

    for j in range(min(depth - 1, n_chunks)):
        gather(j, j).start()

    @pl.loop(0, n_chunks, step=depth)
    def _(j0):
        for slot in range(depth):
            j = j0 + slot

            @pl.when(j < n_chunks)
            def _():
                @pl.when(j + depth - 1 < n_chunks)
                def _():
                    gather(j + depth - 1, (slot + depth - 1) % depth).start()

                gather(j, slot).wait()
                pltpu.sync_copy(bufs[slot], out_hbm.at[pl.ds(out_row0 + j * SC_CHUNK, SC_CHUNK)])


def _sc_chunks_per_worker(n_rows):
    n_chunks = n_rows // (SC_CORES * SC_SUBCORES * SC_CHUNK)
    assert n_chunks * SC_CORES * SC_SUBCORES * SC_CHUNK == n_rows
    return n_chunks


def _sc_gather_rows(idx, table):
    n_out = idx.shape[0]
    n_chunks = _sc_chunks_per_worker(n_out)
    buf = pltpu.VMEM((SC_CHUNK,) + TOKEN_SLAB, _f32)

    @functools.partial(
        pl.kernel, mesh=_sc_mesh(),
        out_type=jax.ShapeDtypeStruct((n_out,) + TOKEN_SLAB, _f32),
        scratch_types=[pltpu.VMEM((n_chunks, SC_CHUNK), jnp.int32)]
        + [buf] * SC_RING + [pltpu.SemaphoreType.DMA] * SC_RING,
    )
    def gather_kernel(table_hbm, idx_hbm, out_hbm, idx_v, *ring):
        chunk0 = _sc_worker() * n_chunks
        pltpu.sync_copy(idx_hbm.at[pl.ds(chunk0, n_chunks)], idx_v)
        _sc_gather_chunks(table_hbm, idx_v, out_hbm, chunk0 * SC_CHUNK, n_chunks,
                          ring[:SC_RING], ring[SC_RING:])

    return gather_kernel(table, idx.reshape(n_out // SC_CHUNK, SC_CHUNK))


def _sc_gather_by_slot(pos, table, n_slots):
    n_tok = pos.shape[0]
    per_worker = n_slots // (SC_CORES * SC_SUBCORES)
    n_chunks = _sc_chunks_per_worker(n_slots)
    buf = pltpu.VMEM((SC_CHUNK,) + TOKEN_SLAB, _f32)

    @functools.partial(
        pl.kernel, mesh=_sc_mesh(),
        compiler_params=pltpu.CompilerParams(needs_layout_passes=False),
        out_type=jax.ShapeDtypeStruct((n_slots,) + TOKEN_SLAB, _f32),
        scratch_types=[pltpu.VMEM((n_tok,), jnp.int32), pltpu.VMEM((n_chunks, SC_CHUNK), jnp.int32)]
        + [buf] * SC_RING + [pltpu.SemaphoreType.DMA] * SC_RING,
    )
    def slot_kernel(table_hbm, pos_hbm, out_hbm, pos_v, inv_v, *ring):
        slot0 = _sc_worker() * per_worker
        pltpu.sync_copy(pos_hbm, pos_v)
        lane = lax.iota(jnp.int32, SC_LANES)

        @pl.loop(0, n_chunks)
        def _(j):
            for h in range(SC_CHUNK // SC_LANES):
                s = slot0 + j * SC_CHUNK + h * SC_LANES + lane
                inv_v[j, pl.ds(h * SC_LANES, SC_LANES)] = jnp.where(s >= n_tok, s - n_tok, s)

        @pl.loop(0, n_tok // SC_LANES)
        def _(i):
            local = pos_v[pl.ds(i * SC_LANES, SC_LANES)] - slot0
            mine = (local >= 0) & (local < per_worker)
            local = jnp.where(mine, local, 0)
            plsc.store_scatter(inv_v, [local // SC_CHUNK, local % SC_CHUNK], i * SC_LANES + lane, mask=mine)

        _sc_gather_chunks(table_hbm, inv_v, out_hbm, slot0, n_chunks, ring[:SC_RING], ring[SC_RING:])

    return slot_kernel(table, pos)


def _final_kernel(x_ref, y_ref, g_ref, fg_ref, o_ref):
    out = x_ref[...] + g_ref[...] * y_ref[...].reshape(x_ref.shape)
    o_ref[...] = out * lax.rsqrt(jnp.mean(out * out, axis=-1, keepdims=True) + EPS) * fg_ref[...]


def _final_norm(x, y_tok, gate_f, final_g):
    batch, seq, d = x.shape
    row_spec = pl.BlockSpec((None, ROW_TILE, d), lambda b, s: (b, s, 0))
    return pl.pallas_call(
        _final_kernel,
        grid=(batch, seq // ROW_TILE),
        in_specs=[row_spec, pl.BlockSpec((None, ROW_TILE) + TOKEN_SLAB, lambda b, s: (b, s, 0, 0)),
                  pl.BlockSpec((None, 1, d), lambda b, s: (b, 0, 0)),
                  pl.BlockSpec((1, d), lambda b, s: (0, 0))],
        out_specs=row_spec,
        out_shape=jax.ShapeDtypeStruct((batch, seq, d), _f32),
        compiler_params=pltpu.CompilerParams(dimension_semantics=("arbitrary", "arbitrary")),
        name="residual_final_norm",
    )(x, y_tok, gate_f, final_g)


def _experts_kernel(ea_ref, eb_ref, nused_ref, xs_ref, rwa_ref, rwb_ref,
                    wga_ref, wgb_ref, wua_ref, wub_ref, wda_ref, wdb_ref, ys_ref):
    i = pl.program_id(0)

    @pl.when(i < nused_ref[0])
    def _():
        x = xs_ref[...].reshape(EXPERT_TILE, D_MODEL)
        la = jnp.sum(x * rwa_ref[...], axis=-1, keepdims=True)
        lb = jnp.sum(x * rwb_ref[...], axis=-1, keepdims=True)
        wa = jax.nn.sigmoid(la - lb)
        wb = jax.nn.sigmoid(lb - la)
        xb = x.astype(_bf16)
        act_a = (jax.nn.silu(_dot(xb, wga_ref[...])) * _dot(xb, wua_ref[...]) * wa).astype(_bf16)
        act_b = (jax.nn.silu(_dot(xb, wgb_ref[...])) * _dot(xb, wub_ref[...]) * wb).astype(_bf16)
        y = _dot(act_a, wda_ref[...]) + _dot(act_b, wdb_ref[...])
        ys_ref[...] = y.reshape(ys_ref.shape)

    @pl.when(i >= nused_ref[0])
    def _():
        ys_ref[...] = jnp.zeros_like(ys_ref)


def _experts(tile_ea, tile_eb, n_used, xs, router_wt, w_gate, w_up, w_down):
    n_slots = xs.shape[0]
    d = D_MODEL
    n_tiles = n_slots // EXPERT_TILE
    f = D_EXPERT

    def row(i, ea, eb, nu):
        return (jnp.maximum(jnp.minimum(i, nu[0] - 1), 0), 0, 0)

    grid_spec = pltpu.PrefetchScalarGridSpec(
        num_scalar_prefetch=3,
        grid=(n_tiles,),
        in_specs=[
            pl.BlockSpec((EXPERT_TILE,) + TOKEN_SLAB, row),
            pl.BlockSpec((None, 1, d), lambda i, ea, eb, nu: (ea[i], 0, 0)),
            pl.BlockSpec((None, 1, d), lambda i, ea, eb, nu: (eb[i], 0, 0)),
            pl.BlockSpec((None, d, f), lambda i, ea, eb, nu: (ea[i], 0, 0)),
            pl.BlockSpec((None, d, f), lambda i, ea, eb, nu: (eb[i], 0, 0)),
            pl.BlockSpec((None, d, f), lambda i, ea, eb, nu: (ea[i], 0, 0)),
            pl.BlockSpec((None, d, f), lambda i, ea, eb, nu: (eb[i], 0, 0)),
            pl.BlockSpec((None, f, d), lambda i, ea, eb, nu: (ea[i], 0, 0)),
            pl.BlockSpec((None, f, d), lambda i, ea, eb, nu: (eb[i], 0, 0)),
        ],
        out_specs=pl.BlockSpec((EXPERT_TILE,) + TOKEN_SLAB, lambda i, ea, eb, nu: (i, 0, 0)),
    )
    return pl.pallas_call(
        _experts_kernel,
        grid_spec=grid_spec,
        out_shape=jax.ShapeDtypeStruct((n_slots,) + TOKEN_SLAB, _f32),
        compiler_params=pltpu.CompilerParams(
            dimension_semantics=("arbitrary",), vmem_limit_bytes=VMEM_LIMIT_BYTES),
        name="grouped_experts",
    )(tile_ea, tile_eb, n_used, xs, router_wt, router_wt, w_gate, w_gate, w_up, w_up, w_down, w_down)


def _routing_tables(route, counts, n_tiles_max):
    bucket = route[:, 0, :].reshape(-1).astype(jnp.int32)
    rank = route[:, 1, :].reshape(-1).astype(jnp.int32)
    cnt = counts[:N_BUCKETS, 0].astype(jnp.int32)
    tiles_b = (cnt + EXPERT_TILE - 1) // EXPERT_TILE
    tile_end = jnp.cumsum(tiles_b)
    tile_start = tile_end - tiles_b
    n_used = tile_end[-1]
    onehot = bucket[:, None] == jnp.arange(N_BUCKETS, dtype=jnp.int32)[None, :]
    pos = jnp.sum(jnp.where(onehot, (tile_start * EXPERT_TILE)[None, :], 0), axis=1) + rank
    tile_ids = jnp.minimum(jnp.arange(n_tiles_max, dtype=jnp.int32), n_used - 1)
    tile_bucket = jnp.sum((tile_ids[:, None] >= tile_end[None, :]).astype(jnp.int32), axis=1)
    group = tile_bucket // PAIRS_PER_GROUP
    pair = tile_bucket % PAIRS_PER_GROUP
    tile_ea = group * EXPERTS_PER_GROUP + jnp.asarray(_PAIR_A, jnp.int32)[pair]
    tile_eb = group * EXPERTS_PER_GROUP + jnp.asarray(_PAIR_B, jnp.int32)[pair]
    return pos, tile_ea, tile_eb, n_used.reshape(1)


def kernel(x, c, w_ada, b_ada, norm_mix_g, w_in, v_norm_g, sgu_w, sgu_b, pool_w, pool_scale,
           w_branch_a, w_branch_b, w_out, norm_ffn_g, router_w, router_bias,
           w_exp_gate, w_exp_up, w_exp_down, final_norm_g):
    batch, seq, d = x.shape
    depth = w_ada.shape[0]
    t = batch * seq
    n_tiles_max = t // EXPERT_TILE + N_BUCKETS
    n_slots = n_tiles_max * EXPERT_TILE

    mod = _ada(c, w_ada, b_ada)
    rw_pad = jnp.pad(router_w, ((0, 0), (0, 128 - N_EXPERTS)))
    rw_hi = rw_pad.astype(_bf16)
    rw_lo = (rw_pad - rw_hi.astype(_f32)).astype(_bf16)
    router_w_pad = jnp.concatenate([rw_hi, rw_lo], axis=1)
    router_wt = router_w.T.reshape(N_EXPERTS, 1, d)
    final_g = final_norm_g.reshape(1, d)

    def to_bf16(weights, gate=None):
        if gate is not None:
            weights, gate = lax.optimization_barrier((weights, gate))
        return tuple(w.astype(_bf16) for w in weights), gate

    def mixer_weights(l, gate=None):
        return to_bf16((w_in[l], sgu_w[l], pool_w[l], w_branch_a[l], w_branch_b[l], w_out[l]), gate)

    def expert_weights(l, gate=None):
        return to_bf16((w_exp_gate[l], w_exp_up[l], w_exp_down[l]), gate)

    y_tok, gate_prev = None, None
    mixer_w, expert_w = mixer_weights(0)[0], expert_weights(0)[0]
    for l in range(depth):
        mod_l = mod[l].reshape(batch, 1, N_MOD * d)
        w_in_l, sgu_w_l, pool_w_l, w_a_l, w_b_l, w_o_l = mixer_w
        x, hf, route, counts = _mixer(
            x, y_tok, gate_prev, mod_l, norm_mix_g[l].reshape(1, d), norm_ffn_g[l].reshape(1, d),
            w_in_l, v_norm_g[l].reshape(1, SGU_WIDTH), sgu_w_l, sgu_b[l].T,
            pool_w_l, pool_scale[l].reshape(1, POOL_WIDTH), w_a_l, w_b_l, w_o_l,
            router_w_pad, router_bias)
        pos, tile_ea, tile_eb, n_used = _routing_tables(route, counts, n_tiles_max)
        if l + 1 < depth:
            mixer_w, pos = mixer_weights(l + 1, pos)
        xs = _sc_gather_by_slot(pos, hf.reshape((t,) + TOKEN_SLAB), n_slots)
        ys = _experts(tile_ea, tile_eb, n_used, xs, router_wt, *expert_w)
        if l + 1 < depth:
            expert_w, ys = expert_weights(l + 1, ys)
        y_tok = _sc_gather_rows(pos, ys).reshape((batch, seq) + TOKEN_SLAB)
        gate_prev = mod_l[:, :, 5 * d:6 * d]
    return _final_norm(x, y_tok, gate_prev, final_g)
```

```python
import functools

import jax
import jax.numpy as jnp
from jax import lax
from jax.experimental import pallas as pl
from jax.experimental.pallas import tpu as pltpu
from jax.experimental.pallas import tpu_sc as plsc

D_MODEL = 1024
CHUNK = 128
SGU_HEADS = 8
SGU_HEAD_DIM = 128
SGU_WIDTH = 1024
POOL_WINDOWS = (2, 4, 8, 16)
POOL_GROUP_DIM = 256
POOL_WIDTH = 1024
HALO = 16
N_EXPERTS = 16
N_EXPERT_GROUPS = 4
EXPERTS_PER_GROUP = 4
PAIRS_PER_GROUP = 6
N_BUCKETS = N_EXPERT_GROUPS * PAIRS_PER_GROUP
BUCKET_ROWS = 32
D_EXPERT = 512
N_MOD = 6
EPS = 1e-6

SEQ_TILE = 512
SUB_TILE = 256
MIXER_STAGES = 2
ROW_TILE = 512
EXPERT_TILE = 256
VMEM_LIMIT_BYTES = 58 * 1024 * 1024
TOKEN_SLAB = (8, 128)
SC_CORES = 2
SC_SUBCORES = 16
SC_CHUNK = 32
SC_LANES = 16
SC_RING = 3

_PAIR_A = (0, 0, 0, 1, 1, 3)
_PAIR_B = (1, 2, 3, 3, 2, 2)

_bf16 = jnp.bfloat16
_f32 = jnp.float32


def _dot(a, b):
    return jnp.dot(a, b, preferred_element_type=_f32)


def _rms_modulate(x, g, shift, scale):
    y = x * lax.rsqrt(jnp.mean(x * x, axis=-1, keepdims=True) + EPS)
    return (y * g) * (1.0 + scale) + shift


def _ada_kernel(c_ref, w_ref, b_ref, o_ref):
    c = c_ref[...]
    c_act = (c * jax.nn.sigmoid(c)).astype(_bf16)
    o_ref[...] = _dot(c_act, w_ref[...].astype(_bf16)) + b_ref[...]


def _ada(c, w_ada, b_ada):
    depth, d, n = w_ada.shape
    batch = c.shape[0]
    tn = 2048
    return pl.pallas_call(
        _ada_kernel,
        grid=(depth, n // tn),
        in_specs=[
            pl.BlockSpec((batch, d), lambda l, j: (0, 0)),
            pl.BlockSpec((None, d, tn), lambda l, j: (l, 0, j)),
            pl.BlockSpec((None, 1, tn), lambda l, j: (l, 0, j)),
        ],
        out_specs=pl.BlockSpec((None, batch, tn), lambda l, j: (l, 0, j)),
        out_shape=jax.ShapeDtypeStruct((depth, batch, n), _f32),
        compiler_params=pltpu.CompilerParams(
            dimension_semantics=("arbitrary", "arbitrary"), vmem_limit_bytes=VMEM_LIMIT_BYTES),
        name="ada_modulation",
    )(c, w_ada, b_ada.reshape(depth, 1, n))


def _route(hf, rw_ref, rb_ref, carry_ref):
    ts = hf.shape[0]
    hf_hi = hf.astype(_bf16)
    hf_lo = (hf - hf_hi.astype(_f32)).astype(_bf16)
    both = _dot(hf_hi, rw_ref[...])
    logits = both[:, 0:128] + both[:, 128:256] + _dot(hf_lo, rw_ref[:, 0:128])
    lt = logits.T
    rows = [lt[e:e + 1, :] for e in range(N_EXPERTS)]
    m = functools.reduce(jnp.maximum, rows)
    ex = [jnp.exp(r - m) for r in rows]
    den = functools.reduce(lambda a, b: a + b, ex)
    probs = [e / den for e in ex]
    sel = [probs[e] + rb_ref[e] for e in range(N_EXPERTS)]

    def top2_sum(v):
        pairs = [v[i] + v[j] for i, j in zip(_PAIR_A, _PAIR_B)]
        return functools.reduce(jnp.maximum, pairs)

    gscore = [top2_sum(sel[4 * g:4 * g + 4]) for g in range(N_EXPERT_GROUPS)]
    best = gscore[0]
    gidx = jnp.zeros_like(best, dtype=jnp.int32)
    for g in range(1, N_EXPERT_GROUPS):
        better = gscore[g] > best
        best = jnp.where(better, gscore[g], best)
        gidx = jnp.where(better, g, gidx)
    ing = []
    for k in range(EXPERTS_PER_GROUP):
        v = sel[k]
        for g in range(1, N_EXPERT_GROUPS):
            v = jnp.where(gidx == g, sel[4 * g + k], v)
        ing.append(v)
    chosen = []
    for k in range(EXPERTS_PER_GROUP):
        r = jnp.zeros_like(gidx)
        for j in range(EXPERTS_PER_GROUP):
            if j == k:
                continue
            beats = (ing[j] >= ing[k]) if j < k else (ing[j] > ing[k])
            r = r + beats.astype(jnp.int32)
        chosen.append(r < 2)
    lo = jnp.where(chosen[0], 0, jnp.where(chosen[1], 1, 2))
    hi = jnp.where(chosen[3], 3, jnp.where(chosen[2], 2, 1))
    pair = jnp.where(lo == 0, hi - 1, jnp.where(lo == 2, 5, jnp.where(hi == 3, 3, 4)))
    bucket = gidx * PAIRS_PER_GROUP + pair

    brow = lax.broadcasted_iota(jnp.int32, (BUCKET_ROWS, ts), 0)
    onehot = (brow == bucket).astype(_f32)
    jj = lax.broadcasted_iota(jnp.int32, (ts, ts), 0)
    tt = lax.broadcasted_iota(jnp.int32, (ts, ts), 1)
    upper = (jj <= tt).astype(_bf16)
    cum = _dot(onehot.astype(_bf16), upper)
    carry = carry_ref[...][:, 0:1]
    rank = jnp.sum(onehot * (cum - 1.0 + carry), axis=0, keepdims=True)
    carry_ref[...] = carry_ref[...] + jnp.sum(onehot, axis=1, keepdims=True)
    return bucket.astype(_f32), rank


def _mixer_kernel(*refs, has_prev):
    if has_prev:
        x_ref, y_ref, gprev_ref = refs[:3]
        refs = refs[3:]
        x_in = lambda rows: x_ref[rows, :] + gprev_ref[...] * y_ref[rows].reshape(SUB_TILE, D_MODEL)
    else:
        x_ref = refs[0]
        refs = refs[1:]
        x_in = lambda rows: x_ref[rows, :]
    (mod_ref, gmix_ref, gffn_ref, win_ref, vg_ref, sw_ref, sb_ref, pw_ref, ps_ref, wa_ref, wb_ref,
     wo_ref, rw_ref, rb_ref, xo_ref, hf_ref, route_ref, counts_ref, ext_ref, carry_ref) = refs
    b = pl.program_id(0)
    s = pl.program_id(1)
    ts = x_ref.shape[0]
    d = D_MODEL

    @pl.when((b == 0) & (s == 0))
    def _():
        carry_ref[...] = jnp.zeros_like(carry_ref)

    @pl.when(s == 0)
    def _():
        ext_ref[0:HALO, :] = jnp.zeros((HALO, POOL_WIDTH), _f32)

    mod = mod_ref[...]
    ci = lax.broadcasted_iota(jnp.int32, (CHUNK, CHUNK), 0)
    cj = lax.broadcasted_iota(jnp.int32, (CHUNK, CHUNK), 1)
    sgu_w = [jnp.where(ci >= cj, sw_ref[h], jnp.zeros((), _bf16)) for h in range(SGU_HEADS)]
    sub_tiles = [_mixer_rows(r * SUB_TILE, s * ts + r * SUB_TILE, mod, sgu_w,
                             x_in, gmix_ref, gffn_ref, win_ref, vg_ref, sb_ref, pw_ref, ps_ref, wa_ref,
                             wb_ref, wo_ref, rw_ref, rb_ref, xo_ref, hf_ref, route_ref, ext_ref, carry_ref)
                 for r in range(ts // SUB_TILE)]
    for _ in range(MIXER_STAGES):
        for sub_tile in sub_tiles:
            next(sub_tile, None)
    ext_ref[0:HALO, :] = ext_ref[ts:ts + HALO, :]
    route_ref[2:8, :] = jnp.zeros((6, ts), _f32)
    counts_ref[...] = carry_ref[...]


def _mixer_rows(row0, seq_pos0, mod, sgu_w, x_in, gmix_ref, gffn_ref, win_ref, vg_ref, sb_ref,
                pw_ref, ps_ref, wa_ref, wb_ref, wo_ref, rw_ref, rb_ref, xo_ref, hf_ref, route_ref,
                ext_ref, carry_ref):
    d = D_MODEL
    ts = SUB_TILE
    rows = slice(row0, row0 + ts)
    sh_m, sc_m, g_m = mod[:, 0:d], mod[:, d:2 * d], mod[:, 2 * d:3 * d]
    sh_f, sc_f = mod[:, 3 * d:4 * d], mod[:, 4 * d:5 * d]
    x = x_in(rows)

    hb = _rms_modulate(x, gmix_ref[...], sh_m, sc_m).astype(_bf16)

    v = jax.nn.gelu(_dot(hb, win_ref[:, SGU_WIDTH:2 * SGU_WIDTH]))
    p = _dot(hb, win_ref[:, 2 * SGU_WIDTH:2 * SGU_WIDTH + POOL_WIDTH])
    u = jax.nn.gelu(_dot(hb, win_ref[:, 0:SGU_WIDTH]))
    gate_a = jax.nn.sigmoid(_dot(hb, win_ref[:, 3 * d:4 * d]))
    gate_b = jax.nn.sigmoid(_dot(hb, win_ref[:, 4 * d:5 * d]))
    e0 = HALO + row0
    ext_ref[e0:e0 + ts, :] = p
    yield

    vc = v - jnp.mean(v, axis=-1, keepdims=True)
    vn = (vc * lax.rsqrt(jnp.mean(vc * vc, axis=-1, keepdims=True) + EPS) * vg_ref[...]).astype(_bf16)
    n_chunks = ts // CHUNK
    ya_cols = []
    for h in range(SGU_HEADS):
        cols = slice(h * SGU_HEAD_DIM, (h + 1) * SGU_HEAD_DIM)
        rhs = jnp.concatenate([vn[n * CHUNK:(n + 1) * CHUNK, cols] for n in range(n_chunks)], axis=1)
        sg = _dot(sgu_w[h], rhs) + sb_ref[:, h:h + 1]
        s_h = jnp.concatenate([sg[:, n * SGU_HEAD_DIM:(n + 1) * SGU_HEAD_DIM] for n in range(n_chunks)],
                              axis=0)
        ya_cols.append((u[:, cols] * s_h).astype(_bf16))
    ya = jnp.concatenate(ya_cols, axis=1)
    merged = gate_a * _dot(ya, wa_ref[...])

    pos1 = (seq_pos0 + 1 + lax.broadcasted_iota(jnp.int32, (ts, 1), 0)).astype(_f32)
    yb_cols = []
    for gi, w in enumerate(POOL_WINDOWS):
        cols = slice(gi * POOL_GROUP_DIM, (gi + 1) * POOL_GROUP_DIM)
        acc = p[:, cols]
        for k in range(1, w):
            acc = acc + ext_ref[e0 - k:e0 - k + ts, cols]
        count = jnp.minimum(pos1, float(w))
        pooled = (acc / count - p[:, cols]).astype(_bf16)
        yb_cols.append(_dot(pooled, pw_ref[gi]))
    yb = (jnp.concatenate(yb_cols, axis=1) * ps_ref[...]).astype(_bf16)
    merged = merged + gate_b * _dot(yb, wb_ref[...])

    x_new = x + g_m * _dot(merged.astype(_bf16), wo_ref[...])
    xo_ref[rows, :] = x_new

    hf = _rms_modulate(x_new, gffn_ref[...], sh_f, sc_f)
    hf_ref[rows] = hf.reshape((ts,) + TOKEN_SLAB)
    bucket, rank = _route(hf, rw_ref, rb_ref, carry_ref)
    route_ref[0:1, rows] = bucket
    route_ref[1:2, rows] = rank
    yield


def _mixer(x, y_prev, gate_prev, mod_l, gmix, gffn, w_in, v_g, sgu_w, sgu_bt, pool_w, pool_scale,
           w_a, w_b, w_o, router_w_pad, router_bias):
    batch, seq, d = x.shape
    has_prev = y_prev is not None
    ts = SEQ_TILE
    n_tiles = batch * (seq // ts)
    tiles_per_seq = seq // ts
    const = lambda *shape: pl.BlockSpec(shape, lambda b, s: (0,) * len(shape),
                                        pipeline_mode=pl.Buffered(1))
    row_spec = pl.BlockSpec((None, ts, d), lambda b, s: (b, s, 0))
    slab_spec = pl.BlockSpec((None, ts) + TOKEN_SLAB, lambda b, s: (b, s, 0, 0))
    prev_specs = [slab_spec, pl.BlockSpec((None, 1, d), lambda b, s: (b, 0, 0))] if has_prev else []
    prev_args = (y_prev, gate_prev) if has_prev else ()
    return pl.pallas_call(
        functools.partial(_mixer_kernel, has_prev=has_prev),
        grid=(batch, tiles_per_seq),
        in_specs=[row_spec] + prev_specs + [
            pl.BlockSpec((None, 1, N_MOD * d), lambda b, s: (b, 0, 0)),
            const(1, d), const(1, d),
            const(*w_in.shape),
            const(1, SGU_WIDTH),
            const(*sgu_w.shape),
            const(*sgu_bt.shape),
            const(*pool_w.shape),
            const(1, POOL_WIDTH),
            const(d, d), const(d, d), const(d, d),
            const(*router_w_pad.shape),
            pl.BlockSpec(memory_space=pltpu.SMEM),
        ],
        out_specs=[
            pl.BlockSpec((None, ts, d), lambda b, s: (b, s, 0)),
            pl.BlockSpec((None, ts) + TOKEN_SLAB, lambda b, s: (b, s, 0, 0)),
            pl.BlockSpec((None, 8, ts), lambda b, s: (b * tiles_per_seq + s, 0, 0)),
            pl.BlockSpec((BUCKET_ROWS, 128), lambda b, s: (0, 0)),
        ],
        out_shape=[
            jax.ShapeDtypeStruct((batch, seq, d), _f32),
            jax.ShapeDtypeStruct((batch, seq) + TOKEN_SLAB, _f32),
            jax.ShapeDtypeStruct((n_tiles, 8, ts), _f32),
            jax.ShapeDtypeStruct((BUCKET_ROWS, 128), _f32),
        ],
        scratch_shapes=[
            pltpu.VMEM((HALO + ts, POOL_WIDTH), _f32),
            pltpu.VMEM((BUCKET_ROWS, 128), _f32),
        ],
        compiler_params=pltpu.CompilerParams(
            dimension_semantics=("arbitrary", "arbitrary"), vmem_limit_bytes=VMEM_LIMIT_BYTES),
        name="mixer_router",
    )(x, *prev_args, mod_l, gmix, gffn, w_in, v_g, sgu_w, sgu_bt, pool_w, pool_scale, w_a, w_b, w_o,
      router_w_pad, router_bias)


def _sc_mesh():
    return plsc.VectorSubcoreMesh(core_axis_name="c", subcore_axis_name="s")


def _sc_worker():
    return lax.axis_index("s") * SC_CORES + lax.axis_index("c")


def _sc_gather_chunks(table_hbm, idx_v, out_hbm, out_row0, n_chunks, bufs, sems):
    depth = len(bufs)

    def gather(j, slot):
        return pltpu.make_async_copy(table_hbm.at[idx_v.at[j]], bufs[slot], sems[slot])

    for j in range(min(depth - 1, n_chunks)):
        gather(j, j).start()

    @pl.loop(0, n_chunks, step=depth)
    def _(j0):
        for slot in range(depth):
            j = j0 + slot

            @pl.when(j < n_chunks)
            def _():
                @pl.when(j + depth - 1 < n_chunks)
                def _():
                    gather(j + depth - 1, (slot + depth - 1) % depth).start()

                gather(j, slot).wait()
                pltpu.sync_copy(bufs[slot], out_hbm.at[pl.ds(out_row0 + j * SC_CHUNK, SC_CHUNK)])


def _sc_chunks_per_worker(n_rows):
    n_chunks = n_rows // (SC_CORES * SC_SUBCORES * SC_CHUNK)
    assert n_chunks * SC_CORES * SC_SUBCORES * SC_CHUNK == n_rows
    return n_chunks


def _sc_gather_rows(idx, table):
    n_out = idx.shape[0]
    n_chunks = _sc_chunks_per_worker(n_out)
    buf = pltpu.VMEM((SC_CHUNK,) + TOKEN_SLAB, _f32)

    @functools.partial(
        pl.kernel, mesh=_sc_mesh(),
        out_type=jax.ShapeDtypeStruct((n_out,) + TOKEN_SLAB, _f32),
        scratch_types=[pltpu.VMEM((n_chunks, SC_CHUNK), jnp.int32)]
        + [buf] * SC_RING + [pltpu.SemaphoreType.DMA] * SC_RING,
    )
    def gather_kernel(table_hbm, idx_hbm, out_hbm, idx_v, *ring):
        chunk0 = _sc_worker() * n_chunks
        pltpu.sync_copy(idx_hbm.at[pl.ds(chunk0, n_chunks)], idx_v)
        _sc_gather_chunks(table_hbm, idx_v, out_hbm, chunk0 * SC_CHUNK, n_chunks,
                          ring[:SC_RING], ring[SC_RING:])

    return gather_kernel(table, idx.reshape(n_out // SC_CHUNK, SC_CHUNK))


def _sc_gather_by_slot(pos, table, n_slots):
    n_tok = pos.shape[0]
    per_worker = n_slots // (SC_CORES * SC_SUBCORES)
    n_chunks = _sc_chunks_per_worker(n_slots)
    buf = pltpu.VMEM((SC_CHUNK,) + TOKEN_SLAB, _f32)

    @functools.partial(
        pl.kernel, mesh=_sc_mesh(),
        compiler_params=pltpu.CompilerParams(needs_layout_passes=False),
        out_type=jax.ShapeDtypeStruct((n_slots,) + TOKEN_SLAB, _f32),
        scratch_types=[pltpu.VMEM((n_tok,), jnp.int32), pltpu.VMEM((n_chunks, SC_CHUNK), jnp.int32)]
        + [buf] * SC_RING + [pltpu.SemaphoreType.DMA] * SC_RING,
    )
    def slot_kernel(table_hbm, pos_hbm, out_hbm, pos_v, inv_v, *ring):
        slot0 = _sc_worker() * per_worker
        pltpu.sync_copy(pos_hbm, pos_v)
        lane = lax.iota(jnp.int32, SC_LANES)

        @pl.loop(0, n_chunks)
        def _(j):
            for h in range(SC_CHUNK // SC_LANES):
                s = slot0 + j * SC_CHUNK + h * SC_LANES + lane
                inv_v[j, pl.ds(h * SC_LANES, SC_LANES)] = jnp.where(s >= n_tok, s - n_tok, s)

        @pl.loop(0, n_tok // SC_LANES)
        def _(i):
            local = pos_v[pl.ds(i * SC_LANES, SC_LANES)] - slot0
            mine = (local >= 0) & (local < per_worker)
            local = jnp.where(mine, local, 0)
            plsc.store_scatter(inv_v, [local // SC_CHUNK, local % SC_CHUNK], i * SC_LANES + lane, mask=mine)

        _sc_gather_chunks(table_hbm, inv_v, out_hbm, slot0, n_chunks, ring[:SC_RING], ring[SC_RING:])

    return slot_kernel(table, pos)


def _final_kernel(x_ref, y_ref, g_ref, fg_ref, o_ref):
    out = x_ref[...] + g_ref[...] * y_ref[...].reshape(x_ref.shape)
    o_ref[...] = out * lax.rsqrt(jnp.mean(out * out, axis=-1, keepdims=True) + EPS) * fg_ref[...]


def _final_norm(x, y_tok, gate_f, final_g):
    batch, seq, d = x.shape
    row_spec = pl.BlockSpec((None, ROW_TILE, d), lambda b, s: (b, s, 0))
    return pl.pallas_call(
        _final_kernel,
        grid=(batch, seq // ROW_TILE),
        in_specs=[row_spec, pl.BlockSpec((None, ROW_TILE) + TOKEN_SLAB, lambda b, s: (b, s, 0, 0)),
                  pl.BlockSpec((None, 1, d), lambda b, s: (b, 0, 0)),
                  pl.BlockSpec((1, d), lambda b, s: (0, 0))],
        out_specs=row_spec,
        out_shape=jax.ShapeDtypeStruct((batch, seq, d), _f32),
        compiler_params=pltpu.CompilerParams(dimension_semantics=("arbitrary", "arbitrary")),
        name="residual_final_norm",
    )(x, y_tok, gate_f, final_g)


def _experts_kernel(ea_ref, eb_ref, nused_ref, xs_ref, rwa_ref, rwb_ref,
                    wga_ref, wgb_ref, wua_ref, wub_ref, wda_ref, wdb_ref, ys_ref):
    i = pl.program_id(0)

    @pl.when(i < nused_ref[0])
    def _():
        x = xs_ref[...].reshape(EXPERT_TILE, D_MODEL)
        la = jnp.sum(x * rwa_ref[...], axis=-1, keepdims=True)
        lb = jnp.sum(x * rwb_ref[...], axis=-1, keepdims=True)
        wa = jax.nn.sigmoid(la - lb)
        wb = jax.nn.sigmoid(lb - la)
        xb = x.astype(_bf16)
        act_a = (jax.nn.silu(_dot(xb, wga_ref[...])) * _dot(xb, wua_ref[...]) * wa).astype(_bf16)
        act_b = (jax.nn.silu(_dot(xb, wgb_ref[...])) * _dot(xb, wub_ref[...]) * wb).astype(_bf16)
        y = _dot(act_a, wda_ref[...]) + _dot(act_b, wdb_ref[...])
        ys_ref[...] = y.reshape(ys_ref.shape)

    @pl.when(i >= nused_ref[0])
    def _():
        ys_ref[...] = jnp.zeros_like(ys_ref)


def _experts(tile_ea, tile_eb, n_used, xs, router_wt, w_gate, w_up, w_down):
    n_slots = xs.shape[0]
    d = D_MODEL
    n_tiles = n_slots // EXPERT_TILE
    f = D_EXPERT

    def row(i, ea, eb, nu):
        return (jnp.maximum(jnp.minimum(i, nu[0] - 1), 0), 0, 0)

    grid_spec = pltpu.PrefetchScalarGridSpec(
        num_scalar_prefetch=3,
        grid=(n_tiles,),
        in_specs=[
            pl.BlockSpec((EXPERT_TILE,) + TOKEN_SLAB, row),
            pl.BlockSpec((None, 1, d), lambda i, ea, eb, nu: (ea[i], 0, 0)),
            pl.BlockSpec((None, 1, d), lambda i, ea, eb, nu: (eb[i], 0, 0)),
            pl.BlockSpec((None, d, f), lambda i, ea, eb, nu: (ea[i], 0, 0)),
            pl.BlockSpec((None, d, f), lambda i, ea, eb, nu: (eb[i], 0, 0)),
            pl.BlockSpec((None, d, f), lambda i, ea, eb, nu: (ea[i], 0, 0)),
            pl.BlockSpec((None, d, f), lambda i, ea, eb, nu: (eb[i], 0, 0)),
            pl.BlockSpec((None, f, d), lambda i, ea, eb, nu: (ea[i], 0, 0)),
            pl.BlockSpec((None, f, d), lambda i, ea, eb, nu: (eb[i], 0, 0)),
        ],
        out_specs=pl.BlockSpec((EXPERT_TILE,) + TOKEN_SLAB, lambda i, ea, eb, nu: (i, 0, 0)),
    )
    return pl.pallas_call(
        _experts_kernel,
        grid_spec=grid_spec,
        out_shape=jax.ShapeDtypeStruct((n_slots,) + TOKEN_SLAB, _f32),
        compiler_params=pltpu.CompilerParams(
            dimension_semantics=("arbitrary",), vmem_limit_bytes=VMEM_LIMIT_BYTES),
        name="grouped_experts",
    )(tile_ea, tile_eb, n_used, xs, router_wt, router_wt, w_gate, w_gate, w_up, w_up, w_down, w_down)


def _routing_tables(route, counts, n_tiles_max):
    bucket = route[:, 0, :].reshape(-1).astype(jnp.int32)
    rank = route[:, 1, :].reshape(-1).astype(jnp.int32)
    cnt = counts[:N_BUCKETS, 0].astype(jnp.int32)
    tiles_b = (cnt + EXPERT_TILE - 1) // EXPERT_TILE
    tile_end = jnp.cumsum(tiles_b)
    tile_start = tile_end - tiles_b
    n_used = tile_end[-1]
    onehot = bucket[:, None] == jnp.arange(N_BUCKETS, dtype=jnp.int32)[None, :]
    pos = jnp.sum(jnp.where(onehot, (tile_start * EXPERT_TILE)[None, :], 0), axis=1) + rank
    tile_ids = jnp.minimum(jnp.arange(n_tiles_max, dtype=jnp.int32), n_used - 1)
    tile_bucket = jnp.sum((tile_ids[:, None] >= tile_end[None, :]).astype(jnp.int32), axis=1)
    group = tile_bucket // PAIRS_PER_GROUP
    pair = tile_bucket % PAIRS_PER_GROUP
    tile_ea = group * EXPERTS_PER_GROUP + jnp.asarray(_PAIR_A, jnp.int32)[pair]
    tile_eb = group * EXPERTS_PER_GROUP + jnp.asarray(_PAIR_B, jnp.int32)[pair]
    return pos, tile_ea, tile_eb, n_used.reshape(1)


def kernel(x, c, w_ada, b_ada, norm_mix_g, w_in, v_norm_g, sgu_w, sgu_b, pool_w, pool_scale,
           w_branch_a, w_branch_b, w_out, norm_ffn_g, router_w, router_bias,
           w_exp_gate, w_exp_up, w_exp_down, final_norm_g):
    batch, seq, d = x.shape
    depth = w_ada.shape[0]
    t = batch * seq
    n_tiles_max = t // EXPERT_TILE + N_BUCKETS
    n_slots = n_tiles_max * EXPERT_TILE

    mod = _ada(c, w_ada, b_ada)
    rw_pad = jnp.pad(router_w, ((0, 0), (0, 128 - N_EXPERTS)))
    rw_hi = rw_pad.astype(_bf16)
    rw_lo = (rw_pad - rw_hi.astype(_f32)).astype(_bf16)
    router_w_pad = jnp.concatenate([rw_hi, rw_lo], axis=1)
    router_wt = router_w.T.reshape(N_EXPERTS, 1, d)
    final_g = final_norm_g.reshape(1, d)

    def layer_bf16(stacks, l):
        return tuple(w[l].astype(_bf16) for w in stacks)

    def tie(stacks, gate):
        return lax.optimization_barrier((stacks, gate))

    mixer_stacks = (w_in, sgu_w, pool_w, w_branch_a, w_branch_b, w_out)
    expert_stacks = (w_exp_gate, w_exp_up, w_exp_down)
    y_tok, gate_prev = None, None
    mixer_w, expert_w = layer_bf16(mixer_stacks, 0), layer_bf16(expert_stacks, 0)
    for l in range(depth):
        mod_l = mod[l].reshape(batch, 1, N_MOD * d)
        w_in_l, sgu_w_l, pool_w_l, w_a_l, w_b_l, w_o_l = mixer_w
        x, hf, route, counts = _mixer(
            x, y_tok, gate_prev, mod_l, norm_mix_g[l].reshape(1, d), norm_ffn_g[l].reshape(1, d),
            w_in_l, v_norm_g[l].reshape(1, SGU_WIDTH), sgu_w_l, sgu_b[l].T,
            pool_w_l, pool_scale[l].reshape(1, POOL_WIDTH), w_a_l, w_b_l, w_o_l,
            router_w_pad, router_bias)
        pos, tile_ea, tile_eb, n_used = _routing_tables(route, counts, n_tiles_max)
        if l + 1 < depth:
            mixer_stacks, pos = tie(mixer_stacks, pos)
            mixer_w = layer_bf16(mixer_stacks, l + 1)
        xs = _sc_gather_by_slot(pos, hf.reshape((t,) + TOKEN_SLAB), n_slots)
        ys = _experts(tile_ea, tile_eb, n_used, xs, router_wt, *expert_w)
        if l + 1 < depth:
            expert_stacks, ys = tie(expert_stacks, ys)
            expert_w = layer_bf16(expert_stacks, l + 1)
        y_tok = _sc_gather_rows(pos, ys).reshape((batch, seq) + TOKEN_SLAB)
        gate_prev = mod_l[:, :, 5 * d:6 * d]
    return _final_norm(x, y_tok, gate_prev, final_g)
```

```python
import functools

import jax
import jax.numpy as jnp
from jax import lax
from jax.experimental import pallas as pl
from jax.experimental.pallas import tpu as pltpu
from jax.experimental.pallas import tpu_sc as plsc

D_MODEL = 1024
CHUNK = 128
SGU_HEADS = 8
SGU_HEAD_DIM = 128
SGU_WIDTH = 1024
POOL_WINDOWS = (2, 4, 8, 16)
POOL_GROUP_DIM = 256
POOL_WIDTH = 1024
HALO = 16
N_EXPERTS = 16
N_EXPERT_GROUPS = 4
EXPERTS_PER_GROUP = 4
PAIRS_PER_GROUP = 6
N_BUCKETS = N_EXPERT_GROUPS * PAIRS_PER_GROUP
BUCKET_ROWS = 32
D_EXPERT = 512
N_MOD = 6
EPS = 1e-6

SEQ_TILE = 512
SUB_TILE = 256
MIXER_STAGES = 2
ROW_TILE = 512
EXPERT_TILE = 256
VMEM_LIMIT_BYTES = 58 * 1024 * 1024
TOKEN_SLAB = (8, 128)
SC_CORES = 2
SC_SUBCORES = 16
SC_CHUNK = 32
SC_LANES = 16
SC_RING = 3

_PAIR_A = (0, 0, 0, 1, 1, 3)
_PAIR_B = (1, 2, 3, 3, 2, 2)

_bf16 = jnp.bfloat16
_f32 = jnp.float32


def _dot(a, b):
    return jnp.dot(a, b, preferred_element_type=_f32)


def _rms_modulate(x, g, shift, scale):
    y = x * lax.rsqrt(jnp.mean(x * x, axis=-1, keepdims=True) + EPS)
    return (y * g) * (1.0 + scale) + shift


def _ada_kernel(c_ref, w_ref, b_ref, o_ref):
    c = c_ref[...]
    c_act = (c * jax.nn.sigmoid(c)).astype(_bf16)
    o_ref[...] = _dot(c_act, w_ref[...].astype(_bf16)) + b_ref[...]


def _ada(c, w_ada, b_ada):
    depth, d, n = w_ada.shape
    batch = c.shape[0]
    tn = 2048
    return pl.pallas_call(
        _ada_kernel,
        grid=(depth, n // tn),
        in_specs=[
            pl.BlockSpec((batch, d), lambda l, j: (0, 0)),
            pl.BlockSpec((None, d, tn), lambda l, j: (l, 0, j)),
            pl.BlockSpec((None, 1, tn), lambda l, j: (l, 0, j)),
        ],
        out_specs=pl.BlockSpec((None, batch, tn), lambda l, j: (l, 0, j)),
        out_shape=jax.ShapeDtypeStruct((depth, batch, n), _f32),
        compiler_params=pltpu.CompilerParams(
            dimension_semantics=("arbitrary", "arbitrary"), vmem_limit_bytes=VMEM_LIMIT_BYTES),
        name="ada_modulation",
    )(c, w_ada, b_ada.reshape(depth, 1, n))


def _route(hf, rw_ref, rb_ref, carry_ref):
    ts = hf.shape[0]
    hf_hi = hf.astype(_bf16)
    hf_lo = (hf - hf_hi.astype(_f32)).astype(_bf16)
    both = _dot(hf_hi, rw_ref[...])
    logits = both[:, 0:128] + both[:, 128:256] + _dot(hf_lo, rw_ref[:, 0:128])
    lt = logits.T
    rows = [lt[e:e + 1, :] for e in range(N_EXPERTS)]
    m = functools.reduce(jnp.maximum, rows)
    ex = [jnp.exp(r - m) for r in rows]
    den = functools.reduce(lambda a, b: a + b, ex)
    probs = [e / den for e in ex]
    sel = [probs[e] + rb_ref[e] for e in range(N_EXPERTS)]

    def top2_sum(v):
        pairs = [v[i] + v[j] for i, j in zip(_PAIR_A, _PAIR_B)]
        return functools.reduce(jnp.maximum, pairs)

    gscore = [top2_sum(sel[4 * g:4 * g + 4]) for g in range(N_EXPERT_GROUPS)]
    best = gscore[0]
    gidx = jnp.zeros_like(best, dtype=jnp.int32)
    for g in range(1, N_EXPERT_GROUPS):
        better = gscore[g] > best
        best = jnp.where(better, gscore[g], best)
        gidx = jnp.where(better, g, gidx)
    ing = []
    for k in range(EXPERTS_PER_GROUP):
        v = sel[k]
        for g in range(1, N_EXPERT_GROUPS):
            v = jnp.where(gidx == g, sel[4 * g + k], v)
        ing.append(v)
    chosen = []
    for k in range(EXPERTS_PER_GROUP):
        r = jnp.zeros_like(gidx)
        for j in range(EXPERTS_PER_GROUP):
            if j == k:
                continue
            beats = (ing[j] >= ing[k]) if j < k else (ing[j] > ing[k])
            r = r + beats.astype(jnp.int32)
        chosen.append(r < 2)
    lo = jnp.where(chosen[0], 0, jnp.where(chosen[1], 1, 2))
    hi = jnp.where(chosen[3], 3, jnp.where(chosen[2], 2, 1))
    pair = jnp.where(lo == 0, hi - 1, jnp.where(lo == 2, 5, jnp.where(hi == 3, 3, 4)))
    bucket = gidx * PAIRS_PER_GROUP + pair

    brow = lax.broadcasted_iota(jnp.int32, (BUCKET_ROWS, ts), 0)
    onehot = (brow == bucket).astype(_f32)
    jj = lax.broadcasted_iota(jnp.int32, (ts, ts), 0)
    tt = lax.broadcasted_iota(jnp.int32, (ts, ts), 1)
    upper = (jj <= tt).astype(_bf16)
    cum = _dot(onehot.astype(_bf16), upper)
    carry = carry_ref[...][:, 0:1]
    rank = jnp.sum(onehot * (cum - 1.0 + carry), axis=0, keepdims=True)
    carry_ref[...] = carry_ref[...] + jnp.sum(onehot, axis=1, keepdims=True)
    return bucket.astype(_f32), rank


def _mixer_kernel(*refs, has_prev):
    if has_prev:
        x_ref, y_ref, gprev_ref = refs[:3]
        refs = refs[3:]
        x_in = lambda rows: x_ref[rows, :] + gprev_ref[...] * y_ref[rows].reshape(SUB_TILE, D_MODEL)
    else:
        x_ref = refs[0]
        refs = refs[1:]
        x_in = lambda rows: x_ref[rows, :]
    (mod_ref, gmix_ref, gffn_ref, win_ref, vg_ref, sw_ref, sb_ref, pw_ref, ps_ref, wa_ref, wb_ref,
     wo_ref, rw_ref, rb_ref, xo_ref, hf_ref, route_ref, counts_ref, ext_ref, carry_ref) = refs
    b = pl.program_id(0)
    s = pl.program_id(1)
    ts = x_ref.shape[0]
    d = D_MODEL

    @pl.when((b == 0) & (s == 0))
    def _():
        carry_ref[...] = jnp.zeros_like(carry_ref)

    @pl.when(s == 0)
    def _():
        ext_ref[0:HALO, :] = jnp.zeros((HALO, POOL_WIDTH), _f32)

    mod = mod_ref[...]
    ci = lax.broadcasted_iota(jnp.int32, (CHUNK, CHUNK), 0)
    cj = lax.broadcasted_iota(jnp.int32, (CHUNK, CHUNK), 1)
    sgu_w = [jnp.where(ci >= cj, sw_ref[h], jnp.zeros((), _bf16)) for h in range(SGU_HEADS)]
    sub_tiles = [_mixer_rows(r * SUB_TILE, s * ts + r * SUB_TILE, mod, sgu_w,
                             x_in, gmix_ref, gffn_ref, win_ref, vg_ref, sb_ref, pw_ref, ps_ref, wa_ref,
                             wb_ref, wo_ref, rw_ref, rb_ref, xo_ref, hf_ref, route_ref, ext_ref, carry_ref)
                 for r in range(ts // SUB_TILE)]
    for _ in range(MIXER_STAGES):
        for sub_tile in sub_tiles:
            next(sub_tile, None)
    ext_ref[0:HALO, :] = ext_ref[ts:ts + HALO, :]
    route_ref[2:8, :] = jnp.zeros((6, ts), _f32)
    counts_ref[...] = carry_ref[...]


def _mixer_rows(row0, seq_pos0, mod, sgu_w, x_in, gmix_ref, gffn_ref, win_ref, vg_ref, sb_ref,
                pw_ref, ps_ref, wa_ref, wb_ref, wo_ref, rw_ref, rb_ref, xo_ref, hf_ref, route_ref,
                ext_ref, carry_ref):
    d = D_MODEL
    ts = SUB_TILE
    rows = slice(row0, row0 + ts)
    sh_m, sc_m, g_m = mod[:, 0:d], mod[:, d:2 * d], mod[:, 2 * d:3 * d]
    sh_f, sc_f = mod[:, 3 * d:4 * d], mod[:, 4 * d:5 * d]
    x = x_in(rows)

    hb = _rms_modulate(x, gmix_ref[...], sh_m, sc_m).astype(_bf16)

    v = jax.nn.gelu(_dot(hb, win_ref[:, SGU_WIDTH:2 * SGU_WIDTH]))
    p = _dot(hb, win_ref[:, 2 * SGU_WIDTH:2 * SGU_WIDTH + POOL_WIDTH])
    u = jax.nn.gelu(_dot(hb, win_ref[:, 0:SGU_WIDTH]))
    gate_a = jax.nn.sigmoid(_dot(hb, win_ref[:, 3 * d:4 * d]))
    gate_b = jax.nn.sigmoid(_dot(hb, win_ref[:, 4 * d:5 * d]))
    e0 = HALO + row0
    ext_ref[e0:e0 + ts, :] = p
    yield

    vc = v - jnp.mean(v, axis=-1, keepdims=True)
    vn = (vc * lax.rsqrt(jnp.mean(vc * vc, axis=-1, keepdims=True) + EPS) * vg_ref[...]).astype(_bf16)
    n_chunks = ts // CHUNK
    ya_cols = []
    for h in range(SGU_HEADS):
        cols = slice(h * SGU_HEAD_DIM, (h + 1) * SGU_HEAD_DIM)
        rhs = jnp.concatenate([vn[n * CHUNK:(n + 1) * CHUNK, cols] for n in range(n_chunks)], axis=1)
        sg = _dot(sgu_w[h], rhs) + sb_ref[:, h:h + 1]
        s_h = jnp.concatenate([sg[:, n * SGU_HEAD_DIM:(n + 1) * SGU_HEAD_DIM] for n in range(n_chunks)],
                              axis=0)
        ya_cols.append((u[:, cols] * s_h).astype(_bf16))
    ya = jnp.concatenate(ya_cols, axis=1)
    merged = gate_a * _dot(ya, wa_ref[...])

    pos1 = (seq_pos0 + 1 + lax.broadcasted_iota(jnp.int32, (ts, 1), 0)).astype(_f32)
    yb_cols = []
    for gi, w in enumerate(POOL_WINDOWS):
        cols = slice(gi * POOL_GROUP_DIM, (gi + 1) * POOL_GROUP_DIM)
        acc = p[:, cols]
        for k in range(1, w):
            acc = acc + ext_ref[e0 - k:e0 - k + ts, cols]
        count = jnp.minimum(pos1, float(w))
        pooled = (acc / count - p[:, cols]).astype(_bf16)
        yb_cols.append(_dot(pooled, pw_ref[gi]))
    yb = (jnp.concatenate(yb_cols, axis=1) * ps_ref[...]).astype(_bf16)
    merged = merged + gate_b * _dot(yb, wb_ref[...])

    x_new = x + g_m * _dot(merged.astype(_bf16), wo_ref[...])
    xo_ref[rows, :] = x_new

    hf = _rms_modulate(x_new, gffn_ref[...], sh_f, sc_f)
    hf_ref[rows] = hf.reshape((ts,) + TOKEN_SLAB)
    bucket, rank = _route(hf, rw_ref, rb_ref, carry_ref)
    route_ref[0:1, rows] = bucket
    route_ref[1:2, rows] = rank
    yield


def _mixer(x, y_prev, gate_prev, mod_l, gmix, gffn, w_in, v_g, sgu_w, sgu_bt, pool_w, pool_scale,
           w_a, w_b, w_o, router_w_pad, router_bias):
    batch, seq, d = x.shape
    has_prev = y_prev is not None
    ts = SEQ_TILE
    n_tiles = batch * (seq // ts)
    tiles_per_seq = seq // ts
    const = lambda *shape: pl.BlockSpec(shape, lambda b, s: (0,) * len(shape),
                                        pipeline_mode=pl.Buffered(1))
    row_spec = pl.BlockSpec((None, ts, d), lambda b, s: (b, s, 0))
    slab_spec = pl.BlockSpec((None, ts) + TOKEN_SLAB, lambda b, s: (b, s, 0, 0))
    prev_specs = [slab_spec, pl.BlockSpec((None, 1, d), lambda b, s: (b, 0, 0))] if has_prev else []
    prev_args = (y_prev, gate_prev) if has_prev else ()
    return pl.pallas_call(
        functools.partial(_mixer_kernel, has_prev=has_prev),
        grid=(batch, tiles_per_seq),
        in_specs=[row_spec] + prev_specs + [
            pl.BlockSpec((None, 1, N_MOD * d), lambda b, s: (b, 0, 0)),
            const(1, d), const(1, d),
            const(*w_in.shape),
            const(1, SGU_WIDTH),
            const(*sgu_w.shape),
            const(*sgu_bt.shape),
            const(*pool_w.shape),
            const(1, POOL_WIDTH),
            const(d, d), const(d, d), const(d, d),
            const(*router_w_pad.shape),
            pl.BlockSpec(memory_space=pltpu.SMEM),
        ],
        out_specs=[
            pl.BlockSpec((None, ts, d), lambda b, s: (b, s, 0)),
            pl.BlockSpec((None, ts) + TOKEN_SLAB, lambda b, s: (b, s, 0, 0)),
            pl.BlockSpec((None, 8, ts), lambda b, s: (b * tiles_per_seq + s, 0, 0)),
            pl.BlockSpec((BUCKET_ROWS, 128), lambda b, s: (0, 0)),
        ],
        out_shape=[
            jax.ShapeDtypeStruct((batch, seq, d), _f32),
            jax.ShapeDtypeStruct((batch, seq) + TOKEN_SLAB, _f32),
            jax.ShapeDtypeStruct((n_tiles, 8, ts), _f32),
            jax.ShapeDtypeStruct((BUCKET_ROWS, 128), _f32),
        ],
        scratch_shapes=[
            pltpu.VMEM((HALO + ts, POOL_WIDTH), _f32),
            pltpu.VMEM((BUCKET_ROWS, 128), _f32),
        ],
        compiler_params=pltpu.CompilerParams(
            dimension_semantics=("arbitrary", "arbitrary"), vmem_limit_bytes=VMEM_LIMIT_BYTES),
        name="mixer_router",
    )(x, *prev_args, mod_l, gmix, gffn, w_in, v_g, sgu_w, sgu_bt, pool_w, pool_scale, w_a, w_b, w_o,
      router_w_pad, router_bias)


def _sc_mesh():
    return plsc.VectorSubcoreMesh(core_axis_name="c", subcore_axis_name="s")


def _sc_worker():
    return lax.axis_index("s") * SC_CORES + lax.axis_index("c")


def _sc_gather_chunks(table_hbm, idx_v, out_hbm, out_row0, n_chunks, bufs, sems):
    depth = len(bufs)

    def gather(j, slot):
        return pltpu.make_async_copy(table_hbm.at[idx_v.at[j]], bufs[slot], sems[slot])

    for j in range(min(depth - 1, n_chunks)):
        gather(j, j).start()

    @pl.loop(0, n_chunks, step=depth)
    def _(j0):
        for slot in range(depth):
            j = j0 + slot

            @pl.when(j < n_chunks)
            def _():
                @pl.when(j + depth - 1 < n_chunks)
                def _():
                    gather(j + depth - 1, (slot + depth - 1) % depth).start()

                gather(j, slot).wait()
                pltpu.sync_copy(bufs[slot], out_hbm.at[pl.ds(out_row0 + j * SC_CHUNK, SC_CHUNK)])


def _sc_chunks_per_worker(n_rows):
    n_chunks = n_rows // (SC_CORES * SC_SUBCORES * SC_CHUNK)
    assert n_chunks * SC_CORES * SC_SUBCORES * SC_CHUNK == n_rows
    return n_chunks


def _sc_gather_rows(idx, table):
    n_out = idx.shape[0]
    n_chunks = _sc_chunks_per_worker(n_out)
    buf = pltpu.VMEM((SC_CHUNK,) + TOKEN_SLAB, _f32)

    @functools.partial(
        pl.kernel, mesh=_sc_mesh(),
        out_type=jax.ShapeDtypeStruct((n_out,) + TOKEN_SLAB, _f32),
        scratch_types=[pltpu.VMEM((n_chunks, SC_CHUNK), jnp.int32)]
        + [buf] * SC_RING + [pltpu.SemaphoreType.DMA] * SC_RING,
    )
    def gather_kernel(table_hbm, idx_hbm, out_hbm, idx_v, *ring):
        chunk0 = _sc_worker() * n_chunks
        pltpu.sync_copy(idx_hbm.at[pl.ds(chunk0, n_chunks)], idx_v)
        _sc_gather_chunks(table_hbm, idx_v, out_hbm, chunk0 * SC_CHUNK, n_chunks,
                          ring[:SC_RING], ring[SC_RING:])

    return gather_kernel(table, idx.reshape(n_out // SC_CHUNK, SC_CHUNK))


def _sc_gather_by_slot(pos, table, n_slots):
    n_tok = pos.shape[0]
    per_worker = n_slots // (SC_CORES * SC_SUBCORES)
    n_chunks = _sc_chunks_per_worker(n_slots)
    buf = pltpu.VMEM((SC_CHUNK,) + TOKEN_SLAB, _f32)

    @functools.partial(
        pl.kernel, mesh=_sc_mesh(),
        compiler_params=pltpu.CompilerParams(needs_layout_passes=False),
        out_type=jax.ShapeDtypeStruct((n_slots,) + TOKEN_SLAB, _f32),
        scratch_types=[pltpu.VMEM((n_tok,), jnp.int32), pltpu.VMEM((n_chunks, SC_CHUNK), jnp.int32)]
        + [buf] * SC_RING + [pltpu.SemaphoreType.DMA] * SC_RING,
    )
    def slot_kernel(table_hbm, pos_hbm, out_hbm, pos_v, inv_v, *ring):
        slot0 = _sc_worker() * per_worker
        pltpu.sync_copy(pos_hbm, pos_v)
        lane = lax.iota(jnp.int32, SC_LANES)

        @pl.loop(0, n_chunks)
        def _(j):
            for h in range(SC_CHUNK // SC_LANES):
                s = slot0 + j * SC_CHUNK + h * SC_LANES + lane
                inv_v[j, pl.ds(h * SC_LANES, SC_LANES)] = jnp.where(s >= n_tok, s - n_tok, s)

        @pl.loop(0, n_tok // SC_LANES)
        def _(i):
            local = pos_v[pl.ds(i * SC_LANES, SC_LANES)] - slot0
            mine = (local >= 0) & (local < per_worker)
            local = jnp.where(mine, local, 0)
            plsc.store_scatter(inv_v, [local // SC_CHUNK, local % SC_CHUNK], i * SC_LANES + lane, mask=mine)

        _sc_gather_chunks(table_hbm, inv_v, out_hbm, slot0, n_chunks, ring[:SC_RING], ring[SC_RING:])

    return slot_kernel(table, pos)


def _final_kernel(x_ref, y_ref, g_ref, fg_ref, o_ref):
    out = x_ref[...] + g_ref[...] * y_ref[...].reshape(x_ref.shape)
    o_ref[...] = out * lax.rsqrt(jnp.mean(out * out, axis=-1, keepdims=True) + EPS) * fg_ref[...]


def _final_norm(x, y_tok, gate_f, final_g):
    batch, seq, d = x.shape
    row_spec = pl.BlockSpec((None, ROW_TILE, d), lambda b, s: (b, s, 0))
    return pl.pallas_call(
        _final_kernel,
        grid=(batch, seq // ROW_TILE),
        in_specs=[row_spec, pl.BlockSpec((None, ROW_TILE) + TOKEN_SLAB, lambda b, s: (b, s, 0, 0)),
                  pl.BlockSpec((None, 1, d), lambda b, s: (b, 0, 0)),
                  pl.BlockSpec((1, d), lambda b, s: (0, 0))],
        out_specs=row_spec,
        out_shape=jax.ShapeDtypeStruct((batch, seq, d), _f32),
        compiler_params=pltpu.CompilerParams(dimension_semantics=("arbitrary", "arbitrary")),
        name="residual_final_norm",
    )(x, y_tok, gate_f, final_g)


def _experts_kernel(ea_ref, eb_ref, nused_ref, xs_ref, rwa_ref, rwb_ref,
                    wga_ref, wgb_ref, wua_ref, wub_ref, wda_ref, wdb_ref, ys_ref):
    i = pl.program_id(0)

    @pl.when(i < nused_ref[0])
    def _():
        x = xs_ref[...].reshape(EXPERT_TILE, D_MODEL)
        la = jnp.sum(x * rwa_ref[...], axis=-1, keepdims=True)
        lb = jnp.sum(x * rwb_ref[...], axis=-1, keepdims=True)
        wa = jax.nn.sigmoid(la - lb)
        wb = jax.nn.sigmoid(lb - la)
        xb = x.astype(_bf16)
        act_a = (jax.nn.silu(_dot(xb, wga_ref[...])) * _dot(xb, wua_ref[...]) * wa).astype(_bf16)
        act_b = (jax.nn.silu(_dot(xb, wgb_ref[...])) * _dot(xb, wub_ref[...]) * wb).astype(_bf16)
        y = _dot(act_a, wda_ref[...]) + _dot(act_b, wdb_ref[...])
        ys_ref[...] = y.reshape(ys_ref.shape)

    @pl.when(i >= nused_ref[0])
    def _():
        ys_ref[...] = jnp.zeros_like(ys_ref)


def _experts(tile_ea, tile_eb, n_used, xs, router_wt, w_gate, w_up, w_down):
    n_slots = xs.shape[0]
    d = D_MODEL
    n_tiles = n_slots // EXPERT_TILE
    f = D_EXPERT

    def row(i, ea, eb, nu):
        return (jnp.maximum(jnp.minimum(i, nu[0] - 1), 0), 0, 0)

    grid_spec = pltpu.PrefetchScalarGridSpec(
        num_scalar_prefetch=3,
        grid=(n_tiles,),
        in_specs=[
            pl.BlockSpec((EXPERT_TILE,) + TOKEN_SLAB, row),
            pl.BlockSpec((None, 1, d), lambda i, ea, eb, nu: (ea[i], 0, 0)),
            pl.BlockSpec((None, 1, d), lambda i, ea, eb, nu: (eb[i], 0, 0)),
            pl.BlockSpec((None, d, f), lambda i, ea, eb, nu: (ea[i], 0, 0)),
            pl.BlockSpec((None, d, f), lambda i, ea, eb, nu: (eb[i], 0, 0)),
            pl.BlockSpec((None, d, f), lambda i, ea, eb, nu: (ea[i], 0, 0)),
            pl.BlockSpec((None, d, f), lambda i, ea, eb, nu: (eb[i], 0, 0)),
            pl.BlockSpec((None, f, d), lambda i, ea, eb, nu: (ea[i], 0, 0)),
            pl.BlockSpec((None, f, d), lambda i, ea, eb, nu: (eb[i], 0, 0)),
        ],
        out_specs=pl.BlockSpec((EXPERT_TILE,) + TOKEN_SLAB, lambda i, ea, eb, nu: (i, 0, 0)),
    )
    return pl.pallas_call(
        _experts_kernel,
        grid_spec=grid_spec,
        out_shape=jax.ShapeDtypeStruct((n_slots,) + TOKEN_SLAB, _f32),
        compiler_params=pltpu.CompilerParams(
            dimension_semantics=("arbitrary",), vmem_limit_bytes=VMEM_LIMIT_BYTES,
            allow_input_fusion=[False] * 6 + [True] * 6),
        name="grouped_experts",
    )(tile_ea, tile_eb, n_used, xs, router_wt, router_wt, w_gate, w_gate, w_up, w_up, w_down, w_down)


def _routing_tables(route, counts, n_tiles_max):
    bucket = route[:, 0, :].reshape(-1).astype(jnp.int32)
    rank = route[:, 1, :].reshape(-1).astype(jnp.int32)
    cnt = counts[:N_BUCKETS, 0].astype(jnp.int32)
    tiles_b = (cnt + EXPERT_TILE - 1) // EXPERT_TILE
    tile_end = jnp.cumsum(tiles_b)
    tile_start = tile_end - tiles_b
    n_used = tile_end[-1]
    onehot = bucket[:, None] == jnp.arange(N_BUCKETS, dtype=jnp.int32)[None, :]
    pos = jnp.sum(jnp.where(onehot, (tile_start * EXPERT_TILE)[None, :], 0), axis=1) + rank
    tile_ids = jnp.minimum(jnp.arange(n_tiles_max, dtype=jnp.int32), n_used - 1)
    tile_bucket = jnp.sum((tile_ids[:, None] >= tile_end[None, :]).astype(jnp.int32), axis=1)
    group = tile_bucket // PAIRS_PER_GROUP
    pair = tile_bucket % PAIRS_PER_GROUP
    tile_ea = group * EXPERTS_PER_GROUP + jnp.asarray(_PAIR_A, jnp.int32)[pair]
    tile_eb = group * EXPERTS_PER_GROUP + jnp.asarray(_PAIR_B, jnp.int32)[pair]
    return pos, tile_ea, tile_eb, n_used.reshape(1)


def kernel(x, c, w_ada, b_ada, norm_mix_g, w_in, v_norm_g, sgu_w, sgu_b, pool_w, pool_scale,
           w_branch_a, w_branch_b, w_out, norm_ffn_g, router_w, router_bias,
           w_exp_gate, w_exp_up, w_exp_down, final_norm_g):
    batch, seq, d = x.shape
    depth = w_ada.shape[0]
    t = batch * seq
    n_tiles_max = t // EXPERT_TILE + N_BUCKETS
    n_slots = n_tiles_max * EXPERT_TILE

    mod = _ada(c, w_ada, b_ada)
    rw_pad = jnp.pad(router_w, ((0, 0), (0, 128 - N_EXPERTS)))
    rw_hi = rw_pad.astype(_bf16)
    rw_lo = (rw_pad - rw_hi.astype(_f32)).astype(_bf16)
    router_w_pad = jnp.concatenate([rw_hi, rw_lo], axis=1)
    router_wt = router_w.T.reshape(N_EXPERTS, 1, d)
    final_g = final_norm_g.reshape(1, d)

    def layer_bf16(stacks, l):
        return tuple(w[l].astype(_bf16) for w in stacks)

    mixer_stacks = (w_in, sgu_w, pool_w, w_branch_a, w_branch_b, w_out)
    expert_stacks = (w_exp_gate, w_exp_up, w_exp_down)
    y_tok, gate_prev = None, None
    for l in range(depth):
        mod_l = mod[l].reshape(batch, 1, N_MOD * d)
        w_in_l, sgu_w_l, pool_w_l, w_a_l, w_b_l, w_o_l = layer_bf16(mixer_stacks, l)
        x, hf, route, counts = _mixer(
            x, y_tok, gate_prev, mod_l, norm_mix_g[l].reshape(1, d), norm_ffn_g[l].reshape(1, d),
            w_in_l, v_norm_g[l].reshape(1, SGU_WIDTH), sgu_w_l, sgu_b[l].T,
            pool_w_l, pool_scale[l].reshape(1, POOL_WIDTH), w_a_l, w_b_l, w_o_l,
            router_w_pad, router_bias)
        pos, tile_ea, tile_eb, n_used = _routing_tables(route, counts, n_tiles_max)
        xs = _sc_gather_by_slot(pos, hf.reshape((t,) + TOKEN_SLAB), n_slots)
        ys = _experts(tile_ea, tile_eb, n_used, xs, router_wt, *layer_bf16(expert_stacks, l))
        y_tok = _sc_gather_rows(pos, ys).reshape((batch, seq) + TOKEN_SLAB)
        gate_prev = mod_l[:, :, 5 * d:6 * d]
    return _final_norm(x, y_tok, gate_prev, final_g)
```

```python
import functools

import jax
import jax.numpy as jnp
from jax import lax
from jax.experimental import pallas as pl
from jax.experimental.pallas import tpu as pltpu
from jax.experimental.pallas import tpu_sc as plsc

D_MODEL = 1024
CHUNK = 128
SGU_HEADS = 8
SGU_HEAD_DIM = 128
SGU_WIDTH = 1024
POOL_WINDOWS = (2, 4, 8, 16)
POOL_GROUP_DIM = 256
POOL_WIDTH = 1024
HALO = 16
POOL_PAD = 8
N_EXPERTS = 16
N_EXPERT_GROUPS = 4
EXPERTS_PER_GROUP = 4
PAIRS_PER_GROUP = 6
N_BUCKETS = N_EXPERT_GROUPS * PAIRS_PER_GROUP
BUCKET_ROWS = 32
D_EXPERT = 512
N_MOD = 6
EPS = 1e-6

SEQ_TILE = 512
SUB_TILE = 256
MIXER_STAGES = 2
ROW_TILE = 512
EXPERT_TILE = 256
VMEM_LIMIT_BYTES = 58 * 1024 * 1024
TOKEN_SLAB = (8, 128)
SC_CORES = 2
SC_SUBCORES = 16
SC_CHUNK = 32
SC_LANES = 16
SC_RING = 3

_PAIR_A = (0, 0, 0, 1, 1, 3)
_PAIR_B = (1, 2, 3, 3, 2, 2)

_bf16 = jnp.bfloat16
_f32 = jnp.float32


def _dot(a, b):
    return jnp.dot(a, b, preferred_element_type=_f32)


def _rms_modulate(x, g, shift, scale):
    y = x * lax.rsqrt(jnp.mean(x * x, axis=-1, keepdims=True) + EPS)
    return (y * g) * (1.0 + scale) + shift


def _ada_kernel(c_ref, w_ref, b_ref, o_ref):
    c = c_ref[...]
    c_act = (c * jax.nn.sigmoid(c)).astype(_bf16)
    o_ref[...] = _dot(c_act, w_ref[...].astype(_bf16)) + b_ref[...]


def _ada(c, w_ada, b_ada):
    depth, d, n = w_ada.shape
    batch = c.shape[0]
    tn = 2048
    return pl.pallas_call(
        _ada_kernel,
        grid=(depth, n // tn),
        in_specs=[
            pl.BlockSpec((batch, d), lambda l, j: (0, 0)),
            pl.BlockSpec((None, d, tn), lambda l, j: (l, 0, j)),
            pl.BlockSpec((None, 1, tn), lambda l, j: (l, 0, j)),
        ],
        out_specs=pl.BlockSpec((None, batch, tn), lambda l, j: (l, 0, j)),
        out_shape=jax.ShapeDtypeStruct((depth, batch, n), _f32),
        compiler_params=pltpu.CompilerParams(
            dimension_semantics=("arbitrary", "arbitrary"), vmem_limit_bytes=VMEM_LIMIT_BYTES),
        name="ada_modulation",
    )(c, w_ada, b_ada.reshape(depth, 1, n))


def _route(hf, rw_ref, rb_ref, carry_ref):
    ts = hf.shape[0]
    hf_hi = hf.astype(_bf16)
    hf_lo = (hf - hf_hi.astype(_f32)).astype(_bf16)
    both = _dot(hf_hi, rw_ref[...])
    logits = both[:, 0:128] + both[:, 128:256] + _dot(hf_lo, rw_ref[:, 0:128])
    lt = logits.T
    rows = [lt[e:e + 1, :] for e in range(N_EXPERTS)]
    m = functools.reduce(jnp.maximum, rows)
    ex = [jnp.exp(r - m) for r in rows]
    den = functools.reduce(lambda a, b: a + b, ex)
    probs = [e / den for e in ex]
    sel = [probs[e] + rb_ref[e] for e in range(N_EXPERTS)]

    def top2_sum(v):
        pairs = [v[i] + v[j] for i, j in zip(_PAIR_A, _PAIR_B)]
        return functools.reduce(jnp.maximum, pairs)

    gscore = [top2_sum(sel[4 * g:4 * g + 4]) for g in range(N_EXPERT_GROUPS)]
    best = gscore[0]
    gidx = jnp.zeros_like(best, dtype=jnp.int32)
    for g in range(1, N_EXPERT_GROUPS):
        better = gscore[g] > best
        best = jnp.where(better, gscore[g], best)
        gidx = jnp.where(better, g, gidx)
    ing = []
    for k in range(EXPERTS_PER_GROUP):
        v = sel[k]
        for g in range(1, N_EXPERT_GROUPS):
            v = jnp.where(gidx == g, sel[4 * g + k], v)
        ing.append(v)
    chosen = []
    for k in range(EXPERTS_PER_GROUP):
        r = jnp.zeros_like(gidx)
        for j in range(EXPERTS_PER_GROUP):
            if j == k:
                continue
            beats = (ing[j] >= ing[k]) if j < k else (ing[j] > ing[k])
            r = r + beats.astype(jnp.int32)
        chosen.append(r < 2)
    lo = jnp.where(chosen[0], 0, jnp.where(chosen[1], 1, 2))
    hi = jnp.where(chosen[3], 3, jnp.where(chosen[2], 2, 1))
    pair = jnp.where(lo == 0, hi - 1, jnp.where(lo == 2, 5, jnp.where(hi == 3, 3, 4)))
    bucket = gidx * PAIRS_PER_GROUP + pair

    brow = lax.broadcasted_iota(jnp.int32, (BUCKET_ROWS, ts), 0)
    onehot = (brow == bucket).astype(_f32)
    jj = lax.broadcasted_iota(jnp.int32, (ts, ts), 0)
    tt = lax.broadcasted_iota(jnp.int32, (ts, ts), 1)
    upper = (jj <= tt).astype(_bf16)
    cum = _dot(onehot.astype(_bf16), upper)
    carry = carry_ref[...][:, 0:1]
    rank = jnp.sum(onehot * (cum - 1.0 + carry), axis=0, keepdims=True)
    carry_ref[...] = carry_ref[...] + jnp.sum(onehot, axis=1, keepdims=True)
    return bucket.astype(_f32), rank


def _mixer_kernel(*refs, has_prev):
    if has_prev:
        x_ref, y_ref, gprev_ref = refs[:3]
        refs = refs[3:]
        x_in = lambda rows: x_ref[rows, :] + gprev_ref[...] * y_ref[rows].reshape(SUB_TILE, D_MODEL)
    else:
        x_ref = refs[0]
        refs = refs[1:]
        x_in = lambda rows: x_ref[rows, :]
    (mod_ref, gmix_ref, gffn_ref, win_ref, vg_ref, sw_ref, sb_ref, pw_ref, ps_ref, wa_ref, wb_ref,
     wo_ref, rw_ref, rb_ref, xo_ref, hf_ref, route_ref, counts_ref, ext_ref, sum2_ref, sum4_ref,
     carry_ref) = refs
    b = pl.program_id(0)
    s = pl.program_id(1)
    ts = x_ref.shape[0]
    d = D_MODEL

    @pl.when((b == 0) & (s == 0))
    def _():
        carry_ref[...] = jnp.zeros_like(carry_ref)
        ext_ref[0:POOL_PAD, :] = jnp.zeros((POOL_PAD, POOL_WIDTH), _f32)
        sum2_ref[:, 0:POOL_PAD, :] = jnp.zeros((sum2_ref.shape[0], POOL_PAD, sum2_ref.shape[2]), _f32)
        sum4_ref[:, 0:POOL_PAD, :] = jnp.zeros((sum4_ref.shape[0], POOL_PAD, sum4_ref.shape[2]), _f32)

    @pl.when(s == 0)
    def _():
        ext_ref[POOL_PAD:POOL_PAD + HALO, :] = jnp.zeros((HALO, POOL_WIDTH), _f32)

    mod = mod_ref[...]
    ci = lax.broadcasted_iota(jnp.int32, (CHUNK, CHUNK), 0)
    cj = lax.broadcasted_iota(jnp.int32, (CHUNK, CHUNK), 1)
    sgu_w = [jnp.where(ci >= cj, sw_ref[h], jnp.zeros((), _bf16)) for h in range(SGU_HEADS)]
    sub_tiles = [_mixer_rows(r * SUB_TILE, s * ts + r * SUB_TILE, mod, sgu_w,
                             x_in, gmix_ref, gffn_ref, win_ref, vg_ref, sb_ref, pw_ref, ps_ref, wa_ref,
                             wb_ref, wo_ref, rw_ref, rb_ref, xo_ref, hf_ref, route_ref, ext_ref,
                             sum2_ref.at[r], sum4_ref.at[r], carry_ref)
                 for r in range(ts // SUB_TILE)]
    for _ in range(MIXER_STAGES):
        for sub_tile in sub_tiles:
            next(sub_tile, None)
    ext_ref[POOL_PAD:POOL_PAD + HALO, :] = ext_ref[POOL_PAD + ts:POOL_PAD + ts + HALO, :]
    route_ref[2:8, :] = jnp.zeros((6, ts), _f32)
    counts_ref[...] = carry_ref[...]


def _mixer_rows(row0, seq_pos0, mod, sgu_w, x_in, gmix_ref, gffn_ref, win_ref, vg_ref, sb_ref,
                pw_ref, ps_ref, wa_ref, wb_ref, wo_ref, rw_ref, rb_ref, xo_ref, hf_ref, route_ref,
                ext_ref, sum2_ref, sum4_ref, carry_ref):
    d = D_MODEL
    ts = SUB_TILE
    rows = slice(row0, row0 + ts)
    sh_m, sc_m, g_m = mod[:, 0:d], mod[:, d:2 * d], mod[:, 2 * d:3 * d]
    sh_f, sc_f = mod[:, 3 * d:4 * d], mod[:, 4 * d:5 * d]
    x = x_in(rows)

    hb = _rms_modulate(x, gmix_ref[...], sh_m, sc_m).astype(_bf16)

    v = jax.nn.gelu(_dot(hb, win_ref[:, SGU_WIDTH:2 * SGU_WIDTH]))
    p = _dot(hb, win_ref[:, 2 * SGU_WIDTH:2 * SGU_WIDTH + POOL_WIDTH])
    u = jax.nn.gelu(_dot(hb, win_ref[:, 0:SGU_WIDTH]))
    gate_a = jax.nn.sigmoid(_dot(hb, win_ref[:, 3 * d:4 * d]))
    gate_b = jax.nn.sigmoid(_dot(hb, win_ref[:, 4 * d:5 * d]))
    e0 = POOL_PAD + HALO + row0
    ext_ref[e0:e0 + ts, :] = p
    yield

    vc = v - jnp.mean(v, axis=-1, keepdims=True)
    vn = (vc * lax.rsqrt(jnp.mean(vc * vc, axis=-1, keepdims=True) + EPS) * vg_ref[...]).astype(_bf16)
    n_chunks = ts // CHUNK
    ya_cols = []
    for h in range(SGU_HEADS):
        cols = slice(h * SGU_HEAD_DIM, (h + 1) * SGU_HEAD_DIM)
        rhs = jnp.concatenate([vn[n * CHUNK:(n + 1) * CHUNK, cols] for n in range(n_chunks)], axis=1)
        sg = _dot(sgu_w[h], rhs) + sb_ref[:, h:h + 1]
        s_h = jnp.concatenate([sg[:, n * SGU_HEAD_DIM:(n + 1) * SGU_HEAD_DIM] for n in range(n_chunks)],
                              axis=0)
        ya_cols.append((u[:, cols] * s_h).astype(_bf16))
    ya = jnp.concatenate(ya_cols, axis=1)
    merged = gate_a * _dot(ya, wa_ref[...])

    pos1 = (seq_pos0 + 1 + lax.broadcasted_iota(jnp.int32, (ts, 1), 0)).astype(_f32)
    g = POOL_GROUP_DIM
    lo, n = e0 - HALO, HALO + ts
    sum2 = ext_ref[lo:lo + n, :] + ext_ref[lo - 1:lo - 1 + n, :]
    sum2_ref[POOL_PAD:POOL_PAD + n, :] = sum2[:, g:]
    sum4 = sum2[:, g:] + sum2_ref[POOL_PAD - 2:POOL_PAD - 2 + n, :]
    sum4_ref[POOL_PAD:POOL_PAD + n, :] = sum4[:, g:]
    sum8 = sum4[:, g:] + sum4_ref[POOL_PAD - 4:POOL_PAD - 4 + n, :]
    sum16 = sum8[8:, g:] + sum8[:n - 8, g:]
    window_sums = (sum2[HALO:, 0:g], sum4[HALO:, 0:g], sum8[HALO:, 0:g], sum16[HALO - 8:, :])
    yb_cols = []
    for gi, w in enumerate(POOL_WINDOWS):
        cols = slice(gi * g, (gi + 1) * g)
        count = jnp.minimum(pos1, float(w))
        pooled = (window_sums[gi] / count - p[:, cols]).astype(_bf16)
        yb_cols.append(_dot(pooled, pw_ref[gi]))
    yb = (jnp.concatenate(yb_cols, axis=1) * ps_ref[...]).astype(_bf16)
    merged = merged + gate_b * _dot(yb, wb_ref[...])

    x_new = x + g_m * _dot(merged.astype(_bf16), wo_ref[...])
    xo_ref[rows, :] = x_new

    hf = _rms_modulate(x_new, gffn_ref[...], sh_f, sc_f)
    hf_ref[rows] = hf.reshape((ts,) + TOKEN_SLAB)
    bucket, rank = _route(hf, rw_ref, rb_ref, carry_ref)
    route_ref[0:1, rows] = bucket
    route_ref[1:2, rows] = rank
    yield


def _mixer(x, y_prev, gate_prev, mod_l, gmix, gffn, w_in, v_g, sgu_w, sgu_bt, pool_w, pool_scale,
           w_a, w_b, w_o, router_w_pad, router_bias):
    batch, seq, d = x.shape
    has_prev = y_prev is not None
    ts = SEQ_TILE
    n_tiles = batch * (seq // ts)
    tiles_per_seq = seq // ts
    const = lambda *shape: pl.BlockSpec(shape, lambda b, s: (0,) * len(shape),
                                        pipeline_mode=pl.Buffered(1))
    row_spec = pl.BlockSpec((None, ts, d), lambda b, s: (b, s, 0))
    slab_spec = pl.BlockSpec((None, ts) + TOKEN_SLAB, lambda b, s: (b, s, 0, 0))
    prev_specs = [slab_spec, pl.BlockSpec((None, 1, d), lambda b, s: (b, 0, 0))] if has_prev else []
    prev_args = (y_prev, gate_prev) if has_prev else ()
    return pl.pallas_call(
        functools.partial(_mixer_kernel, has_prev=has_prev),
        grid=(batch, tiles_per_seq),
        in_specs=[row_spec] + prev_specs + [
            pl.BlockSpec((None, 1, N_MOD * d), lambda b, s: (b, 0, 0)),
            const(1, d), const(1, d),
            const(*w_in.shape),
            const(1, SGU_WIDTH),
            const(*sgu_w.shape),
            const(*sgu_bt.shape),
            const(*pool_w.shape),
            const(1, POOL_WIDTH),
            const(d, d), const(d, d), const(d, d),
            const(*router_w_pad.shape),
            pl.BlockSpec(memory_space=pltpu.SMEM),
        ],
        out_specs=[
            pl.BlockSpec((None, ts, d), lambda b, s: (b, s, 0)),
            pl.BlockSpec((None, ts) + TOKEN_SLAB, lambda b, s: (b, s, 0, 0)),
            pl.BlockSpec((None, 8, ts), lambda b, s: (b * tiles_per_seq + s, 0, 0)),
            pl.BlockSpec((BUCKET_ROWS, 128), lambda b, s: (0, 0)),
        ],
        out_shape=[
            jax.ShapeDtypeStruct((batch, seq, d), _f32),
            jax.ShapeDtypeStruct((batch, seq) + TOKEN_SLAB, _f32),
            jax.ShapeDtypeStruct((n_tiles, 8, ts), _f32),
            jax.ShapeDtypeStruct((BUCKET_ROWS, 128), _f32),
        ],
        scratch_shapes=[
            pltpu.VMEM((POOL_PAD + HALO + ts, POOL_WIDTH), _f32),
            pltpu.VMEM((ts // SUB_TILE, POOL_PAD + HALO + SUB_TILE, POOL_WIDTH - POOL_GROUP_DIM), _f32),
            pltpu.VMEM((ts // SUB_TILE, POOL_PAD + HALO + SUB_TILE, POOL_WIDTH - 2 * POOL_GROUP_DIM), _f32),
            pltpu.VMEM((BUCKET_ROWS, 128), _f32),
        ],
        compiler_params=pltpu.CompilerParams(
            dimension_semantics=("arbitrary", "arbitrary"), vmem_limit_bytes=VMEM_LIMIT_BYTES),
        name="mixer_router",
    )(x, *prev_args, mod_l, gmix, gffn, w_in, v_g, sgu_w, sgu_bt, pool_w, pool_scale, w_a, w_b, w_o,
      router_w_pad, router_bias)


def _sc_mesh():
    return plsc.VectorSubcoreMesh(core_axis_name="c", subcore_axis_name="s")


def _sc_worker():
    return lax.axis_index("s") * SC_CORES + lax.axis_index("c")


def _sc_gather_chunks(table_hbm, idx_v, out_hbm, out_row0, n_chunks, bufs, sems):
    depth = len(bufs)

    def gather(j, slot):
        return pltpu.make_async_copy(table_hbm.at[idx_v.at[j]], bufs[slot], sems[slot])

    for j in range(min(depth - 1, n_chunks)):
        gather(j, j).start()

    @pl.loop(0, n_chunks, step=depth)
    def _(j0):
        for slot in range(depth):
            j = j0 + slot

            @pl.when(j < n_chunks)
            def _():
                @pl.when(j + depth - 1 < n_chunks)
                def _():
                    gather(j + depth - 1, (slot + depth - 1) % depth).start()

                gather(j, slot).wait()
                pltpu.sync_copy(bufs[slot], out_hbm.at[pl.ds(out_row0 + j * SC_CHUNK, SC_CHUNK)])


def _sc_chunks_per_worker(n_rows):
    n_chunks = n_rows // (SC_CORES * SC_SUBCORES * SC_CHUNK)
    assert n_chunks * SC_CORES * SC_SUBCORES * SC_CHUNK == n_rows
    return n_chunks


def _sc_gather_rows(idx, table):
    n_out = idx.shape[0]
    n_chunks = _sc_chunks_per_worker(n_out)
    buf = pltpu.VMEM((SC_CHUNK,) + TOKEN_SLAB, _f32)

    @functools.partial(
        pl.kernel, mesh=_sc_mesh(),
        out_type=jax.ShapeDtypeStruct((n_out,) + TOKEN_SLAB, _f32),
        scratch_types=[pltpu.VMEM((n_chunks, SC_CHUNK), jnp.int32)]
        + [buf] * SC_RING + [pltpu.SemaphoreType.DMA] * SC_RING,
    )
    def gather_kernel(table_hbm, idx_hbm, out_hbm, idx_v, *ring):
        chunk0 = _sc_worker() * n_chunks
        pltpu.sync_copy(idx_hbm.at[pl.ds(chunk0, n_chunks)], idx_v)
        _sc_gather_chunks(table_hbm, idx_v, out_hbm, chunk0 * SC_CHUNK, n_chunks,
                          ring[:SC_RING], ring[SC_RING:])

    return gather_kernel(table, idx.reshape(n_out // SC_CHUNK, SC_CHUNK))


def _sc_gather_by_slot(pos, table, n_slots):
    n_tok = pos.shape[0]
    per_worker = n_slots // (SC_CORES * SC_SUBCORES)
    n_chunks = _sc_chunks_per_worker(n_slots)
    buf = pltpu.VMEM((SC_CHUNK,) + TOKEN_SLAB, _f32)

    @functools.partial(
        pl.kernel, mesh=_sc_mesh(),
        compiler_params=pltpu.CompilerParams(needs_layout_passes=False),
        out_type=jax.ShapeDtypeStruct((n_slots,) + TOKEN_SLAB, _f32),
        scratch_types=[pltpu.VMEM((n_tok,), jnp.int32), pltpu.VMEM((n_chunks, SC_CHUNK), jnp.int32)]
        + [buf] * SC_RING + [pltpu.SemaphoreType.DMA] * SC_RING,
    )
    def slot_kernel(table_hbm, pos_hbm, out_hbm, pos_v, inv_v, *ring):
        slot0 = _sc_worker() * per_worker
        pltpu.sync_copy(pos_hbm, pos_v)
        lane = lax.iota(jnp.int32, SC_LANES)

        @pl.loop(0, n_chunks)
        def _(j):
            for h in range(SC_CHUNK // SC_LANES):
                s = slot0 + j * SC_CHUNK + h * SC_LANES + lane
                inv_v[j, pl.ds(h * SC_LANES, SC_LANES)] = jnp.where(s >= n_tok, s - n_tok, s)

        @pl.loop(0, n_tok // SC_LANES)
        def _(i):
            local = pos_v[pl.ds(i * SC_LANES, SC_LANES)] - slot0
            mine = (local >= 0) & (local < per_worker)
            local = jnp.where(mine, local, 0)
            plsc.store_scatter(inv_v, [local // SC_CHUNK, local % SC_CHUNK], i * SC_LANES + lane, mask=mine)

        _sc_gather_chunks(table_hbm, inv_v, out_hbm, slot0, n_chunks, ring[:SC_RING], ring[SC_RING:])

    return slot_kernel(table, pos)


def _final_kernel(x_ref, y_ref, g_ref, fg_ref, o_ref):
    out = x_ref[...] + g_ref[...] * y_ref[...].reshape(x_ref.shape)
    o_ref[...] = out * lax.rsqrt(jnp.mean(out * out, axis=-1, keepdims=True) + EPS) * fg_ref[...]


def _final_norm(x, y_tok, gate_f, final_g):
    batch, seq, d = x.shape
    row_spec = pl.BlockSpec((None, ROW_TILE, d), lambda b, s: (b, s, 0))
    return pl.pallas_call(
        _final_kernel,
        grid=(batch, seq // ROW_TILE),
        in_specs=[row_spec, pl.BlockSpec((None, ROW_TILE) + TOKEN_SLAB, lambda b, s: (b, s, 0, 0)),
                  pl.BlockSpec((None, 1, d), lambda b, s: (b, 0, 0)),
                  pl.BlockSpec((1, d), lambda b, s: (0, 0))],
        out_specs=row_spec,
        out_shape=jax.ShapeDtypeStruct((batch, seq, d), _f32),
        compiler_params=pltpu.CompilerParams(dimension_semantics=("arbitrary", "arbitrary")),
        name="residual_final_norm",
    )(x, y_tok, gate_f, final_g)


def _experts_kernel(ea_ref, eb_ref, nused_ref, xs_ref, rwa_ref, rwb_ref,
                    wga_ref, wgb_ref, wua_ref, wub_ref, wda_ref, wdb_ref, ys_ref):
    i = pl.program_id(0)

    @pl.when(i < nused_ref[0])
    def _():
        x = xs_ref[...].reshape(EXPERT_TILE, D_MODEL)
        la = jnp.sum(x * rwa_ref[...], axis=-1, keepdims=True)
        lb = jnp.sum(x * rwb_ref[...], axis=-1, keepdims=True)
        wa = jax.nn.sigmoid(la - lb)
        wb = jax.nn.sigmoid(lb - la)
        xb = x.astype(_bf16)
        act_a = (jax.nn.silu(_dot(xb, wga_ref[...])) * _dot(xb, wua_ref[...]) * wa).astype(_bf16)
        act_b = (jax.nn.silu(_dot(xb, wgb_ref[...])) * _dot(xb, wub_ref[...]) * wb).astype(_bf16)
        y = _dot(act_a, wda_ref[...]) + _dot(act_b, wdb_ref[...])
        ys_ref[...] = y.reshape(ys_ref.shape)

    @pl.when(i >= nused_ref[0])
    def _():
        ys_ref[...] = jnp.zeros_like(ys_ref)


def _experts(tile_ea, tile_eb, n_used, xs, router_wt, w_gate, w_up, w_down):
    n_slots = xs.shape[0]
    d = D_MODEL
    n_tiles = n_slots // EXPERT_TILE
    f = D_EXPERT

    def row(i, ea, eb, nu):
        return (jnp.maximum(jnp.minimum(i, nu[0] - 1), 0), 0, 0)

    grid_spec = pltpu.PrefetchScalarGridSpec(
        num_scalar_prefetch=3,
        grid=(n_tiles,),
        in_specs=[
            pl.BlockSpec((EXPERT_TILE,) + TOKEN_SLAB, row),
            pl.BlockSpec((None, 1, d), lambda i, ea, eb, nu: (ea[i], 0, 0)),
            pl.BlockSpec((None, 1, d), lambda i, ea, eb, nu: (eb[i], 0, 0)),
            pl.BlockSpec((None, d, f), lambda i, ea, eb, nu: (ea[i], 0, 0)),
            pl.BlockSpec((None, d, f), lambda i, ea, eb, nu: (eb[i], 0, 0)),
            pl.BlockSpec((None, d, f), lambda i, ea, eb, nu: (ea[i], 0, 0)),
            pl.BlockSpec((None, d, f), lambda i, ea, eb, nu: (eb[i], 0, 0)),
            pl.BlockSpec((None, f, d), lambda i, ea, eb, nu: (ea[i], 0, 0)),
            pl.BlockSpec((None, f, d), lambda i, ea, eb, nu: (eb[i], 0, 0)),
        ],
        out_specs=pl.BlockSpec((EXPERT_TILE,) + TOKEN_SLAB, lambda i, ea, eb, nu: (i, 0, 0)),
    )
    return pl.pallas_call(
        _experts_kernel,
        grid_spec=grid_spec,
        out_shape=jax.ShapeDtypeStruct((n_slots,) + TOKEN_SLAB, _f32),
        compiler_params=pltpu.CompilerParams(
            dimension_semantics=("arbitrary",), vmem_limit_bytes=VMEM_LIMIT_BYTES),
        name="grouped_experts",
    )(tile_ea, tile_eb, n_used, xs, router_wt, router_wt, w_gate, w_gate, w_up, w_up, w_down, w_down)


def _routing_tables(route, counts, n_tiles_max):
    bucket = route[:, 0, :].reshape(-1).astype(jnp.int32)
    rank = route[:, 1, :].reshape(-1).astype(jnp.int32)
    cnt = counts[:N_BUCKETS, 0].astype(jnp.int32)
    tiles_b = (cnt + EXPERT_TILE - 1) // EXPERT_TILE
    tile_end = jnp.cumsum(tiles_b)
    tile_start = tile_end - tiles_b
    n_used = tile_end[-1]
    onehot = bucket[:, None] == jnp.arange(N_BUCKETS, dtype=jnp.int32)[None, :]
    pos = jnp.sum(jnp.where(onehot, (tile_start * EXPERT_TILE)[None, :], 0), axis=1) + rank
    tile_ids = jnp.minimum(jnp.arange(n_tiles_max, dtype=jnp.int32), n_used - 1)
    tile_bucket = jnp.sum((tile_ids[:, None] >= tile_end[None, :]).astype(jnp.int32), axis=1)
    group = tile_bucket // PAIRS_PER_GROUP
    pair = tile_bucket % PAIRS_PER_GROUP
    tile_ea = group * EXPERTS_PER_GROUP + jnp.asarray(_PAIR_A, jnp.int32)[pair]
    tile_eb = group * EXPERTS_PER_GROUP + jnp.asarray(_PAIR_B, jnp.int32)[pair]
    return pos, tile_ea, tile_eb, n_used.reshape(1)


def kernel(x, c, w_ada, b_ada, norm_mix_g, w_in, v_norm_g, sgu_w, sgu_b, pool_w, pool_scale,
           w_branch_a, w_branch_b, w_out, norm_ffn_g, router_w, router_bias,
           w_exp_gate, w_exp_up, w_exp_down, final_norm_g):
    batch, seq, d = x.shape
    depth = w_ada.shape[0]
    t = batch * seq
    n_tiles_max = t // EXPERT_TILE + N_BUCKETS
    n_slots = n_tiles_max * EXPERT_TILE

    mod = _ada(c, w_ada, b_ada)
    rw_pad = jnp.pad(router_w, ((0, 0), (0, 128 - N_EXPERTS)))
    rw_hi = rw_pad.astype(_bf16)
    rw_lo = (rw_pad - rw_hi.astype(_f32)).astype(_bf16)
    router_w_pad = jnp.concatenate([rw_hi, rw_lo], axis=1)
    router_wt = router_w.T.reshape(N_EXPERTS, 1, d)
    final_g = final_norm_g.reshape(1, d)

    def layer_bf16(stacks, l):
        return tuple(w[l].astype(_bf16) for w in stacks)

    def tie(stacks, gate):
        return lax.optimization_barrier((stacks, gate))

    mixer_stacks = (w_in, sgu_w, pool_w, w_branch_a, w_branch_b, w_out)
    expert_stacks = (w_exp_gate, w_exp_up, w_exp_down)
    y_tok, gate_prev = None, None
    mixer_w, expert_w = layer_bf16(mixer_stacks, 0), layer_bf16(expert_stacks, 0)
    for l in range(depth):
        mod_l = mod[l].reshape(batch, 1, N_MOD * d)
        w_in_l, sgu_w_l, pool_w_l, w_a_l, w_b_l, w_o_l = mixer_w
        x, hf, route, counts = _mixer(
            x, y_tok, gate_prev, mod_l, norm_mix_g[l].reshape(1, d), norm_ffn_g[l].reshape(1, d),
            w_in_l, v_norm_g[l].reshape(1, SGU_WIDTH), sgu_w_l, sgu_b[l].T,
            pool_w_l, pool_scale[l].reshape(1, POOL_WIDTH), w_a_l, w_b_l, w_o_l,
            router_w_pad, router_bias)
        pos, tile_ea, tile_eb, n_used = _routing_tables(route, counts, n_tiles_max)
        if l + 1 < depth:
            mixer_stacks, pos = tie(mixer_stacks, pos)
            mixer_w = layer_bf16(mixer_stacks, l + 1)
        xs = _sc_gather_by_slot(pos, hf.reshape((t,) + TOKEN_SLAB), n_slots)
        ys = _experts(tile_ea, tile_eb, n_used, xs, router_wt, *expert_w)
        if l + 1 < depth:
            expert_stacks, ys = tie(expert_stacks, ys)
            expert_w = layer_bf16(expert_stacks, l + 1)
        y_tok = _sc_gather_rows(pos, ys).reshape((batch, seq) + TOKEN_SLAB)
        gate_prev = mod_l[:, :, 5 * d:6 * d]
    return _final_norm(x, y_tok, gate_prev, final_g)
```

```python
import functools

import jax
import jax.numpy as jnp
from jax import lax
from jax.experimental import pallas as pl
from jax.experimental.pallas import tpu as pltpu
from jax.experimental.pallas import tpu_sc as plsc

D_MODEL = 1024
CHUNK = 128
SGU_HEADS = 8
SGU_HEAD_DIM = 128
SGU_WIDTH = 1024
POOL_WINDOWS = (2, 4, 8, 16)
POOL_GROUP_DIM = 256
POOL_WIDTH = 1024
HALO = 16
POOL_PAD = 8
N_EXPERTS = 16
N_EXPERT_GROUPS = 4
EXPERTS_PER_GROUP = 4
PAIRS_PER_GROUP = 6
N_BUCKETS = N_EXPERT_GROUPS * PAIRS_PER_GROUP
BUCKET_ROWS = 32
D_EXPERT = 512
N_MOD = 6
EPS = 1e-6

SEQ_TILE = 512
SUB_TILE = 256
MIXER_STAGES = 2
ROW_TILE = 512
EXPERT_TILE = 256
VMEM_LIMIT_BYTES = 58 * 1024 * 1024
TOKEN_SLAB = (8, 128)
SC_CORES = 2
SC_SUBCORES = 16
SC_CHUNK = 32
SC_LANES = 16
SC_RING = 3

_PAIR_A = (0, 0, 0, 1, 1, 3)
_PAIR_B = (1, 2, 3, 3, 2, 2)

_bf16 = jnp.bfloat16
_f32 = jnp.float32


def _dot(a, b):
    return jnp.dot(a, b, preferred_element_type=_f32)


def _rms_modulate(x, g, shift, scale):
    y = x * lax.rsqrt(jnp.mean(x * x, axis=-1, keepdims=True) + EPS)
    return (y * g) * (1.0 + scale) + shift


def _ada_kernel(c_ref, w_ref, b_ref, o_ref):
    c = c_ref[...]
    c_act = (c * jax.nn.sigmoid(c)).astype(_bf16)
    o_ref[...] = _dot(c_act, w_ref[...].astype(_bf16)) + b_ref[...]


def _ada(c, w_ada, b_ada):
    depth, d, n = w_ada.shape
    batch = c.shape[0]
    tn = 2048
    return pl.pallas_call(
        _ada_kernel,
        grid=(depth, n // tn),
        in_specs=[
            pl.BlockSpec((batch, d), lambda l, j: (0, 0)),
            pl.BlockSpec((None, d, tn), lambda l, j: (l, 0, j)),
            pl.BlockSpec((None, 1, tn), lambda l, j: (l, 0, j)),
        ],
        out_specs=pl.BlockSpec((None, batch, tn), lambda l, j: (l, 0, j)),
        out_shape=jax.ShapeDtypeStruct((depth, batch, n), _f32),
        compiler_params=pltpu.CompilerParams(
            dimension_semantics=("arbitrary", "arbitrary"), vmem_limit_bytes=VMEM_LIMIT_BYTES),
        name="ada_modulation",
    )(c, w_ada, b_ada.reshape(depth, 1, n))


def _route(hf, rw_ref, rb_ref, carry_ref):
    ts = hf.shape[0]
    hf_hi = hf.astype(_bf16)
    hf_lo = (hf - hf_hi.astype(_f32)).astype(_bf16)
    both = _dot(hf_hi, rw_ref[...])
    logits = both[:, 0:128] + both[:, 128:256] + _dot(hf_lo, rw_ref[:, 0:128])
    lt = logits.T
    rows = [lt[e:e + 1, :] for e in range(N_EXPERTS)]
    m = functools.reduce(jnp.maximum, rows)
    ex = [jnp.exp(r - m) for r in rows]
    den = functools.reduce(lambda a, b: a + b, ex)
    probs = [e / den for e in ex]
    sel = [probs[e] + rb_ref[e] for e in range(N_EXPERTS)]

    def top2_sum(v):
        pairs = [v[i] + v[j] for i, j in zip(_PAIR_A, _PAIR_B)]
        return functools.reduce(jnp.maximum, pairs)

    gscore = [top2_sum(sel[4 * g:4 * g + 4]) for g in range(N_EXPERT_GROUPS)]
    best = gscore[0]
    gidx = jnp.zeros_like(best, dtype=jnp.int32)
    for g in range(1, N_EXPERT_GROUPS):
        better = gscore[g] > best
        best = jnp.where(better, gscore[g], best)
        gidx = jnp.where(better, g, gidx)
    ing = []
    for k in range(EXPERTS_PER_GROUP):
        v = sel[k]
        for g in range(1, N_EXPERT_GROUPS):
            v = jnp.where(gidx == g, sel[4 * g + k], v)
        ing.append(v)
    chosen = []
    for k in range(EXPERTS_PER_GROUP):
        r = jnp.zeros_like(gidx)
        for j in range(EXPERTS_PER_GROUP):
            if j == k:
                continue
            beats = (ing[j] >= ing[k]) if j < k else (ing[j] > ing[k])
            r = r + beats.astype(jnp.int32)
        chosen.append(r < 2)
    lo = jnp.where(chosen[0], 0, jnp.where(chosen[1], 1, 2))
    hi = jnp.where(chosen[3], 3, jnp.where(chosen[2], 2, 1))
    pair = jnp.where(lo == 0, hi - 1, jnp.where(lo == 2, 5, jnp.where(hi == 3, 3, 4)))
    bucket = gidx * PAIRS_PER_GROUP + pair

    brow = lax.broadcasted_iota(jnp.int32, (BUCKET_ROWS, ts), 0)
    onehot = (brow == bucket).astype(_f32)
    jj = lax.broadcasted_iota(jnp.int32, (ts, ts), 0)
    tt = lax.broadcasted_iota(jnp.int32, (ts, ts), 1)
    upper = (jj <= tt).astype(_bf16)
    cum = _dot(onehot.astype(_bf16), upper)
    carry = carry_ref[...][:, 0:1]
    rank = jnp.sum(onehot * (cum - 1.0 + carry), axis=0, keepdims=True)
    carry_ref[...] = carry_ref[...] + jnp.sum(onehot, axis=1, keepdims=True)
    return bucket.astype(_f32), rank


def _mixer_kernel(*refs, has_prev):
    if has_prev:
        x_ref, y_ref, gprev_ref = refs[:3]
        refs = refs[3:]
        x_in = lambda rows: x_ref[rows, :] + gprev_ref[...] * y_ref[rows].reshape(SUB_TILE, D_MODEL)
    else:
        x_ref = refs[0]
        refs = refs[1:]
        x_in = lambda rows: x_ref[rows, :]
    (mod_ref, gmix_ref, gffn_ref, win_ref, vg_ref, sw_ref, sb_ref, pw_ref, ps_ref, wa_ref, wb_ref,
     wo_ref, rw_ref, rb_ref, xo_ref, hf_ref, route_ref, counts_ref, ext_ref, sum2_ref, sum4_ref,
     carry_ref) = refs
    b = pl.program_id(0)
    s = pl.program_id(1)
    ts = x_ref.shape[0]
    d = D_MODEL

    @pl.when((b == 0) & (s == 0))
    def _():
        carry_ref[...] = jnp.zeros_like(carry_ref)
        ext_ref[0:POOL_PAD, :] = jnp.zeros((POOL_PAD, POOL_WIDTH), _f32)
        sum2_ref[:, 0:POOL_PAD, :] = jnp.zeros((sum2_ref.shape[0], POOL_PAD, sum2_ref.shape[2]), _f32)
        sum4_ref[:, 0:POOL_PAD, :] = jnp.zeros((sum4_ref.shape[0], POOL_PAD, sum4_ref.shape[2]), _f32)

    @pl.when(s == 0)
    def _():
        ext_ref[POOL_PAD:POOL_PAD + HALO, :] = jnp.zeros((HALO, POOL_WIDTH), _f32)

    mod = mod_ref[...]
    ci = lax.broadcasted_iota(jnp.int32, (CHUNK, CHUNK), 0)
    cj = lax.broadcasted_iota(jnp.int32, (CHUNK, CHUNK), 1)
    sgu_w = [jnp.where(ci >= cj, sw_ref[h], jnp.zeros((), _bf16)) for h in range(SGU_HEADS)]
    sub_tiles = [_mixer_rows(r * SUB_TILE, s * ts + r * SUB_TILE, mod, sgu_w,
                             x_in, gmix_ref, gffn_ref, win_ref, vg_ref, sb_ref, pw_ref, ps_ref, wa_ref,
                             wb_ref, wo_ref, rw_ref, rb_ref, xo_ref, hf_ref, route_ref, ext_ref,
                             sum2_ref.at[r], sum4_ref.at[r], carry_ref)
                 for r in range(ts // SUB_TILE)]
    for _ in range(MIXER_STAGES):
        for sub_tile in sub_tiles:
            next(sub_tile, None)
    ext_ref[POOL_PAD:POOL_PAD + HALO, :] = ext_ref[POOL_PAD + ts:POOL_PAD + ts + HALO, :]
    route_ref[2:8, :] = jnp.zeros((6, ts), _f32)
    counts_ref[...] = carry_ref[...]


def _mixer_rows(row0, seq_pos0, mod, sgu_w, x_in, gmix_ref, gffn_ref, win_ref, vg_ref, sb_ref,
                pw_ref, ps_ref, wa_ref, wb_ref, wo_ref, rw_ref, rb_ref, xo_ref, hf_ref, route_ref,
                ext_ref, sum2_ref, sum4_ref, carry_ref):
    d = D_MODEL
    ts = SUB_TILE
    rows = slice(row0, row0 + ts)
    sh_m, sc_m, g_m = mod[:, 0:d], mod[:, d:2 * d], mod[:, 2 * d:3 * d]
    sh_f, sc_f = mod[:, 3 * d:4 * d], mod[:, 4 * d:5 * d]
    x = x_in(rows)

    hb = _rms_modulate(x, gmix_ref[...], sh_m, sc_m).astype(_bf16)

    v = jax.nn.gelu(_dot(hb, win_ref[:, SGU_WIDTH:2 * SGU_WIDTH]))
    p = _dot(hb, win_ref[:, 2 * SGU_WIDTH:2 * SGU_WIDTH + POOL_WIDTH])
    u = jax.nn.gelu(_dot(hb, win_ref[:, 0:SGU_WIDTH]))
    gate_a = jax.nn.sigmoid(_dot(hb, win_ref[:, 3 * d:4 * d]))
    gate_b = jax.nn.sigmoid(_dot(hb, win_ref[:, 4 * d:5 * d]))
    e0 = POOL_PAD + HALO + row0
    ext_ref[e0:e0 + ts, :] = p
    yield

    vc = v - jnp.mean(v, axis=-1, keepdims=True)
    vn = (vc * lax.rsqrt(jnp.mean(vc * vc, axis=-1, keepdims=True) + EPS) * vg_ref[...]).astype(_bf16)
    n_chunks = ts // CHUNK
    ya_cols = []
    for h in range(SGU_HEADS):
        cols = slice(h * SGU_HEAD_DIM, (h + 1) * SGU_HEAD_DIM)
        rhs = jnp.concatenate([vn[n * CHUNK:(n + 1) * CHUNK, cols] for n in range(n_chunks)], axis=1)
        sg = _dot(sgu_w[h], rhs) + sb_ref[:, h:h + 1]
        s_h = jnp.concatenate([sg[:, n * SGU_HEAD_DIM:(n + 1) * SGU_HEAD_DIM] for n in range(n_chunks)],
                              axis=0)
        ya_cols.append((u[:, cols] * s_h).astype(_bf16))
    ya = jnp.concatenate(ya_cols, axis=1)
    merged = gate_a * _dot(ya, wa_ref[...])

    pos1 = (seq_pos0 + 1 + lax.broadcasted_iota(jnp.int32, (ts, 1), 0)).astype(_f32)
    g = POOL_GROUP_DIM
    lo, n = e0 - HALO, HALO + ts
    sum2 = ext_ref[lo:lo + n, :] + ext_ref[lo - 1:lo - 1 + n, :]
    sum2_ref[POOL_PAD:POOL_PAD + n, :] = sum2[:, g:]
    sum4 = sum2[:, g:] + sum2_ref[POOL_PAD - 2:POOL_PAD - 2 + n, :]
    sum4_ref[POOL_PAD:POOL_PAD + n, :] = sum4[:, g:]
    sum8 = sum4[:, g:] + sum4_ref[POOL_PAD - 4:POOL_PAD - 4 + n, :]
    sum16 = sum8[8:, g:] + sum8[:n - 8, g:]
    window_sums = (sum2[HALO:, 0:g], sum4[HALO:, 0:g], sum8[HALO:, 0:g], sum16[HALO - 8:, :])
    yb_cols = []
    for gi, w in enumerate(POOL_WINDOWS):
        cols = slice(gi * g, (gi + 1) * g)
        count = jnp.minimum(pos1, float(w))
        pooled = (window_sums[gi] / count - p[:, cols]).astype(_bf16)
        yb_cols.append(_dot(pooled, pw_ref[gi]))
    yb = (jnp.concatenate(yb_cols, axis=1) * ps_ref[...]).astype(_bf16)
    merged = merged + gate_b * _dot(yb, wb_ref[...])

    x_new = x + g_m * _dot(merged.astype(_bf16), wo_ref[...])
    xo_ref[rows, :] = x_new

    hf = _rms_modulate(x_new, gffn_ref[...], sh_f, sc_f)
    hf_ref[rows] = hf.reshape((ts,) + TOKEN_SLAB)
    bucket, rank = _route(hf, rw_ref, rb_ref, carry_ref)
    route_ref[0:1, rows] = bucket
    route_ref[1:2, rows] = rank
    yield


def _mixer(x, y_prev, gate_prev, mod_l, gmix, gffn, w_in, v_g, sgu_w, sgu_bt, pool_w, pool_scale,
           w_a, w_b, w_o, router_w_pad, router_bias):
    batch, seq, d = x.shape
    has_prev = y_prev is not None
    ts = SEQ_TILE
    n_tiles = batch * (seq // ts)
    tiles_per_seq = seq // ts
    const = lambda *shape: pl.BlockSpec(shape, lambda b, s: (0,) * len(shape),
                                        pipeline_mode=pl.Buffered(1))
    row_spec = pl.BlockSpec((None, ts, d), lambda b, s: (b, s, 0))
    slab_spec = pl.BlockSpec((None, ts) + TOKEN_SLAB, lambda b, s: (b, s, 0, 0))
    prev_specs = [slab_spec, pl.BlockSpec((None, 1, d), lambda b, s: (b, 0, 0))] if has_prev else []
    prev_args = (y_prev, gate_prev) if has_prev else ()
    return pl.pallas_call(
        functools.partial(_mixer_kernel, has_prev=has_prev),
        grid=(batch, tiles_per_seq),
        in_specs=[row_spec] + prev_specs + [
            pl.BlockSpec((None, 1, N_MOD * d), lambda b, s: (b, 0, 0)),
            const(1, d), const(1, d),
            const(*w_in.shape),
            const(1, SGU_WIDTH),
            const(*sgu_w.shape),
            const(*sgu_bt.shape),
            const(*pool_w.shape),
            const(1, POOL_WIDTH),
            const(d, d), const(d, d), const(d, d),
            const(*router_w_pad.shape),
            pl.BlockSpec(memory_space=pltpu.SMEM),
        ],
        out_specs=[
            pl.BlockSpec((None, ts, d), lambda b, s: (b, s, 0)),
            pl.BlockSpec((None, ts) + TOKEN_SLAB, lambda b, s: (b, s, 0, 0)),
            pl.BlockSpec((None, 8, ts), lambda b, s: (b * tiles_per_seq + s, 0, 0)),
            pl.BlockSpec((BUCKET_ROWS, 128), lambda b, s: (0, 0)),
        ],
        out_shape=[
            jax.ShapeDtypeStruct((batch, seq, d), _f32),
            jax.ShapeDtypeStruct((batch, seq) + TOKEN_SLAB, _f32),
            jax.ShapeDtypeStruct((n_tiles, 8, ts), _f32),
            jax.ShapeDtypeStruct((BUCKET_ROWS, 128), _f32),
        ],
        scratch_shapes=[
            pltpu.VMEM((POOL_PAD + HALO + ts, POOL_WIDTH), _f32),
            pltpu.VMEM((ts // SUB_TILE, POOL_PAD + HALO + SUB_TILE, POOL_WIDTH - POOL_GROUP_DIM), _f32),
            pltpu.VMEM((ts // SUB_TILE, POOL_PAD + HALO + SUB_TILE, POOL_WIDTH - 2 * POOL_GROUP_DIM), _f32),
            pltpu.VMEM((BUCKET_ROWS, 128), _f32),
        ],
        compiler_params=pltpu.CompilerParams(
            dimension_semantics=("arbitrary", "arbitrary"), vmem_limit_bytes=VMEM_LIMIT_BYTES),
        name="mixer_router",
    )(x, *prev_args, mod_l, gmix, gffn, w_in, v_g, sgu_w, sgu_bt, pool_w, pool_scale, w_a, w_b, w_o,
      router_w_pad, router_bias)


def _sc_mesh():
    return plsc.VectorSubcoreMesh(core_axis_name="c", subcore_axis_name="s")


def _sc_worker():
    return lax.axis_index("s") * SC_CORES + lax.axis_index("c")


def _sc_gather_chunks(table_hbm, idx_v, out_hbm, out_row0, n_chunks, bufs, sems):
    depth = len(bufs)

    def gather(j, slot):
        return pltpu.make_async_copy(table_hbm.at[idx_v.at[j]], bufs[slot], sems[slot])

    for j in range(depth - 1):
        @pl.when(j < n_chunks)
        def _():
            gather(j, j).start()

    @pl.loop(0, n_chunks, step=depth)
    def _(j0):
        for slot in range(depth):
            j = j0 + slot

            @pl.when(j < n_chunks)
            def _():
                @pl.when(j + depth - 1 < n_chunks)
                def _():
                    gather(j + depth - 1, (slot + depth - 1) % depth).start()

                gather(j, slot).wait()
                pltpu.sync_copy(bufs[slot], out_hbm.at[pl.ds(out_row0 + j * SC_CHUNK, SC_CHUNK)])


def _sc_chunks_per_worker(n_rows):
    n_chunks = n_rows // (SC_CORES * SC_SUBCORES * SC_CHUNK)
    assert n_chunks * SC_CORES * SC_SUBCORES * SC_CHUNK == n_rows
    return n_chunks


def _sc_gather_rows(idx, table):
    n_out = idx.shape[0]
    n_chunks = _sc_chunks_per_worker(n_out)
    buf = pltpu.VMEM((SC_CHUNK,) + TOKEN_SLAB, _f32)

    @functools.partial(
        pl.kernel, mesh=_sc_mesh(),
        out_type=jax.ShapeDtypeStruct((n_out,) + TOKEN_SLAB, _f32),
        scratch_types=[pltpu.VMEM((n_chunks, SC_CHUNK), jnp.int32)]
        + [buf] * SC_RING + [pltpu.SemaphoreType.DMA] * SC_RING,
    )
    def gather_kernel(table_hbm, idx_hbm, out_hbm, idx_v, *ring):
        chunk0 = _sc_worker() * n_chunks
        pltpu.sync_copy(idx_hbm.at[pl.ds(chunk0, n_chunks)], idx_v)
        _sc_gather_chunks(table_hbm, idx_v, out_hbm, chunk0 * SC_CHUNK, n_chunks,
                          ring[:SC_RING], ring[SC_RING:])

    return gather_kernel(table, idx.reshape(n_out // SC_CHUNK, SC_CHUNK))


def _sc_gather_by_slot(pos, table, n_slots, n_live_slots):
    n_tok = pos.shape[0]
    per_worker = n_slots // (SC_CORES * SC_SUBCORES)
    n_chunks = _sc_chunks_per_worker(n_slots)
    buf = pltpu.VMEM((SC_CHUNK,) + TOKEN_SLAB, _f32)

    @functools.partial(
        pl.kernel, mesh=_sc_mesh(),
        compiler_params=pltpu.CompilerParams(needs_layout_passes=False),
        out_type=jax.ShapeDtypeStruct((n_slots,) + TOKEN_SLAB, _f32),
        scratch_types=[pltpu.VMEM((n_tok,), jnp.int32), pltpu.VMEM((n_chunks, SC_CHUNK), jnp.int32),
                       pltpu.VMEM((SC_LANES,), jnp.int32)]
        + [buf] * SC_RING + [pltpu.SemaphoreType.DMA] * SC_RING,
    )
    def slot_kernel(table_hbm, pos_hbm, live_hbm, out_hbm, pos_v, inv_v, live_v, *ring):
        slot0 = _sc_worker() * per_worker
        pltpu.sync_copy(pos_hbm, pos_v)
        pltpu.sync_copy(live_hbm, live_v)
        n_live_chunks = jnp.clip((jnp.max(live_v[...]) - slot0) // SC_CHUNK, 0, n_chunks)
        lane = lax.iota(jnp.int32, SC_LANES)

        @pl.loop(0, n_chunks)
        def _(j):
            for h in range(SC_CHUNK // SC_LANES):
                s = slot0 + j * SC_CHUNK + h * SC_LANES + lane
                inv_v[j, pl.ds(h * SC_LANES, SC_LANES)] = jnp.where(s >= n_tok, s - n_tok, s)

        @pl.loop(0, n_tok // SC_LANES)
        def _(i):
            local = pos_v[pl.ds(i * SC_LANES, SC_LANES)] - slot0
            mine = (local >= 0) & (local < per_worker)
            local = jnp.where(mine, local, 0)
            plsc.store_scatter(inv_v, [local // SC_CHUNK, local % SC_CHUNK], i * SC_LANES + lane, mask=mine)

        _sc_gather_chunks(table_hbm, inv_v, out_hbm, slot0, n_live_chunks, ring[:SC_RING], ring[SC_RING:])

    return slot_kernel(table, pos, jnp.broadcast_to(n_live_slots, (SC_LANES,)))


def _final_kernel(x_ref, y_ref, g_ref, fg_ref, o_ref):
    out = x_ref[...] + g_ref[...] * y_ref[...].reshape(x_ref.shape)
    o_ref[...] = out * lax.rsqrt(jnp.mean(out * out, axis=-1, keepdims=True) + EPS) * fg_ref[...]


def _final_norm(x, y_tok, gate_f, final_g):
    batch, seq, d = x.shape
    row_spec = pl.BlockSpec((None, ROW_TILE, d), lambda b, s: (b, s, 0))
    return pl.pallas_call(
        _final_kernel,
        grid=(batch, seq // ROW_TILE),
        in_specs=[row_spec, pl.BlockSpec((None, ROW_TILE) + TOKEN_SLAB, lambda b, s: (b, s, 0, 0)),
                  pl.BlockSpec((None, 1, d), lambda b, s: (b, 0, 0)),
                  pl.BlockSpec((1, d), lambda b, s: (0, 0))],
        out_specs=row_spec,
        out_shape=jax.ShapeDtypeStruct((batch, seq, d), _f32),
        compiler_params=pltpu.CompilerParams(dimension_semantics=("arbitrary", "arbitrary")),
        name="residual_final_norm",
    )(x, y_tok, gate_f, final_g)


def _experts_kernel(ea_ref, eb_ref, nused_ref, xs_ref, rwa_ref, rwb_ref,
                    wga_ref, wgb_ref, wua_ref, wub_ref, wda_ref, wdb_ref, ys_ref):
    i = pl.program_id(0)

    @pl.when(i < nused_ref[0])
    def _():
        x = xs_ref[...].reshape(EXPERT_TILE, D_MODEL)
        la = jnp.sum(x * rwa_ref[...], axis=-1, keepdims=True)
        lb = jnp.sum(x * rwb_ref[...], axis=-1, keepdims=True)
        wa = jax.nn.sigmoid(la - lb)
        wb = jax.nn.sigmoid(lb - la)
        xb = x.astype(_bf16)
        act_a = (jax.nn.silu(_dot(xb, wga_ref[...])) * _dot(xb, wua_ref[...]) * wa).astype(_bf16)
        act_b = (jax.nn.silu(_dot(xb, wgb_ref[...])) * _dot(xb, wub_ref[...]) * wb).astype(_bf16)
        y = _dot(act_a, wda_ref[...]) + _dot(act_b, wdb_ref[...])
        ys_ref[...] = y.reshape(ys_ref.shape)

    @pl.when(i >= nused_ref[0])
    def _():
        ys_ref[...] = jnp.zeros_like(ys_ref)


def _experts(tile_ea, tile_eb, n_used, xs, router_wt, w_gate, w_up, w_down):
    n_slots = xs.shape[0]
    d = D_MODEL
    n_tiles = n_slots // EXPERT_TILE
    f = D_EXPERT

    def row(i, ea, eb, nu):
        return (jnp.maximum(jnp.minimum(i, nu[0] - 1), 0), 0, 0)

    grid_spec = pltpu.PrefetchScalarGridSpec(
        num_scalar_prefetch=3,
        grid=(n_tiles,),
        in_specs=[
            pl.BlockSpec((EXPERT_TILE,) + TOKEN_SLAB, row),
            pl.BlockSpec((None, 1, d), lambda i, ea, eb, nu: (ea[i], 0, 0)),
            pl.BlockSpec((None, 1, d), lambda i, ea, eb, nu: (eb[i], 0, 0)),
            pl.BlockSpec((None, d, f), lambda i, ea, eb, nu: (ea[i], 0, 0)),
            pl.BlockSpec((None, d, f), lambda i, ea, eb, nu: (eb[i], 0, 0)),
            pl.BlockSpec((None, d, f), lambda i, ea, eb, nu: (ea[i], 0, 0)),
            pl.BlockSpec((None, d, f), lambda i, ea, eb, nu: (eb[i], 0, 0)),
            pl.BlockSpec((None, f, d), lambda i, ea, eb, nu: (ea[i], 0, 0)),
            pl.BlockSpec((None, f, d), lambda i, ea, eb, nu: (eb[i], 0, 0)),
        ],
        out_specs=pl.BlockSpec((EXPERT_TILE,) + TOKEN_SLAB, lambda i, ea, eb, nu: (i, 0, 0)),
    )
    return pl.pallas_call(
        _experts_kernel,
        grid_spec=grid_spec,
        out_shape=jax.ShapeDtypeStruct((n_slots,) + TOKEN_SLAB, _f32),
        compiler_params=pltpu.CompilerParams(
            dimension_semantics=("arbitrary",), vmem_limit_bytes=VMEM_LIMIT_BYTES),
        name="grouped_experts",
    )(tile_ea, tile_eb, n_used, xs, router_wt, router_wt, w_gate, w_gate, w_up, w_up, w_down, w_down)


def _routing_tables(route, counts, n_tiles_max):
    bucket = route[:, 0, :].reshape(-1).astype(jnp.int32)
    rank = route[:, 1, :].reshape(-1).astype(jnp.int32)
    cnt = counts[:N_BUCKETS, 0].astype(jnp.int32)
    tiles_b = (cnt + EXPERT_TILE - 1) // EXPERT_TILE
    tile_end = jnp.cumsum(tiles_b)
    tile_start = tile_end - tiles_b
    n_used = tile_end[-1]
    onehot = bucket[:, None] == jnp.arange(N_BUCKETS, dtype=jnp.int32)[None, :]
    pos = jnp.sum(jnp.where(onehot, (tile_start * EXPERT_TILE)[None, :], 0), axis=1) + rank
    tile_ids = jnp.minimum(jnp.arange(n_tiles_max, dtype=jnp.int32), n_used - 1)
    tile_bucket = jnp.sum((tile_ids[:, None] >= tile_end[None, :]).astype(jnp.int32), axis=1)
    group = tile_bucket // PAIRS_PER_GROUP
    pair = tile_bucket % PAIRS_PER_GROUP
    tile_ea = group * EXPERTS_PER_GROUP + jnp.asarray(_PAIR_A, jnp.int32)[pair]
    tile_eb = group * EXPERTS_PER_GROUP + jnp.asarray(_PAIR_B, jnp.int32)[pair]
    return pos, tile_ea, tile_eb, n_used.reshape(1)


def kernel(x, c, w_ada, b_ada, norm_mix_g, w_in, v_norm_g, sgu_w, sgu_b, pool_w, pool_scale,
           w_branch_a, w_branch_b, w_out, norm_ffn_g, router_w, router_bias,
           w_exp_gate, w_exp_up, w_exp_down, final_norm_g):
    batch, seq, d = x.shape
    depth = w_ada.shape[0]
    t = batch * seq
    n_tiles_max = t // EXPERT_TILE + N_BUCKETS
    n_slots = n_tiles_max * EXPERT_TILE

    mod = _ada(c, w_ada, b_ada)
    rw_pad = jnp.pad(router_w, ((0, 0), (0, 128 - N_EXPERTS)))
    rw_hi = rw_pad.astype(_bf16)
    rw_lo = (rw_pad - rw_hi.astype(_f32)).astype(_bf16)
    router_w_pad = jnp.concatenate([rw_hi, rw_lo], axis=1)
    router_wt = router_w.T.reshape(N_EXPERTS, 1, d)
    final_g = final_norm_g.reshape(1, d)

    def layer_bf16(stacks, l):
        return tuple(w[l].astype(_bf16) for w in stacks)

    def tie(stacks, gate):
        return lax.optimization_barrier((stacks, gate))

    mixer_stacks = (w_in, sgu_w, pool_w, w_branch_a, w_branch_b, w_out)
    expert_stacks = (w_exp_gate, w_exp_up, w_exp_down)
    y_tok, gate_prev = None, None
    mixer_w, expert_w = layer_bf16(mixer_stacks, 0), layer_bf16(expert_stacks, 0)
    for l in range(depth):
        mod_l = mod[l].reshape(batch, 1, N_MOD * d)
        w_in_l, sgu_w_l, pool_w_l, w_a_l, w_b_l, w_o_l = mixer_w
        x, hf, route, counts = _mixer(
            x, y_tok, gate_prev, mod_l, norm_mix_g[l].reshape(1, d), norm_ffn_g[l].reshape(1, d),
            w_in_l, v_norm_g[l].reshape(1, SGU_WIDTH), sgu_w_l, sgu_b[l].T,
            pool_w_l, pool_scale[l].reshape(1, POOL_WIDTH), w_a_l, w_b_l, w_o_l,
            router_w_pad, router_bias)
        pos, tile_ea, tile_eb, n_used = _routing_tables(route, counts, n_tiles_max)
        if l + 1 < depth:
            mixer_stacks, pos = tie(mixer_stacks, pos)
            mixer_w = layer_bf16(mixer_stacks, l + 1)
        xs = _sc_gather_by_slot(pos, hf.reshape((t,) + TOKEN_SLAB), n_slots, n_used * EXPERT_TILE)
        ys = _experts(tile_ea, tile_eb, n_used, xs, router_wt, *expert_w)
        if l + 1 < depth:
            expert_stacks, ys = tie(expert_stacks, ys)
            expert_w = layer_bf16(expert_stacks, l + 1)
        y_tok = _sc_gather_rows(pos, ys).reshape((batch, seq) + TOKEN_SLAB)
        gate_prev = mod_l[:, :, 5 * d:6 * d]
    return _final_norm(x, y_tok, gate_prev, final_g)
```

```python
import functools

import jax
import jax.numpy as jnp
from jax import lax
from jax.experimental import pallas as pl
from jax.experimental.pallas import tpu as pltpu
from jax.experimental.pallas import tpu_sc as plsc

D_MODEL = 1024
CHUNK = 128
SGU_HEADS = 8
SGU_HEAD_DIM = 128
SGU_WIDTH = 1024
POOL_WINDOWS = (2, 4, 8, 16)
POOL_GROUP_DIM = 256
POOL_WIDTH = 1024
HALO = 16
POOL_PAD = 8
N_EXPERTS = 16
N_EXPERT_GROUPS = 4
EXPERTS_PER_GROUP = 4
PAIRS_PER_GROUP = 6
N_BUCKETS = N_EXPERT_GROUPS * PAIRS_PER_GROUP
BUCKET_ROWS = 32
D_EXPERT = 512
N_MOD = 6
EPS = 1e-6

SEQ_TILE = 512
SUB_TILE = 256
MIXER_STAGES = 2
ROW_TILE = 512
EXPERT_TILE = 256
VMEM_LIMIT_BYTES = 58 * 1024 * 1024
TOKEN_SLAB = (8, 128)
PACKED_SLAB = (4, 128)
SC_CORES = 2
SC_SUBCORES = 16
SC_CHUNK = 32
SC_LANES = 16
SC_RING = 3

_PAIR_A = (0, 0, 0, 1, 1, 3)
_PAIR_B = (1, 2, 3, 3, 2, 2)

_bf16 = jnp.bfloat16
_f32 = jnp.float32


def _dot(a, b):
    return jnp.dot(a, b, preferred_element_type=_f32)


def _pack_bf16_pairs(x):
    rows, d = x.shape
    hi = lax.bitcast_convert_type(x[:, :d // 2].astype(_bf16).astype(_f32), jnp.uint32)
    lo = lax.bitcast_convert_type(x[:, d // 2:].astype(_bf16).astype(_f32), jnp.uint32)
    return (hi | (lo >> 16)).reshape((rows,) + PACKED_SLAB)


def _unpack_bf16_pairs(words):
    rows = words.shape[0]
    words = words.reshape(rows, PACKED_SLAB[0] * PACKED_SLAB[1])
    hi = lax.bitcast_convert_type(words & jnp.uint32(0xFFFF0000), _f32)
    lo = lax.bitcast_convert_type(words << 16, _f32)
    return jnp.concatenate([hi, lo], axis=1)


def _rms_modulate(x, g, shift, scale):
    y = x * lax.rsqrt(jnp.mean(x * x, axis=-1, keepdims=True) + EPS)
    return (y * g) * (1.0 + scale) + shift


def _ada_kernel(c_ref, w_ref, b_ref, o_ref):
    c = c_ref[...]
    c_act = (c * jax.nn.sigmoid(c)).astype(_bf16)
    o_ref[...] = _dot(c_act, w_ref[...].astype(_bf16)) + b_ref[...]


def _ada(c, w_ada, b_ada):
    depth, d, n = w_ada.shape
    batch = c.shape[0]
    tn = 2048
    return pl.pallas_call(
        _ada_kernel,
        grid=(depth, n // tn),
        in_specs=[
            pl.BlockSpec((batch, d), lambda l, j: (0, 0)),
            pl.BlockSpec((None, d, tn), lambda l, j: (l, 0, j)),
            pl.BlockSpec((None, 1, tn), lambda l, j: (l, 0, j)),
        ],
        out_specs=pl.BlockSpec((None, batch, tn), lambda l, j: (l, 0, j)),
        out_shape=jax.ShapeDtypeStruct((depth, batch, n), _f32),
        compiler_params=pltpu.CompilerParams(
            dimension_semantics=("arbitrary", "arbitrary"), vmem_limit_bytes=VMEM_LIMIT_BYTES),
        name="ada_modulation",
    )(c, w_ada, b_ada.reshape(depth, 1, n))


def _route(hf, rw_ref, rb_ref, carry_ref):
    ts = hf.shape[0]
    hf_hi = hf.astype(_bf16)
    hf_lo = (hf - hf_hi.astype(_f32)).astype(_bf16)
    both = _dot(hf_hi, rw_ref[...])
    logits = both[:, 0:128] + both[:, 128:256] + _dot(hf_lo, rw_ref[:, 0:128])
    lt = logits.T
    rows = [lt[e:e + 1, :] for e in range(N_EXPERTS)]
    m = functools.reduce(jnp.maximum, rows)
    ex = [jnp.exp(r - m) for r in rows]
    den = functools.reduce(lambda a, b: a + b, ex)
    probs = [e / den for e in ex]
    sel = [probs[e] + rb_ref[e] for e in range(N_EXPERTS)]

    def top2_sum(v):
        pairs = [v[i] + v[j] for i, j in zip(_PAIR_A, _PAIR_B)]
        return functools.reduce(jnp.maximum, pairs)

    gscore = [top2_sum(sel[4 * g:4 * g + 4]) for g in range(N_EXPERT_GROUPS)]
    best = gscore[0]
    gidx = jnp.zeros_like(best, dtype=jnp.int32)
    for g in range(1, N_EXPERT_GROUPS):
        better = gscore[g] > best
        best = jnp.where(better, gscore[g], best)
        gidx = jnp.where(better, g, gidx)
    ing = []
    for k in range(EXPERTS_PER_GROUP):
        v = sel[k]
        for g in range(1, N_EXPERT_GROUPS):
            v = jnp.where(gidx == g, sel[4 * g + k], v)
        ing.append(v)
    chosen = []
    for k in range(EXPERTS_PER_GROUP):
        r = jnp.zeros_like(gidx)
        for j in range(EXPERTS_PER_GROUP):
            if j == k:
                continue
            beats = (ing[j] >= ing[k]) if j < k else (ing[j] > ing[k])
            r = r + beats.astype(jnp.int32)
        chosen.append(r < 2)
    lo = jnp.where(chosen[0], 0, jnp.where(chosen[1], 1, 2))
    hi = jnp.where(chosen[3], 3, jnp.where(chosen[2], 2, 1))
    pair = jnp.where(lo == 0, hi - 1, jnp.where(lo == 2, 5, jnp.where(hi == 3, 3, 4)))
    bucket = gidx * PAIRS_PER_GROUP + pair

    brow = lax.broadcasted_iota(jnp.int32, (BUCKET_ROWS, ts), 0)
    onehot = (brow == bucket).astype(_f32)
    jj = lax.broadcasted_iota(jnp.int32, (ts, ts), 0)
    tt = lax.broadcasted_iota(jnp.int32, (ts, ts), 1)
    upper = (jj <= tt).astype(_bf16)
    cum = _dot(onehot.astype(_bf16), upper)
    carry = carry_ref[...][:, 0:1]
    rank = jnp.sum(onehot * (cum - 1.0 + carry), axis=0, keepdims=True)
    carry_ref[...] = carry_ref[...] + jnp.sum(onehot, axis=1, keepdims=True)
    return bucket.astype(_f32), rank


def _mixer_kernel(*refs, has_prev):
    if has_prev:
        x_ref, y_ref, gprev_ref = refs[:3]
        refs = refs[3:]
        x_in = lambda rows: x_ref[rows, :] + gprev_ref[...] * y_ref[rows].reshape(SUB_TILE, D_MODEL)
    else:
        x_ref = refs[0]
        refs = refs[1:]
        x_in = lambda rows: x_ref[rows, :]
    (mod_ref, gmix_ref, gffn_ref, win_ref, vg_ref, sw_ref, sb_ref, pw_ref, ps_ref, wa_ref, wb_ref,
     wo_ref, rw_ref, rb_ref, xo_ref, hf_ref, route_ref, counts_ref, ext_ref, sum2_ref, sum4_ref,
     carry_ref) = refs
    b = pl.program_id(0)
    s = pl.program_id(1)
    ts = x_ref.shape[0]
    d = D_MODEL

    @pl.when((b == 0) & (s == 0))
    def _():
        carry_ref[...] = jnp.zeros_like(carry_ref)
        ext_ref[0:POOL_PAD, :] = jnp.zeros((POOL_PAD, POOL_WIDTH), _f32)
        sum2_ref[:, 0:POOL_PAD, :] = jnp.zeros((sum2_ref.shape[0], POOL_PAD, sum2_ref.shape[2]), _f32)
        sum4_ref[:, 0:POOL_PAD, :] = jnp.zeros((sum4_ref.shape[0], POOL_PAD, sum4_ref.shape[2]), _f32)

    @pl.when(s == 0)
    def _():
        ext_ref[POOL_PAD:POOL_PAD + HALO, :] = jnp.zeros((HALO, POOL_WIDTH), _f32)

    mod = mod_ref[...]
    ci = lax.broadcasted_iota(jnp.int32, (CHUNK, CHUNK), 0)
    cj = lax.broadcasted_iota(jnp.int32, (CHUNK, CHUNK), 1)
    sgu_w = [jnp.where(ci >= cj, sw_ref[h], jnp.zeros((), _bf16)) for h in range(SGU_HEADS)]
    sub_tiles = [_mixer_rows(r * SUB_TILE, s * ts + r * SUB_TILE, mod, sgu_w,
                             x_in, gmix_ref, gffn_ref, win_ref, vg_ref, sb_ref, pw_ref, ps_ref, wa_ref,
                             wb_ref, wo_ref, rw_ref, rb_ref, xo_ref, hf_ref, route_ref, ext_ref,
                             sum2_ref.at[r], sum4_ref.at[r], carry_ref)
                 for r in range(ts // SUB_TILE)]
    for _ in range(MIXER_STAGES):
        for sub_tile in sub_tiles:
            next(sub_tile, None)
    ext_ref[POOL_PAD:POOL_PAD + HALO, :] = ext_ref[POOL_PAD + ts:POOL_PAD + ts + HALO, :]
    route_ref[2:8, :] = jnp.zeros((6, ts), _f32)
    counts_ref[...] = carry_ref[...]


def _mixer_rows(row0, seq_pos0, mod, sgu_w, x_in, gmix_ref, gffn_ref, win_ref, vg_ref, sb_ref,
                pw_ref, ps_ref, wa_ref, wb_ref, wo_ref, rw_ref, rb_ref, xo_ref, hf_ref, route_ref,
                ext_ref, sum2_ref, sum4_ref, carry_ref):
    d = D_MODEL
    ts = SUB_TILE
    rows = slice(row0, row0 + ts)
    sh_m, sc_m, g_m = mod[:, 0:d], mod[:, d:2 * d], mod[:, 2 * d:3 * d]
    sh_f, sc_f = mod[:, 3 * d:4 * d], mod[:, 4 * d:5 * d]
    x = x_in(rows)

    hb = _rms_modulate(x, gmix_ref[...], sh_m, sc_m).astype(_bf16)

    v = jax.nn.gelu(_dot(hb, win_ref[:, SGU_WIDTH:2 * SGU_WIDTH]))
    p = _dot(hb, win_ref[:, 2 * SGU_WIDTH:2 * SGU_WIDTH + POOL_WIDTH])
    u = jax.nn.gelu(_dot(hb, win_ref[:, 0:SGU_WIDTH]))
    gate_a = jax.nn.sigmoid(_dot(hb, win_ref[:, 3 * d:4 * d]))
    gate_b = jax.nn.sigmoid(_dot(hb, win_ref[:, 4 * d:5 * d]))
    e0 = POOL_PAD + HALO + row0
    ext_ref[e0:e0 + ts, :] = p
    yield

    vc = v - jnp.mean(v, axis=-1, keepdims=True)
    vn = (vc * lax.rsqrt(jnp.mean(vc * vc, axis=-1, keepdims=True) + EPS) * vg_ref[...]).astype(_bf16)
    n_chunks = ts // CHUNK
    ya_cols = []
    for h in range(SGU_HEADS):
        cols = slice(h * SGU_HEAD_DIM, (h + 1) * SGU_HEAD_DIM)
        rhs = jnp.concatenate([vn[n * CHUNK:(n + 1) * CHUNK, cols] for n in range(n_chunks)], axis=1)
        sg = _dot(sgu_w[h], rhs) + sb_ref[:, h:h + 1]
        s_h = jnp.concatenate([sg[:, n * SGU_HEAD_DIM:(n + 1) * SGU_HEAD_DIM] for n in range(n_chunks)],
                              axis=0)
        ya_cols.append((u[:, cols] * s_h).astype(_bf16))
    ya = jnp.concatenate(ya_cols, axis=1)
    merged = gate_a * _dot(ya, wa_ref[...])

    pos1 = (seq_pos0 + 1 + lax.broadcasted_iota(jnp.int32, (ts, 1), 0)).astype(_f32)
    g = POOL_GROUP_DIM
    lo, n = e0 - HALO, HALO + ts
    sum2 = ext_ref[lo:lo + n, :] + ext_ref[lo - 1:lo - 1 + n, :]
    sum2_ref[POOL_PAD:POOL_PAD + n, :] = sum2[:, g:]
    sum4 = sum2[:, g:] + sum2_ref[POOL_PAD - 2:POOL_PAD - 2 + n, :]
    sum4_ref[POOL_PAD:POOL_PAD + n, :] = sum4[:, g:]
    sum8 = sum4[:, g:] + sum4_ref[POOL_PAD - 4:POOL_PAD - 4 + n, :]
    sum16 = sum8[8:, g:] + sum8[:n - 8, g:]
    window_sums = (sum2[HALO:, 0:g], sum4[HALO:, 0:g], sum8[HALO:, 0:g], sum16[HALO - 8:, :])
    yb_cols = []
    for gi, w in enumerate(POOL_WINDOWS):
        cols = slice(gi * g, (gi + 1) * g)
        count = jnp.minimum(pos1, float(w))
        pooled = (window_sums[gi] / count - p[:, cols]).astype(_bf16)
        yb_cols.append(_dot(pooled, pw_ref[gi]))
    yb = (jnp.concatenate(yb_cols, axis=1) * ps_ref[...]).astype(_bf16)
    merged = merged + gate_b * _dot(yb, wb_ref[...])

    x_new = x + g_m * _dot(merged.astype(_bf16), wo_ref[...])
    xo_ref[rows, :] = x_new

    hf = _rms_modulate(x_new, gffn_ref[...], sh_f, sc_f)
    hf_ref[rows] = _pack_bf16_pairs(hf)
    bucket, rank = _route(hf, rw_ref, rb_ref, carry_ref)
    route_ref[0:1, rows] = bucket
    route_ref[1:2, rows] = rank
    yield


def _mixer(x, y_prev, gate_prev, mod_l, gmix, gffn, w_in, v_g, sgu_w, sgu_bt, pool_w, pool_scale,
           w_a, w_b, w_o, router_w_pad, router_bias):
    batch, seq, d = x.shape
    has_prev = y_prev is not None
    ts = SEQ_TILE
    n_tiles = batch * (seq // ts)
    tiles_per_seq = seq // ts
    const = lambda *shape: pl.BlockSpec(shape, lambda b, s: (0,) * len(shape),
                                        pipeline_mode=pl.Buffered(1))
    row_spec = pl.BlockSpec((None, ts, d), lambda b, s: (b, s, 0))
    slab_spec = pl.BlockSpec((None, ts) + TOKEN_SLAB, lambda b, s: (b, s, 0, 0))
    prev_specs = [slab_spec, pl.BlockSpec((None, 1, d), lambda b, s: (b, 0, 0))] if has_prev else []
    prev_args = (y_prev, gate_prev) if has_prev else ()
    return pl.pallas_call(
        functools.partial(_mixer_kernel, has_prev=has_prev),
        grid=(batch, tiles_per_seq),
        in_specs=[row_spec] + prev_specs + [
            pl.BlockSpec((None, 1, N_MOD * d), lambda b, s: (b, 0, 0)),
            const(1, d), const(1, d),
            const(*w_in.shape),
            const(1, SGU_WIDTH),
            const(*sgu_w.shape),
            const(*sgu_bt.shape),
            const(*pool_w.shape),
            const(1, POOL_WIDTH),
            const(d, d), const(d, d), const(d, d),
            const(*router_w_pad.shape),
            pl.BlockSpec(memory_space=pltpu.SMEM),
        ],
        out_specs=[
            pl.BlockSpec((None, ts, d), lambda b, s: (b, s, 0)),
            pl.BlockSpec((None, ts) + PACKED_SLAB, lambda b, s: (b, s, 0, 0)),
            pl.BlockSpec((None, 8, ts), lambda b, s: (b * tiles_per_seq + s, 0, 0)),
            pl.BlockSpec((BUCKET_ROWS, 128), lambda b, s: (0, 0)),
        ],
        out_shape=[
            jax.ShapeDtypeStruct((batch, seq, d), _f32),
            jax.ShapeDtypeStruct((batch, seq) + PACKED_SLAB, jnp.uint32),
            jax.ShapeDtypeStruct((n_tiles, 8, ts), _f32),
            jax.ShapeDtypeStruct((BUCKET_ROWS, 128), _f32),
        ],
        scratch_shapes=[
            pltpu.VMEM((POOL_PAD + HALO + ts, POOL_WIDTH), _f32),
            pltpu.VMEM((ts // SUB_TILE, POOL_PAD + HALO + SUB_TILE, POOL_WIDTH - POOL_GROUP_DIM), _f32),
            pltpu.VMEM((ts // SUB_TILE, POOL_PAD + HALO + SUB_TILE, POOL_WIDTH - 2 * POOL_GROUP_DIM), _f32),
            pltpu.VMEM((BUCKET_ROWS, 128), _f32),
        ],
        compiler_params=pltpu.CompilerParams(
            dimension_semantics=("arbitrary", "arbitrary"), vmem_limit_bytes=VMEM_LIMIT_BYTES),
        name="mixer_router",
    )(x, *prev_args, mod_l, gmix, gffn, w_in, v_g, sgu_w, sgu_bt, pool_w, pool_scale, w_a, w_b, w_o,
      router_w_pad, router_bias)


def _sc_mesh():
    return plsc.VectorSubcoreMesh(core_axis_name="c", subcore_axis_name="s")


def _sc_worker():
    return lax.axis_index("s") * SC_CORES + lax.axis_index("c")


def _sc_gather_chunks(table_hbm, idx_v, out_hbm, out_row0, n_chunks, bufs, sems):
    depth = len(bufs)

    def gather(j, slot):
        return pltpu.make_async_copy(table_hbm.at[idx_v.at[j]], bufs[slot], sems[slot])

    for j in range(depth - 1):
        @pl.when(j < n_chunks)
        def _():
            gather(j, j).start()

    @pl.loop(0, n_chunks, step=depth)
    def _(j0):
        for slot in range(depth):
            j = j0 + slot

            @pl.when(j < n_chunks)
            def _():
                @pl.when(j + depth - 1 < n_chunks)
                def _():
                    gather(j + depth - 1, (slot + depth - 1) % depth).start()

                gather(j, slot).wait()
                pltpu.sync_copy(bufs[slot], out_hbm.at[pl.ds(out_row0 + j * SC_CHUNK, SC_CHUNK)])


def _sc_chunks_per_worker(n_rows):
    n_chunks = n_rows // (SC_CORES * SC_SUBCORES * SC_CHUNK)
    assert n_chunks * SC_CORES * SC_SUBCORES * SC_CHUNK == n_rows
    return n_chunks


def _sc_gather_rows(idx, table):
    n_out = idx.shape[0]
    n_chunks = _sc_chunks_per_worker(n_out)
    slab = table.shape[1:]
    buf = pltpu.VMEM((SC_CHUNK,) + slab, table.dtype)

    @functools.partial(
        pl.kernel, mesh=_sc_mesh(),
        out_type=jax.ShapeDtypeStruct((n_out,) + slab, table.dtype),
        scratch_types=[pltpu.VMEM((n_chunks, SC_CHUNK), jnp.int32)]
        + [buf] * SC_RING + [pltpu.SemaphoreType.DMA] * SC_RING,
    )
    def gather_kernel(table_hbm, idx_hbm, out_hbm, idx_v, *ring):
        chunk0 = _sc_worker() * n_chunks
        pltpu.sync_copy(idx_hbm.at[pl.ds(chunk0, n_chunks)], idx_v)
        _sc_gather_chunks(table_hbm, idx_v, out_hbm, chunk0 * SC_CHUNK, n_chunks,
                          ring[:SC_RING], ring[SC_RING:])

    return gather_kernel(table, idx.reshape(n_out // SC_CHUNK, SC_CHUNK))


def _sc_gather_by_slot(pos, table, n_slots, n_live_slots):
    n_tok = pos.shape[0]
    per_worker = n_slots // (SC_CORES * SC_SUBCORES)
    n_chunks = _sc_chunks_per_worker(n_slots)
    slab = table.shape[1:]
    buf = pltpu.VMEM((SC_CHUNK,) + slab, table.dtype)

    @functools.partial(
        pl.kernel, mesh=_sc_mesh(),
        compiler_params=pltpu.CompilerParams(needs_layout_passes=False),
        out_type=jax.ShapeDtypeStruct((n_slots,) + slab, table.dtype),
        scratch_types=[pltpu.VMEM((n_tok,), jnp.int32), pltpu.VMEM((n_chunks, SC_CHUNK), jnp.int32),
                       pltpu.VMEM((SC_LANES,), jnp.int32)]
        + [buf] * SC_RING + [pltpu.SemaphoreType.DMA] * SC_RING,
    )
    def slot_kernel(table_hbm, pos_hbm, live_hbm, out_hbm, pos_v, inv_v, live_v, *ring):
        slot0 = _sc_worker() * per_worker
        pltpu.sync_copy(pos_hbm, pos_v)
        pltpu.sync_copy(live_hbm, live_v)
        n_live_chunks = jnp.clip((jnp.max(live_v[...]) - slot0) // SC_CHUNK, 0, n_chunks)
        lane = lax.iota(jnp.int32, SC_LANES)

        @pl.loop(0, n_chunks)
        def _(j):
            for h in range(SC_CHUNK // SC_LANES):
                s = slot0 + j * SC_CHUNK + h * SC_LANES + lane
                inv_v[j, pl.ds(h * SC_LANES, SC_LANES)] = jnp.where(s >= n_tok, s - n_tok, s)

        @pl.loop(0, n_tok // SC_LANES)
        def _(i):
            local = pos_v[pl.ds(i * SC_LANES, SC_LANES)] - slot0
            mine = (local >= 0) & (local < per_worker)
            local = jnp.where(mine, local, 0)
            plsc.store_scatter(inv_v, [local // SC_CHUNK, local % SC_CHUNK], i * SC_LANES + lane, mask=mine)

        _sc_gather_chunks(table_hbm, inv_v, out_hbm, slot0, n_live_chunks, ring[:SC_RING], ring[SC_RING:])

    return slot_kernel(table, pos, jnp.broadcast_to(n_live_slots, (SC_LANES,)))


def _final_kernel(x_ref, y_ref, g_ref, fg_ref, o_ref):
    out = x_ref[...] + g_ref[...] * y_ref[...].reshape(x_ref.shape)
    o_ref[...] = out * lax.rsqrt(jnp.mean(out * out, axis=-1, keepdims=True) + EPS) * fg_ref[...]


def _final_norm(x, y_tok, gate_f, final_g):
    batch, seq, d = x.shape
    row_spec = pl.BlockSpec((None, ROW_TILE, d), lambda b, s: (b, s, 0))
    return pl.pallas_call(
        _final_kernel,
        grid=(batch, seq // ROW_TILE),
        in_specs=[row_spec, pl.BlockSpec((None, ROW_TILE) + TOKEN_SLAB, lambda b, s: (b, s, 0, 0)),
                  pl.BlockSpec((None, 1, d), lambda b, s: (b, 0, 0)),
                  pl.BlockSpec((1, d), lambda b, s: (0, 0))],
        out_specs=row_spec,
        out_shape=jax.ShapeDtypeStruct((batch, seq, d), _f32),
        compiler_params=pltpu.CompilerParams(dimension_semantics=("arbitrary", "arbitrary")),
        name="residual_final_norm",
    )(x, y_tok, gate_f, final_g)


def _experts_kernel(ea_ref, eb_ref, nused_ref, xs_ref, rwa_ref, rwb_ref,
                    wga_ref, wgb_ref, wua_ref, wub_ref, wda_ref, wdb_ref, ys_ref):
    i = pl.program_id(0)

    @pl.when(i < nused_ref[0])
    def _():
        x = _unpack_bf16_pairs(xs_ref[...])
        la = jnp.sum(x * rwa_ref[...], axis=-1, keepdims=True)
        lb = jnp.sum(x * rwb_ref[...], axis=-1, keepdims=True)
        wa = jax.nn.sigmoid(la - lb)
        wb = jax.nn.sigmoid(lb - la)
        xb = x.astype(_bf16)
        act_a = (jax.nn.silu(_dot(xb, wga_ref[...])) * _dot(xb, wua_ref[...]) * wa).astype(_bf16)
        act_b = (jax.nn.silu(_dot(xb, wgb_ref[...])) * _dot(xb, wub_ref[...]) * wb).astype(_bf16)
        y = _dot(act_a, wda_ref[...]) + _dot(act_b, wdb_ref[...])
        ys_ref[...] = y.reshape(ys_ref.shape)

    @pl.when(i >= nused_ref[0])
    def _():
        ys_ref[...] = jnp.zeros_like(ys_ref)


def _experts(tile_ea, tile_eb, n_used, xs, router_wt, w_gate, w_up, w_down):
    n_slots = xs.shape[0]
    d = D_MODEL
    n_tiles = n_slots // EXPERT_TILE
    f = D_EXPERT

    def row(i, ea, eb, nu):
        return (jnp.maximum(jnp.minimum(i, nu[0] - 1), 0), 0, 0)

    grid_spec = pltpu.PrefetchScalarGridSpec(
        num_scalar_prefetch=3,
        grid=(n_tiles,),
        in_specs=[
            pl.BlockSpec((EXPERT_TILE,) + PACKED_SLAB, row),
            pl.BlockSpec((None, 1, d), lambda i, ea, eb, nu: (ea[i], 0, 0)),
            pl.BlockSpec((None, 1, d), lambda i, ea, eb, nu: (eb[i], 0, 0)),
            pl.BlockSpec((None, d, f), lambda i, ea, eb, nu: (ea[i], 0, 0)),
            pl.BlockSpec((None, d, f), lambda i, ea, eb, nu: (eb[i], 0, 0)),
            pl.BlockSpec((None, d, f), lambda i, ea, eb, nu: (ea[i], 0, 0)),
            pl.BlockSpec((None, d, f), lambda i, ea, eb, nu: (eb[i], 0, 0)),
            pl.BlockSpec((None, f, d), lambda i, ea, eb, nu: (ea[i], 0, 0)),
            pl.BlockSpec((None, f, d), lambda i, ea, eb, nu: (eb[i], 0, 0)),
        ],
        out_specs=pl.BlockSpec((EXPERT_TILE,) + TOKEN_SLAB, lambda i, ea, eb, nu: (i, 0, 0)),
    )
    return pl.pallas_call(
        _experts_kernel,
        grid_spec=grid_spec,
        out_shape=jax.ShapeDtypeStruct((n_slots,) + TOKEN_SLAB, _f32),
        compiler_params=pltpu.CompilerParams(
            dimension_semantics=("arbitrary",), vmem_limit_bytes=VMEM_LIMIT_BYTES),
        name="grouped_experts",
    )(tile_ea, tile_eb, n_used, xs, router_wt, router_wt, w_gate, w_gate, w_up, w_up, w_down, w_down)


def _routing_tables(route, counts, n_tiles_max):
    bucket = route[:, 0, :].reshape(-1).astype(jnp.int32)
    rank = route[:, 1, :].reshape(-1).astype(jnp.int32)
    cnt = counts[:N_BUCKETS, 0].astype(jnp.int32)
    tiles_b = (cnt + EXPERT_TILE - 1) // EXPERT_TILE
    tile_end = jnp.cumsum(tiles_b)
    tile_start = tile_end - tiles_b
    n_used = tile_end[-1]
    onehot = bucket[:, None] == jnp.arange(N_BUCKETS, dtype=jnp.int32)[None, :]
    pos = jnp.sum(jnp.where(onehot, (tile_start * EXPERT_TILE)[None, :], 0), axis=1) + rank
    tile_ids = jnp.minimum(jnp.arange(n_tiles_max, dtype=jnp.int32), n_used - 1)
    tile_bucket = jnp.sum((tile_ids[:, None] >= tile_end[None, :]).astype(jnp.int32), axis=1)
    group = tile_bucket // PAIRS_PER_GROUP
    pair = tile_bucket % PAIRS_PER_GROUP
    tile_ea = group * EXPERTS_PER_GROUP + jnp.asarray(_PAIR_A, jnp.int32)[pair]
    tile_eb = group * EXPERTS_PER_GROUP + jnp.asarray(_PAIR_B, jnp.int32)[pair]
    return pos, tile_ea, tile_eb, n_used.reshape(1)


def kernel(x, c, w_ada, b_ada, norm_mix_g, w_in, v_norm_g, sgu_w, sgu_b, pool_w, pool_scale,
           w_branch_a, w_branch_b, w_out, norm_ffn_g, router_w, router_bias,
           w_exp_gate, w_exp_up, w_exp_down, final_norm_g):
    batch, seq, d = x.shape
    depth = w_ada.shape[0]
    t = batch * seq
    n_tiles_max = t // EXPERT_TILE + N_BUCKETS
    n_slots = n_tiles_max * EXPERT_TILE

    mod = _ada(c, w_ada, b_ada)
    rw_pad = jnp.pad(router_w, ((0, 0), (0, 128 - N_EXPERTS)))
    rw_hi = rw_pad.astype(_bf16)
    rw_lo = (rw_pad - rw_hi.astype(_f32)).astype(_bf16)
    router_w_pad = jnp.concatenate([rw_hi, rw_lo], axis=1)
    router_wt = router_w.T.reshape(N_EXPERTS, 1, d)
    final_g = final_norm_g.reshape(1, d)

    def layer_bf16(stacks, l):
        return tuple(w[l].astype(_bf16) for w in stacks)

    def tie(stacks, gate):
        return lax.optimization_barrier((stacks, gate))

    mixer_stacks = (w_in, sgu_w, pool_w, w_branch_a, w_branch_b, w_out)
    expert_stacks = (w_exp_gate, w_exp_up, w_exp_down)
    y_tok, gate_prev = None, None
    mixer_w, expert_w = layer_bf16(mixer_stacks, 0), layer_bf16(expert_stacks, 0)
    for l in range(depth):
        mod_l = mod[l].reshape(batch, 1, N_MOD * d)
        w_in_l, sgu_w_l, pool_w_l, w_a_l, w_b_l, w_o_l = mixer_w
        x, hf, route, counts = _mixer(
            x, y_tok, gate_prev, mod_l, norm_mix_g[l].reshape(1, d), norm_ffn_g[l].reshape(1, d),
            w_in_l, v_norm_g[l].reshape(1, SGU_WIDTH), sgu_w_l, sgu_b[l].T,
            pool_w_l, pool_scale[l].reshape(1, POOL_WIDTH), w_a_l, w_b_l, w_o_l,
            router_w_pad, router_bias)
        pos, tile_ea, tile_eb, n_used = _routing_tables(route, counts, n_tiles_max)
        if l + 1 < depth:
            mixer_stacks, pos = tie(mixer_stacks, pos)
            mixer_w = layer_bf16(mixer_stacks, l + 1)
        xs = _sc_gather_by_slot(pos, hf.reshape((t,) + PACKED_SLAB), n_slots, n_used * EXPERT_TILE)
        ys = _experts(tile_ea, tile_eb, n_used, xs, router_wt, *expert_w)
        if l + 1 < depth:
            expert_stacks, ys = tie(expert_stacks, ys)
            expert_w = layer_bf16(expert_stacks, l + 1)
        y_tok = _sc_gather_rows(pos, ys).reshape((batch, seq) + TOKEN_SLAB)
        gate_prev = mod_l[:, :, 5 * d:6 * d]
    return _final_norm(x, y_tok, gate_prev, final_g)
```

```python
import functools

import jax
import jax.numpy as jnp
from jax import lax
from jax.experimental import pallas as pl
from jax.experimental.pallas import tpu as pltpu
from jax.experimental.pallas import tpu_sc as plsc

D_MODEL = 1024
CHUNK = 128
SGU_HEADS = 8
SGU_HEAD_DIM = 128
SGU_WIDTH = 1024
POOL_WINDOWS = (2, 4, 8, 16)
POOL_GROUP_DIM = 256
POOL_WIDTH = 1024
HALO = 16
POOL_PAD = 8
N_EXPERTS = 16
N_EXPERT_GROUPS = 4
EXPERTS_PER_GROUP = 4
PAIRS_PER_GROUP = 6
N_BUCKETS = N_EXPERT_GROUPS * PAIRS_PER_GROUP
BUCKET_ROWS = 32
D_EXPERT = 512
N_MOD = 6
EPS = 1e-6

SEQ_TILE = 512
SUB_TILE = 256
MIXER_STAGES = 2
ROW_TILE = 512
EXPERT_TILE = 256
VMEM_LIMIT_BYTES = 58 * 1024 * 1024
PACKED_SLAB = (4, 128)
SC_CORES = 2
SC_SUBCORES = 16
SC_CHUNK = 32
SC_LANES = 16
SC_RING = 3

_PAIR_A = (0, 0, 0, 1, 1, 3)
_PAIR_B = (1, 2, 3, 3, 2, 2)

_bf16 = jnp.bfloat16
_f32 = jnp.float32


def _dot(a, b):
    return jnp.dot(a, b, preferred_element_type=_f32)


def _pack_bf16_pairs(x):
    rows, d = x.shape
    hi = lax.bitcast_convert_type(x[:, :d // 2].astype(_bf16).astype(_f32), jnp.uint32)
    lo = lax.bitcast_convert_type(x[:, d // 2:].astype(_bf16).astype(_f32), jnp.uint32)
    return (hi | (lo >> 16)).reshape((rows,) + PACKED_SLAB)


def _unpack_bf16_pairs(words):
    rows = words.shape[0]
    words = words.reshape(rows, PACKED_SLAB[0] * PACKED_SLAB[1])
    hi = lax.bitcast_convert_type(words & jnp.uint32(0xFFFF0000), _f32)
    lo = lax.bitcast_convert_type(words << 16, _f32)
    return jnp.concatenate([hi, lo], axis=1)


def _rms_modulate(x, g, shift, scale):
    y = x * lax.rsqrt(jnp.mean(x * x, axis=-1, keepdims=True) + EPS)
    return (y * g) * (1.0 + scale) + shift


def _ada_kernel(c_ref, w_ref, b_ref, o_ref):
    c = c_ref[...]
    c_act = (c * jax.nn.sigmoid(c)).astype(_bf16)
    o_ref[...] = _dot(c_act, w_ref[...].astype(_bf16)) + b_ref[...]


def _ada(c, w_ada, b_ada):
    depth, d, n = w_ada.shape
    batch = c.shape[0]
    tn = 2048
    return pl.pallas_call(
        _ada_kernel,
        grid=(depth, n // tn),
        in_specs=[
            pl.BlockSpec((batch, d), lambda l, j: (0, 0)),
            pl.BlockSpec((None, d, tn), lambda l, j: (l, 0, j)),
            pl.BlockSpec((None, 1, tn), lambda l, j: (l, 0, j)),
        ],
        out_specs=pl.BlockSpec((None, batch, tn), lambda l, j: (l, 0, j)),
        out_shape=jax.ShapeDtypeStruct((depth, batch, n), _f32),
        compiler_params=pltpu.CompilerParams(
            dimension_semantics=("arbitrary", "arbitrary"), vmem_limit_bytes=VMEM_LIMIT_BYTES),
        name="ada_modulation",
    )(c, w_ada, b_ada.reshape(depth, 1, n))


def _route(hf, rw_ref, rb_ref, carry_ref):
    ts = hf.shape[0]
    hf_hi = hf.astype(_bf16)
    hf_lo = (hf - hf_hi.astype(_f32)).astype(_bf16)
    both = _dot(hf_hi, rw_ref[...])
    logits = both[:, 0:128] + both[:, 128:256] + _dot(hf_lo, rw_ref[:, 0:128])
    lt = logits.T
    rows = [lt[e:e + 1, :] for e in range(N_EXPERTS)]
    m = functools.reduce(jnp.maximum, rows)
    ex = [jnp.exp(r - m) for r in rows]
    den = functools.reduce(lambda a, b: a + b, ex)
    probs = [e / den for e in ex]
    sel = [probs[e] + rb_ref[e] for e in range(N_EXPERTS)]

    def top2_sum(v):
        pairs = [v[i] + v[j] for i, j in zip(_PAIR_A, _PAIR_B)]
        return functools.reduce(jnp.maximum, pairs)

    gscore = [top2_sum(sel[4 * g:4 * g + 4]) for g in range(N_EXPERT_GROUPS)]
    best = gscore[0]
    gidx = jnp.zeros_like(best, dtype=jnp.int32)
    for g in range(1, N_EXPERT_GROUPS):
        better = gscore[g] > best
        best = jnp.where(better, gscore[g], best)
        gidx = jnp.where(better, g, gidx)
    ing = []
    for k in range(EXPERTS_PER_GROUP):
        v = sel[k]
        for g in range(1, N_EXPERT_GROUPS):
            v = jnp.where(gidx == g, sel[4 * g + k], v)
        ing.append(v)
    chosen = []
    for k in range(EXPERTS_PER_GROUP):
        r = jnp.zeros_like(gidx)
        for j in range(EXPERTS_PER_GROUP):
            if j == k:
                continue
            beats = (ing[j] >= ing[k]) if j < k else (ing[j] > ing[k])
            r = r + beats.astype(jnp.int32)
        chosen.append(r < 2)
    lo = jnp.where(chosen[0], 0, jnp.where(chosen[1], 1, 2))
    hi = jnp.where(chosen[3], 3, jnp.where(chosen[2], 2, 1))
    pair = jnp.where(lo == 0, hi - 1, jnp.where(lo == 2, 5, jnp.where(hi == 3, 3, 4)))
    bucket = gidx * PAIRS_PER_GROUP + pair

    brow = lax.broadcasted_iota(jnp.int32, (BUCKET_ROWS, ts), 0)
    onehot = (brow == bucket).astype(_f32)
    jj = lax.broadcasted_iota(jnp.int32, (ts, ts), 0)
    tt = lax.broadcasted_iota(jnp.int32, (ts, ts), 1)
    upper = (jj <= tt).astype(_bf16)
    cum = _dot(onehot.astype(_bf16), upper)
    carry = carry_ref[...][:, 0:1]
    rank = jnp.sum(onehot * (cum - 1.0 + carry), axis=0, keepdims=True)
    carry_ref[...] = carry_ref[...] + jnp.sum(onehot, axis=1, keepdims=True)
    return bucket.astype(_f32), rank


def _mixer_kernel(*refs, has_prev):
    if has_prev:
        x_ref, y_ref, gprev_ref = refs[:3]
        refs = refs[3:]
        x_in = lambda rows: x_ref[rows, :] + gprev_ref[...] * _unpack_bf16_pairs(y_ref[rows])
    else:
        x_ref = refs[0]
        refs = refs[1:]
        x_in = lambda rows: x_ref[rows, :]
    (mod_ref, gmix_ref, gffn_ref, win_ref, vg_ref, sw_ref, sb_ref, pw_ref, ps_ref, wa_ref, wb_ref,
     wo_ref, rw_ref, rb_ref, xo_ref, hf_ref, route_ref, counts_ref, ext_ref, sum2_ref, sum4_ref,
     carry_ref) = refs
    b = pl.program_id(0)
    s = pl.program_id(1)
    ts = x_ref.shape[0]
    d = D_MODEL

    @pl.when((b == 0) & (s == 0))
    def _():
        carry_ref[...] = jnp.zeros_like(carry_ref)
        ext_ref[0:POOL_PAD, :] = jnp.zeros((POOL_PAD, POOL_WIDTH), _f32)
        sum2_ref[:, 0:POOL_PAD, :] = jnp.zeros((sum2_ref.shape[0], POOL_PAD, sum2_ref.shape[2]), _f32)
        sum4_ref[:, 0:POOL_PAD, :] = jnp.zeros((sum4_ref.shape[0], POOL_PAD, sum4_ref.shape[2]), _f32)

    @pl.when(s == 0)
    def _():
        ext_ref[POOL_PAD:POOL_PAD + HALO, :] = jnp.zeros((HALO, POOL_WIDTH), _f32)

    mod = mod_ref[...]
    ci = lax.broadcasted_iota(jnp.int32, (CHUNK, CHUNK), 0)
    cj = lax.broadcasted_iota(jnp.int32, (CHUNK, CHUNK), 1)
    sgu_w = [jnp.where(ci >= cj, sw_ref[h], jnp.zeros((), _bf16)) for h in range(SGU_HEADS)]
    sub_tiles = [_mixer_rows(r * SUB_TILE, s * ts + r * SUB_TILE, mod, sgu_w,
                             x_in, gmix_ref, gffn_ref, win_ref, vg_ref, sb_ref, pw_ref, ps_ref, wa_ref,
                             wb_ref, wo_ref, rw_ref, rb_ref, xo_ref, hf_ref, route_ref, ext_ref,
                             sum2_ref.at[r], sum4_ref.at[r], carry_ref)
                 for r in range(ts // SUB_TILE)]
    for _ in range(MIXER_STAGES):
        for sub_tile in sub_tiles:
            next(sub_tile, None)
    ext_ref[POOL_PAD:POOL_PAD + HALO, :] = ext_ref[POOL_PAD + ts:POOL_PAD + ts + HALO, :]
    route_ref[2:8, :] = jnp.zeros((6, ts), _f32)
    counts_ref[...] = carry_ref[...]


def _mixer_rows(row0, seq_pos0, mod, sgu_w, x_in, gmix_ref, gffn_ref, win_ref, vg_ref, sb_ref,
                pw_ref, ps_ref, wa_ref, wb_ref, wo_ref, rw_ref, rb_ref, xo_ref, hf_ref, route_ref,
                ext_ref, sum2_ref, sum4_ref, carry_ref):
    d = D_MODEL
    ts = SUB_TILE
    rows = slice(row0, row0 + ts)
    sh_m, sc_m, g_m = mod[:, 0:d], mod[:, d:2 * d], mod[:, 2 * d:3 * d]
    sh_f, sc_f = mod[:, 3 * d:4 * d], mod[:, 4 * d:5 * d]
    x = x_in(rows)

    hb = _rms_modulate(x, gmix_ref[...], sh_m, sc_m).astype(_bf16)

    v = jax.nn.gelu(_dot(hb, win_ref[:, SGU_WIDTH:2 * SGU_WIDTH]))
    p = _dot(hb, win_ref[:, 2 * SGU_WIDTH:2 * SGU_WIDTH + POOL_WIDTH])
    u = jax.nn.gelu(_dot(hb, win_ref[:, 0:SGU_WIDTH]))
    gate_a = jax.nn.sigmoid(_dot(hb, win_ref[:, 3 * d:4 * d]))
    gate_b = jax.nn.sigmoid(_dot(hb, win_ref[:, 4 * d:5 * d]))
    e0 = POOL_PAD + HALO + row0
    ext_ref[e0:e0 + ts, :] = p
    yield

    vc = v - jnp.mean(v, axis=-1, keepdims=True)
    vn = (vc * lax.rsqrt(jnp.mean(vc * vc, axis=-1, keepdims=True) + EPS) * vg_ref[...]).astype(_bf16)
    n_chunks = ts // CHUNK
    ya_cols = []
    for h in range(SGU_HEADS):
        cols = slice(h * SGU_HEAD_DIM, (h + 1) * SGU_HEAD_DIM)
        rhs = jnp.concatenate([vn[n * CHUNK:(n + 1) * CHUNK, cols] for n in range(n_chunks)], axis=1)
        sg = _dot(sgu_w[h], rhs) + sb_ref[:, h:h + 1]
        s_h = jnp.concatenate([sg[:, n * SGU_HEAD_DIM:(n + 1) * SGU_HEAD_DIM] for n in range(n_chunks)],
                              axis=0)
        ya_cols.append((u[:, cols] * s_h).astype(_bf16))
    ya = jnp.concatenate(ya_cols, axis=1)
    merged = gate_a * _dot(ya, wa_ref[...])

    pos1 = (seq_pos0 + 1 + lax.broadcasted_iota(jnp.int32, (ts, 1), 0)).astype(_f32)
    g = POOL_GROUP_DIM
    lo, n = e0 - HALO, HALO + ts
    sum2 = ext_ref[lo:lo + n, :] + ext_ref[lo - 1:lo - 1 + n, :]
    sum2_ref[POOL_PAD:POOL_PAD + n, :] = sum2[:, g:]
    sum4 = sum2[:, g:] + sum2_ref[POOL_PAD - 2:POOL_PAD - 2 + n, :]
    sum4_ref[POOL_PAD:POOL_PAD + n, :] = sum4[:, g:]
    sum8 = sum4[:, g:] + sum4_ref[POOL_PAD - 4:POOL_PAD - 4 + n, :]
    sum16 = sum8[8:, g:] + sum8[:n - 8, g:]
    window_sums = (sum2[HALO:, 0:g], sum4[HALO:, 0:g], sum8[HALO:, 0:g], sum16[HALO - 8:, :])
    yb_cols = []
    for gi, w in enumerate(POOL_WINDOWS):
        cols = slice(gi * g, (gi + 1) * g)
        count = jnp.minimum(pos1, float(w))
        pooled = (window_sums[gi] / count - p[:, cols]).astype(_bf16)
        yb_cols.append(_dot(pooled, pw_ref[gi]))
    yb = (jnp.concatenate(yb_cols, axis=1) * ps_ref[...]).astype(_bf16)
    merged = merged + gate_b * _dot(yb, wb_ref[...])

    x_new = x + g_m * _dot(merged.astype(_bf16), wo_ref[...])
    xo_ref[rows, :] = x_new

    hf = _rms_modulate(x_new, gffn_ref[...], sh_f, sc_f)
    hf_ref[rows] = _pack_bf16_pairs(hf)
    bucket, rank = _route(hf, rw_ref, rb_ref, carry_ref)
    route_ref[0:1, rows] = bucket
    route_ref[1:2, rows] = rank
    yield


def _mixer(x, y_prev, gate_prev, mod_l, gmix, gffn, w_in, v_g, sgu_w, sgu_bt, pool_w, pool_scale,
           w_a, w_b, w_o, router_w_pad, router_bias):
    batch, seq, d = x.shape
    has_prev = y_prev is not None
    ts = SEQ_TILE
    n_tiles = batch * (seq // ts)
    tiles_per_seq = seq // ts
    const = lambda *shape: pl.BlockSpec(shape, lambda b, s: (0,) * len(shape),
                                        pipeline_mode=pl.Buffered(1))
    row_spec = pl.BlockSpec((None, ts, d), lambda b, s: (b, s, 0))
    slab_spec = pl.BlockSpec((None, ts) + PACKED_SLAB, lambda b, s: (b, s, 0, 0))
    prev_specs = [slab_spec, pl.BlockSpec((None, 1, d), lambda b, s: (b, 0, 0))] if has_prev else []
    prev_args = (y_prev, gate_prev) if has_prev else ()
    return pl.pallas_call(
        functools.partial(_mixer_kernel, has_prev=has_prev),
        grid=(batch, tiles_per_seq),
        in_specs=[row_spec] + prev_specs + [
            pl.BlockSpec((None, 1, N_MOD * d), lambda b, s: (b, 0, 0)),
            const(1, d), const(1, d),
            const(*w_in.shape),
            const(1, SGU_WIDTH),
            const(*sgu_w.shape),
            const(*sgu_bt.shape),
            const(*pool_w.shape),
            const(1, POOL_WIDTH),
            const(d, d), const(d, d), const(d, d),
            const(*router_w_pad.shape),
            pl.BlockSpec(memory_space=pltpu.SMEM),
        ],
        out_specs=[
            pl.BlockSpec((None, ts, d), lambda b, s: (b, s, 0)),
            pl.BlockSpec((None, ts) + PACKED_SLAB, lambda b, s: (b, s, 0, 0)),
            pl.BlockSpec((None, 8, ts), lambda b, s: (b * tiles_per_seq + s, 0, 0)),
            pl.BlockSpec((BUCKET_ROWS, 128), lambda b, s: (0, 0)),
        ],
        out_shape=[
            jax.ShapeDtypeStruct((batch, seq, d), _f32),
            jax.ShapeDtypeStruct((batch, seq) + PACKED_SLAB, jnp.uint32),
            jax.ShapeDtypeStruct((n_tiles, 8, ts), _f32),
            jax.ShapeDtypeStruct((BUCKET_ROWS, 128), _f32),
        ],
        scratch_shapes=[
            pltpu.VMEM((POOL_PAD + HALO + ts, POOL_WIDTH), _f32),
            pltpu.VMEM((ts // SUB_TILE, POOL_PAD + HALO + SUB_TILE, POOL_WIDTH - POOL_GROUP_DIM), _f32),
            pltpu.VMEM((ts // SUB_TILE, POOL_PAD + HALO + SUB_TILE, POOL_WIDTH - 2 * POOL_GROUP_DIM), _f32),
            pltpu.VMEM((BUCKET_ROWS, 128), _f32),
        ],
        compiler_params=pltpu.CompilerParams(
            dimension_semantics=("arbitrary", "arbitrary"), vmem_limit_bytes=VMEM_LIMIT_BYTES),
        name="mixer_router",
    )(x, *prev_args, mod_l, gmix, gffn, w_in, v_g, sgu_w, sgu_bt, pool_w, pool_scale, w_a, w_b, w_o,
      router_w_pad, router_bias)


def _sc_mesh():
    return plsc.VectorSubcoreMesh(core_axis_name="c", subcore_axis_name="s")


def _sc_worker():
    return lax.axis_index("s") * SC_CORES + lax.axis_index("c")


def _sc_gather_chunks(table_hbm, idx_v, out_hbm, out_row0, n_chunks, bufs, sems):
    depth = len(bufs)

    def gather(j, slot):
        return pltpu.make_async_copy(table_hbm.at[idx_v.at[j]], bufs[slot], sems[slot])

    for j in range(depth - 1):
        @pl.when(j < n_chunks)
        def _():
            gather(j, j).start()

    @pl.loop(0, n_chunks, step=depth)
    def _(j0):
        for slot in range(depth):
            j = j0 + slot

            @pl.when(j < n_chunks)
            def _():
                @pl.when(j + depth - 1 < n_chunks)
                def _():
                    gather(j + depth - 1, (slot + depth - 1) % depth).start()

                gather(j, slot).wait()
                pltpu.sync_copy(bufs[slot], out_hbm.at[pl.ds(out_row0 + j * SC_CHUNK, SC_CHUNK)])


def _sc_chunks_per_worker(n_rows):
    n_chunks = n_rows // (SC_CORES * SC_SUBCORES * SC_CHUNK)
    assert n_chunks * SC_CORES * SC_SUBCORES * SC_CHUNK == n_rows
    return n_chunks


def _sc_gather_rows(idx, table):
    n_out = idx.shape[0]
    n_chunks = _sc_chunks_per_worker(n_out)
    slab = table.shape[1:]
    buf = pltpu.VMEM((SC_CHUNK,) + slab, table.dtype)

    @functools.partial(
        pl.kernel, mesh=_sc_mesh(),
        out_type=jax.ShapeDtypeStruct((n_out,) + slab, table.dtype),
        scratch_types=[pltpu.VMEM((n_chunks, SC_CHUNK), jnp.int32)]
        + [buf] * SC_RING + [pltpu.SemaphoreType.DMA] * SC_RING,
    )
    def gather_kernel(table_hbm, idx_hbm, out_hbm, idx_v, *ring):
        chunk0 = _sc_worker() * n_chunks
        pltpu.sync_copy(idx_hbm.at[pl.ds(chunk0, n_chunks)], idx_v)
        _sc_gather_chunks(table_hbm, idx_v, out_hbm, chunk0 * SC_CHUNK, n_chunks,
                          ring[:SC_RING], ring[SC_RING:])

    return gather_kernel(table, idx.reshape(n_out // SC_CHUNK, SC_CHUNK))


def _sc_gather_by_slot(pos, table, n_slots, n_live_slots):
    n_tok = pos.shape[0]
    per_worker = n_slots // (SC_CORES * SC_SUBCORES)
    n_chunks = _sc_chunks_per_worker(n_slots)
    slab = table.shape[1:]
    buf = pltpu.VMEM((SC_CHUNK,) + slab, table.dtype)

    @functools.partial(
        pl.kernel, mesh=_sc_mesh(),
        compiler_params=pltpu.CompilerParams(needs_layout_passes=False),
        out_type=jax.ShapeDtypeStruct((n_slots,) + slab, table.dtype),
        scratch_types=[pltpu.VMEM((n_tok,), jnp.int32), pltpu.VMEM((n_chunks, SC_CHUNK), jnp.int32),
                       pltpu.VMEM((SC_LANES,), jnp.int32)]
        + [buf] * SC_RING + [pltpu.SemaphoreType.DMA] * SC_RING,
    )
    def slot_kernel(table_hbm, pos_hbm, live_hbm, out_hbm, pos_v, inv_v, live_v, *ring):
        slot0 = _sc_worker() * per_worker
        pltpu.sync_copy(pos_hbm, pos_v)
        pltpu.sync_copy(live_hbm, live_v)
        n_live_chunks = jnp.clip((jnp.max(live_v[...]) - slot0) // SC_CHUNK, 0, n_chunks)
        lane = lax.iota(jnp.int32, SC_LANES)

        @pl.loop(0, n_chunks)
        def _(j):
            for h in range(SC_CHUNK // SC_LANES):
                s = slot0 + j * SC_CHUNK + h * SC_LANES + lane
                inv_v[j, pl.ds(h * SC_LANES, SC_LANES)] = jnp.where(s >= n_tok, s - n_tok, s)

        @pl.loop(0, n_tok // SC_LANES)
        def _(i):
            local = pos_v[pl.ds(i * SC_LANES, SC_LANES)] - slot0
            mine = (local >= 0) & (local < per_worker)
            local = jnp.where(mine, local, 0)
            plsc.store_scatter(inv_v, [local // SC_CHUNK, local % SC_CHUNK], i * SC_LANES + lane, mask=mine)

        _sc_gather_chunks(table_hbm, inv_v, out_hbm, slot0, n_live_chunks, ring[:SC_RING], ring[SC_RING:])

    return slot_kernel(table, pos, jnp.broadcast_to(n_live_slots, (SC_LANES,)))


def _final_kernel(x_ref, y_ref, g_ref, fg_ref, o_ref):
    out = x_ref[...] + g_ref[...] * _unpack_bf16_pairs(y_ref[...])
    o_ref[...] = out * lax.rsqrt(jnp.mean(out * out, axis=-1, keepdims=True) + EPS) * fg_ref[...]


def _final_norm(x, y_tok, gate_f, final_g):
    batch, seq, d = x.shape
    row_spec = pl.BlockSpec((None, ROW_TILE, d), lambda b, s: (b, s, 0))
    return pl.pallas_call(
        _final_kernel,
        grid=(batch, seq // ROW_TILE),
        in_specs=[row_spec, pl.BlockSpec((None, ROW_TILE) + PACKED_SLAB, lambda b, s: (b, s, 0, 0)),
                  pl.BlockSpec((None, 1, d), lambda b, s: (b, 0, 0)),
                  pl.BlockSpec((1, d), lambda b, s: (0, 0))],
        out_specs=row_spec,
        out_shape=jax.ShapeDtypeStruct((batch, seq, d), _f32),
        compiler_params=pltpu.CompilerParams(dimension_semantics=("arbitrary", "arbitrary")),
        name="residual_final_norm",
    )(x, y_tok, gate_f, final_g)


def _experts_kernel(ea_ref, eb_ref, nused_ref, xs_ref, rwa_ref, rwb_ref,
                    wga_ref, wgb_ref, wua_ref, wub_ref, wda_ref, wdb_ref, ys_ref):
    i = pl.program_id(0)

    @pl.when(i < nused_ref[0])
    def _():
        x = _unpack_bf16_pairs(xs_ref[...])
        la = jnp.sum(x * rwa_ref[...], axis=-1, keepdims=True)
        lb = jnp.sum(x * rwb_ref[...], axis=-1, keepdims=True)
        wa = jax.nn.sigmoid(la - lb)
        wb = jax.nn.sigmoid(lb - la)
        xb = x.astype(_bf16)
        act_a = (jax.nn.silu(_dot(xb, wga_ref[...])) * _dot(xb, wua_ref[...]) * wa).astype(_bf16)
        act_b = (jax.nn.silu(_dot(xb, wgb_ref[...])) * _dot(xb, wub_ref[...]) * wb).astype(_bf16)
        y = _dot(act_a, wda_ref[...]) + _dot(act_b, wdb_ref[...])
        ys_ref[...] = _pack_bf16_pairs(y)

    @pl.when(i >= nused_ref[0])
    def _():
        ys_ref[...] = jnp.zeros_like(ys_ref)


def _experts(tile_ea, tile_eb, n_used, xs, router_wt, w_gate, w_up, w_down):
    n_slots = xs.shape[0]
    d = D_MODEL
    n_tiles = n_slots // EXPERT_TILE
    f = D_EXPERT

    def row(i, ea, eb, nu):
        return (jnp.maximum(jnp.minimum(i, nu[0] - 1), 0), 0, 0)

    grid_spec = pltpu.PrefetchScalarGridSpec(
        num_scalar_prefetch=3,
        grid=(n_tiles,),
        in_specs=[
            pl.BlockSpec((EXPERT_TILE,) + PACKED_SLAB, row),
            pl.BlockSpec((None, 1, d), lambda i, ea, eb, nu: (ea[i], 0, 0)),
            pl.BlockSpec((None, 1, d), lambda i, ea, eb, nu: (eb[i], 0, 0)),
            pl.BlockSpec((None, d, f), lambda i, ea, eb, nu: (ea[i], 0, 0)),
            pl.BlockSpec((None, d, f), lambda i, ea, eb, nu: (eb[i], 0, 0)),
            pl.BlockSpec((None, d, f), lambda i, ea, eb, nu: (ea[i], 0, 0)),
            pl.BlockSpec((None, d, f), lambda i, ea, eb, nu: (eb[i], 0, 0)),
            pl.BlockSpec((None, f, d), lambda i, ea, eb, nu: (ea[i], 0, 0)),
            pl.BlockSpec((None, f, d), lambda i, ea, eb, nu: (eb[i], 0, 0)),
        ],
        out_specs=pl.BlockSpec((EXPERT_TILE,) + PACKED_SLAB, lambda i, ea, eb, nu: (i, 0, 0)),
    )
    return pl.pallas_call(
        _experts_kernel,
        grid_spec=grid_spec,
        out_shape=jax.ShapeDtypeStruct((n_slots,) + PACKED_SLAB, jnp.uint32),
        compiler_params=pltpu.CompilerParams(
            dimension_semantics=("arbitrary",), vmem_limit_bytes=VMEM_LIMIT_BYTES),
        name="grouped_experts",
    )(tile_ea, tile_eb, n_used, xs, router_wt, router_wt, w_gate, w_gate, w_up, w_up, w_down, w_down)


def _routing_tables(route, counts, n_tiles_max):
    bucket = route[:, 0, :].reshape(-1).astype(jnp.int32)
    rank = route[:, 1, :].reshape(-1).astype(jnp.int32)
    cnt = counts[:N_BUCKETS, 0].astype(jnp.int32)
    tiles_b = (cnt + EXPERT_TILE - 1) // EXPERT_TILE
    tile_end = jnp.cumsum(tiles_b)
    tile_start = tile_end - tiles_b
    n_used = tile_end[-1]
    onehot = bucket[:, None] == jnp.arange(N_BUCKETS, dtype=jnp.int32)[None, :]
    pos = jnp.sum(jnp.where(onehot, (tile_start * EXPERT_TILE)[None, :], 0), axis=1) + rank
    tile_ids = jnp.minimum(jnp.arange(n_tiles_max, dtype=jnp.int32), n_used - 1)
    tile_bucket = jnp.sum((tile_ids[:, None] >= tile_end[None, :]).astype(jnp.int32), axis=1)
    group = tile_bucket // PAIRS_PER_GROUP
    pair = tile_bucket % PAIRS_PER_GROUP
    tile_ea = group * EXPERTS_PER_GROUP + jnp.asarray(_PAIR_A, jnp.int32)[pair]
    tile_eb = group * EXPERTS_PER_GROUP + jnp.asarray(_PAIR_B, jnp.int32)[pair]
    return pos, tile_ea, tile_eb, n_used.reshape(1)


def kernel(x, c, w_ada, b_ada, norm_mix_g, w_in, v_norm_g, sgu_w, sgu_b, pool_w, pool_scale,
           w_branch_a, w_branch_b, w_out, norm_ffn_g, router_w, router_bias,
           w_exp_gate, w_exp_up, w_exp_down, final_norm_g):
    batch, seq, d = x.shape
    depth = w_ada.shape[0]
    t = batch * seq
    n_tiles_max = t // EXPERT_TILE + N_BUCKETS
    n_slots = n_tiles_max * EXPERT_TILE

    mod = _ada(c, w_ada, b_ada)
    rw_pad = jnp.pad(router_w, ((0, 0), (0, 128 - N_EXPERTS)))
    rw_hi = rw_pad.astype(_bf16)
    rw_lo = (rw_pad - rw_hi.astype(_f32)).astype(_bf16)
    router_w_pad = jnp.concatenate([rw_hi, rw_lo], axis=1)
    router_wt = router_w.T.reshape(N_EXPERTS, 1, d)
    final_g = final_norm_g.reshape(1, d)

    def layer_bf16(stacks, l):
        return tuple(w[l].astype(_bf16) for w in stacks)

    def tie(stacks, gate):
        return lax.optimization_barrier((stacks, gate))

    mixer_stacks = (w_in, sgu_w, pool_w, w_branch_a, w_branch_b, w_out)
    expert_stacks = (w_exp_gate, w_exp_up, w_exp_down)
    y_tok, gate_prev = None, None
    mixer_w, expert_w = layer_bf16(mixer_stacks, 0), layer_bf16(expert_stacks, 0)
    for l in range(depth):
        mod_l = mod[l].reshape(batch, 1, N_MOD * d)
        w_in_l, sgu_w_l, pool_w_l, w_a_l, w_b_l, w_o_l = mixer_w
        x, hf, route, counts = _mixer(
            x, y_tok, gate_prev, mod_l, norm_mix_g[l].reshape(1, d), norm_ffn_g[l].reshape(1, d),
            w_in_l, v_norm_g[l].reshape(1, SGU_WIDTH), sgu_w_l, sgu_b[l].T,
            pool_w_l, pool_scale[l].reshape(1, POOL_WIDTH), w_a_l, w_b_l, w_o_l,
            router_w_pad, router_bias)
        pos, tile_ea, tile_eb, n_used = _routing_tables(route, counts, n_tiles_max)
        if l + 1 < depth:
            mixer_stacks, pos = tie(mixer_stacks, pos)
            mixer_w = layer_bf16(mixer_stacks, l + 1)
        xs = _sc_gather_by_slot(pos, hf.reshape((t,) + PACKED_SLAB), n_slots, n_used * EXPERT_TILE)
        ys = _experts(tile_ea, tile_eb, n_used, xs, router_wt, *expert_w)
        if l + 1 < depth:
            expert_stacks, ys = tie(expert_stacks, ys)
            expert_w = layer_bf16(expert_stacks, l + 1)
        y_tok = _sc_gather_rows(pos, ys).reshape((batch, seq) + PACKED_SLAB)
        gate_prev = mod_l[:, :, 5 * d:6 * d]
    return _final_norm(x, y_tok, gate_prev, final_g)
```

```python
import functools

import jax
import jax.numpy as jnp
from jax import lax
from jax.experimental import pallas as pl
from jax.experimental.pallas import tpu as pltpu
from jax.experimental.pallas import tpu_sc as plsc

D_MODEL = 1024
CHUNK = 128
SGU_HEADS = 8
SGU_HEAD_DIM = 128
SGU_WIDTH = 1024
POOL_WINDOWS = (2, 4, 8, 16)
POOL_GROUP_DIM = 256
POOL_WIDTH = 1024
HALO = 16
POOL_PAD = 8
N_EXPERTS = 16
N_EXPERT_GROUPS = 4
EXPERTS_PER_GROUP = 4
PAIRS_PER_GROUP = 6
N_BUCKETS = N_EXPERT_GROUPS * PAIRS_PER_GROUP
BUCKET_ROWS = 32
D_EXPERT = 512
N_MOD = 6
EPS = 1e-6

SEQ_TILE = 512
SUB_TILE = 256
MIXER_STAGES = 2
ROW_TILE = 512
EXPERT_TILE = 256
VMEM_LIMIT_BYTES = 58 * 1024 * 1024
PACKED_SLAB = (4, 128)
SC_CORES = 2
SC_SUBCORES = 16
SC_CHUNK = 32
SC_LANES = 16
SC_RING = 3

_PAIR_A = (0, 0, 0, 1, 1, 3)
_PAIR_B = (1, 2, 3, 3, 2, 2)

_bf16 = jnp.bfloat16
_f32 = jnp.float32


def _dot(a, b):
    return jnp.dot(a, b, preferred_element_type=_f32)


def _pack_bf16_pairs(x):
    rows, d = x.shape
    hi = lax.bitcast_convert_type(x[:, :d // 2].astype(_bf16).astype(_f32), jnp.uint32)
    lo = lax.bitcast_convert_type(x[:, d // 2:].astype(_bf16).astype(_f32), jnp.uint32)
    return (hi | (lo >> 16)).reshape((rows,) + PACKED_SLAB)


def _unpack_bf16_pairs(words):
    rows = words.shape[0]
    words = words.reshape(rows, PACKED_SLAB[0] * PACKED_SLAB[1])
    hi = lax.bitcast_convert_type(words & jnp.uint32(0xFFFF0000), _f32)
    lo = lax.bitcast_convert_type(words << 16, _f32)
    return jnp.concatenate([hi, lo], axis=1)


def _rms_modulate(x, g, shift, scale):
    y = x * lax.rsqrt(jnp.mean(x * x, axis=-1, keepdims=True) + EPS)
    return (y * g) * (1.0 + scale) + shift


def _ada_kernel(c_ref, w_ref, b_ref, o_ref):
    c = c_ref[...]
    c_act = (c * jax.nn.sigmoid(c)).astype(_bf16)
    o_ref[...] = _dot(c_act, w_ref[...].astype(_bf16)) + b_ref[...]


def _ada(c, w_ada, b_ada):
    depth, d, n = w_ada.shape
    batch = c.shape[0]
    tn = 2048
    return pl.pallas_call(
        _ada_kernel,
        grid=(depth, n // tn),
        in_specs=[
            pl.BlockSpec((batch, d), lambda l, j: (0, 0)),
            pl.BlockSpec((None, d, tn), lambda l, j: (l, 0, j)),
            pl.BlockSpec((None, 1, tn), lambda l, j: (l, 0, j)),
        ],
        out_specs=pl.BlockSpec((None, batch, tn), lambda l, j: (l, 0, j)),
        out_shape=jax.ShapeDtypeStruct((depth, batch, n), _f32),
        compiler_params=pltpu.CompilerParams(
            dimension_semantics=("arbitrary", "arbitrary"), vmem_limit_bytes=VMEM_LIMIT_BYTES),
        name="ada_modulation",
    )(c, w_ada, b_ada.reshape(depth, 1, n))


def _route(hf, rw_ref, rb_ref, carry_ref):
    ts = hf.shape[0]
    hf_hi = hf.astype(_bf16)
    hf_lo = (hf - hf_hi.astype(_f32)).astype(_bf16)
    both = _dot(hf_hi, rw_ref[...])
    logits = both[:, 0:128] + both[:, 128:256] + _dot(hf_lo, rw_ref[:, 0:128])
    lt = logits.T
    rows = [lt[e:e + 1, :] for e in range(N_EXPERTS)]
    m = functools.reduce(jnp.maximum, rows)
    ex = [jnp.exp(r - m) for r in rows]
    den = functools.reduce(lambda a, b: a + b, ex)
    probs = [e / den for e in ex]
    sel = [probs[e] + rb_ref[e] for e in range(N_EXPERTS)]

    def top2_sum(v):
        pairs = [v[i] + v[j] for i, j in zip(_PAIR_A, _PAIR_B)]
        return functools.reduce(jnp.maximum, pairs)

    gscore = [top2_sum(sel[4 * g:4 * g + 4]) for g in range(N_EXPERT_GROUPS)]
    best = gscore[0]
    gidx = jnp.zeros_like(best, dtype=jnp.int32)
    for g in range(1, N_EXPERT_GROUPS):
        better = gscore[g] > best
        best = jnp.where(better, gscore[g], best)
        gidx = jnp.where(better, g, gidx)
    ing = []
    for k in range(EXPERTS_PER_GROUP):
        v = sel[k]
        for g in range(1, N_EXPERT_GROUPS):
            v = jnp.where(gidx == g, sel[4 * g + k], v)
        ing.append(v)
    chosen = []
    for k in range(EXPERTS_PER_GROUP):
        r = jnp.zeros_like(gidx)
        for j in range(EXPERTS_PER_GROUP):
            if j == k:
                continue
            beats = (ing[j] >= ing[k]) if j < k else (ing[j] > ing[k])
            r = r + beats.astype(jnp.int32)
        chosen.append(r < 2)
    lo = jnp.where(chosen[0], 0, jnp.where(chosen[1], 1, 2))
    hi = jnp.where(chosen[3], 3, jnp.where(chosen[2], 2, 1))
    pair = jnp.where(lo == 0, hi - 1, jnp.where(lo == 2, 5, jnp.where(hi == 3, 3, 4)))
    bucket = gidx * PAIRS_PER_GROUP + pair

    brow = lax.broadcasted_iota(jnp.int32, (BUCKET_ROWS, ts), 0)
    onehot = (brow == bucket).astype(_f32)
    jj = lax.broadcasted_iota(jnp.int32, (ts, ts), 0)
    tt = lax.broadcasted_iota(jnp.int32, (ts, ts), 1)
    upper = (jj <= tt).astype(_bf16)
    cum = _dot(onehot.astype(_bf16), upper)
    carry = carry_ref[...][:, 0:1]
    rank = jnp.sum(onehot * (cum - 1.0 + carry), axis=0, keepdims=True)
    carry_ref[...] = carry_ref[...] + jnp.sum(onehot, axis=1, keepdims=True)
    return bucket.astype(_f32), rank


def _mixer_kernel(*refs, has_prev):
    if has_prev:
        x_ref, y_ref, gprev_ref = refs[:3]
        refs = refs[3:]
        x_in = lambda rows: x_ref[rows, :] + gprev_ref[...] * _unpack_bf16_pairs(y_ref[rows])
    else:
        x_ref = refs[0]
        refs = refs[1:]
        x_in = lambda rows: x_ref[rows, :]
    (mod_ref, gmix_ref, gffn_ref, win_ref, vg_ref, sw_ref, sb_ref, pw_ref, ps_ref, wa_ref, wb_ref,
     wo_ref, rw_ref, rb_ref, xo_ref, hf_ref, route_ref, counts_ref, ext_ref, sum2_ref, sum4_ref,
     carry_ref) = refs
    b = pl.program_id(0)
    s = pl.program_id(1)
    ts = x_ref.shape[0]
    d = D_MODEL

    @pl.when((b == 0) & (s == 0))
    def _():
        carry_ref[...] = jnp.zeros_like(carry_ref)
        ext_ref[0:POOL_PAD, :] = jnp.zeros((POOL_PAD, POOL_WIDTH), _f32)
        sum2_ref[:, 0:POOL_PAD, :] = jnp.zeros((sum2_ref.shape[0], POOL_PAD, sum2_ref.shape[2]), _f32)
        sum4_ref[:, 0:POOL_PAD, :] = jnp.zeros((sum4_ref.shape[0], POOL_PAD, sum4_ref.shape[2]), _f32)

    @pl.when(s == 0)
    def _():
        ext_ref[POOL_PAD:POOL_PAD + HALO, :] = jnp.zeros((HALO, POOL_WIDTH), _f32)

    mod = mod_ref[...]
    ci = lax.broadcasted_iota(jnp.int32, (CHUNK, CHUNK), 0)
    cj = lax.broadcasted_iota(jnp.int32, (CHUNK, CHUNK), 1)
    sgu_w = [jnp.where(ci >= cj, sw_ref[h], jnp.zeros((), _bf16)) for h in range(SGU_HEADS)]
    sub_tiles = [_mixer_rows(r * SUB_TILE, s * ts + r * SUB_TILE, mod, sgu_w,
                             x_in, gmix_ref, gffn_ref, win_ref, vg_ref, sb_ref, pw_ref, ps_ref, wa_ref,
                             wb_ref, wo_ref, rw_ref, rb_ref, xo_ref, hf_ref, route_ref, ext_ref,
                             sum2_ref.at[r], sum4_ref.at[r], carry_ref)
                 for r in range(ts // SUB_TILE)]
    for _ in range(MIXER_STAGES):
        for sub_tile in sub_tiles:
            next(sub_tile, None)
    ext_ref[POOL_PAD:POOL_PAD + HALO, :] = ext_ref[POOL_PAD + ts:POOL_PAD + ts + HALO, :]
    route_ref[2:8, :] = jnp.zeros((6, ts), _f32)
    counts_ref[...] = carry_ref[...]


def _mixer_rows(row0, seq_pos0, mod, sgu_w, x_in, gmix_ref, gffn_ref, win_ref, vg_ref, sb_ref,
                pw_ref, ps_ref, wa_ref, wb_ref, wo_ref, rw_ref, rb_ref, xo_ref, hf_ref, route_ref,
                ext_ref, sum2_ref, sum4_ref, carry_ref):
    d = D_MODEL
    ts = SUB_TILE
    rows = slice(row0, row0 + ts)
    sh_m, sc_m, g_m = mod[:, 0:d], mod[:, d:2 * d], mod[:, 2 * d:3 * d]
    sh_f, sc_f = mod[:, 3 * d:4 * d], mod[:, 4 * d:5 * d]
    x = x_in(rows)

    hb = _rms_modulate(x, gmix_ref[...], sh_m, sc_m).astype(_bf16)

    v = jax.nn.gelu(_dot(hb, win_ref[:, SGU_WIDTH:2 * SGU_WIDTH]))
    p = _dot(hb, win_ref[:, 2 * SGU_WIDTH:2 * SGU_WIDTH + POOL_WIDTH])
    u = jax.nn.gelu(_dot(hb, win_ref[:, 0:SGU_WIDTH]))
    gate_a = jax.nn.sigmoid(_dot(hb, win_ref[:, 3 * d:4 * d]))
    gate_b = jax.nn.sigmoid(_dot(hb, win_ref[:, 4 * d:5 * d]))
    e0 = POOL_PAD + HALO + row0
    ext_ref[e0:e0 + ts, :] = p
    yield

    vc = v - jnp.mean(v, axis=-1, keepdims=True)
    vn = (vc * lax.rsqrt(jnp.mean(vc * vc, axis=-1, keepdims=True) + EPS) * vg_ref[...]).astype(_bf16)
    n_chunks = ts // CHUNK
    ya_cols = []
    for h in range(SGU_HEADS):
        cols = slice(h * SGU_HEAD_DIM, (h + 1) * SGU_HEAD_DIM)
        rhs = jnp.concatenate([vn[n * CHUNK:(n + 1) * CHUNK, cols] for n in range(n_chunks)], axis=1)
        sg = _dot(sgu_w[h], rhs) + sb_ref[:, h:h + 1]
        s_h = jnp.concatenate([sg[:, n * SGU_HEAD_DIM:(n + 1) * SGU_HEAD_DIM] for n in range(n_chunks)],
                              axis=0)
        ya_cols.append((u[:, cols] * s_h).astype(_bf16))
    ya = jnp.concatenate(ya_cols, axis=1)
    merged = gate_a * _dot(ya, wa_ref[...])

    pos1 = (seq_pos0 + 1 + lax.broadcasted_iota(jnp.int32, (ts, 1), 0)).astype(_f32)
    g = POOL_GROUP_DIM
    lo, n = e0 - HALO, HALO + ts
    sum2 = ext_ref[lo:lo + n, :] + ext_ref[lo - 1:lo - 1 + n, :]
    sum2_ref[POOL_PAD:POOL_PAD + n, :] = sum2[:, g:]
    sum4 = sum2[:, g:] + sum2_ref[POOL_PAD - 2:POOL_PAD - 2 + n, :]
    sum4_ref[POOL_PAD:POOL_PAD + n, :] = sum4[:, g:]
    sum8 = sum4[:, g:] + sum4_ref[POOL_PAD - 4:POOL_PAD - 4 + n, :]
    sum16 = sum8[8:, g:] + sum8[:n - 8, g:]
    window_sums = (sum2[HALO:, 0:g], sum4[HALO:, 0:g], sum8[HALO:, 0:g], sum16[HALO - 8:, :])
    yb_cols = []
    for gi, w in enumerate(POOL_WINDOWS):
        cols = slice(gi * g, (gi + 1) * g)
        count = jnp.minimum(pos1, float(w))
        pooled = (window_sums[gi] / count - p[:, cols]).astype(_bf16)
        yb_cols.append(_dot(pooled, pw_ref[gi]))
    yb = (jnp.concatenate(yb_cols, axis=1) * ps_ref[...]).astype(_bf16)
    merged = merged + gate_b * _dot(yb, wb_ref[...])

    x_new = x + g_m * _dot(merged.astype(_bf16), wo_ref[...])
    xo_ref[rows, :] = x_new

    hf = _rms_modulate(x_new, gffn_ref[...], sh_f, sc_f)
    hf_ref[rows] = _pack_bf16_pairs(hf)
    bucket, rank = _route(hf, rw_ref, rb_ref, carry_ref)
    route_ref[0:1, rows] = bucket
    route_ref[1:2, rows] = rank
    yield


def _mixer(x, y_prev, gate_prev, mod_l, gmix, gffn, w_in, v_g, sgu_w, sgu_bt, pool_w, pool_scale,
           w_a, w_b, w_o, router_w_pad, router_bias):
    batch, seq, d = x.shape
    has_prev = y_prev is not None
    ts = SEQ_TILE
    n_tiles = batch * (seq // ts)
    tiles_per_seq = seq // ts
    const = lambda *shape: pl.BlockSpec(shape, lambda b, s: (0,) * len(shape),
                                        pipeline_mode=pl.Buffered(1))
    row_spec = pl.BlockSpec((None, ts, d), lambda b, s: (b, s, 0))
    slab_spec = pl.BlockSpec((None, ts) + PACKED_SLAB, lambda b, s: (b, s, 0, 0))
    prev_specs = [slab_spec, pl.BlockSpec((None, 1, d), lambda b, s: (b, 0, 0))] if has_prev else []
    prev_args = (y_prev, gate_prev) if has_prev else ()
    return pl.pallas_call(
        functools.partial(_mixer_kernel, has_prev=has_prev),
        grid=(batch, tiles_per_seq),
        in_specs=[row_spec] + prev_specs + [
            pl.BlockSpec((None, 1, N_MOD * d), lambda b, s: (b, 0, 0)),
            const(1, d), const(1, d),
            const(*w_in.shape),
            const(1, SGU_WIDTH),
            const(*sgu_w.shape),
            const(*sgu_bt.shape),
            const(*pool_w.shape),
            const(1, POOL_WIDTH),
            const(d, d), const(d, d), const(d, d),
            const(*router_w_pad.shape),
            pl.BlockSpec(memory_space=pltpu.SMEM),
        ],
        out_specs=[
            pl.BlockSpec((None, ts, d), lambda b, s: (b, s, 0)),
            pl.BlockSpec((None, ts) + PACKED_SLAB, lambda b, s: (b, s, 0, 0)),
            pl.BlockSpec((None, 8, ts), lambda b, s: (b * tiles_per_seq + s, 0, 0)),
            pl.BlockSpec((BUCKET_ROWS, 128), lambda b, s: (0, 0)),
        ],
        out_shape=[
            jax.ShapeDtypeStruct((batch, seq, d), _f32),
            jax.ShapeDtypeStruct((batch, seq) + PACKED_SLAB, jnp.uint32),
            jax.ShapeDtypeStruct((n_tiles, 8, ts), _f32),
            jax.ShapeDtypeStruct((BUCKET_ROWS, 128), _f32),
        ],
        scratch_shapes=[
            pltpu.VMEM((POOL_PAD + HALO + ts, POOL_WIDTH), _f32),
            pltpu.VMEM((ts // SUB_TILE, POOL_PAD + HALO + SUB_TILE, POOL_WIDTH - POOL_GROUP_DIM), _f32),
            pltpu.VMEM((ts // SUB_TILE, POOL_PAD + HALO + SUB_TILE, POOL_WIDTH - 2 * POOL_GROUP_DIM), _f32),
            pltpu.VMEM((BUCKET_ROWS, 128), _f32),
        ],
        compiler_params=pltpu.CompilerParams(
            dimension_semantics=("arbitrary", "arbitrary"), vmem_limit_bytes=VMEM_LIMIT_BYTES),
        name="mixer_router",
    )(x, *prev_args, mod_l, gmix, gffn, w_in, v_g, sgu_w, sgu_bt, pool_w, pool_scale, w_a, w_b, w_o,
      router_w_pad, router_bias)


def _sc_mesh():
    return plsc.VectorSubcoreMesh(core_axis_name="c", subcore_axis_name="s")


def _sc_worker():
    return lax.axis_index("s") * SC_CORES + lax.axis_index("c")


def _sc_gather_chunks(table_hbm, idx_v, out_hbm, out_row0, n_chunks, bufs, sems):
    depth = len(bufs)

    def gather(j, slot):
        return pltpu.make_async_copy(table_hbm.at[idx_v.at[j]], bufs[slot], sems[slot])

    for j in range(depth - 1):
        @pl.when(j < n_chunks)
        def _():
            gather(j, j).start()

    @pl.loop(0, n_chunks, step=depth)
    def _(j0):
        for slot in range(depth):
            j = j0 + slot

            @pl.when(j < n_chunks)
            def _():
                @pl.when(j + depth - 1 < n_chunks)
                def _():
                    gather(j + depth - 1, (slot + depth - 1) % depth).start()

                gather(j, slot).wait()
                pltpu.sync_copy(bufs[slot], out_hbm.at[pl.ds(out_row0 + j * SC_CHUNK, SC_CHUNK)])


def _sc_chunks_per_worker(n_rows):
    n_chunks = n_rows // (SC_CORES * SC_SUBCORES * SC_CHUNK)
    assert n_chunks * SC_CORES * SC_SUBCORES * SC_CHUNK == n_rows
    return n_chunks


def _sc_gather_rows(idx, table):
    n_out = idx.shape[0]
    n_chunks = _sc_chunks_per_worker(n_out)
    slab = table.shape[1:]
    buf = pltpu.VMEM((SC_CHUNK,) + slab, table.dtype)

    @functools.partial(
        pl.kernel, mesh=_sc_mesh(),
        out_type=jax.ShapeDtypeStruct((n_out,) + slab, table.dtype),
        scratch_types=[pltpu.VMEM((n_chunks, SC_CHUNK), jnp.int32)]
        + [buf] * SC_RING + [pltpu.SemaphoreType.DMA] * SC_RING,
    )
    def gather_kernel(table_hbm, idx_hbm, out_hbm, idx_v, *ring):
        chunk0 = _sc_worker() * n_chunks
        pltpu.sync_copy(idx_hbm.at[pl.ds(chunk0, n_chunks)], idx_v)
        _sc_gather_chunks(table_hbm, idx_v, out_hbm, chunk0 * SC_CHUNK, n_chunks,
                          ring[:SC_RING], ring[SC_RING:])

    return gather_kernel(table, idx.reshape(n_out // SC_CHUNK, SC_CHUNK))


def _sc_gather_by_slot(pos, table, n_slots, n_live_slots):
    n_tok = pos.shape[0]
    per_worker = n_slots // (SC_CORES * SC_SUBCORES)
    n_chunks = _sc_chunks_per_worker(n_slots)
    slab = table.shape[1:]
    buf = pltpu.VMEM((SC_CHUNK,) + slab, table.dtype)

    @functools.partial(
        pl.kernel, mesh=_sc_mesh(),
        compiler_params=pltpu.CompilerParams(needs_layout_passes=False),
        out_type=jax.ShapeDtypeStruct((n_slots,) + slab, table.dtype),
        scratch_types=[pltpu.VMEM((n_tok,), jnp.int32), pltpu.VMEM((n_chunks, SC_CHUNK), jnp.int32),
                       pltpu.VMEM((SC_LANES,), jnp.int32)]
        + [buf] * SC_RING + [pltpu.SemaphoreType.DMA] * SC_RING,
    )
    def slot_kernel(table_hbm, pos_hbm, live_hbm, out_hbm, pos_v, inv_v, live_v, *ring):
        slot0 = _sc_worker() * per_worker
        pltpu.sync_copy(pos_hbm, pos_v)
        pltpu.sync_copy(live_hbm, live_v)
        n_live_chunks = jnp.clip((jnp.max(live_v[...]) - slot0) // SC_CHUNK, 0, n_chunks)
        lane = lax.iota(jnp.int32, SC_LANES)

        @pl.loop(0, n_chunks)
        def _(j):
            for h in range(SC_CHUNK // SC_LANES):
                s = slot0 + j * SC_CHUNK + h * SC_LANES + lane
                inv_v[j, pl.ds(h * SC_LANES, SC_LANES)] = jnp.where(s >= n_tok, s - n_tok, s)

        @pl.loop(0, n_tok // SC_LANES)
        def _(i):
            local = pos_v[pl.ds(i * SC_LANES, SC_LANES)] - slot0
            mine = (local >= 0) & (local < per_worker)
            local = jnp.where(mine, local, 0)
            plsc.store_scatter(inv_v, [local // SC_CHUNK, local % SC_CHUNK], i * SC_LANES + lane, mask=mine)

        _sc_gather_chunks(table_hbm, inv_v, out_hbm, slot0, n_live_chunks, ring[:SC_RING], ring[SC_RING:])

    return slot_kernel(table, pos, jnp.broadcast_to(n_live_slots, (SC_LANES,)))


def _final_kernel(x_ref, y_ref, g_ref, fg_ref, o_ref):
    out = x_ref[...] + g_ref[...] * _unpack_bf16_pairs(y_ref[...])
    o_ref[...] = out * lax.rsqrt(jnp.mean(out * out, axis=-1, keepdims=True) + EPS) * fg_ref[...]


def _final_norm(x, y_tok, gate_f, final_g):
    batch, seq, d = x.shape
    row_spec = pl.BlockSpec((None, ROW_TILE, d), lambda b, s: (b, s, 0))
    return pl.pallas_call(
        _final_kernel,
        grid=(batch, seq // ROW_TILE),
        in_specs=[row_spec, pl.BlockSpec((None, ROW_TILE) + PACKED_SLAB, lambda b, s: (b, s, 0, 0)),
                  pl.BlockSpec((None, 1, d), lambda b, s: (b, 0, 0)),
                  pl.BlockSpec((1, d), lambda b, s: (0, 0))],
        out_specs=row_spec,
        out_shape=jax.ShapeDtypeStruct((batch, seq, d), _f32),
        compiler_params=pltpu.CompilerParams(dimension_semantics=("arbitrary", "arbitrary")),
        name="residual_final_norm",
    )(x, y_tok, gate_f, final_g)


def _experts_kernel(ea_ref, eb_ref, nused_ref, xs_ref, rw_ref, wgua_ref, wgub_ref, wda_ref, wdb_ref, ys_ref):
    i = pl.program_id(0)

    @pl.when(i < nused_ref[0])
    def _():
        x = _unpack_bf16_pairs(xs_ref[...])
        la = jnp.sum(x * rw_ref[ea_ref[i]], axis=-1, keepdims=True)
        lb = jnp.sum(x * rw_ref[eb_ref[i]], axis=-1, keepdims=True)
        xb = x.astype(_bf16)

        def expert(wgu_ref, wd_ref, weight):
            gate_up = _dot(xb, wgu_ref[...])
            act = jax.nn.silu(gate_up[:, :D_EXPERT]) * gate_up[:, D_EXPERT:] * weight
            return _dot(act.astype(_bf16), wd_ref[...])

        y = (expert(wgua_ref, wda_ref, jax.nn.sigmoid(la - lb))
             + expert(wgub_ref, wdb_ref, jax.nn.sigmoid(lb - la)))
        ys_ref[...] = _pack_bf16_pairs(y)

    @pl.when(i >= nused_ref[0])
    def _():
        ys_ref[...] = jnp.zeros_like(ys_ref)


def _experts(tile_ea, tile_eb, n_used, xs, router_wt, w_gate_up, w_down):
    n_slots = xs.shape[0]
    d = D_MODEL
    n_tiles = n_slots // EXPERT_TILE
    f = D_EXPERT

    def row(i, ea, eb, nu):
        return (jnp.maximum(jnp.minimum(i, nu[0] - 1), 0), 0, 0)

    grid_spec = pltpu.PrefetchScalarGridSpec(
        num_scalar_prefetch=3,
        grid=(n_tiles,),
        in_specs=[
            pl.BlockSpec((EXPERT_TILE,) + PACKED_SLAB, row),
            pl.BlockSpec(router_wt.shape, lambda i, ea, eb, nu: (0, 0, 0)),
            pl.BlockSpec((None, d, 2 * f), lambda i, ea, eb, nu: (ea[i], 0, 0)),
            pl.BlockSpec((None, d, 2 * f), lambda i, ea, eb, nu: (eb[i], 0, 0)),
            pl.BlockSpec((None, f, d), lambda i, ea, eb, nu: (ea[i], 0, 0)),
            pl.BlockSpec((None, f, d), lambda i, ea, eb, nu: (eb[i], 0, 0)),
        ],
        out_specs=pl.BlockSpec((EXPERT_TILE,) + PACKED_SLAB, lambda i, ea, eb, nu: (i, 0, 0)),
    )
    return pl.pallas_call(
        _experts_kernel,
        grid_spec=grid_spec,
        out_shape=jax.ShapeDtypeStruct((n_slots,) + PACKED_SLAB, jnp.uint32),
        compiler_params=pltpu.CompilerParams(
            dimension_semantics=("arbitrary",), vmem_limit_bytes=VMEM_LIMIT_BYTES),
        name="grouped_experts",
    )(tile_ea, tile_eb, n_used, xs, router_wt, w_gate_up, w_gate_up, w_down, w_down)


def _routing_tables(route, counts, n_tiles_max):
    bucket = route[:, 0, :].reshape(-1).astype(jnp.int32)
    rank = route[:, 1, :].reshape(-1).astype(jnp.int32)
    cnt = counts[:N_BUCKETS, 0].astype(jnp.int32)
    tiles_b = (cnt + EXPERT_TILE - 1) // EXPERT_TILE
    tile_end = jnp.cumsum(tiles_b)
    tile_start = tile_end - tiles_b
    n_used = tile_end[-1]
    onehot = bucket[:, None] == jnp.arange(N_BUCKETS, dtype=jnp.int32)[None, :]
    pos = jnp.sum(jnp.where(onehot, (tile_start * EXPERT_TILE)[None, :], 0), axis=1) + rank
    tile_ids = jnp.minimum(jnp.arange(n_tiles_max, dtype=jnp.int32), n_used - 1)
    tile_bucket = jnp.sum((tile_ids[:, None] >= tile_end[None, :]).astype(jnp.int32), axis=1)
    group = tile_bucket // PAIRS_PER_GROUP
    pair = tile_bucket % PAIRS_PER_GROUP
    tile_ea = group * EXPERTS_PER_GROUP + jnp.asarray(_PAIR_A, jnp.int32)[pair]
    tile_eb = group * EXPERTS_PER_GROUP + jnp.asarray(_PAIR_B, jnp.int32)[pair]
    return pos, tile_ea, tile_eb, n_used.reshape(1)


def kernel(x, c, w_ada, b_ada, norm_mix_g, w_in, v_norm_g, sgu_w, sgu_b, pool_w, pool_scale,
           w_branch_a, w_branch_b, w_out, norm_ffn_g, router_w, router_bias,
           w_exp_gate, w_exp_up, w_exp_down, final_norm_g):
    batch, seq, d = x.shape
    depth = w_ada.shape[0]
    t = batch * seq
    n_tiles_max = t // EXPERT_TILE + N_BUCKETS
    n_slots = n_tiles_max * EXPERT_TILE

    mod = _ada(c, w_ada, b_ada)
    rw_pad = jnp.pad(router_w, ((0, 0), (0, 128 - N_EXPERTS)))
    rw_hi = rw_pad.astype(_bf16)
    rw_lo = (rw_pad - rw_hi.astype(_f32)).astype(_bf16)
    router_w_pad = jnp.concatenate([rw_hi, rw_lo], axis=1)
    router_wt = router_w.T.reshape(N_EXPERTS, 1, d)
    final_g = final_norm_g.reshape(1, d)

    def layer_bf16(stacks, l):
        return tuple(w[l].astype(_bf16) for w in stacks)

    def expert_layer_bf16(stacks, l):
        gate, up, down = stacks
        return jnp.concatenate([gate[l], up[l]], axis=-1).astype(_bf16), down[l].astype(_bf16)

    def tie(stacks, gate):
        return lax.optimization_barrier((stacks, gate))

    mixer_stacks = (w_in, sgu_w, pool_w, w_branch_a, w_branch_b, w_out)
    expert_stacks = (w_exp_gate, w_exp_up, w_exp_down)
    y_tok, gate_prev = None, None
    mixer_w, expert_w = layer_bf16(mixer_stacks, 0), expert_layer_bf16(expert_stacks, 0)
    for l in range(depth):
        mod_l = mod[l].reshape(batch, 1, N_MOD * d)
        w_in_l, sgu_w_l, pool_w_l, w_a_l, w_b_l, w_o_l = mixer_w
        x, hf, route, counts = _mixer(
            x, y_tok, gate_prev, mod_l, norm_mix_g[l].reshape(1, d), norm_ffn_g[l].reshape(1, d),
            w_in_l, v_norm_g[l].reshape(1, SGU_WIDTH), sgu_w_l, sgu_b[l].T,
            pool_w_l, pool_scale[l].reshape(1, POOL_WIDTH), w_a_l, w_b_l, w_o_l,
            router_w_pad, router_bias)
        pos, tile_ea, tile_eb, n_used = _routing_tables(route, counts, n_tiles_max)
        if l + 1 < depth:
            mixer_stacks, pos = tie(mixer_stacks, pos)
            mixer_w = layer_bf16(mixer_stacks, l + 1)
        xs = _sc_gather_by_slot(pos, hf.reshape((t,) + PACKED_SLAB), n_slots, n_used * EXPERT_TILE)
        ys = _experts(tile_ea, tile_eb, n_used, xs, router_wt, *expert_w)
        if l + 1 < depth:
            expert_stacks, ys = tie(expert_stacks, ys)
            expert_w = expert_layer_bf16(expert_stacks, l + 1)
        y_tok = _sc_gather_rows(pos, ys).reshape((batch, seq) + PACKED_SLAB)
        gate_prev = mod_l[:, :, 5 * d:6 * d]
    return _final_norm(x, y_tok, gate_prev, final_g)
```

```python
import functools

import jax
import jax.numpy as jnp
from jax import lax
from jax.experimental import pallas as pl
from jax.experimental.pallas import tpu as pltpu
from jax.experimental.pallas import tpu_sc as plsc

D_MODEL = 1024
CHUNK = 128
SGU_HEADS = 8
SGU_HEAD_DIM = 128
SGU_WIDTH = 1024
POOL_WINDOWS = (2, 4, 8, 16)
POOL_GROUP_DIM = 256
POOL_WIDTH = 1024
HALO = 16
POOL_PAD = 8
N_EXPERTS = 16
N_EXPERT_GROUPS = 4
EXPERTS_PER_GROUP = 4
PAIRS_PER_GROUP = 6
N_BUCKETS = N_EXPERT_GROUPS * PAIRS_PER_GROUP
BUCKET_ROWS = 32
D_EXPERT = 512
N_MOD = 6
EPS = 1e-6

SEQ_TILE = 512
SUB_TILE = 256
MIXER_STAGES = 2
ROW_TILE = 512
EXPERT_TILE = 256
VMEM_LIMIT_BYTES = 58 * 1024 * 1024
PACKED_SLAB = (4, 128)
SC_CORES = 2
SC_SUBCORES = 16
SC_CHUNK = 32
SC_LANES = 16
SC_RING = 3

_PAIR_A = (0, 0, 0, 1, 1, 3)
_PAIR_B = (1, 2, 3, 3, 2, 2)

_bf16 = jnp.bfloat16
_f32 = jnp.float32


def _dot(a, b):
    return jnp.dot(a, b, preferred_element_type=_f32)


def _pack_bf16_pairs(x):
    rows, d = x.shape
    hi = lax.bitcast_convert_type(x[:, :d // 2].astype(_bf16).astype(_f32), jnp.uint32)
    lo = lax.bitcast_convert_type(x[:, d // 2:].astype(_bf16).astype(_f32), jnp.uint32)
    return (hi | (lo >> 16)).reshape((rows,) + PACKED_SLAB)


def _unpack_bf16_pairs(words):
    rows = words.shape[0]
    words = words.reshape(rows, PACKED_SLAB[0] * PACKED_SLAB[1])
    hi = lax.bitcast_convert_type(words & jnp.uint32(0xFFFF0000), _f32)
    lo = lax.bitcast_convert_type(words << 16, _f32)
    return jnp.concatenate([hi, lo], axis=1)


def _rms_modulate(x, g, shift, scale):
    y = x * lax.rsqrt(jnp.mean(x * x, axis=-1, keepdims=True) + EPS)
    return (y * g) * (1.0 + scale) + shift


def _ada_kernel(c_ref, w_ref, b_ref, o_ref):
    c = c_ref[...]
    c_act = (c * jax.nn.sigmoid(c)).astype(_bf16)
    o_ref[...] = _dot(c_act, w_ref[...].astype(_bf16)) + b_ref[...]


def _ada(c, w_ada, b_ada):
    depth, d, n = w_ada.shape
    batch = c.shape[0]
    tn = 2048
    return pl.pallas_call(
        _ada_kernel,
        grid=(depth, n // tn),
        in_specs=[
            pl.BlockSpec((batch, d), lambda l, j: (0, 0)),
            pl.BlockSpec((None, d, tn), lambda l, j: (l, 0, j)),
            pl.BlockSpec((None, 1, tn), lambda l, j: (l, 0, j)),
        ],
        out_specs=pl.BlockSpec((None, batch, tn), lambda l, j: (l, 0, j)),
        out_shape=jax.ShapeDtypeStruct((depth, batch, n), _f32),
        compiler_params=pltpu.CompilerParams(
            dimension_semantics=("arbitrary", "arbitrary"), vmem_limit_bytes=VMEM_LIMIT_BYTES),
        name="ada_modulation",
    )(c, w_ada, b_ada.reshape(depth, 1, n))


def _route(hf, rw_ref, rb_ref, carry_ref):
    ts = hf.shape[0]
    hf_hi = hf.astype(_bf16)
    hf_lo = (hf - hf_hi.astype(_f32)).astype(_bf16)
    both = _dot(hf_hi, rw_ref[...])
    logits = both[:, 0:128] + both[:, 128:256] + _dot(hf_lo, rw_ref[:, 0:128])
    lt = logits.T
    rows = [lt[e:e + 1, :] for e in range(N_EXPERTS)]
    m = functools.reduce(jnp.maximum, rows)
    ex = [jnp.exp(r - m) for r in rows]
    den = functools.reduce(lambda a, b: a + b, ex)
    probs = [e / den for e in ex]
    sel = [probs[e] + rb_ref[e] for e in range(N_EXPERTS)]

    def top2_sum(v):
        pairs = [v[i] + v[j] for i, j in zip(_PAIR_A, _PAIR_B)]
        return functools.reduce(jnp.maximum, pairs)

    gscore = [top2_sum(sel[4 * g:4 * g + 4]) for g in range(N_EXPERT_GROUPS)]
    best = gscore[0]
    gidx = jnp.zeros_like(best, dtype=jnp.int32)
    for g in range(1, N_EXPERT_GROUPS):
        better = gscore[g] > best
        best = jnp.where(better, gscore[g], best)
        gidx = jnp.where(better, g, gidx)
    ing = []
    for k in range(EXPERTS_PER_GROUP):
        v = sel[k]
        for g in range(1, N_EXPERT_GROUPS):
            v = jnp.where(gidx == g, sel[4 * g + k], v)
        ing.append(v)
    chosen = []
    for k in range(EXPERTS_PER_GROUP):
        r = jnp.zeros_like(gidx)
        for j in range(EXPERTS_PER_GROUP):
            if j == k:
                continue
            beats = (ing[j] >= ing[k]) if j < k else (ing[j] > ing[k])
            r = r + beats.astype(jnp.int32)
        chosen.append(r < 2)
    lo = jnp.where(chosen[0], 0, jnp.where(chosen[1], 1, 2))
    hi = jnp.where(chosen[3], 3, jnp.where(chosen[2], 2, 1))
    pair = jnp.where(lo == 0, hi - 1, jnp.where(lo == 2, 5, jnp.where(hi == 3, 3, 4)))
    bucket = gidx * PAIRS_PER_GROUP + pair

    brow = lax.broadcasted_iota(jnp.int32, (BUCKET_ROWS, ts), 0)
    onehot = (brow == bucket).astype(_f32)
    jj = lax.broadcasted_iota(jnp.int32, (ts, ts), 0)
    tt = lax.broadcasted_iota(jnp.int32, (ts, ts), 1)
    upper = (jj <= tt).astype(_bf16)
    cum = _dot(onehot.astype(_bf16), upper)
    carry = carry_ref[...][:, 0:1]
    rank = jnp.sum(onehot * (cum - 1.0 + carry), axis=0, keepdims=True)
    carry_ref[...] = carry_ref[...] + jnp.sum(onehot, axis=1, keepdims=True)
    return bucket.astype(_f32), rank


def _mixer_kernel(*refs, has_prev):
    if has_prev:
        x_ref, y_ref, gprev_ref = refs[:3]
        refs = refs[3:]
        x_in = lambda rows: x_ref[rows, :] + gprev_ref[...] * _unpack_bf16_pairs(y_ref[rows])
    else:
        x_ref = refs[0]
        refs = refs[1:]
        x_in = lambda rows: x_ref[rows, :]
    (mod_ref, gmix_ref, gffn_ref, win_ref, vg_ref, sw_ref, sb_ref, pw_ref, ps_ref, wa_ref, wb_ref,
     wo_ref, rw_ref, rb_ref, xo_ref, hf_ref, route_ref, counts_ref, ext_ref, sum2_ref, sum4_ref,
     carry_ref) = refs
    b = pl.program_id(0)
    s = pl.program_id(1)
    ts = x_ref.shape[0]
    d = D_MODEL

    @pl.when((b == 0) & (s == 0))
    def _():
        carry_ref[...] = jnp.zeros_like(carry_ref)
        ext_ref[0:POOL_PAD, :] = jnp.zeros((POOL_PAD, POOL_WIDTH), _f32)
        sum2_ref[:, 0:POOL_PAD, :] = jnp.zeros((sum2_ref.shape[0], POOL_PAD, sum2_ref.shape[2]), _f32)
        sum4_ref[:, 0:POOL_PAD, :] = jnp.zeros((sum4_ref.shape[0], POOL_PAD, sum4_ref.shape[2]), _f32)

    @pl.when(s == 0)
    def _():
        ext_ref[POOL_PAD:POOL_PAD + HALO, :] = jnp.zeros((HALO, POOL_WIDTH), _f32)

    mod = mod_ref[...]
    ci = lax.broadcasted_iota(jnp.int32, (CHUNK, CHUNK), 0)
    cj = lax.broadcasted_iota(jnp.int32, (CHUNK, CHUNK), 1)
    sgu_w = [jnp.where(ci >= cj, sw_ref[h], jnp.zeros((), _bf16)) for h in range(SGU_HEADS)]
    sub_tiles = [_mixer_rows(r * SUB_TILE, s * ts + r * SUB_TILE, mod, sgu_w,
                             x_in, gmix_ref, gffn_ref, win_ref, vg_ref, sb_ref, pw_ref, ps_ref, wa_ref,
                             wb_ref, wo_ref, rw_ref, rb_ref, xo_ref, hf_ref, route_ref, ext_ref,
                             sum2_ref.at[r], sum4_ref.at[r], carry_ref)
                 for r in range(ts // SUB_TILE)]
    for _ in range(MIXER_STAGES):
        for sub_tile in sub_tiles:
            next(sub_tile, None)
    ext_ref[POOL_PAD:POOL_PAD + HALO, :] = ext_ref[POOL_PAD + ts:POOL_PAD + ts + HALO, :]
    route_ref[2:8, :] = jnp.zeros((6, ts), _f32)
    counts_ref[...] = carry_ref[...]


def _mixer_rows(row0, seq_pos0, mod, sgu_w, x_in, gmix_ref, gffn_ref, win_ref, vg_ref, sb_ref,
                pw_ref, ps_ref, wa_ref, wb_ref, wo_ref, rw_ref, rb_ref, xo_ref, hf_ref, route_ref,
                ext_ref, sum2_ref, sum4_ref, carry_ref):
    d = D_MODEL
    ts = SUB_TILE
    rows = slice(row0, row0 + ts)
    sh_m, sc_m, g_m = mod[:, 0:d], mod[:, d:2 * d], mod[:, 2 * d:3 * d]
    sh_f, sc_f = mod[:, 3 * d:4 * d], mod[:, 4 * d:5 * d]
    x = x_in(rows)

    hb = _rms_modulate(x, gmix_ref[...], sh_m, sc_m).astype(_bf16)

    v = jax.nn.gelu(_dot(hb, win_ref[:, SGU_WIDTH:2 * SGU_WIDTH]))
    p = _dot(hb, win_ref[:, 2 * SGU_WIDTH:2 * SGU_WIDTH + POOL_WIDTH])
    u = jax.nn.gelu(_dot(hb, win_ref[:, 0:SGU_WIDTH]))
    gate_a = jax.nn.sigmoid(_dot(hb, win_ref[:, 3 * d:4 * d]))
    gate_b = jax.nn.sigmoid(_dot(hb, win_ref[:, 4 * d:5 * d]))
    e0 = POOL_PAD + HALO + row0
    ext_ref[e0:e0 + ts, :] = p
    yield

    vc = v - jnp.mean(v, axis=-1, keepdims=True)
    vn = (vc * lax.rsqrt(jnp.mean(vc * vc, axis=-1, keepdims=True) + EPS) * vg_ref[...]).astype(_bf16)
    n_chunks = ts // CHUNK
    ya_cols = []
    for h in range(SGU_HEADS):
        cols = slice(h * SGU_HEAD_DIM, (h + 1) * SGU_HEAD_DIM)
        rhs = jnp.concatenate([vn[n * CHUNK:(n + 1) * CHUNK, cols] for n in range(n_chunks)], axis=1)
        sg = _dot(sgu_w[h], rhs) + sb_ref[:, h:h + 1]
        s_h = jnp.concatenate([sg[:, n * SGU_HEAD_DIM:(n + 1) * SGU_HEAD_DIM] for n in range(n_chunks)],
                              axis=0)
        ya_cols.append((u[:, cols] * s_h).astype(_bf16))
    ya = jnp.concatenate(ya_cols, axis=1)
    merged = gate_a * _dot(ya, wa_ref[...])

    pos1 = (seq_pos0 + 1 + lax.broadcasted_iota(jnp.int32, (ts, 1), 0)).astype(_f32)
    g = POOL_GROUP_DIM
    lo, n = e0 - HALO, HALO + ts
    sum2 = ext_ref[lo:lo + n, :] + ext_ref[lo - 1:lo - 1 + n, :]
    sum2_ref[POOL_PAD:POOL_PAD + n, :] = sum2[:, g:]
    sum4 = sum2[:, g:] + sum2_ref[POOL_PAD - 2:POOL_PAD - 2 + n, :]
    sum4_ref[POOL_PAD:POOL_PAD + n, :] = sum4[:, g:]
    sum8 = sum4[:, g:] + sum4_ref[POOL_PAD - 4:POOL_PAD - 4 + n, :]
    sum16 = sum8[8:, g:] + sum8[:n - 8, g:]
    window_sums = (sum2[HALO:, 0:g], sum4[HALO:, 0:g], sum8[HALO:, 0:g], sum16[HALO - 8:, :])
    yb_cols = []
    for gi, w in enumerate(POOL_WINDOWS):
        cols = slice(gi * g, (gi + 1) * g)
        count = jnp.minimum(pos1, float(w))
        pooled = (window_sums[gi] / count - p[:, cols]).astype(_bf16)
        yb_cols.append(_dot(pooled, pw_ref[gi]))
    yb = (jnp.concatenate(yb_cols, axis=1) * ps_ref[...]).astype(_bf16)
    merged = merged + gate_b * _dot(yb, wb_ref[...])

    x_new = x + g_m * _dot(merged.astype(_bf16), wo_ref[...])
    xo_ref[rows, :] = x_new

    hf = _rms_modulate(x_new, gffn_ref[...], sh_f, sc_f)
    hf_ref[rows] = _pack_bf16_pairs(hf)
    bucket, rank = _route(hf, rw_ref, rb_ref, carry_ref)
    route_ref[0:1, rows] = bucket
    route_ref[1:2, rows] = rank
    yield


def _mixer(x, y_prev, gate_prev, mod_l, gmix, gffn, w_in, v_g, sgu_w, sgu_bt, pool_w, pool_scale,
           w_a, w_b, w_o, router_w_pad, router_bias):
    batch, seq, d = x.shape
    has_prev = y_prev is not None
    ts = SEQ_TILE
    n_tiles = batch * (seq // ts)
    tiles_per_seq = seq // ts
    const = lambda *shape: pl.BlockSpec(shape, lambda b, s: (0,) * len(shape),
                                        pipeline_mode=pl.Buffered(1))
    row_spec = pl.BlockSpec((None, ts, d), lambda b, s: (b, s, 0))
    slab_spec = pl.BlockSpec((None, ts) + PACKED_SLAB, lambda b, s: (b, s, 0, 0))
    prev_specs = [slab_spec, pl.BlockSpec((None, 1, d), lambda b, s: (b, 0, 0))] if has_prev else []
    prev_args = (y_prev, gate_prev) if has_prev else ()
    return pl.pallas_call(
        functools.partial(_mixer_kernel, has_prev=has_prev),
        grid=(batch, tiles_per_seq),
        in_specs=[row_spec] + prev_specs + [
            pl.BlockSpec((None, 1, N_MOD * d), lambda b, s: (b, 0, 0)),
            const(1, d), const(1, d),
            const(*w_in.shape),
            const(1, SGU_WIDTH),
            const(*sgu_w.shape),
            const(*sgu_bt.shape),
            const(*pool_w.shape),
            const(1, POOL_WIDTH),
            const(d, d), const(d, d), const(d, d),
            const(*router_w_pad.shape),
            pl.BlockSpec(memory_space=pltpu.SMEM),
        ],
        out_specs=[
            pl.BlockSpec((None, ts, d), lambda b, s: (b, s, 0)),
            pl.BlockSpec((None, ts) + PACKED_SLAB, lambda b, s: (b, s, 0, 0)),
            pl.BlockSpec((None, 8, ts), lambda b, s: (b * tiles_per_seq + s, 0, 0)),
            pl.BlockSpec((BUCKET_ROWS, 128), lambda b, s: (0, 0)),
        ],
        out_shape=[
            jax.ShapeDtypeStruct((batch, seq, d), _f32),
            jax.ShapeDtypeStruct((batch, seq) + PACKED_SLAB, jnp.uint32),
            jax.ShapeDtypeStruct((n_tiles, 8, ts), _f32),
            jax.ShapeDtypeStruct((BUCKET_ROWS, 128), _f32),
        ],
        scratch_shapes=[
            pltpu.VMEM((POOL_PAD + HALO + ts, POOL_WIDTH), _f32),
            pltpu.VMEM((ts // SUB_TILE, POOL_PAD + HALO + SUB_TILE, POOL_WIDTH - POOL_GROUP_DIM), _f32),
            pltpu.VMEM((ts // SUB_TILE, POOL_PAD + HALO + SUB_TILE, POOL_WIDTH - 2 * POOL_GROUP_DIM), _f32),
            pltpu.VMEM((BUCKET_ROWS, 128), _f32),
        ],
        compiler_params=pltpu.CompilerParams(
            dimension_semantics=("arbitrary", "arbitrary"), vmem_limit_bytes=VMEM_LIMIT_BYTES),
        name="mixer_router",
    )(x, *prev_args, mod_l, gmix, gffn, w_in, v_g, sgu_w, sgu_bt, pool_w, pool_scale, w_a, w_b, w_o,
      router_w_pad, router_bias)


def _sc_mesh():
    return plsc.VectorSubcoreMesh(core_axis_name="c", subcore_axis_name="s")


def _sc_worker():
    return lax.axis_index("s") * SC_CORES + lax.axis_index("c")


def _sc_gather_chunks(table_hbm, idx_v, out_hbm, out_row0, n_chunks, bufs, sems):
    depth = len(bufs)

    def gather(j, slot):
        return pltpu.make_async_copy(table_hbm.at[idx_v.at[j]], bufs[slot], sems[slot])

    for j in range(depth - 1):
        @pl.when(j < n_chunks)
        def _():
            gather(j, j).start()

    @pl.loop(0, n_chunks, step=depth)
    def _(j0):
        for slot in range(depth):
            j = j0 + slot

            @pl.when(j < n_chunks)
            def _():
                @pl.when(j + depth - 1 < n_chunks)
                def _():
                    gather(j + depth - 1, (slot + depth - 1) % depth).start()

                gather(j, slot).wait()
                pltpu.sync_copy(bufs[slot], out_hbm.at[pl.ds(out_row0 + j * SC_CHUNK, SC_CHUNK)])


def _sc_chunks_per_worker(n_rows):
    n_chunks = n_rows // (SC_CORES * SC_SUBCORES * SC_CHUNK)
    assert n_chunks * SC_CORES * SC_SUBCORES * SC_CHUNK == n_rows
    return n_chunks


def _sc_gather_rows(idx, table):
    n_out = idx.shape[0]
    n_chunks = _sc_chunks_per_worker(n_out)
    slab = table.shape[1:]
    buf = pltpu.VMEM((SC_CHUNK,) + slab, table.dtype)

    @functools.partial(
        pl.kernel, mesh=_sc_mesh(),
        out_type=jax.ShapeDtypeStruct((n_out,) + slab, table.dtype),
        scratch_types=[pltpu.VMEM((n_chunks, SC_CHUNK), jnp.int32)]
        + [buf] * SC_RING + [pltpu.SemaphoreType.DMA] * SC_RING,
    )
    def gather_kernel(table_hbm, idx_hbm, out_hbm, idx_v, *ring):
        chunk0 = _sc_worker() * n_chunks
        pltpu.sync_copy(idx_hbm.at[pl.ds(chunk0, n_chunks)], idx_v)
        _sc_gather_chunks(table_hbm, idx_v, out_hbm, chunk0 * SC_CHUNK, n_chunks,
                          ring[:SC_RING], ring[SC_RING:])

    return gather_kernel(table, idx.reshape(n_out // SC_CHUNK, SC_CHUNK))


def _sc_gather_by_slot(pos, table, n_slots, n_live_slots):
    n_tok = pos.shape[0]
    per_worker = n_slots // (SC_CORES * SC_SUBCORES)
    n_chunks = _sc_chunks_per_worker(n_slots)
    slab = table.shape[1:]
    buf = pltpu.VMEM((SC_CHUNK,) + slab, table.dtype)

    @functools.partial(
        pl.kernel, mesh=_sc_mesh(),
        compiler_params=pltpu.CompilerParams(needs_layout_passes=False),
        out_type=jax.ShapeDtypeStruct((n_slots,) + slab, table.dtype),
        scratch_types=[pltpu.VMEM((n_tok,), jnp.int32), pltpu.VMEM((n_chunks, SC_CHUNK), jnp.int32),
                       pltpu.VMEM((SC_LANES,), jnp.int32)]
        + [buf] * SC_RING + [pltpu.SemaphoreType.DMA] * SC_RING,
    )
    def slot_kernel(table_hbm, pos_hbm, live_hbm, out_hbm, pos_v, inv_v, live_v, *ring):
        slot0 = _sc_worker() * per_worker
        pltpu.sync_copy(pos_hbm, pos_v)
        pltpu.sync_copy(live_hbm, live_v)
        n_live_chunks = jnp.clip((jnp.max(live_v[...]) - slot0) // SC_CHUNK, 0, n_chunks)
        lane = lax.iota(jnp.int32, SC_LANES)

        @pl.loop(0, n_chunks)
        def _(j):
            for h in range(SC_CHUNK // SC_LANES):
                s = slot0 + j * SC_CHUNK + h * SC_LANES + lane
                inv_v[j, pl.ds(h * SC_LANES, SC_LANES)] = jnp.where(s >= n_tok, s - n_tok, s)

        @pl.loop(0, n_tok // SC_LANES)
        def _(i):
            local = pos_v[pl.ds(i * SC_LANES, SC_LANES)] - slot0
            mine = (local >= 0) & (local < per_worker)
            local = jnp.where(mine, local, 0)
            plsc.store_scatter(inv_v, [local // SC_CHUNK, local % SC_CHUNK], i * SC_LANES + lane, mask=mine)

        _sc_gather_chunks(table_hbm, inv_v, out_hbm, slot0, n_live_chunks, ring[:SC_RING], ring[SC_RING:])

    return slot_kernel(table, pos, jnp.broadcast_to(n_live_slots, (SC_LANES,)))


def _final_kernel(x_ref, y_ref, g_ref, fg_ref, o_ref):
    out = x_ref[...] + g_ref[...] * _unpack_bf16_pairs(y_ref[...])
    o_ref[...] = out * lax.rsqrt(jnp.mean(out * out, axis=-1, keepdims=True) + EPS) * fg_ref[...]


def _final_norm(x, y_tok, gate_f, final_g):
    batch, seq, d = x.shape
    row_spec = pl.BlockSpec((None, ROW_TILE, d), lambda b, s: (b, s, 0))
    return pl.pallas_call(
        _final_kernel,
        grid=(batch, seq // ROW_TILE),
        in_specs=[row_spec, pl.BlockSpec((None, ROW_TILE) + PACKED_SLAB, lambda b, s: (b, s, 0, 0)),
                  pl.BlockSpec((None, 1, d), lambda b, s: (b, 0, 0)),
                  pl.BlockSpec((1, d), lambda b, s: (0, 0))],
        out_specs=row_spec,
        out_shape=jax.ShapeDtypeStruct((batch, seq, d), _f32),
        compiler_params=pltpu.CompilerParams(dimension_semantics=("arbitrary", "arbitrary")),
        name="residual_final_norm",
    )(x, y_tok, gate_f, final_g)


def _experts_kernel(ea_ref, eb_ref, nused_ref, xs_ref, rwa_ref, rwb_ref,
                    wga_ref, wgb_ref, wua_ref, wub_ref, wda_ref, wdb_ref, ys_ref):
    i = pl.program_id(0)

    @pl.when(i < nused_ref[0])
    def _():
        x = _unpack_bf16_pairs(xs_ref[...])
        la = jnp.sum(x * rwa_ref[...], axis=-1, keepdims=True)
        lb = jnp.sum(x * rwb_ref[...], axis=-1, keepdims=True)
        wa = jax.nn.sigmoid(la - lb)
        wb = jax.nn.sigmoid(lb - la)
        xb = x.astype(_bf16)
        act_a = (jax.nn.silu(_dot(xb, wga_ref[...])) * _dot(xb, wua_ref[...]) * wa).astype(_bf16)
        act_b = (jax.nn.silu(_dot(xb, wgb_ref[...])) * _dot(xb, wub_ref[...]) * wb).astype(_bf16)
        y = _dot(act_a, wda_ref[...]) + _dot(act_b, wdb_ref[...])
        ys_ref[...] = _pack_bf16_pairs(y)

    @pl.when(i >= nused_ref[0])
    def _():
        ys_ref[...] = jnp.zeros_like(ys_ref)


def _experts(tile_ea, tile_eb, n_used, xs, router_wt, w_gate, w_up, w_down):
    n_slots = xs.shape[0]
    d = D_MODEL
    n_tiles = n_slots // EXPERT_TILE
    f = D_EXPERT

    def row(i, ea, eb, nu):
        return (jnp.maximum(jnp.minimum(i, nu[0] - 1), 0), 0, 0)

    grid_spec = pltpu.PrefetchScalarGridSpec(
        num_scalar_prefetch=3,
        grid=(n_tiles,),
        in_specs=[
            pl.BlockSpec((EXPERT_TILE,) + PACKED_SLAB, row),
            pl.BlockSpec((None, 1, d), lambda i, ea, eb, nu: (ea[i], 0, 0)),
            pl.BlockSpec((None, 1, d), lambda i, ea, eb, nu: (eb[i], 0, 0)),
            pl.BlockSpec((None, d, f), lambda i, ea, eb, nu: (ea[i], 0, 0)),
            pl.BlockSpec((None, d, f), lambda i, ea, eb, nu: (eb[i], 0, 0)),
            pl.BlockSpec((None, d, f), lambda i, ea, eb, nu: (ea[i], 0, 0)),
            pl.BlockSpec((None, d, f), lambda i, ea, eb, nu: (eb[i], 0, 0)),
            pl.BlockSpec((None, f, d), lambda i, ea, eb, nu: (ea[i], 0, 0)),
            pl.BlockSpec((None, f, d), lambda i, ea, eb, nu: (eb[i], 0, 0)),
        ],
        out_specs=pl.BlockSpec((EXPERT_TILE,) + PACKED_SLAB, lambda i, ea, eb, nu: (i, 0, 0)),
    )
    return pl.pallas_call(
        _experts_kernel,
        grid_spec=grid_spec,
        out_shape=jax.ShapeDtypeStruct((n_slots,) + PACKED_SLAB, jnp.uint32),
        compiler_params=pltpu.CompilerParams(
            dimension_semantics=("arbitrary",), vmem_limit_bytes=VMEM_LIMIT_BYTES),
        name="grouped_experts",
    )(tile_ea, tile_eb, n_used, xs, router_wt, router_wt, w_gate, w_gate, w_up, w_up, w_down, w_down)


def _routing_tables(route, counts, n_tiles_max):
    bucket = route[:, 0, :].reshape(-1).astype(jnp.int32)
    rank = route[:, 1, :].reshape(-1).astype(jnp.int32)
    cnt = counts[:N_BUCKETS, 0].astype(jnp.int32)
    tiles_b = (cnt + EXPERT_TILE - 1) // EXPERT_TILE
    tile_end = jnp.cumsum(tiles_b)
    tile_start = tile_end - tiles_b
    n_used = tile_end[-1]
    onehot = bucket[:, None] == jnp.arange(N_BUCKETS, dtype=jnp.int32)[None, :]
    pos = jnp.sum(jnp.where(onehot, (tile_start * EXPERT_TILE)[None, :], 0), axis=1) + rank
    tile_ids = jnp.minimum(jnp.arange(n_tiles_max, dtype=jnp.int32), n_used - 1)
    tile_bucket = jnp.sum((tile_ids[:, None] >= tile_end[None, :]).astype(jnp.int32), axis=1)
    group = tile_bucket // PAIRS_PER_GROUP
    pair = tile_bucket % PAIRS_PER_GROUP
    tile_ea = group * EXPERTS_PER_GROUP + jnp.asarray(_PAIR_A, jnp.int32)[pair]
    tile_eb = group * EXPERTS_PER_GROUP + jnp.asarray(_PAIR_B, jnp.int32)[pair]
    return pos, tile_ea, tile_eb, n_used.reshape(1)


def kernel(x, c, w_ada, b_ada, norm_mix_g, w_in, v_norm_g, sgu_w, sgu_b, pool_w, pool_scale,
           w_branch_a, w_branch_b, w_out, norm_ffn_g, router_w, router_bias,
           w_exp_gate, w_exp_up, w_exp_down, final_norm_g):
    batch, seq, d = x.shape
    depth = w_ada.shape[0]
    t = batch * seq
    n_tiles_max = t // EXPERT_TILE + N_BUCKETS
    n_slots = n_tiles_max * EXPERT_TILE

    mod = _ada(c, w_ada, b_ada)
    rw_pad = jnp.pad(router_w, ((0, 0), (0, 128 - N_EXPERTS)))
    rw_hi = rw_pad.astype(_bf16)
    rw_lo = (rw_pad - rw_hi.astype(_f32)).astype(_bf16)
    router_w_pad = jnp.concatenate([rw_hi, rw_lo], axis=1)
    router_wt = router_w.T.reshape(N_EXPERTS, 1, d)
    final_g = final_norm_g.reshape(1, d)

    def layer_bf16(stacks, l):
        return tuple(w[l].astype(_bf16) for w in stacks)

    def tie(stacks, gate):
        return lax.optimization_barrier((stacks, gate))

    mixer_stacks = (w_in, sgu_w, pool_w, w_branch_a, w_branch_b, w_out)
    expert_stacks = (w_exp_gate, w_exp_up, w_exp_down)
    y_tok, gate_prev = None, None
    mixer_w, expert_w = layer_bf16(mixer_stacks, 0), layer_bf16(expert_stacks, 0)
    for l in range(depth):
        mod_l = mod[l].reshape(batch, 1, N_MOD * d)
        w_in_l, sgu_w_l, pool_w_l, w_a_l, w_b_l, w_o_l = mixer_w
        x, hf, route, counts = _mixer(
            x, y_tok, gate_prev, mod_l, norm_mix_g[l].reshape(1, d), norm_ffn_g[l].reshape(1, d),
            w_in_l, v_norm_g[l].reshape(1, SGU_WIDTH), sgu_w_l, sgu_b[l].T,
            pool_w_l, pool_scale[l].reshape(1, POOL_WIDTH), w_a_l, w_b_l, w_o_l,
            router_w_pad, router_bias)
        pos, tile_ea, tile_eb, n_used = _routing_tables(route, counts, n_tiles_max)
        if l + 1 < depth:
            mixer_stacks, pos = tie(mixer_stacks, pos)
            mixer_w = layer_bf16(mixer_stacks, l + 1)
        xs = _sc_gather_by_slot(pos, hf.reshape((t,) + PACKED_SLAB), n_slots, n_used * EXPERT_TILE)
        ys = _experts(tile_ea, tile_eb, n_used, xs, router_wt, *expert_w)
        if l + 1 < depth:
            expert_stacks, ys = tie(expert_stacks, ys)
            expert_w = layer_bf16(expert_stacks, l + 1)
        y_tok = _sc_gather_rows(pos, ys).reshape((batch, seq) + PACKED_SLAB)
        gate_prev = mod_l[:, :, 5 * d:6 * d]
    return _final_norm(x, y_tok, gate_prev, final_g)
```

```python
import functools

import jax
import jax.numpy as jnp
from jax import lax
from jax.experimental import pallas as pl
from jax.experimental.pallas import tpu as pltpu
from jax.experimental.pallas import tpu_sc as plsc

D_MODEL = 1024
CHUNK = 128
SGU_HEADS = 8
SGU_HEAD_DIM = 128
SGU_WIDTH = 1024
POOL_WINDOWS = (2, 4, 8, 16)
POOL_GROUP_DIM = 256
POOL_WIDTH = 1024
HALO = 16
POOL_PAD = 8
N_EXPERTS = 16
N_EXPERT_GROUPS = 4
EXPERTS_PER_GROUP = 4
PAIRS_PER_GROUP = 6
N_BUCKETS = N_EXPERT_GROUPS * PAIRS_PER_GROUP
BUCKET_ROWS = 32
D_EXPERT = 512
N_MOD = 6
EPS = 1e-6

SEQ_TILE = 512
SUB_TILE = 256
MIXER_STAGES = 2
ROW_TILE = 512
EXPERT_TILE = 256
VMEM_LIMIT_BYTES = 58 * 1024 * 1024
PACKED_SLAB = (4, 128)
SC_CORES = 2
SC_SUBCORES = 16
SC_CHUNK = 32
SC_LANES = 16
SC_RING = 3

_PAIR_A = (0, 0, 0, 1, 1, 3)
_PAIR_B = (1, 2, 3, 3, 2, 2)

_bf16 = jnp.bfloat16
_f32 = jnp.float32


def _dot(a, b):
    return jnp.dot(a, b, preferred_element_type=_f32)


def _pack_bf16_pairs(x):
    rows, d = x.shape
    hi = lax.bitcast_convert_type(x[:, :d // 2].astype(_bf16).astype(_f32), jnp.uint32)
    lo = lax.bitcast_convert_type(x[:, d // 2:].astype(_bf16).astype(_f32), jnp.uint32)
    return (hi | (lo >> 16)).reshape((rows,) + PACKED_SLAB)


def _unpack_bf16_pairs(words):
    rows = words.shape[0]
    words = words.reshape(rows, PACKED_SLAB[0] * PACKED_SLAB[1])
    hi = lax.bitcast_convert_type(words & jnp.uint32(0xFFFF0000), _f32)
    lo = lax.bitcast_convert_type(words << 16, _f32)
    return jnp.concatenate([hi, lo], axis=1)


def _rms_modulate(x, g, shift, scale):
    y = x * lax.rsqrt(jnp.mean(x * x, axis=-1, keepdims=True) + EPS)
    return (y * g) * (1.0 + scale) + shift


def _ada_kernel(c_ref, w_ref, b_ref, o_ref):
    c = c_ref[...]
    c_act = (c * jax.nn.sigmoid(c)).astype(_bf16)
    o_ref[...] = _dot(c_act, w_ref[...].astype(_bf16)) + b_ref[...]


def _ada(c, w_ada, b_ada):
    depth, d, n = w_ada.shape
    batch = c.shape[0]
    tn = 2048
    return pl.pallas_call(
        _ada_kernel,
        grid=(depth, n // tn),
        in_specs=[
            pl.BlockSpec((batch, d), lambda l, j: (0, 0)),
            pl.BlockSpec((None, d, tn), lambda l, j: (l, 0, j)),
            pl.BlockSpec((None, 1, tn), lambda l, j: (l, 0, j)),
        ],
        out_specs=pl.BlockSpec((None, batch, tn), lambda l, j: (l, 0, j)),
        out_shape=jax.ShapeDtypeStruct((depth, batch, n), _f32),
        compiler_params=pltpu.CompilerParams(
            dimension_semantics=("arbitrary", "arbitrary"), vmem_limit_bytes=VMEM_LIMIT_BYTES),
        name="ada_modulation",
    )(c, w_ada, b_ada.reshape(depth, 1, n))


def _route(hf, rw_ref, rb_ref, carry_ref):
    ts = hf.shape[0]
    hf_hi = hf.astype(_bf16)
    hf_lo = (hf - hf_hi.astype(_f32)).astype(_bf16)
    both = _dot(hf_hi, rw_ref[...])
    logits = both[:, 0:128] + both[:, 128:256] + _dot(hf_lo, rw_ref[:, 0:128])
    lt = logits.T
    rows = [lt[e:e + 1, :] for e in range(N_EXPERTS)]
    m = functools.reduce(jnp.maximum, rows)
    ex = [jnp.exp(r - m) for r in rows]
    den = functools.reduce(lambda a, b: a + b, ex)
    probs = [e / den for e in ex]
    sel = [probs[e] + rb_ref[e] for e in range(N_EXPERTS)]

    def top2_sum(v):
        pairs = [v[i] + v[j] for i, j in zip(_PAIR_A, _PAIR_B)]
        return functools.reduce(jnp.maximum, pairs)

    gscore = [top2_sum(sel[4 * g:4 * g + 4]) for g in range(N_EXPERT_GROUPS)]
    best = gscore[0]
    gidx = jnp.zeros_like(best, dtype=jnp.int32)
    for g in range(1, N_EXPERT_GROUPS):
        better = gscore[g] > best
        best = jnp.where(better, gscore[g], best)
        gidx = jnp.where(better, g, gidx)
    ing = []
    for k in range(EXPERTS_PER_GROUP):
        v = sel[k]
        for g in range(1, N_EXPERT_GROUPS):
            v = jnp.where(gidx == g, sel[4 * g + k], v)
        ing.append(v)
    chosen = []
    for k in range(EXPERTS_PER_GROUP):
        r = jnp.zeros_like(gidx)
        for j in range(EXPERTS_PER_GROUP):
            if j == k:
                continue
            beats = (ing[j] >= ing[k]) if j < k else (ing[j] > ing[k])
            r = r + beats.astype(jnp.int32)
        chosen.append(r < 2)
    lo = jnp.where(chosen[0], 0, jnp.where(chosen[1], 1, 2))
    hi = jnp.where(chosen[3], 3, jnp.where(chosen[2], 2, 1))
    pair = jnp.where(lo == 0, hi - 1, jnp.where(lo == 2, 5, jnp.where(hi == 3, 3, 4)))
    bucket = gidx * PAIRS_PER_GROUP + pair

    brow = lax.broadcasted_iota(jnp.int32, (BUCKET_ROWS, ts), 0)
    onehot = (brow == bucket).astype(_f32)
    jj = lax.broadcasted_iota(jnp.int32, (ts, ts), 0)
    tt = lax.broadcasted_iota(jnp.int32, (ts, ts), 1)
    upper = (jj <= tt).astype(_bf16)
    cum = _dot(onehot.astype(_bf16), upper)
    carry = carry_ref[...][:, 0:1]
    rank = jnp.sum(onehot * (cum - 1.0 + carry), axis=0, keepdims=True)
    carry_ref[...] = carry_ref[...] + jnp.sum(onehot, axis=1, keepdims=True)
    return bucket.astype(_f32), rank


def _mixer_kernel(*refs, has_prev):
    if has_prev:
        x_ref, y_ref, gprev_ref = refs[:3]
        refs = refs[3:]
        x_in = lambda rows: x_ref[rows, :] + gprev_ref[...] * _unpack_bf16_pairs(y_ref[rows])
    else:
        x_ref = refs[0]
        refs = refs[1:]
        x_in = lambda rows: x_ref[rows, :]
    (mod_ref, gmix_ref, gffn_ref, win_ref, vg_ref, sw_ref, sb_ref, pw_ref, ps_ref, wa_ref, wb_ref,
     wo_ref, rw_ref, rb_ref, eg_ref, eu_ref, ed_ref,
     xo_ref, hf_ref, route_ref, counts_ref, eg16_ref, eu16_ref, ed16_ref,
     ext_ref, sum2_ref, sum4_ref, carry_ref) = refs
    b = pl.program_id(0)
    s = pl.program_id(1)
    ts = x_ref.shape[0]
    d = D_MODEL

    @pl.when((b == 0) & (s == 0))
    def _():
        carry_ref[...] = jnp.zeros_like(carry_ref)
        ext_ref[0:POOL_PAD, :] = jnp.zeros((POOL_PAD, POOL_WIDTH), _f32)
        sum2_ref[:, 0:POOL_PAD, :] = jnp.zeros((sum2_ref.shape[0], POOL_PAD, sum2_ref.shape[2]), _f32)
        sum4_ref[:, 0:POOL_PAD, :] = jnp.zeros((sum4_ref.shape[0], POOL_PAD, sum4_ref.shape[2]), _f32)

    @pl.when(s == 0)
    def _():
        ext_ref[POOL_PAD:POOL_PAD + HALO, :] = jnp.zeros((HALO, POOL_WIDTH), _f32)

    for w32_ref, w16_ref in ((eg_ref, eg16_ref), (eu_ref, eu16_ref), (ed_ref, ed16_ref)):
        w16_ref[...] = w32_ref[...].astype(_bf16)

    mod = mod_ref[...]
    ci = lax.broadcasted_iota(jnp.int32, (CHUNK, CHUNK), 0)
    cj = lax.broadcasted_iota(jnp.int32, (CHUNK, CHUNK), 1)
    sgu_w = [jnp.where(ci >= cj, sw_ref[h], jnp.zeros((), _bf16)) for h in range(SGU_HEADS)]
    sub_tiles = [_mixer_rows(r * SUB_TILE, s * ts + r * SUB_TILE, mod, sgu_w,
                             x_in, gmix_ref, gffn_ref, win_ref, vg_ref, sb_ref, pw_ref, ps_ref, wa_ref,
                             wb_ref, wo_ref, rw_ref, rb_ref, xo_ref, hf_ref, route_ref, ext_ref,
                             sum2_ref.at[r], sum4_ref.at[r], carry_ref)
                 for r in range(ts // SUB_TILE)]
    for _ in range(MIXER_STAGES):
        for sub_tile in sub_tiles:
            next(sub_tile, None)
    ext_ref[POOL_PAD:POOL_PAD + HALO, :] = ext_ref[POOL_PAD + ts:POOL_PAD + ts + HALO, :]
    route_ref[2:8, :] = jnp.zeros((6, ts), _f32)
    counts_ref[...] = carry_ref[...]


def _mixer_rows(row0, seq_pos0, mod, sgu_w, x_in, gmix_ref, gffn_ref, win_ref, vg_ref, sb_ref,
                pw_ref, ps_ref, wa_ref, wb_ref, wo_ref, rw_ref, rb_ref, xo_ref, hf_ref, route_ref,
                ext_ref, sum2_ref, sum4_ref, carry_ref):
    d = D_MODEL
    ts = SUB_TILE
    rows = slice(row0, row0 + ts)
    sh_m, sc_m, g_m = mod[:, 0:d], mod[:, d:2 * d], mod[:, 2 * d:3 * d]
    sh_f, sc_f = mod[:, 3 * d:4 * d], mod[:, 4 * d:5 * d]
    x = x_in(rows)

    hb = _rms_modulate(x, gmix_ref[...], sh_m, sc_m).astype(_bf16)

    v = jax.nn.gelu(_dot(hb, win_ref[:, SGU_WIDTH:2 * SGU_WIDTH]))
    p = _dot(hb, win_ref[:, 2 * SGU_WIDTH:2 * SGU_WIDTH + POOL_WIDTH])
    u = jax.nn.gelu(_dot(hb, win_ref[:, 0:SGU_WIDTH]))
    gate_a = jax.nn.sigmoid(_dot(hb, win_ref[:, 3 * d:4 * d]))
    gate_b = jax.nn.sigmoid(_dot(hb, win_ref[:, 4 * d:5 * d]))
    e0 = POOL_PAD + HALO + row0
    ext_ref[e0:e0 + ts, :] = p
    yield

    vc = v - jnp.mean(v, axis=-1, keepdims=True)
    vn = (vc * lax.rsqrt(jnp.mean(vc * vc, axis=-1, keepdims=True) + EPS) * vg_ref[...]).astype(_bf16)
    n_chunks = ts // CHUNK
    ya_cols = []
    for h in range(SGU_HEADS):
        cols = slice(h * SGU_HEAD_DIM, (h + 1) * SGU_HEAD_DIM)
        rhs = jnp.concatenate([vn[n * CHUNK:(n + 1) * CHUNK, cols] for n in range(n_chunks)], axis=1)
        sg = _dot(sgu_w[h], rhs) + sb_ref[:, h:h + 1]
        s_h = jnp.concatenate([sg[:, n * SGU_HEAD_DIM:(n + 1) * SGU_HEAD_DIM] for n in range(n_chunks)],
                              axis=0)
        ya_cols.append((u[:, cols] * s_h).astype(_bf16))
    ya = jnp.concatenate(ya_cols, axis=1)
    merged = gate_a * _dot(ya, wa_ref[...])

    pos1 = (seq_pos0 + 1 + lax.broadcasted_iota(jnp.int32, (ts, 1), 0)).astype(_f32)
    g = POOL_GROUP_DIM
    lo, n = e0 - HALO, HALO + ts
    sum2 = ext_ref[lo:lo + n, :] + ext_ref[lo - 1:lo - 1 + n, :]
    sum2_ref[POOL_PAD:POOL_PAD + n, :] = sum2[:, g:]
    sum4 = sum2[:, g:] + sum2_ref[POOL_PAD - 2:POOL_PAD - 2 + n, :]
    sum4_ref[POOL_PAD:POOL_PAD + n, :] = sum4[:, g:]
    sum8 = sum4[:, g:] + sum4_ref[POOL_PAD - 4:POOL_PAD - 4 + n, :]
    sum16 = sum8[8:, g:] + sum8[:n - 8, g:]
    window_sums = (sum2[HALO:, 0:g], sum4[HALO:, 0:g], sum8[HALO:, 0:g], sum16[HALO - 8:, :])
    yb_cols = []
    for gi, w in enumerate(POOL_WINDOWS):
        cols = slice(gi * g, (gi + 1) * g)
        count = jnp.minimum(pos1, float(w))
        pooled = (window_sums[gi] / count - p[:, cols]).astype(_bf16)
        yb_cols.append(_dot(pooled, pw_ref[gi]))
    yb = (jnp.concatenate(yb_cols, axis=1) * ps_ref[...]).astype(_bf16)
    merged = merged + gate_b * _dot(yb, wb_ref[...])

    x_new = x + g_m * _dot(merged.astype(_bf16), wo_ref[...])
    xo_ref[rows, :] = x_new

    hf = _rms_modulate(x_new, gffn_ref[...], sh_f, sc_f)
    hf_ref[rows] = _pack_bf16_pairs(hf)
    bucket, rank = _route(hf, rw_ref, rb_ref, carry_ref)
    route_ref[0:1, rows] = bucket
    route_ref[1:2, rows] = rank
    yield


def _mixer(x, y_prev, gate_prev, mod_l, gmix, gffn, w_in, v_g, sgu_w, sgu_bt, pool_w, pool_scale,
           w_a, w_b, w_o, router_w_pad, router_bias, layer, expert_stacks):
    batch, seq, d = x.shape
    has_prev = y_prev is not None
    ts = SEQ_TILE
    n_tiles = batch * (seq // ts)
    tiles_per_seq = seq // ts
    const = lambda *shape: pl.BlockSpec(shape, lambda b, s: (0,) * len(shape),
                                        pipeline_mode=pl.Buffered(1))
    row_spec = pl.BlockSpec((None, ts, d), lambda b, s: (b, s, 0))
    slab_spec = pl.BlockSpec((None, ts) + PACKED_SLAB, lambda b, s: (b, s, 0, 0))
    prev_specs = [slab_spec, pl.BlockSpec((None, 1, d), lambda b, s: (b, 0, 0))] if has_prev else []
    prev_args = (y_prev, gate_prev) if has_prev else ()
    n_steps = batch * tiles_per_seq
    parts = n_steps // N_EXPERTS
    assert parts * N_EXPERTS == n_steps

    def part_index(b, s):
        step = b * tiles_per_seq + s
        return step // parts, step % parts

    def part_specs(w, with_layer):
        rows, cols = w.shape[-2] // parts, w.shape[-1]
        if with_layer:
            return pl.BlockSpec((None, None, rows, cols), lambda b, s: (layer, *part_index(b, s), 0))
        return pl.BlockSpec((None, rows, cols), lambda b, s: (*part_index(b, s), 0))

    return pl.pallas_call(
        functools.partial(_mixer_kernel, has_prev=has_prev),
        grid=(batch, tiles_per_seq),
        in_specs=[row_spec] + prev_specs + [
            pl.BlockSpec((None, 1, N_MOD * d), lambda b, s: (b, 0, 0)),
            const(1, d), const(1, d),
            const(*w_in.shape),
            const(1, SGU_WIDTH),
            const(*sgu_w.shape),
            const(*sgu_bt.shape),
            const(*pool_w.shape),
            const(1, POOL_WIDTH),
            const(d, d), const(d, d), const(d, d),
            const(*router_w_pad.shape),
            pl.BlockSpec(memory_space=pltpu.SMEM),
        ] + [part_specs(w, True) for w in expert_stacks],
        out_specs=[
            pl.BlockSpec((None, ts, d), lambda b, s: (b, s, 0)),
            pl.BlockSpec((None, ts) + PACKED_SLAB, lambda b, s: (b, s, 0, 0)),
            pl.BlockSpec((None, 8, ts), lambda b, s: (b * tiles_per_seq + s, 0, 0)),
            pl.BlockSpec((BUCKET_ROWS, 128), lambda b, s: (0, 0)),
        ] + [part_specs(w, False) for w in expert_stacks],
        out_shape=[
            jax.ShapeDtypeStruct((batch, seq, d), _f32),
            jax.ShapeDtypeStruct((batch, seq) + PACKED_SLAB, jnp.uint32),
            jax.ShapeDtypeStruct((n_tiles, 8, ts), _f32),
            jax.ShapeDtypeStruct((BUCKET_ROWS, 128), _f32),
        ] + [jax.ShapeDtypeStruct(w.shape[1:], _bf16) for w in expert_stacks],
        scratch_shapes=[
            pltpu.VMEM((POOL_PAD + HALO + ts, POOL_WIDTH), _f32),
            pltpu.VMEM((ts // SUB_TILE, POOL_PAD + HALO + SUB_TILE, POOL_WIDTH - POOL_GROUP_DIM), _f32),
            pltpu.VMEM((ts // SUB_TILE, POOL_PAD + HALO + SUB_TILE, POOL_WIDTH - 2 * POOL_GROUP_DIM), _f32),
            pltpu.VMEM((BUCKET_ROWS, 128), _f32),
        ],
        compiler_params=pltpu.CompilerParams(
            dimension_semantics=("arbitrary", "arbitrary"), vmem_limit_bytes=VMEM_LIMIT_BYTES),
        name="mixer_router",
    )(x, *prev_args, mod_l, gmix, gffn, w_in, v_g, sgu_w, sgu_bt, pool_w, pool_scale, w_a, w_b, w_o,
      router_w_pad, router_bias, *expert_stacks)


def _sc_mesh():
    return plsc.VectorSubcoreMesh(core_axis_name="c", subcore_axis_name="s")


def _sc_worker():
    return lax.axis_index("s") * SC_CORES + lax.axis_index("c")


def _sc_gather_chunks(table_hbm, idx_v, out_hbm, out_row0, n_chunks, bufs, sems):
    depth = len(bufs)

    def gather(j, slot):
        return pltpu.make_async_copy(table_hbm.at[idx_v.at[j]], bufs[slot], sems[slot])

    for j in range(depth - 1):
        @pl.when(j < n_chunks)
        def _():
            gather(j, j).start()

    @pl.loop(0, n_chunks, step=depth)
    def _(j0):
        for slot in range(depth):
            j = j0 + slot

            @pl.when(j < n_chunks)
            def _():
                @pl.when(j + depth - 1 < n_chunks)
                def _():
                    gather(j + depth - 1, (slot + depth - 1) % depth).start()

                gather(j, slot).wait()
                pltpu.sync_copy(bufs[slot], out_hbm.at[pl.ds(out_row0 + j * SC_CHUNK, SC_CHUNK)])


def _sc_chunks_per_worker(n_rows):
    n_chunks = n_rows // (SC_CORES * SC_SUBCORES * SC_CHUNK)
    assert n_chunks * SC_CORES * SC_SUBCORES * SC_CHUNK == n_rows
    return n_chunks


def _sc_gather_rows(idx, table):
    n_out = idx.shape[0]
    n_chunks = _sc_chunks_per_worker(n_out)
    slab = table.shape[1:]
    buf = pltpu.VMEM((SC_CHUNK,) + slab, table.dtype)

    @functools.partial(
        pl.kernel, mesh=_sc_mesh(),
        out_type=jax.ShapeDtypeStruct((n_out,) + slab, table.dtype),
        scratch_types=[pltpu.VMEM((n_chunks, SC_CHUNK), jnp.int32)]
        + [buf] * SC_RING + [pltpu.SemaphoreType.DMA] * SC_RING,
    )
    def gather_kernel(table_hbm, idx_hbm, out_hbm, idx_v, *ring):
        chunk0 = _sc_worker() * n_chunks
        pltpu.sync_copy(idx_hbm.at[pl.ds(chunk0, n_chunks)], idx_v)
        _sc_gather_chunks(table_hbm, idx_v, out_hbm, chunk0 * SC_CHUNK, n_chunks,
                          ring[:SC_RING], ring[SC_RING:])

    return gather_kernel(table, idx.reshape(n_out // SC_CHUNK, SC_CHUNK))


def _sc_gather_by_slot(pos, table, n_slots, n_live_slots):
    n_tok = pos.shape[0]
    per_worker = n_slots // (SC_CORES * SC_SUBCORES)
    n_chunks = _sc_chunks_per_worker(n_slots)
    slab = table.shape[1:]
    buf = pltpu.VMEM((SC_CHUNK,) + slab, table.dtype)

    @functools.partial(
        pl.kernel, mesh=_sc_mesh(),
        compiler_params=pltpu.CompilerParams(needs_layout_passes=False),
        out_type=jax.ShapeDtypeStruct((n_slots,) + slab, table.dtype),
        scratch_types=[pltpu.VMEM((n_tok,), jnp.int32), pltpu.VMEM((n_chunks, SC_CHUNK), jnp.int32),
                       pltpu.VMEM((SC_LANES,), jnp.int32)]
        + [buf] * SC_RING + [pltpu.SemaphoreType.DMA] * SC_RING,
    )
    def slot_kernel(table_hbm, pos_hbm, live_hbm, out_hbm, pos_v, inv_v, live_v, *ring):
        slot0 = _sc_worker() * per_worker
        pltpu.sync_copy(pos_hbm, pos_v)
        pltpu.sync_copy(live_hbm, live_v)
        n_live_chunks = jnp.clip((jnp.max(live_v[...]) - slot0) // SC_CHUNK, 0, n_chunks)
        lane = lax.iota(jnp.int32, SC_LANES)

        @pl.loop(0, n_chunks)
        def _(j):
            for h in range(SC_CHUNK // SC_LANES):
                s = slot0 + j * SC_CHUNK + h * SC_LANES + lane
                inv_v[j, pl.ds(h * SC_LANES, SC_LANES)] = jnp.where(s >= n_tok, s - n_tok, s)

        @pl.loop(0, n_tok // SC_LANES)
        def _(i):
            local = pos_v[pl.ds(i * SC_LANES, SC_LANES)] - slot0
            mine = (local >= 0) & (local < per_worker)
            local = jnp.where(mine, local, 0)
            plsc.store_scatter(inv_v, [local // SC_CHUNK, local % SC_CHUNK], i * SC_LANES + lane, mask=mine)

        _sc_gather_chunks(table_hbm, inv_v, out_hbm, slot0, n_live_chunks, ring[:SC_RING], ring[SC_RING:])

    return slot_kernel(table, pos, jnp.broadcast_to(n_live_slots, (SC_LANES,)))


def _final_kernel(x_ref, y_ref, g_ref, fg_ref, o_ref):
    out = x_ref[...] + g_ref[...] * _unpack_bf16_pairs(y_ref[...])
    o_ref[...] = out * lax.rsqrt(jnp.mean(out * out, axis=-1, keepdims=True) + EPS) * fg_ref[...]


def _final_norm(x, y_tok, gate_f, final_g):
    batch, seq, d = x.shape
    row_spec = pl.BlockSpec((None, ROW_TILE, d), lambda b, s: (b, s, 0))
    return pl.pallas_call(
        _final_kernel,
        grid=(batch, seq // ROW_TILE),
        in_specs=[row_spec, pl.BlockSpec((None, ROW_TILE) + PACKED_SLAB, lambda b, s: (b, s, 0, 0)),
                  pl.BlockSpec((None, 1, d), lambda b, s: (b, 0, 0)),
                  pl.BlockSpec((1, d), lambda b, s: (0, 0))],
        out_specs=row_spec,
        out_shape=jax.ShapeDtypeStruct((batch, seq, d), _f32),
        compiler_params=pltpu.CompilerParams(dimension_semantics=("arbitrary", "arbitrary")),
        name="residual_final_norm",
    )(x, y_tok, gate_f, final_g)


def _experts_kernel(ea_ref, eb_ref, nused_ref, xs_ref, rwa_ref, rwb_ref,
                    wga_ref, wgb_ref, wua_ref, wub_ref, wda_ref, wdb_ref, ys_ref):
    i = pl.program_id(0)

    @pl.when(i < nused_ref[0])
    def _():
        x = _unpack_bf16_pairs(xs_ref[...])
        la = jnp.sum(x * rwa_ref[...], axis=-1, keepdims=True)
        lb = jnp.sum(x * rwb_ref[...], axis=-1, keepdims=True)
        wa = jax.nn.sigmoid(la - lb)
        wb = jax.nn.sigmoid(lb - la)
        xb = x.astype(_bf16)
        act_a = (jax.nn.silu(_dot(xb, wga_ref[...])) * _dot(xb, wua_ref[...]) * wa).astype(_bf16)
        act_b = (jax.nn.silu(_dot(xb, wgb_ref[...])) * _dot(xb, wub_ref[...]) * wb).astype(_bf16)
        y = _dot(act_a, wda_ref[...]) + _dot(act_b, wdb_ref[...])
        ys_ref[...] = _pack_bf16_pairs(y)

    @pl.when(i >= nused_ref[0])
    def _():
        ys_ref[...] = jnp.zeros_like(ys_ref)


def _experts(tile_ea, tile_eb, n_used, xs, router_wt, w_gate, w_up, w_down):
    n_slots = xs.shape[0]
    d = D_MODEL
    n_tiles = n_slots // EXPERT_TILE
    f = D_EXPERT

    def row(i, ea, eb, nu):
        return (jnp.maximum(jnp.minimum(i, nu[0] - 1), 0), 0, 0)

    grid_spec = pltpu.PrefetchScalarGridSpec(
        num_scalar_prefetch=3,
        grid=(n_tiles,),
        in_specs=[
            pl.BlockSpec((EXPERT_TILE,) + PACKED_SLAB, row),
            pl.BlockSpec((None, 1, d), lambda i, ea, eb, nu: (ea[i], 0, 0)),
            pl.BlockSpec((None, 1, d), lambda i, ea, eb, nu: (eb[i], 0, 0)),
            pl.BlockSpec((None, d, f), lambda i, ea, eb, nu: (ea[i], 0, 0)),
            pl.BlockSpec((None, d, f), lambda i, ea, eb, nu: (eb[i], 0, 0)),
            pl.BlockSpec((None, d, f), lambda i, ea, eb, nu: (ea[i], 0, 0)),
            pl.BlockSpec((None, d, f), lambda i, ea, eb, nu: (eb[i], 0, 0)),
            pl.BlockSpec((None, f, d), lambda i, ea, eb, nu: (ea[i], 0, 0)),
            pl.BlockSpec((None, f, d), lambda i, ea, eb, nu: (eb[i], 0, 0)),
        ],
        out_specs=pl.BlockSpec((EXPERT_TILE,) + PACKED_SLAB, lambda i, ea, eb, nu: (i, 0, 0)),
    )
    return pl.pallas_call(
        _experts_kernel,
        grid_spec=grid_spec,
        out_shape=jax.ShapeDtypeStruct((n_slots,) + PACKED_SLAB, jnp.uint32),
        compiler_params=pltpu.CompilerParams(
            dimension_semantics=("arbitrary",), vmem_limit_bytes=VMEM_LIMIT_BYTES),
        name="grouped_experts",
    )(tile_ea, tile_eb, n_used, xs, router_wt, router_wt, w_gate, w_gate, w_up, w_up, w_down, w_down)


def _routing_tables(route, counts, n_tiles_max):
    bucket = route[:, 0, :].reshape(-1).astype(jnp.int32)
    rank = route[:, 1, :].reshape(-1).astype(jnp.int32)
    cnt = counts[:N_BUCKETS, 0].astype(jnp.int32)
    tiles_b = (cnt + EXPERT_TILE - 1) // EXPERT_TILE
    tile_end = jnp.cumsum(tiles_b)
    tile_start = tile_end - tiles_b
    n_used = tile_end[-1]
    onehot = bucket[:, None] == jnp.arange(N_BUCKETS, dtype=jnp.int32)[None, :]
    pos = jnp.sum(jnp.where(onehot, (tile_start * EXPERT_TILE)[None, :], 0), axis=1) + rank
    tile_ids = jnp.minimum(jnp.arange(n_tiles_max, dtype=jnp.int32), n_used - 1)
    tile_bucket = jnp.sum((tile_ids[:, None] >= tile_end[None, :]).astype(jnp.int32), axis=1)
    group = tile_bucket // PAIRS_PER_GROUP
    pair = tile_bucket % PAIRS_PER_GROUP
    tile_ea = group * EXPERTS_PER_GROUP + jnp.asarray(_PAIR_A, jnp.int32)[pair]
    tile_eb = group * EXPERTS_PER_GROUP + jnp.asarray(_PAIR_B, jnp.int32)[pair]
    return pos, tile_ea, tile_eb, n_used.reshape(1)


def kernel(x, c, w_ada, b_ada, norm_mix_g, w_in, v_norm_g, sgu_w, sgu_b, pool_w, pool_scale,
           w_branch_a, w_branch_b, w_out, norm_ffn_g, router_w, router_bias,
           w_exp_gate, w_exp_up, w_exp_down, final_norm_g):
    batch, seq, d = x.shape
    depth = w_ada.shape[0]
    t = batch * seq
    n_tiles_max = t // EXPERT_TILE + N_BUCKETS
    n_slots = n_tiles_max * EXPERT_TILE

    mod = _ada(c, w_ada, b_ada)
    rw_pad = jnp.pad(router_w, ((0, 0), (0, 128 - N_EXPERTS)))
    rw_hi = rw_pad.astype(_bf16)
    rw_lo = (rw_pad - rw_hi.astype(_f32)).astype(_bf16)
    router_w_pad = jnp.concatenate([rw_hi, rw_lo], axis=1)
    router_wt = router_w.T.reshape(N_EXPERTS, 1, d)
    final_g = final_norm_g.reshape(1, d)

    def layer_bf16(stacks, l):
        return tuple(w[l].astype(_bf16) for w in stacks)

    def tie(stacks, gate):
        return lax.optimization_barrier((stacks, gate))

    mixer_stacks = (w_in, sgu_w, pool_w, w_branch_a, w_branch_b, w_out)
    expert_stacks = (w_exp_gate, w_exp_up, w_exp_down)
    y_tok, gate_prev = None, None
    mixer_w = layer_bf16(mixer_stacks, 0)
    for l in range(depth):
        mod_l = mod[l].reshape(batch, 1, N_MOD * d)
        w_in_l, sgu_w_l, pool_w_l, w_a_l, w_b_l, w_o_l = mixer_w
        x, hf, route, counts, *expert_w = _mixer(
            x, y_tok, gate_prev, mod_l, norm_mix_g[l].reshape(1, d), norm_ffn_g[l].reshape(1, d),
            w_in_l, v_norm_g[l].reshape(1, SGU_WIDTH), sgu_w_l, sgu_b[l].T,
            pool_w_l, pool_scale[l].reshape(1, POOL_WIDTH), w_a_l, w_b_l, w_o_l,
            router_w_pad, router_bias, l, expert_stacks)
        pos, tile_ea, tile_eb, n_used = _routing_tables(route, counts, n_tiles_max)
        if l + 1 < depth:
            mixer_stacks, pos = tie(mixer_stacks, pos)
            mixer_w = layer_bf16(mixer_stacks, l + 1)
        xs = _sc_gather_by_slot(pos, hf.reshape((t,) + PACKED_SLAB), n_slots, n_used * EXPERT_TILE)
        ys = _experts(tile_ea, tile_eb, n_used, xs, router_wt, *expert_w)
        y_tok = _sc_gather_rows(pos, ys).reshape((batch, seq) + PACKED_SLAB)
        gate_prev = mod_l[:, :, 5 * d:6 * d]
    return _final_norm(x, y_tok, gate_prev, final_g)
```

```python
import functools

import jax
import jax.numpy as jnp
from jax import lax
from jax.experimental import pallas as pl
from jax.experimental.pallas import tpu as pltpu
from jax.experimental.pallas import tpu_sc as plsc

D_MODEL = 1024
CHUNK = 128
SGU_HEADS = 8
SGU_HEAD_DIM = 128
SGU_WIDTH = 1024
POOL_WINDOWS = (2, 4, 8, 16)
POOL_GROUP_DIM = 256
POOL_WIDTH = 1024
HALO = 16
POOL_PAD = 8
N_EXPERTS = 16
N_EXPERT_GROUPS = 4
EXPERTS_PER_GROUP = 4
PAIRS_PER_GROUP = 6
N_BUCKETS = N_EXPERT_GROUPS * PAIRS_PER_GROUP
BUCKET_ROWS = 32
D_EXPERT = 512
N_MOD = 6
EPS = 1e-6

SEQ_TILE = 512
SUB_TILE = 256
MIXER_STAGES = 2
ROW_TILE = 512
EXPERT_TILE = 256
VMEM_LIMIT_BYTES = 58 * 1024 * 1024
PACKED_SLAB = (4, 128)
SC_CORES = 2
SC_SUBCORES = 16
SC_CHUNK = 32
SC_LANES = 16
SC_RING = 3

_PAIR_A = (0, 0, 0, 1, 1, 3)
_PAIR_B = (1, 2, 3, 3, 2, 2)

_bf16 = jnp.bfloat16
_f32 = jnp.float32


def _dot(a, b):
    return jnp.dot(a, b, preferred_element_type=_f32)


def _pack_bf16_pairs(x):
    rows, d = x.shape
    hi = lax.bitcast_convert_type(x[:, :d // 2].astype(_bf16).astype(_f32), jnp.uint32)
    lo = lax.bitcast_convert_type(x[:, d // 2:].astype(_bf16).astype(_f32), jnp.uint32)
    return (hi | (lo >> 16)).reshape((rows,) + PACKED_SLAB)


def _unpack_bf16_pairs(words):
    rows = words.shape[0]
    words = words.reshape(rows, PACKED_SLAB[0] * PACKED_SLAB[1])
    hi = lax.bitcast_convert_type(words & jnp.uint32(0xFFFF0000), _f32)
    lo = lax.bitcast_convert_type(words << 16, _f32)
    return jnp.concatenate([hi, lo], axis=1)


def _rms_modulate(x, g, shift, scale):
    y = x * lax.rsqrt(jnp.mean(x * x, axis=-1, keepdims=True) + EPS)
    return (y * g) * (1.0 + scale) + shift


def _ada_kernel(c_ref, w_ref, b_ref, o_ref):
    c = c_ref[...]
    c_act = (c * jax.nn.sigmoid(c)).astype(_bf16)
    o_ref[...] = _dot(c_act, w_ref[...].astype(_bf16)) + b_ref[...]


def _ada(c, w_ada, b_ada):
    depth, d, n = w_ada.shape
    batch = c.shape[0]
    tn = 2048
    return pl.pallas_call(
        _ada_kernel,
        grid=(depth, n // tn),
        in_specs=[
            pl.BlockSpec((batch, d), lambda l, j: (0, 0)),
            pl.BlockSpec((None, d, tn), lambda l, j: (l, 0, j)),
            pl.BlockSpec((None, 1, tn), lambda l, j: (l, 0, j)),
        ],
        out_specs=pl.BlockSpec((None, batch, tn), lambda l, j: (l, 0, j)),
        out_shape=jax.ShapeDtypeStruct((depth, batch, n), _f32),
        compiler_params=pltpu.CompilerParams(
            dimension_semantics=("arbitrary", "arbitrary"), vmem_limit_bytes=VMEM_LIMIT_BYTES),
        name="ada_modulation",
    )(c, w_ada, b_ada.reshape(depth, 1, n))


def _route(hf, rw_ref, rb_ref, carry_ref):
    ts = hf.shape[0]
    hf_hi = hf.astype(_bf16)
    hf_lo = (hf - hf_hi.astype(_f32)).astype(_bf16)
    both = _dot(hf_hi, rw_ref[...])
    logits = both[:, 0:128] + both[:, 128:256] + _dot(hf_lo, rw_ref[:, 0:128])
    lt = logits.T
    rows = [lt[e:e + 1, :] for e in range(N_EXPERTS)]
    m = functools.reduce(jnp.maximum, rows)
    ex = [jnp.exp(r - m) for r in rows]
    den = functools.reduce(lambda a, b: a + b, ex)
    probs = [e / den for e in ex]
    sel = [probs[e] + rb_ref[e] for e in range(N_EXPERTS)]

    def top2_sum(v):
        pairs = [v[i] + v[j] for i, j in zip(_PAIR_A, _PAIR_B)]
        return functools.reduce(jnp.maximum, pairs)

    gscore = [top2_sum(sel[4 * g:4 * g + 4]) for g in range(N_EXPERT_GROUPS)]
    best = gscore[0]
    gidx = jnp.zeros_like(best, dtype=jnp.int32)
    for g in range(1, N_EXPERT_GROUPS):
        better = gscore[g] > best
        best = jnp.where(better, gscore[g], best)
        gidx = jnp.where(better, g, gidx)
    ing = []
    for k in range(EXPERTS_PER_GROUP):
        v = sel[k]
        for g in range(1, N_EXPERT_GROUPS):
            v = jnp.where(gidx == g, sel[4 * g + k], v)
        ing.append(v)
    chosen = []
    for k in range(EXPERTS_PER_GROUP):
        r = jnp.zeros_like(gidx)
        for j in range(EXPERTS_PER_GROUP):
            if j == k:
                continue
            beats = (ing[j] >= ing[k]) if j < k else (ing[j] > ing[k])
            r = r + beats.astype(jnp.int32)
        chosen.append(r < 2)
    lo = jnp.where(chosen[0], 0, jnp.where(chosen[1], 1, 2))
    hi = jnp.where(chosen[3], 3, jnp.where(chosen[2], 2, 1))
    pair = jnp.where(lo == 0, hi - 1, jnp.where(lo == 2, 5, jnp.where(hi == 3, 3, 4)))
    bucket = gidx * PAIRS_PER_GROUP + pair

    brow = lax.broadcasted_iota(jnp.int32, (BUCKET_ROWS, ts), 0)
    onehot = (brow == bucket).astype(_f32)
    jj = lax.broadcasted_iota(jnp.int32, (ts, ts), 0)
    tt = lax.broadcasted_iota(jnp.int32, (ts, ts), 1)
    upper = (jj <= tt).astype(_bf16)
    cum = _dot(onehot.astype(_bf16), upper)
    carry = carry_ref[...][:, 0:1]
    rank = jnp.sum(onehot * (cum - 1.0 + carry), axis=0, keepdims=True)
    carry_ref[...] = carry_ref[...] + jnp.sum(onehot, axis=1, keepdims=True)
    return bucket.astype(_f32), rank


def _mixer_kernel(*refs, has_prev):
    if has_prev:
        x_ref, y_ref, gprev_ref = refs[:3]
        refs = refs[3:]
        x_in = lambda rows: x_ref[rows, :] + gprev_ref[...] * _unpack_bf16_pairs(y_ref[rows])
    else:
        x_ref = refs[0]
        refs = refs[1:]
        x_in = lambda rows: x_ref[rows, :]
    (mod_ref, gmix_ref, gffn_ref, win_ref, vg_ref, sw_ref, sb_ref, pw_ref, ps_ref, wa_ref, wb_ref,
     wo_ref, rw_ref, rb_ref, eg_ref, eu_ref, ed_ref,
     xo_ref, hf_ref, route_ref, counts_ref, eg16_ref, eu16_ref, ed16_ref,
     ext_ref, sum2_ref, sum4_ref, carry_ref) = refs
    b = pl.program_id(0)
    s = pl.program_id(1)
    ts = x_ref.shape[0]
    d = D_MODEL

    @pl.when((b == 0) & (s == 0))
    def _():
        carry_ref[...] = jnp.zeros_like(carry_ref)
        ext_ref[0:POOL_PAD, :] = jnp.zeros((POOL_PAD, POOL_WIDTH), _f32)
        sum2_ref[:, 0:POOL_PAD, :] = jnp.zeros((sum2_ref.shape[0], POOL_PAD, sum2_ref.shape[2]), _f32)
        sum4_ref[:, 0:POOL_PAD, :] = jnp.zeros((sum4_ref.shape[0], POOL_PAD, sum4_ref.shape[2]), _f32)

    @pl.when(s == 0)
    def _():
        ext_ref[POOL_PAD:POOL_PAD + HALO, :] = jnp.zeros((HALO, POOL_WIDTH), _f32)

    for w32_ref, w16_ref in ((eg_ref, eg16_ref), (eu_ref, eu16_ref), (ed_ref, ed16_ref)):
        w16_ref[...] = w32_ref[...].astype(_bf16)

    mod = mod_ref[...]
    ci = lax.broadcasted_iota(jnp.int32, (CHUNK, CHUNK), 0)
    cj = lax.broadcasted_iota(jnp.int32, (CHUNK, CHUNK), 1)
    sgu_w = [jnp.where(ci >= cj, sw_ref[h], jnp.zeros((), _bf16)) for h in range(SGU_HEADS)]
    sub_tiles = [_mixer_rows(r * SUB_TILE, s * ts + r * SUB_TILE, mod, sgu_w,
                             x_in, gmix_ref, gffn_ref, win_ref, vg_ref, sb_ref, pw_ref, ps_ref, wa_ref,
                             wb_ref, wo_ref, rw_ref, rb_ref, xo_ref, hf_ref, route_ref, ext_ref,
                             sum2_ref.at[r], sum4_ref.at[r], carry_ref)
                 for r in range(ts // SUB_TILE)]
    for _ in range(MIXER_STAGES):
        for sub_tile in sub_tiles:
            next(sub_tile, None)
    ext_ref[POOL_PAD:POOL_PAD + HALO, :] = ext_ref[POOL_PAD + ts:POOL_PAD + ts + HALO, :]
    route_ref[2:8, :] = jnp.zeros((6, ts), _f32)
    counts_ref[...] = carry_ref[...]


def _mixer_rows(row0, seq_pos0, mod, sgu_w, x_in, gmix_ref, gffn_ref, win_ref, vg_ref, sb_ref,
                pw_ref, ps_ref, wa_ref, wb_ref, wo_ref, rw_ref, rb_ref, xo_ref, hf_ref, route_ref,
                ext_ref, sum2_ref, sum4_ref, carry_ref):
    d = D_MODEL
    ts = SUB_TILE
    rows = slice(row0, row0 + ts)
    sh_m, sc_m, g_m = mod[:, 0:d], mod[:, d:2 * d], mod[:, 2 * d:3 * d]
    sh_f, sc_f = mod[:, 3 * d:4 * d], mod[:, 4 * d:5 * d]
    x = x_in(rows)

    hb = _rms_modulate(x, gmix_ref[...], sh_m, sc_m).astype(_bf16)

    v = jax.nn.gelu(_dot(hb, win_ref[:, SGU_WIDTH:2 * SGU_WIDTH]))
    p = _dot(hb, win_ref[:, 2 * SGU_WIDTH:2 * SGU_WIDTH + POOL_WIDTH])
    u = jax.nn.gelu(_dot(hb, win_ref[:, 0:SGU_WIDTH]))
    gate_a = jax.nn.sigmoid(_dot(hb, win_ref[:, 3 * d:4 * d]))
    gate_b = jax.nn.sigmoid(_dot(hb, win_ref[:, 4 * d:5 * d]))
    e0 = POOL_PAD + HALO + row0
    ext_ref[e0:e0 + ts, :] = p
    yield

    vc = v - jnp.mean(v, axis=-1, keepdims=True)
    vn = (vc * lax.rsqrt(jnp.mean(vc * vc, axis=-1, keepdims=True) + EPS) * vg_ref[...]).astype(_bf16)
    n_chunks = ts // CHUNK
    ya_cols = []
    for h in range(SGU_HEADS):
        cols = slice(h * SGU_HEAD_DIM, (h + 1) * SGU_HEAD_DIM)
        rhs = jnp.concatenate([vn[n * CHUNK:(n + 1) * CHUNK, cols] for n in range(n_chunks)], axis=1)
        sg = _dot(sgu_w[h], rhs) + sb_ref[:, h:h + 1]
        s_h = jnp.concatenate([sg[:, n * SGU_HEAD_DIM:(n + 1) * SGU_HEAD_DIM] for n in range(n_chunks)],
                              axis=0)
        ya_cols.append((u[:, cols] * s_h).astype(_bf16))
    ya = jnp.concatenate(ya_cols, axis=1)
    merged = gate_a * _dot(ya, wa_ref[...])

    pos1 = (seq_pos0 + 1 + lax.broadcasted_iota(jnp.int32, (ts, 1), 0)).astype(_f32)
    g = POOL_GROUP_DIM
    lo, n = e0 - HALO, HALO + ts
    sum2 = ext_ref[lo:lo + n, :] + ext_ref[lo - 1:lo - 1 + n, :]
    sum2_ref[POOL_PAD:POOL_PAD + n, :] = sum2[:, g:]
    sum4 = sum2[:, g:] + sum2_ref[POOL_PAD - 2:POOL_PAD - 2 + n, :]
    sum4_ref[POOL_PAD:POOL_PAD + n, :] = sum4[:, g:]
    sum8 = sum4[:, g:] + sum4_ref[POOL_PAD - 4:POOL_PAD - 4 + n, :]
    sum16 = sum8[8:, g:] + sum8[:n - 8, g:]
    window_sums = (sum2[HALO:, 0:g], sum4[HALO:, 0:g], sum8[HALO:, 0:g], sum16[HALO - 8:, :])
    yb_cols = []
    for gi, w in enumerate(POOL_WINDOWS):
        cols = slice(gi * g, (gi + 1) * g)
        count = jnp.minimum(pos1, float(w))
        pooled = (window_sums[gi] / count - p[:, cols]).astype(_bf16)
        yb_cols.append(_dot(pooled, pw_ref[gi]))
    yb = (jnp.concatenate(yb_cols, axis=1) * ps_ref[...]).astype(_bf16)
    merged = merged + gate_b * _dot(yb, wb_ref[...])

    x_new = x + g_m * _dot(merged.astype(_bf16), wo_ref[...])
    xo_ref[rows, :] = x_new

    hf = _rms_modulate(x_new, gffn_ref[...], sh_f, sc_f)
    hf_ref[rows] = _pack_bf16_pairs(hf)
    bucket, rank = _route(hf, rw_ref, rb_ref, carry_ref)
    route_ref[0:1, rows] = bucket
    route_ref[1:2, rows] = rank
    yield


def _mixer(x, y_prev, gate_prev, mod_l, gmix, gffn, w_in, v_g, sgu_w, sgu_bt, pool_w, pool_scale,
           w_a, w_b, w_o, router_w_pad, router_bias, layer, expert_stacks):
    batch, seq, d = x.shape
    has_prev = y_prev is not None
    ts = SEQ_TILE
    n_tiles = batch * (seq // ts)
    tiles_per_seq = seq // ts
    const = lambda *shape: pl.BlockSpec(shape, lambda b, s: (0,) * len(shape),
                                        pipeline_mode=pl.Buffered(1))
    row_spec = pl.BlockSpec((None, ts, d), lambda b, s: (b, s, 0))
    slab_spec = pl.BlockSpec((None, ts) + PACKED_SLAB, lambda b, s: (b, s, 0, 0))
    prev_specs = [slab_spec, pl.BlockSpec((None, 1, d), lambda b, s: (b, 0, 0))] if has_prev else []
    prev_args = (y_prev, gate_prev) if has_prev else ()
    n_steps = batch * tiles_per_seq
    parts = n_steps // N_EXPERTS
    assert parts * N_EXPERTS == n_steps

    def part_index(b, s):
        step = b * tiles_per_seq + s
        return step // parts, step % parts

    def part_specs(w, with_layer):
        rows, cols = w.shape[-2] // parts, w.shape[-1]
        if with_layer:
            return pl.BlockSpec((None, None, rows, cols), lambda b, s: (layer, *part_index(b, s), 0))
        return pl.BlockSpec((None, rows, cols), lambda b, s: (*part_index(b, s), 0))

    return pl.pallas_call(
        functools.partial(_mixer_kernel, has_prev=has_prev),
        grid=(batch, tiles_per_seq),
        in_specs=[row_spec] + prev_specs + [
            pl.BlockSpec((None, 1, N_MOD * d), lambda b, s: (b, 0, 0)),
            const(1, d), const(1, d),
            const(*w_in.shape),
            const(1, SGU_WIDTH),
            const(*sgu_w.shape),
            const(*sgu_bt.shape),
            const(*pool_w.shape),
            const(1, POOL_WIDTH),
            const(d, d), const(d, d), const(d, d),
            const(*router_w_pad.shape),
            pl.BlockSpec(memory_space=pltpu.SMEM),
        ] + [part_specs(w, True) for w in expert_stacks],
        out_specs=[
            pl.BlockSpec((None, ts, d), lambda b, s: (b, s, 0)),
            pl.BlockSpec((None, ts) + PACKED_SLAB, lambda b, s: (b, s, 0, 0)),
            pl.BlockSpec((None, 8, ts), lambda b, s: (b * tiles_per_seq + s, 0, 0)),
            pl.BlockSpec((BUCKET_ROWS, 128), lambda b, s: (0, 0)),
        ] + [part_specs(w, False) for w in expert_stacks],
        out_shape=[
            jax.ShapeDtypeStruct((batch, seq, d), _f32),
            jax.ShapeDtypeStruct((batch, seq) + PACKED_SLAB, jnp.uint32),
            jax.ShapeDtypeStruct((n_tiles, 8, ts), _f32),
            jax.ShapeDtypeStruct((BUCKET_ROWS, 128), _f32),
        ] + [jax.ShapeDtypeStruct(w.shape[1:], _bf16) for w in expert_stacks],
        scratch_shapes=[
            pltpu.VMEM((POOL_PAD + HALO + ts, POOL_WIDTH), _f32),
            pltpu.VMEM((ts // SUB_TILE, POOL_PAD + HALO + SUB_TILE, POOL_WIDTH - POOL_GROUP_DIM), _f32),
            pltpu.VMEM((ts // SUB_TILE, POOL_PAD + HALO + SUB_TILE, POOL_WIDTH - 2 * POOL_GROUP_DIM), _f32),
            pltpu.VMEM((BUCKET_ROWS, 128), _f32),
        ],
        compiler_params=pltpu.CompilerParams(
            dimension_semantics=("arbitrary", "arbitrary"), vmem_limit_bytes=VMEM_LIMIT_BYTES),
        name="mixer_router",
    )(x, *prev_args, mod_l, gmix, gffn, w_in, v_g, sgu_w, sgu_bt, pool_w, pool_scale, w_a, w_b, w_o,
      router_w_pad, router_bias, *expert_stacks)


def _sc_mesh():
    return plsc.VectorSubcoreMesh(core_axis_name="c", subcore_axis_name="s")


def _sc_worker():
    return lax.axis_index("s") * SC_CORES + lax.axis_index("c")


def _sc_gather_chunks(table_hbm, idx_v, out_hbm, out_row0, n_chunks, bufs, sems):
    depth = len(bufs)

    def gather(j, slot):
        return pltpu.make_async_copy(table_hbm.at[idx_v.at[j]], bufs[slot], sems[slot])

    for j in range(depth - 1):
        @pl.when(j < n_chunks)
        def _():
            gather(j, j).start()

    @pl.loop(0, n_chunks, step=depth)
    def _(j0):
        for slot in range(depth):
            j = j0 + slot

            @pl.when(j < n_chunks)
            def _():
                @pl.when(j + depth - 1 < n_chunks)
                def _():
                    gather(j + depth - 1, (slot + depth - 1) % depth).start()

                gather(j, slot).wait()
                pltpu.sync_copy(bufs[slot], out_hbm.at[pl.ds(out_row0 + j * SC_CHUNK, SC_CHUNK)])


def _sc_chunks_per_worker(n_rows):
    n_chunks = n_rows // (SC_CORES * SC_SUBCORES * SC_CHUNK)
    assert n_chunks * SC_CORES * SC_SUBCORES * SC_CHUNK == n_rows
    return n_chunks


def _sc_gather_rows(idx, table):
    n_out = idx.shape[0]
    n_chunks = _sc_chunks_per_worker(n_out)
    slab = table.shape[1:]
    buf = pltpu.VMEM((SC_CHUNK,) + slab, table.dtype)

    @functools.partial(
        pl.kernel, mesh=_sc_mesh(),
        out_type=jax.ShapeDtypeStruct((n_out,) + slab, table.dtype),
        scratch_types=[pltpu.VMEM((n_chunks, SC_CHUNK), jnp.int32)]
        + [buf] * SC_RING + [pltpu.SemaphoreType.DMA] * SC_RING,
    )
    def gather_kernel(table_hbm, idx_hbm, out_hbm, idx_v, *ring):
        chunk0 = _sc_worker() * n_chunks
        pltpu.sync_copy(idx_hbm.at[pl.ds(chunk0, n_chunks)], idx_v)
        _sc_gather_chunks(table_hbm, idx_v, out_hbm, chunk0 * SC_CHUNK, n_chunks,
                          ring[:SC_RING], ring[SC_RING:])

    return gather_kernel(table, idx.reshape(n_out // SC_CHUNK, SC_CHUNK))


def _sc_gather_by_slot(pos, table, n_slots, n_live_slots):
    n_tok = pos.shape[0]
    per_worker = n_slots // (SC_CORES * SC_SUBCORES)
    n_chunks = _sc_chunks_per_worker(n_slots)
    slab = table.shape[1:]
    buf = pltpu.VMEM((SC_CHUNK,) + slab, table.dtype)

    @functools.partial(
        pl.kernel, mesh=_sc_mesh(),
        compiler_params=pltpu.CompilerParams(needs_layout_passes=False),
        out_type=jax.ShapeDtypeStruct((n_slots,) + slab, table.dtype),
        scratch_types=[pltpu.VMEM((n_tok,), jnp.int32), pltpu.VMEM((n_chunks, SC_CHUNK), jnp.int32),
                       pltpu.VMEM((SC_LANES,), jnp.int32)]
        + [buf] * SC_RING + [pltpu.SemaphoreType.DMA] * SC_RING,
    )
    def slot_kernel(table_hbm, pos_hbm, live_hbm, out_hbm, pos_v, inv_v, live_v, *ring):
        slot0 = _sc_worker() * per_worker
        pltpu.sync_copy(pos_hbm, pos_v)
        pltpu.sync_copy(live_hbm, live_v)
        n_live_chunks = jnp.clip((jnp.max(live_v[...]) - slot0) // SC_CHUNK, 0, n_chunks)
        lane = lax.iota(jnp.int32, SC_LANES)

        @plsc.parallel_loop(0, n_chunks)
        def _(j):
            for h in range(SC_CHUNK // SC_LANES):
                s = slot0 + j * SC_CHUNK + h * SC_LANES + lane
                inv_v[j, pl.ds(h * SC_LANES, SC_LANES)] = jnp.where(s >= n_tok, s - n_tok, s)

        @plsc.parallel_loop(0, n_tok // SC_LANES, unroll=4)
        def _(i):
            local = pos_v[pl.ds(i * SC_LANES, SC_LANES)] - slot0
            mine = (local >= 0) & (local < per_worker)
            local = jnp.where(mine, local, 0)
            plsc.store_scatter(inv_v, [local // SC_CHUNK, local % SC_CHUNK], i * SC_LANES + lane, mask=mine)

        _sc_gather_chunks(table_hbm, inv_v, out_hbm, slot0, n_live_chunks, ring[:SC_RING], ring[SC_RING:])

    return slot_kernel(table, pos, jnp.broadcast_to(n_live_slots, (SC_LANES,)))


def _final_kernel(x_ref, y_ref, g_ref, fg_ref, o_ref):
    out = x_ref[...] + g_ref[...] * _unpack_bf16_pairs(y_ref[...])
    o_ref[...] = out * lax.rsqrt(jnp.mean(out * out, axis=-1, keepdims=True) + EPS) * fg_ref[...]


def _final_norm(x, y_tok, gate_f, final_g):
    batch, seq, d = x.shape
    row_spec = pl.BlockSpec((None, ROW_TILE, d), lambda b, s: (b, s, 0))
    return pl.pallas_call(
        _final_kernel,
        grid=(batch, seq // ROW_TILE),
        in_specs=[row_spec, pl.BlockSpec((None, ROW_TILE) + PACKED_SLAB, lambda b, s: (b, s, 0, 0)),
                  pl.BlockSpec((None, 1, d), lambda b, s: (b, 0, 0)),
                  pl.BlockSpec((1, d), lambda b, s: (0, 0))],
        out_specs=row_spec,
        out_shape=jax.ShapeDtypeStruct((batch, seq, d), _f32),
        compiler_params=pltpu.CompilerParams(dimension_semantics=("arbitrary", "arbitrary")),
        name="residual_final_norm",
    )(x, y_tok, gate_f, final_g)


def _experts_kernel(ea_ref, eb_ref, nused_ref, xs_ref, rwa_ref, rwb_ref,
                    wga_ref, wgb_ref, wua_ref, wub_ref, wda_ref, wdb_ref, *rest):
    n_cast = (len(rest) - 1) // 2
    cast_in, ys_ref, cast_out = rest[:n_cast], rest[n_cast], rest[n_cast + 1:]
    i = pl.program_id(0)

    def cast_share():
        for w32_ref, w16_ref in zip(cast_in, cast_out):
            w16_ref[...] = w32_ref[...].astype(_bf16)

    @pl.when(i < nused_ref[0])
    def _():
        cast_share()
        x = _unpack_bf16_pairs(xs_ref[...])
        la = jnp.sum(x * rwa_ref[...], axis=-1, keepdims=True)
        lb = jnp.sum(x * rwb_ref[...], axis=-1, keepdims=True)
        wa = jax.nn.sigmoid(la - lb)
        wb = jax.nn.sigmoid(lb - la)
        xb = x.astype(_bf16)
        act_a = (jax.nn.silu(_dot(xb, wga_ref[...])) * _dot(xb, wua_ref[...]) * wa).astype(_bf16)
        act_b = (jax.nn.silu(_dot(xb, wgb_ref[...])) * _dot(xb, wub_ref[...]) * wb).astype(_bf16)
        y = _dot(act_a, wda_ref[...]) + _dot(act_b, wdb_ref[...])
        ys_ref[...] = _pack_bf16_pairs(y)

    @pl.when(i >= nused_ref[0])
    def _():
        cast_share()
        ys_ref[...] = jnp.zeros_like(ys_ref)


def _experts(tile_ea, tile_eb, n_used, xs, router_wt, w_gate, w_up, w_down, cast_stacks=(), cast_layer=0):
    n_slots = xs.shape[0]
    d = D_MODEL
    n_tiles = n_slots // EXPERT_TILE
    f = D_EXPERT
    n_cast_steps = n_tiles - N_BUCKETS

    def share(i):
        return jnp.minimum(i, n_cast_steps - 1)

    def cast_rows(w):
        rows = w.shape[1] // n_cast_steps
        assert rows * n_cast_steps == w.shape[1] and rows % 16 == 0
        return rows

    cast_in_specs = [pl.BlockSpec((None, cast_rows(w), w.shape[2]),
                                  lambda i, ea, eb, nu: (cast_layer, share(i), 0)) for w in cast_stacks]
    cast_out_specs = [pl.BlockSpec((cast_rows(w), w.shape[2]), lambda i, ea, eb, nu: (share(i), 0))
                      for w in cast_stacks]
    cast_out_shapes = [jax.ShapeDtypeStruct(w.shape[1:], _bf16) for w in cast_stacks]

    def row(i, ea, eb, nu):
        return (jnp.maximum(jnp.minimum(i, nu[0] - 1), 0), 0, 0)

    grid_spec = pltpu.PrefetchScalarGridSpec(
        num_scalar_prefetch=3,
        grid=(n_tiles,),
        in_specs=[
            pl.BlockSpec((EXPERT_TILE,) + PACKED_SLAB, row),
            pl.BlockSpec((None, 1, d), lambda i, ea, eb, nu: (ea[i], 0, 0)),
            pl.BlockSpec((None, 1, d), lambda i, ea, eb, nu: (eb[i], 0, 0)),
            pl.BlockSpec((None, d, f), lambda i, ea, eb, nu: (ea[i], 0, 0)),
            pl.BlockSpec((None, d, f), lambda i, ea, eb, nu: (eb[i], 0, 0)),
            pl.BlockSpec((None, d, f), lambda i, ea, eb, nu: (ea[i], 0, 0)),
            pl.BlockSpec((None, d, f), lambda i, ea, eb, nu: (eb[i], 0, 0)),
            pl.BlockSpec((None, f, d), lambda i, ea, eb, nu: (ea[i], 0, 0)),
            pl.BlockSpec((None, f, d), lambda i, ea, eb, nu: (eb[i], 0, 0)),
        ] + cast_in_specs,
        out_specs=[pl.BlockSpec((EXPERT_TILE,) + PACKED_SLAB, lambda i, ea, eb, nu: (i, 0, 0))]
        + cast_out_specs,
    )
    return pl.pallas_call(
        _experts_kernel,
        grid_spec=grid_spec,
        out_shape=[jax.ShapeDtypeStruct((n_slots,) + PACKED_SLAB, jnp.uint32)] + cast_out_shapes,
        compiler_params=pltpu.CompilerParams(
            dimension_semantics=("arbitrary",), vmem_limit_bytes=VMEM_LIMIT_BYTES),
        name="grouped_experts",
    )(tile_ea, tile_eb, n_used, xs, router_wt, router_wt, w_gate, w_gate, w_up, w_up, w_down, w_down,
      *cast_stacks)


def _routing_tables(route, counts, n_tiles_max):
    bucket = route[:, 0, :].reshape(-1).astype(jnp.int32)
    rank = route[:, 1, :].reshape(-1).astype(jnp.int32)
    cnt = counts[:N_BUCKETS, 0].astype(jnp.int32)
    tiles_b = (cnt + EXPERT_TILE - 1) // EXPERT_TILE
    tile_end = jnp.cumsum(tiles_b)
    tile_start = tile_end - tiles_b
    n_used = tile_end[-1]
    onehot = bucket[:, None] == jnp.arange(N_BUCKETS, dtype=jnp.int32)[None, :]
    pos = jnp.sum(jnp.where(onehot, (tile_start * EXPERT_TILE)[None, :], 0), axis=1) + rank
    tile_ids = jnp.minimum(jnp.arange(n_tiles_max, dtype=jnp.int32), n_used - 1)
    tile_bucket = jnp.sum((tile_ids[:, None] >= tile_end[None, :]).astype(jnp.int32), axis=1)
    group = tile_bucket // PAIRS_PER_GROUP
    pair = tile_bucket % PAIRS_PER_GROUP
    tile_ea = group * EXPERTS_PER_GROUP + jnp.asarray(_PAIR_A, jnp.int32)[pair]
    tile_eb = group * EXPERTS_PER_GROUP + jnp.asarray(_PAIR_B, jnp.int32)[pair]
    return pos, tile_ea, tile_eb, n_used.reshape(1)


def kernel(x, c, w_ada, b_ada, norm_mix_g, w_in, v_norm_g, sgu_w, sgu_b, pool_w, pool_scale,
           w_branch_a, w_branch_b, w_out, norm_ffn_g, router_w, router_bias,
           w_exp_gate, w_exp_up, w_exp_down, final_norm_g):
    batch, seq, d = x.shape
    depth = w_ada.shape[0]
    t = batch * seq
    n_tiles_max = t // EXPERT_TILE + N_BUCKETS
    n_slots = n_tiles_max * EXPERT_TILE

    mod = _ada(c, w_ada, b_ada)
    rw_pad = jnp.pad(router_w, ((0, 0), (0, 128 - N_EXPERTS)))
    rw_hi = rw_pad.astype(_bf16)
    rw_lo = (rw_pad - rw_hi.astype(_f32)).astype(_bf16)
    router_w_pad = jnp.concatenate([rw_hi, rw_lo], axis=1)
    router_wt = router_w.T.reshape(N_EXPERTS, 1, d)
    final_g = final_norm_g.reshape(1, d)

    def layer_bf16(stacks, l):
        return tuple(w[l].astype(_bf16) for w in stacks)

    mixer_stacks = (w_in, sgu_w.reshape(depth, d, SGU_HEAD_DIM), pool_w.reshape(depth, d, POOL_GROUP_DIM),
                    w_branch_a, w_branch_b, w_out)
    expert_stacks = (w_exp_gate, w_exp_up, w_exp_down)
    y_tok, gate_prev = None, None
    mixer_w = layer_bf16(mixer_stacks, 0)
    for l in range(depth):
        mod_l = mod[l].reshape(batch, 1, N_MOD * d)
        w_in_l, sgu_w_l, pool_w_l, w_a_l, w_b_l, w_o_l = mixer_w
        sgu_w_l = sgu_w_l.reshape(SGU_HEADS, CHUNK, CHUNK)
        pool_w_l = pool_w_l.reshape(len(POOL_WINDOWS), POOL_GROUP_DIM, POOL_GROUP_DIM)
        x, hf, route, counts, *expert_w = _mixer(
            x, y_tok, gate_prev, mod_l, norm_mix_g[l].reshape(1, d), norm_ffn_g[l].reshape(1, d),
            w_in_l, v_norm_g[l].reshape(1, SGU_WIDTH), sgu_w_l, sgu_b[l].T,
            pool_w_l, pool_scale[l].reshape(1, POOL_WIDTH), w_a_l, w_b_l, w_o_l,
            router_w_pad, router_bias, l, expert_stacks)
        pos, tile_ea, tile_eb, n_used = _routing_tables(route, counts, n_tiles_max)
        xs = _sc_gather_by_slot(pos, hf.reshape((t,) + PACKED_SLAB), n_slots, n_used * EXPERT_TILE)
        ys, *mixer_w = _experts(tile_ea, tile_eb, n_used, xs, router_wt, *expert_w,
                                cast_stacks=mixer_stacks if l + 1 < depth else (), cast_layer=l + 1)
        y_tok = _sc_gather_rows(pos, ys).reshape((batch, seq) + PACKED_SLAB)
        gate_prev = mod_l[:, :, 5 * d:6 * d]
    return _final_norm(x, y_tok, gate_prev, final_g)
```

```python
import functools

import jax
import jax.numpy as jnp
from jax import lax
from jax.experimental import pallas as pl
from jax.experimental.pallas import tpu as pltpu
from jax.experimental.pallas import tpu_sc as plsc

D_MODEL = 1024
CHUNK = 128
SGU_HEADS = 8
SGU_HEAD_DIM = 128
SGU_WIDTH = 1024
POOL_WINDOWS = (2, 4, 8, 16)
POOL_GROUP_DIM = 256
POOL_WIDTH = 1024
HALO = 16
POOL_PAD = 8
N_EXPERTS = 16
N_EXPERT_GROUPS = 4
EXPERTS_PER_GROUP = 4
PAIRS_PER_GROUP = 6
N_BUCKETS = N_EXPERT_GROUPS * PAIRS_PER_GROUP
BUCKET_ROWS = 32
D_EXPERT = 512
N_MOD = 6
EPS = 1e-6

SEQ_TILE = 512
SUB_TILE = 256
MIXER_STAGES = 2
ROW_TILE = 512
EXPERT_TILE = 256
EXPERT_PARTS = 2
VMEM_LIMIT_BYTES = 58 * 1024 * 1024
PACKED_SLAB = (4, 128)
SC_CORES = 2
SC_SUBCORES = 16
SC_CHUNK = 32
SC_LANES = 16
SC_RING = 3

_PAIR_A = (0, 0, 0, 1, 1, 3)
_PAIR_B = (1, 2, 3, 3, 2, 2)

_bf16 = jnp.bfloat16
_f32 = jnp.float32


def _dot(a, b):
    return jnp.dot(a, b, preferred_element_type=_f32)


def _pack_bf16_pairs(x):
    rows, d = x.shape
    hi = lax.bitcast_convert_type(x[:, :d // 2].astype(_bf16).astype(_f32), jnp.uint32)
    lo = lax.bitcast_convert_type(x[:, d // 2:].astype(_bf16).astype(_f32), jnp.uint32)
    return (hi | (lo >> 16)).reshape((rows,) + PACKED_SLAB)


def _unpack_bf16_pairs(words):
    rows = words.shape[0]
    words = words.reshape(rows, PACKED_SLAB[0] * PACKED_SLAB[1])
    hi = lax.bitcast_convert_type(words & jnp.uint32(0xFFFF0000), _f32)
    lo = lax.bitcast_convert_type(words << 16, _f32)
    return jnp.concatenate([hi, lo], axis=1)


def _rms_modulate(x, g, shift, scale):
    y = x * lax.rsqrt(jnp.mean(x * x, axis=-1, keepdims=True) + EPS)
    return (y * g) * (1.0 + scale) + shift


def _ada_kernel(c_ref, w_ref, b_ref, o_ref):
    c = c_ref[...]
    c_act = (c * jax.nn.sigmoid(c)).astype(_bf16)
    o_ref[...] = _dot(c_act, w_ref[...].astype(_bf16)) + b_ref[...]


def _ada(c, w_ada, b_ada):
    depth, d, n = w_ada.shape
    batch = c.shape[0]
    tn = 2048
    return pl.pallas_call(
        _ada_kernel,
        grid=(depth, n // tn),
        in_specs=[
            pl.BlockSpec((batch, d), lambda l, j: (0, 0)),
            pl.BlockSpec((None, d, tn), lambda l, j: (l, 0, j)),
            pl.BlockSpec((None, 1, tn), lambda l, j: (l, 0, j)),
        ],
        out_specs=pl.BlockSpec((None, batch, tn), lambda l, j: (l, 0, j)),
        out_shape=jax.ShapeDtypeStruct((depth, batch, n), _f32),
        compiler_params=pltpu.CompilerParams(
            dimension_semantics=("arbitrary", "arbitrary"), vmem_limit_bytes=VMEM_LIMIT_BYTES),
        name="ada_modulation",
    )(c, w_ada, b_ada.reshape(depth, 1, n))


def _route(hf, rw_ref, rb_ref, carry_ref):
    ts = hf.shape[0]
    hf_hi = hf.astype(_bf16)
    hf_lo = (hf - hf_hi.astype(_f32)).astype(_bf16)
    both = _dot(hf_hi, rw_ref[...])
    logits = both[:, 0:128] + both[:, 128:256] + _dot(hf_lo, rw_ref[:, 0:128])
    lt = logits.T
    rows = [lt[e:e + 1, :] for e in range(N_EXPERTS)]
    m = functools.reduce(jnp.maximum, rows)
    ex = [jnp.exp(r - m) for r in rows]
    den = functools.reduce(lambda a, b: a + b, ex)
    probs = [e / den for e in ex]
    sel = [probs[e] + rb_ref[e] for e in range(N_EXPERTS)]

    def top2_sum(v):
        pairs = [v[i] + v[j] for i, j in zip(_PAIR_A, _PAIR_B)]
        return functools.reduce(jnp.maximum, pairs)

    gscore = [top2_sum(sel[4 * g:4 * g + 4]) for g in range(N_EXPERT_GROUPS)]
    best = gscore[0]
    gidx = jnp.zeros_like(best, dtype=jnp.int32)
    for g in range(1, N_EXPERT_GROUPS):
        better = gscore[g] > best
        best = jnp.where(better, gscore[g], best)
        gidx = jnp.where(better, g, gidx)
    ing = []
    for k in range(EXPERTS_PER_GROUP):
        v = sel[k]
        for g in range(1, N_EXPERT_GROUPS):
            v = jnp.where(gidx == g, sel[4 * g + k], v)
        ing.append(v)
    chosen = []
    for k in range(EXPERTS_PER_GROUP):
        r = jnp.zeros_like(gidx)
        for j in range(EXPERTS_PER_GROUP):
            if j == k:
                continue
            beats = (ing[j] >= ing[k]) if j < k else (ing[j] > ing[k])
            r = r + beats.astype(jnp.int32)
        chosen.append(r < 2)
    lo = jnp.where(chosen[0], 0, jnp.where(chosen[1], 1, 2))
    hi = jnp.where(chosen[3], 3, jnp.where(chosen[2], 2, 1))
    pair = jnp.where(lo == 0, hi - 1, jnp.where(lo == 2, 5, jnp.where(hi == 3, 3, 4)))
    bucket = gidx * PAIRS_PER_GROUP + pair

    brow = lax.broadcasted_iota(jnp.int32, (BUCKET_ROWS, ts), 0)
    onehot = (brow == bucket).astype(_f32)
    jj = lax.broadcasted_iota(jnp.int32, (ts, ts), 0)
    tt = lax.broadcasted_iota(jnp.int32, (ts, ts), 1)
    upper = (jj <= tt).astype(_bf16)
    cum = _dot(onehot.astype(_bf16), upper)
    carry = carry_ref[...][:, 0:1]
    rank = jnp.sum(onehot * (cum - 1.0 + carry), axis=0, keepdims=True)
    carry_ref[...] = carry_ref[...] + jnp.sum(onehot, axis=1, keepdims=True)
    return bucket.astype(_f32), rank


def _mixer_kernel(*refs, has_prev):
    if has_prev:
        x_ref, y_ref, gprev_ref = refs[:3]
        refs = refs[3:]
        x_in = lambda rows: x_ref[rows, :] + gprev_ref[...] * _unpack_bf16_pairs(y_ref[rows])
    else:
        x_ref = refs[0]
        refs = refs[1:]
        x_in = lambda rows: x_ref[rows, :]
    (mod_ref, gmix_ref, gffn_ref, win_ref, vg_ref, sw_ref, sb_ref, pw_ref, ps_ref, wa_ref, wb_ref,
     wo_ref, rw_ref, rb_ref, eg_ref, eu_ref, ed_ref,
     xo_ref, hf_ref, route_ref, counts_ref, eg16_ref, eu16_ref, ed16_ref,
     ext_ref, sum2_ref, sum4_ref, carry_ref) = refs
    b = pl.program_id(0)
    s = pl.program_id(1)
    ts = x_ref.shape[0]
    d = D_MODEL

    @pl.when((b == 0) & (s == 0))
    def _():
        carry_ref[...] = jnp.zeros_like(carry_ref)
        ext_ref[0:POOL_PAD, :] = jnp.zeros((POOL_PAD, POOL_WIDTH), _f32)
        sum2_ref[:, 0:POOL_PAD, :] = jnp.zeros((sum2_ref.shape[0], POOL_PAD, sum2_ref.shape[2]), _f32)
        sum4_ref[:, 0:POOL_PAD, :] = jnp.zeros((sum4_ref.shape[0], POOL_PAD, sum4_ref.shape[2]), _f32)

    @pl.when(s == 0)
    def _():
        ext_ref[POOL_PAD:POOL_PAD + HALO, :] = jnp.zeros((HALO, POOL_WIDTH), _f32)

    for w32_ref, w16_ref in ((eg_ref, eg16_ref), (eu_ref, eu16_ref), (ed_ref, ed16_ref)):
        w16_ref[...] = w32_ref[...].astype(_bf16)

    mod = mod_ref[...]
    ci = lax.broadcasted_iota(jnp.int32, (CHUNK, CHUNK), 0)
    cj = lax.broadcasted_iota(jnp.int32, (CHUNK, CHUNK), 1)
    sgu_w = [jnp.where(ci >= cj, sw_ref[h], jnp.zeros((), _bf16)) for h in range(SGU_HEADS)]
    sub_tiles = [_mixer_rows(r * SUB_TILE, s * ts + r * SUB_TILE, mod, sgu_w,
                             x_in, gmix_ref, gffn_ref, win_ref, vg_ref, sb_ref, pw_ref, ps_ref, wa_ref,
                             wb_ref, wo_ref, rw_ref, rb_ref, xo_ref, hf_ref, route_ref, ext_ref,
                             sum2_ref.at[r], sum4_ref.at[r], carry_ref)
                 for r in range(ts // SUB_TILE)]
    for _ in range(MIXER_STAGES):
        for sub_tile in sub_tiles:
            next(sub_tile, None)
    ext_ref[POOL_PAD:POOL_PAD + HALO, :] = ext_ref[POOL_PAD + ts:POOL_PAD + ts + HALO, :]
    route_ref[2:8, :] = jnp.zeros((6, ts), _f32)
    counts_ref[...] = carry_ref[...]


def _mixer_rows(row0, seq_pos0, mod, sgu_w, x_in, gmix_ref, gffn_ref, win_ref, vg_ref, sb_ref,
                pw_ref, ps_ref, wa_ref, wb_ref, wo_ref, rw_ref, rb_ref, xo_ref, hf_ref, route_ref,
                ext_ref, sum2_ref, sum4_ref, carry_ref):
    d = D_MODEL
    ts = SUB_TILE
    rows = slice(row0, row0 + ts)
    sh_m, sc_m, g_m = mod[:, 0:d], mod[:, d:2 * d], mod[:, 2 * d:3 * d]
    sh_f, sc_f = mod[:, 3 * d:4 * d], mod[:, 4 * d:5 * d]
    x = x_in(rows)

    hb = _rms_modulate(x, gmix_ref[...], sh_m, sc_m).astype(_bf16)

    v = jax.nn.gelu(_dot(hb, win_ref[:, SGU_WIDTH:2 * SGU_WIDTH]))
    p = _dot(hb, win_ref[:, 2 * SGU_WIDTH:2 * SGU_WIDTH + POOL_WIDTH])
    u = jax.nn.gelu(_dot(hb, win_ref[:, 0:SGU_WIDTH]))
    gate_a = jax.nn.sigmoid(_dot(hb, win_ref[:, 3 * d:4 * d]))
    gate_b = jax.nn.sigmoid(_dot(hb, win_ref[:, 4 * d:5 * d]))
    e0 = POOL_PAD + HALO + row0
    ext_ref[e0:e0 + ts, :] = p
    yield

    vc = v - jnp.mean(v, axis=-1, keepdims=True)
    vn = (vc * lax.rsqrt(jnp.mean(vc * vc, axis=-1, keepdims=True) + EPS) * vg_ref[...]).astype(_bf16)
    n_chunks = ts // CHUNK
    ya_cols = []
    for h in range(SGU_HEADS):
        cols = slice(h * SGU_HEAD_DIM, (h + 1) * SGU_HEAD_DIM)
        rhs = jnp.concatenate([vn[n * CHUNK:(n + 1) * CHUNK, cols] for n in range(n_chunks)], axis=1)
        sg = _dot(sgu_w[h], rhs) + sb_ref[:, h:h + 1]
        s_h = jnp.concatenate([sg[:, n * SGU_HEAD_DIM:(n + 1) * SGU_HEAD_DIM] for n in range(n_chunks)],
                              axis=0)
        ya_cols.append((u[:, cols] * s_h).astype(_bf16))
    ya = jnp.concatenate(ya_cols, axis=1)
    merged = gate_a * _dot(ya, wa_ref[...])

    pos1 = (seq_pos0 + 1 + lax.broadcasted_iota(jnp.int32, (ts, 1), 0)).astype(_f32)
    g = POOL_GROUP_DIM
    lo, n = e0 - HALO, HALO + ts
    sum2 = ext_ref[lo:lo + n, :] + ext_ref[lo - 1:lo - 1 + n, :]
    sum2_ref[POOL_PAD:POOL_PAD + n, :] = sum2[:, g:]
    sum4 = sum2[:, g:] + sum2_ref[POOL_PAD - 2:POOL_PAD - 2 + n, :]
    sum4_ref[POOL_PAD:POOL_PAD + n, :] = sum4[:, g:]
    sum8 = sum4[:, g:] + sum4_ref[POOL_PAD - 4:POOL_PAD - 4 + n, :]
    sum16 = sum8[8:, g:] + sum8[:n - 8, g:]
    window_sums = (sum2[HALO:, 0:g], sum4[HALO:, 0:g], sum8[HALO:, 0:g], sum16[HALO - 8:, :])
    yb_cols = []
    for gi, w in enumerate(POOL_WINDOWS):
        cols = slice(gi * g, (gi + 1) * g)
        count = jnp.minimum(pos1, float(w))
        pooled = (window_sums[gi] / count - p[:, cols]).astype(_bf16)
        yb_cols.append(_dot(pooled, pw_ref[gi]))
    yb = (jnp.concatenate(yb_cols, axis=1) * ps_ref[...]).astype(_bf16)
    merged = merged + gate_b * _dot(yb, wb_ref[...])

    x_new = x + g_m * _dot(merged.astype(_bf16), wo_ref[...])
    xo_ref[rows, :] = x_new

    hf = _rms_modulate(x_new, gffn_ref[...], sh_f, sc_f)
    hf_ref[rows] = _pack_bf16_pairs(hf)
    bucket, rank = _route(hf, rw_ref, rb_ref, carry_ref)
    route_ref[0:1, rows] = bucket
    route_ref[1:2, rows] = rank
    yield


def _mixer(x, y_prev, gate_prev, mod_l, gmix, gffn, w_in, v_g, sgu_w, sgu_bt, pool_w, pool_scale,
           w_a, w_b, w_o, router_w_pad, router_bias, layer, expert_stacks):
    batch, seq, d = x.shape
    has_prev = y_prev is not None
    ts = SEQ_TILE
    n_tiles = batch * (seq // ts)
    tiles_per_seq = seq // ts
    const = lambda *shape: pl.BlockSpec(shape, lambda b, s: (0,) * len(shape),
                                        pipeline_mode=pl.Buffered(1))
    row_spec = pl.BlockSpec((None, ts, d), lambda b, s: (b, s, 0))
    slab_spec = pl.BlockSpec((None, ts) + PACKED_SLAB, lambda b, s: (b, s, 0, 0))
    prev_specs = [slab_spec, pl.BlockSpec((None, 1, d), lambda b, s: (b, 0, 0))] if has_prev else []
    prev_args = (y_prev, gate_prev) if has_prev else ()
    n_steps = batch * tiles_per_seq
    parts = n_steps // N_EXPERTS
    assert parts * N_EXPERTS == n_steps

    def part_index(b, s):
        step = b * tiles_per_seq + s
        return step // parts, step % parts

    def part_specs(w, with_layer):
        rows, cols = w.shape[-2] // parts, w.shape[-1]
        if with_layer:
            return pl.BlockSpec((None, None, rows, cols), lambda b, s: (layer, *part_index(b, s), 0))
        return pl.BlockSpec((None, rows, cols), lambda b, s: (*part_index(b, s), 0))

    return pl.pallas_call(
        functools.partial(_mixer_kernel, has_prev=has_prev),
        grid=(batch, tiles_per_seq),
        in_specs=[row_spec] + prev_specs + [
            pl.BlockSpec((None, 1, N_MOD * d), lambda b, s: (b, 0, 0)),
            const(1, d), const(1, d),
            const(*w_in.shape),
            const(1, SGU_WIDTH),
            const(*sgu_w.shape),
            const(*sgu_bt.shape),
            const(*pool_w.shape),
            const(1, POOL_WIDTH),
            const(d, d), const(d, d), const(d, d),
            const(*router_w_pad.shape),
            pl.BlockSpec(memory_space=pltpu.SMEM),
        ] + [part_specs(w, True) for w in expert_stacks],
        out_specs=[
            pl.BlockSpec((None, ts, d), lambda b, s: (b, s, 0)),
            pl.BlockSpec((None, ts) + PACKED_SLAB, lambda b, s: (b, s, 0, 0)),
            pl.BlockSpec((None, 8, ts), lambda b, s: (b * tiles_per_seq + s, 0, 0)),
            pl.BlockSpec((BUCKET_ROWS, 128), lambda b, s: (0, 0)),
        ] + [part_specs(w, False) for w in expert_stacks],
        out_shape=[
            jax.ShapeDtypeStruct((batch, seq, d), _f32),
            jax.ShapeDtypeStruct((batch, seq) + PACKED_SLAB, jnp.uint32),
            jax.ShapeDtypeStruct((n_tiles, 8, ts), _f32),
            jax.ShapeDtypeStruct((BUCKET_ROWS, 128), _f32),
        ] + [jax.ShapeDtypeStruct(w.shape[1:], _bf16) for w in expert_stacks],
        scratch_shapes=[
            pltpu.VMEM((POOL_PAD + HALO + ts, POOL_WIDTH), _f32),
            pltpu.VMEM((ts // SUB_TILE, POOL_PAD + HALO + SUB_TILE, POOL_WIDTH - POOL_GROUP_DIM), _f32),
            pltpu.VMEM((ts // SUB_TILE, POOL_PAD + HALO + SUB_TILE, POOL_WIDTH - 2 * POOL_GROUP_DIM), _f32),
            pltpu.VMEM((BUCKET_ROWS, 128), _f32),
        ],
        compiler_params=pltpu.CompilerParams(
            dimension_semantics=("arbitrary", "arbitrary"), vmem_limit_bytes=VMEM_LIMIT_BYTES),
        name="mixer_router",
    )(x, *prev_args, mod_l, gmix, gffn, w_in, v_g, sgu_w, sgu_bt, pool_w, pool_scale, w_a, w_b, w_o,
      router_w_pad, router_bias, *expert_stacks)


def _sc_mesh():
    return plsc.VectorSubcoreMesh(core_axis_name="c", subcore_axis_name="s")


def _sc_worker():
    return lax.axis_index("s") * SC_CORES + lax.axis_index("c")


def _sc_gather_chunks(table_hbm, idx_v, out_hbm, out_row0, n_chunks, bufs, sems):
    depth = len(bufs)

    def gather(j, slot):
        return pltpu.make_async_copy(table_hbm.at[idx_v.at[j]], bufs[slot], sems[slot])

    for j in range(depth - 1):
        @pl.when(j < n_chunks)
        def _():
            gather(j, j).start()

    @pl.loop(0, n_chunks, step=depth)
    def _(j0):
        for slot in range(depth):
            j = j0 + slot

            @pl.when(j < n_chunks)
            def _():
                @pl.when(j + depth - 1 < n_chunks)
                def _():
                    gather(j + depth - 1, (slot + depth - 1) % depth).start()

                gather(j, slot).wait()
                pltpu.sync_copy(bufs[slot], out_hbm.at[pl.ds(out_row0 + j * SC_CHUNK, SC_CHUNK)])


def _sc_chunks_per_worker(n_rows):
    n_chunks = n_rows // (SC_CORES * SC_SUBCORES * SC_CHUNK)
    assert n_chunks * SC_CORES * SC_SUBCORES * SC_CHUNK == n_rows
    return n_chunks


def _sc_gather_rows(idx, table):
    n_out = idx.shape[0]
    n_chunks = _sc_chunks_per_worker(n_out)
    slab = table.shape[1:]
    buf = pltpu.VMEM((SC_CHUNK,) + slab, table.dtype)

    @functools.partial(
        pl.kernel, mesh=_sc_mesh(),
        out_type=jax.ShapeDtypeStruct((n_out,) + slab, table.dtype),
        scratch_types=[pltpu.VMEM((n_chunks, SC_CHUNK), jnp.int32)]
        + [buf] * SC_RING + [pltpu.SemaphoreType.DMA] * SC_RING,
    )
    def gather_kernel(table_hbm, idx_hbm, out_hbm, idx_v, *ring):
        chunk0 = _sc_worker() * n_chunks
        pltpu.sync_copy(idx_hbm.at[pl.ds(chunk0, n_chunks)], idx_v)
        _sc_gather_chunks(table_hbm, idx_v, out_hbm, chunk0 * SC_CHUNK, n_chunks,
                          ring[:SC_RING], ring[SC_RING:])

    return gather_kernel(table, idx.reshape(n_out // SC_CHUNK, SC_CHUNK))


def _sc_gather_by_slot(pos, table, first_slot, n_slots, n_live_slots):
    n_tok = pos.shape[0]
    per_worker = n_slots // (SC_CORES * SC_SUBCORES)
    n_chunks = _sc_chunks_per_worker(n_slots)
    slab = table.shape[1:]
    buf = pltpu.VMEM((SC_CHUNK,) + slab, table.dtype)

    @functools.partial(
        pl.kernel, mesh=_sc_mesh(),
        compiler_params=pltpu.CompilerParams(needs_layout_passes=False),
        out_type=jax.ShapeDtypeStruct((n_slots,) + slab, table.dtype),
        scratch_types=[pltpu.VMEM((n_tok,), jnp.int32), pltpu.VMEM((n_chunks, SC_CHUNK), jnp.int32),
                       pltpu.VMEM((SC_LANES,), jnp.int32)]
        + [buf] * SC_RING + [pltpu.SemaphoreType.DMA] * SC_RING,
    )
    def slot_kernel(table_hbm, pos_hbm, live_hbm, out_hbm, pos_v, inv_v, live_v, *ring):
        row0 = _sc_worker() * per_worker
        slot0 = first_slot + row0
        pltpu.sync_copy(pos_hbm, pos_v)
        pltpu.sync_copy(live_hbm, live_v)
        n_live_chunks = jnp.clip((jnp.max(live_v[...]) - slot0) // SC_CHUNK, 0, n_chunks)
        lane = lax.iota(jnp.int32, SC_LANES)

        @plsc.parallel_loop(0, n_chunks)
        def _(j):
            for h in range(SC_CHUNK // SC_LANES):
                s = slot0 + j * SC_CHUNK + h * SC_LANES + lane
                inv_v[j, pl.ds(h * SC_LANES, SC_LANES)] = jnp.where(s >= n_tok, s - n_tok, s)

        @plsc.parallel_loop(0, n_tok // SC_LANES, unroll=4)
        def _(i):
            local = pos_v[pl.ds(i * SC_LANES, SC_LANES)] - slot0
            mine = (local >= 0) & (local < per_worker)
            local = jnp.where(mine, local, 0)
            plsc.store_scatter(inv_v, [local // SC_CHUNK, local % SC_CHUNK], i * SC_LANES + lane, mask=mine)

        _sc_gather_chunks(table_hbm, inv_v, out_hbm, row0, n_live_chunks, ring[:SC_RING], ring[SC_RING:])

    return slot_kernel(table, pos, jnp.broadcast_to(n_live_slots, (SC_LANES,)))


def _final_kernel(x_ref, y_ref, g_ref, fg_ref, o_ref):
    out = x_ref[...] + g_ref[...] * _unpack_bf16_pairs(y_ref[...])
    o_ref[...] = out * lax.rsqrt(jnp.mean(out * out, axis=-1, keepdims=True) + EPS) * fg_ref[...]


def _final_norm(x, y_tok, gate_f, final_g):
    batch, seq, d = x.shape
    row_spec = pl.BlockSpec((None, ROW_TILE, d), lambda b, s: (b, s, 0))
    return pl.pallas_call(
        _final_kernel,
        grid=(batch, seq // ROW_TILE),
        in_specs=[row_spec, pl.BlockSpec((None, ROW_TILE) + PACKED_SLAB, lambda b, s: (b, s, 0, 0)),
                  pl.BlockSpec((None, 1, d), lambda b, s: (b, 0, 0)),
                  pl.BlockSpec((1, d), lambda b, s: (0, 0))],
        out_specs=row_spec,
        out_shape=jax.ShapeDtypeStruct((batch, seq, d), _f32),
        compiler_params=pltpu.CompilerParams(dimension_semantics=("arbitrary", "arbitrary")),
        name="residual_final_norm",
    )(x, y_tok, gate_f, final_g)


def _experts_kernel(ea_ref, eb_ref, nused_ref, xs_ref, rwa_ref, rwb_ref,
                    wga_ref, wgb_ref, wua_ref, wub_ref, wda_ref, wdb_ref, *rest, n_cast, first_tile):
    cast_in, ys_ref, cast_out = rest[:n_cast], rest[len(rest) - n_cast - 1], rest[len(rest) - n_cast:]
    i = first_tile + pl.program_id(0)

    def cast_share():
        for w32_ref, w16_ref in zip(cast_in, cast_out):
            w16_ref[...] = w32_ref[...].astype(_bf16)

    @pl.when(i < nused_ref[0])
    def _():
        cast_share()
        x = _unpack_bf16_pairs(xs_ref[...])
        la = jnp.sum(x * rwa_ref[...], axis=-1, keepdims=True)
        lb = jnp.sum(x * rwb_ref[...], axis=-1, keepdims=True)
        wa = jax.nn.sigmoid(la - lb)
        wb = jax.nn.sigmoid(lb - la)
        xb = x.astype(_bf16)
        act_a = (jax.nn.silu(_dot(xb, wga_ref[...])) * _dot(xb, wua_ref[...]) * wa).astype(_bf16)
        act_b = (jax.nn.silu(_dot(xb, wgb_ref[...])) * _dot(xb, wub_ref[...]) * wb).astype(_bf16)
        y = _dot(act_a, wda_ref[...]) + _dot(act_b, wdb_ref[...])
        ys_ref[...] = _pack_bf16_pairs(y)

    @pl.when(i >= nused_ref[0])
    def _():
        cast_share()
        ys_ref[...] = jnp.zeros_like(ys_ref)


def _experts(tile_ea, tile_eb, n_used, xs, first_tile, ys_so_far, n_slots, router_wt, w_gate, w_up, w_down,
             cast_stacks=(), cast_layer=0):
    d = D_MODEL
    n_tiles = xs.shape[0] // EXPERT_TILE
    f = D_EXPERT
    n_cast_steps = 32
    assert n_cast_steps <= n_tiles

    def share(i):
        return jnp.minimum(i, n_cast_steps - 1)

    def cast_rows(w):
        rows = w.shape[1] // n_cast_steps
        assert rows * n_cast_steps == w.shape[1] and rows % 16 == 0
        return rows

    cast_in_specs = [pl.BlockSpec((None, cast_rows(w), w.shape[2]),
                                  lambda i, ea, eb, nu: (cast_layer, share(i), 0)) for w in cast_stacks]
    cast_out_specs = [pl.BlockSpec((cast_rows(w), w.shape[2]), lambda i, ea, eb, nu: (share(i), 0))
                      for w in cast_stacks]
    cast_out_shapes = [jax.ShapeDtypeStruct(w.shape[1:], _bf16) for w in cast_stacks]

    def row(i, ea, eb, nu):
        return (jnp.clip(jnp.minimum(first_tile + i, nu[0] - 1) - first_tile, 0, n_tiles - 1), 0, 0)

    def expert_a(i, ea, eb, nu):
        return (ea[first_tile + i], 0, 0)

    def expert_b(i, ea, eb, nu):
        return (eb[first_tile + i], 0, 0)

    alias_specs = [] if ys_so_far is None else [pl.BlockSpec(memory_space=pl.ANY)]
    alias_args = [] if ys_so_far is None else [ys_so_far]
    n_inputs_before_alias = 3 + 9 + len(cast_stacks)
    grid_spec = pltpu.PrefetchScalarGridSpec(
        num_scalar_prefetch=3,
        grid=(n_tiles,),
        in_specs=[
            pl.BlockSpec((EXPERT_TILE,) + PACKED_SLAB, row),
            pl.BlockSpec((None, 1, d), expert_a),
            pl.BlockSpec((None, 1, d), expert_b),
            pl.BlockSpec((None, d, f), expert_a),
            pl.BlockSpec((None, d, f), expert_b),
            pl.BlockSpec((None, d, f), expert_a),
            pl.BlockSpec((None, d, f), expert_b),
            pl.BlockSpec((None, f, d), expert_a),
            pl.BlockSpec((None, f, d), expert_b),
        ] + cast_in_specs + alias_specs,
        out_specs=[pl.BlockSpec((EXPERT_TILE,) + PACKED_SLAB, lambda i, ea, eb, nu: (first_tile + i, 0, 0))]
        + cast_out_specs,
    )
    return pl.pallas_call(
        functools.partial(_experts_kernel, n_cast=len(cast_stacks), first_tile=first_tile),
        grid_spec=grid_spec,
        out_shape=[jax.ShapeDtypeStruct((n_slots,) + PACKED_SLAB, jnp.uint32)] + cast_out_shapes,
        input_output_aliases={} if ys_so_far is None else {n_inputs_before_alias: 0},
        compiler_params=pltpu.CompilerParams(
            dimension_semantics=("arbitrary",), vmem_limit_bytes=VMEM_LIMIT_BYTES),
        name="grouped_experts",
    )(tile_ea, tile_eb, n_used, xs, router_wt, router_wt, w_gate, w_gate, w_up, w_up, w_down, w_down,
      *cast_stacks, *alias_args)


def _routing_tables(route, counts, n_tiles_max):
    bucket = route[:, 0, :].reshape(-1).astype(jnp.int32)
    rank = route[:, 1, :].reshape(-1).astype(jnp.int32)
    cnt = counts[:N_BUCKETS, 0].astype(jnp.int32)
    tiles_b = (cnt + EXPERT_TILE - 1) // EXPERT_TILE
    tile_end = jnp.cumsum(tiles_b)
    tile_start = tile_end - tiles_b
    n_used = tile_end[-1]
    onehot = bucket[:, None] == jnp.arange(N_BUCKETS, dtype=jnp.int32)[None, :]
    pos = jnp.sum(jnp.where(onehot, (tile_start * EXPERT_TILE)[None, :], 0), axis=1) + rank
    tile_ids = jnp.minimum(jnp.arange(n_tiles_max, dtype=jnp.int32), n_used - 1)
    tile_bucket = jnp.sum((tile_ids[:, None] >= tile_end[None, :]).astype(jnp.int32), axis=1)
    group = tile_bucket // PAIRS_PER_GROUP
    pair = tile_bucket % PAIRS_PER_GROUP
    tile_ea = group * EXPERTS_PER_GROUP + jnp.asarray(_PAIR_A, jnp.int32)[pair]
    tile_eb = group * EXPERTS_PER_GROUP + jnp.asarray(_PAIR_B, jnp.int32)[pair]
    return pos, tile_ea, tile_eb, n_used.reshape(1)


def kernel(x, c, w_ada, b_ada, norm_mix_g, w_in, v_norm_g, sgu_w, sgu_b, pool_w, pool_scale,
           w_branch_a, w_branch_b, w_out, norm_ffn_g, router_w, router_bias,
           w_exp_gate, w_exp_up, w_exp_down, final_norm_g):
    batch, seq, d = x.shape
    depth = w_ada.shape[0]
    t = batch * seq
    n_tiles_max = t // EXPERT_TILE + N_BUCKETS
    n_slots = n_tiles_max * EXPERT_TILE

    mod = _ada(c, w_ada, b_ada)
    rw_pad = jnp.pad(router_w, ((0, 0), (0, 128 - N_EXPERTS)))
    rw_hi = rw_pad.astype(_bf16)
    rw_lo = (rw_pad - rw_hi.astype(_f32)).astype(_bf16)
    router_w_pad = jnp.concatenate([rw_hi, rw_lo], axis=1)
    router_wt = router_w.T.reshape(N_EXPERTS, 1, d)
    final_g = final_norm_g.reshape(1, d)

    def layer_bf16(stacks, l):
        return tuple(w[l].astype(_bf16) for w in stacks)

    mixer_stacks = (w_in, sgu_w.reshape(depth, d, SGU_HEAD_DIM), pool_w.reshape(depth, d, POOL_GROUP_DIM),
                    w_branch_a, w_branch_b, w_out)
    expert_stacks = (w_exp_gate, w_exp_up, w_exp_down)
    y_tok, gate_prev = None, None
    mixer_w = layer_bf16(mixer_stacks, 0)
    for l in range(depth):
        mod_l = mod[l].reshape(batch, 1, N_MOD * d)
        w_in_l, sgu_w_l, pool_w_l, w_a_l, w_b_l, w_o_l = mixer_w
        sgu_w_l = sgu_w_l.reshape(SGU_HEADS, CHUNK, CHUNK)
        pool_w_l = pool_w_l.reshape(len(POOL_WINDOWS), POOL_GROUP_DIM, POOL_GROUP_DIM)
        x, hf, route, counts, *expert_w = _mixer(
            x, y_tok, gate_prev, mod_l, norm_mix_g[l].reshape(1, d), norm_ffn_g[l].reshape(1, d),
            w_in_l, v_norm_g[l].reshape(1, SGU_WIDTH), sgu_w_l, sgu_b[l].T,
            pool_w_l, pool_scale[l].reshape(1, POOL_WIDTH), w_a_l, w_b_l, w_o_l,
            router_w_pad, router_bias, l, expert_stacks)
        pos, tile_ea, tile_eb, n_used = _routing_tables(route, counts, n_tiles_max)
        part_slots = n_slots // EXPERT_PARTS
        xs_parts = [_sc_gather_by_slot(pos, hf.reshape((t,) + PACKED_SLAB), k * part_slots, part_slots,
                                       n_used * EXPERT_TILE) for k in range(EXPERT_PARTS)]
        ys = None
        for k, xs in enumerate(xs_parts):
            cast_stacks = mixer_stacks if (k == 0 and l + 1 < depth) else ()
            ys, *cast_w = _experts(tile_ea, tile_eb, n_used, xs, k * (part_slots // EXPERT_TILE), ys, n_slots,
                                   router_wt, *expert_w, cast_stacks=cast_stacks, cast_layer=l + 1)
            if cast_stacks:
                mixer_w = cast_w
        y_tok = _sc_gather_rows(pos, ys).reshape((batch, seq) + PACKED_SLAB)
        gate_prev = mod_l[:, :, 5 * d:6 * d]
    return _final_norm(x, y_tok, gate_prev, final_g)
```

```python
import functools

import jax
import jax.numpy as jnp
from jax import lax
from jax.experimental import pallas as pl
from jax.experimental.pallas import tpu as pltpu
from jax.experimental.pallas import tpu_sc as plsc

D_MODEL = 1024
CHUNK = 128
SGU_HEADS = 8
SGU_HEAD_DIM = 128
SGU_WIDTH = 1024
POOL_WINDOWS = (2, 4, 8, 16)
POOL_GROUP_DIM = 256
POOL_WIDTH = 1024
HALO = 16
POOL_PAD = 8
N_EXPERTS = 16
N_EXPERT_GROUPS = 4
EXPERTS_PER_GROUP = 4
PAIRS_PER_GROUP = 6
N_BUCKETS = N_EXPERT_GROUPS * PAIRS_PER_GROUP
BUCKET_ROWS = 32
D_EXPERT = 512
N_MOD = 6
EPS = 1e-6

SEQ_TILE = 512
ROW_TILE = 512
EXPERT_TILE = 256
VMEM_LIMIT_BYTES = 58 * 1024 * 1024
PACKED_SLAB = (4, 128)
SC_CORES = 2
SC_SUBCORES = 16
SC_CHUNK = 32
SC_LANES = 16
SC_RING = 3

_PAIR_A = (0, 0, 0, 1, 1, 3)
_PAIR_B = (1, 2, 3, 3, 2, 2)

_bf16 = jnp.bfloat16
_f32 = jnp.float32


def _dot(a, b):
    return jnp.dot(a, b, preferred_element_type=_f32)


def _pack_bf16_pairs(x):
    rows, d = x.shape
    hi = lax.bitcast_convert_type(x[:, :d // 2].astype(_bf16).astype(_f32), jnp.uint32)
    lo = lax.bitcast_convert_type(x[:, d // 2:].astype(_bf16).astype(_f32), jnp.uint32)
    return (hi | (lo >> 16)).reshape((rows,) + PACKED_SLAB)


def _unpack_bf16_pairs(words):
    rows = words.shape[0]
    words = words.reshape(rows, PACKED_SLAB[0] * PACKED_SLAB[1])
    hi = lax.bitcast_convert_type(words & jnp.uint32(0xFFFF0000), _f32)
    lo = lax.bitcast_convert_type(words << 16, _f32)
    return jnp.concatenate([hi, lo], axis=1)


def _rms_modulate(x, g, shift, scale):
    y = x * lax.rsqrt(jnp.mean(x * x, axis=-1, keepdims=True) + EPS)
    return (y * g) * (1.0 + scale) + shift


def _ada_kernel(c_ref, w_ref, b_ref, o_ref):
    c = c_ref[...]
    c_act = (c * jax.nn.sigmoid(c)).astype(_bf16)
    o_ref[...] = _dot(c_act, w_ref[...].astype(_bf16)) + b_ref[...]


def _ada(c, w_ada, b_ada):
    depth, d, n = w_ada.shape
    batch = c.shape[0]
    tn = 2048
    return pl.pallas_call(
        _ada_kernel,
        grid=(depth, n // tn),
        in_specs=[
            pl.BlockSpec((batch, d), lambda l, j: (0, 0)),
            pl.BlockSpec((None, d, tn), lambda l, j: (l, 0, j)),
            pl.BlockSpec((None, 1, tn), lambda l, j: (l, 0, j)),
        ],
        out_specs=pl.BlockSpec((None, batch, tn), lambda l, j: (l, 0, j)),
        out_shape=jax.ShapeDtypeStruct((depth, batch, n), _f32),
        compiler_params=pltpu.CompilerParams(
            dimension_semantics=("arbitrary", "arbitrary"), vmem_limit_bytes=VMEM_LIMIT_BYTES),
        name="ada_modulation",
    )(c, w_ada, b_ada.reshape(depth, 1, n))


def _route(hf, rw_ref, rb_ref, carry_ref):
    ts = hf.shape[0]
    hf_hi = hf.astype(_bf16)
    hf_lo = (hf - hf_hi.astype(_f32)).astype(_bf16)
    both = _dot(hf_hi, rw_ref[...])
    logits = both[:, 0:128] + both[:, 128:256] + _dot(hf_lo, rw_ref[:, 0:128])
    lt = logits.T
    rows = [lt[e:e + 1, :] for e in range(N_EXPERTS)]
    m = functools.reduce(jnp.maximum, rows)
    ex = [jnp.exp(r - m) for r in rows]
    den = functools.reduce(lambda a, b: a + b, ex)
    probs = [e / den for e in ex]
    sel = [probs[e] + rb_ref[e] for e in range(N_EXPERTS)]

    def top2_sum(v):
        pairs = [v[i] + v[j] for i, j in zip(_PAIR_A, _PAIR_B)]
        return functools.reduce(jnp.maximum, pairs)

    gscore = [top2_sum(sel[4 * g:4 * g + 4]) for g in range(N_EXPERT_GROUPS)]
    best = gscore[0]
    gidx = jnp.zeros_like(best, dtype=jnp.int32)
    for g in range(1, N_EXPERT_GROUPS):
        better = gscore[g] > best
        best = jnp.where(better, gscore[g], best)
        gidx = jnp.where(better, g, gidx)
    ing = []
    for k in range(EXPERTS_PER_GROUP):
        v = sel[k]
        for g in range(1, N_EXPERT_GROUPS):
            v = jnp.where(gidx == g, sel[4 * g + k], v)
        ing.append(v)
    chosen = []
    for k in range(EXPERTS_PER_GROUP):
        r = jnp.zeros_like(gidx)
        for j in range(EXPERTS_PER_GROUP):
            if j == k:
                continue
            beats = (ing[j] >= ing[k]) if j < k else (ing[j] > ing[k])
            r = r + beats.astype(jnp.int32)
        chosen.append(r < 2)
    lo = jnp.where(chosen[0], 0, jnp.where(chosen[1], 1, 2))
    hi = jnp.where(chosen[3], 3, jnp.where(chosen[2], 2, 1))
    pair = jnp.where(lo == 0, hi - 1, jnp.where(lo == 2, 5, jnp.where(hi == 3, 3, 4)))
    bucket = gidx * PAIRS_PER_GROUP + pair

    brow = lax.broadcasted_iota(jnp.int32, (BUCKET_ROWS, ts), 0)
    onehot = (brow == bucket).astype(_f32)
    jj = lax.broadcasted_iota(jnp.int32, (ts, ts), 0)
    tt = lax.broadcasted_iota(jnp.int32, (ts, ts), 1)
    upper = (jj <= tt).astype(_bf16)
    cum = _dot(onehot.astype(_bf16), upper)
    carry = carry_ref[...][:, 0:1]
    rank = jnp.sum(onehot * (cum - 1.0 + carry), axis=0, keepdims=True)
    carry_ref[...] = carry_ref[...] + jnp.sum(onehot, axis=1, keepdims=True)
    return bucket.astype(_f32), rank


def _mixer_kernel(*refs, has_prev):
    if has_prev:
        x_ref, y_ref, gprev_ref = refs[:3]
        refs = refs[3:]
        x_in = lambda rows: x_ref[rows, :] + gprev_ref[...] * _unpack_bf16_pairs(y_ref[rows])
    else:
        x_ref = refs[0]
        refs = refs[1:]
        x_in = lambda rows: x_ref[rows, :]
    (mod_ref, gmix_ref, gffn_ref, win_ref, vg_ref, sw_ref, sb_ref, pw_ref, ps_ref, wa_ref, wb_ref,
     wo_ref, rw_ref, rb_ref, eg_ref, eu_ref, ed_ref,
     xo_ref, hf_ref, route_ref, counts_ref, eg16_ref, eu16_ref, ed16_ref,
     ext_ref, sum2_ref, sum4_ref, carry_ref) = refs
    b = pl.program_id(0)
    s = pl.program_id(1)
    ts = x_ref.shape[0]
    d = D_MODEL

    @pl.when((b == 0) & (s == 0))
    def _():
        carry_ref[...] = jnp.zeros_like(carry_ref)
        ext_ref[0:POOL_PAD, :] = jnp.zeros((POOL_PAD, POOL_WIDTH), _f32)
        sum2_ref[0:POOL_PAD, :] = jnp.zeros((POOL_PAD, sum2_ref.shape[1]), _f32)
        sum4_ref[0:POOL_PAD, :] = jnp.zeros((POOL_PAD, sum4_ref.shape[1]), _f32)

    @pl.when(s == 0)
    def _():
        ext_ref[POOL_PAD:POOL_PAD + HALO, :] = jnp.zeros((HALO, POOL_WIDTH), _f32)

    for w32_ref, w16_ref in ((eg_ref, eg16_ref), (eu_ref, eu16_ref), (ed_ref, ed16_ref)):
        w16_ref[...] = w32_ref[...].astype(_bf16)

    mod = mod_ref[...]
    ci = lax.broadcasted_iota(jnp.int32, (CHUNK, CHUNK), 0)
    cj = lax.broadcasted_iota(jnp.int32, (CHUNK, CHUNK), 1)
    sgu_w = [jnp.where(ci >= cj, sw_ref[h], jnp.zeros((), _bf16)) for h in range(SGU_HEADS)]
    _mixer_rows(s * ts, mod, sgu_w, x_in, gmix_ref, gffn_ref, win_ref, vg_ref, sb_ref, pw_ref, ps_ref,
                wa_ref, wb_ref, wo_ref, rw_ref, rb_ref, xo_ref, hf_ref, route_ref, ext_ref, sum2_ref,
                sum4_ref, carry_ref)
    ext_ref[POOL_PAD:POOL_PAD + HALO, :] = ext_ref[POOL_PAD + ts:POOL_PAD + ts + HALO, :]
    route_ref[2:8, :] = jnp.zeros((6, ts), _f32)
    counts_ref[...] = carry_ref[...]


def _mixer_rows(seq_pos0, mod, sgu_w, x_in, gmix_ref, gffn_ref, win_ref, vg_ref, sb_ref,
                pw_ref, ps_ref, wa_ref, wb_ref, wo_ref, rw_ref, rb_ref, xo_ref, hf_ref, route_ref,
                ext_ref, sum2_ref, sum4_ref, carry_ref):
    d = D_MODEL
    ts = SEQ_TILE
    rows = slice(0, ts)
    sh_m, sc_m, g_m = mod[:, 0:d], mod[:, d:2 * d], mod[:, 2 * d:3 * d]
    sh_f, sc_f = mod[:, 3 * d:4 * d], mod[:, 4 * d:5 * d]
    x = x_in(rows)

    hb = _rms_modulate(x, gmix_ref[...], sh_m, sc_m).astype(_bf16)

    v = jax.nn.gelu(_dot(hb, win_ref[:, SGU_WIDTH:2 * SGU_WIDTH]))
    p = _dot(hb, win_ref[:, 2 * SGU_WIDTH:2 * SGU_WIDTH + POOL_WIDTH])
    u = jax.nn.gelu(_dot(hb, win_ref[:, 0:SGU_WIDTH]))
    gate_a = jax.nn.sigmoid(_dot(hb, win_ref[:, 3 * d:4 * d]))
    gate_b = jax.nn.sigmoid(_dot(hb, win_ref[:, 4 * d:5 * d]))
    e0 = POOL_PAD + HALO
    ext_ref[e0:e0 + ts, :] = p

    vc = v - jnp.mean(v, axis=-1, keepdims=True)
    vn = (vc * lax.rsqrt(jnp.mean(vc * vc, axis=-1, keepdims=True) + EPS) * vg_ref[...]).astype(_bf16)
    n_chunks = ts // CHUNK
    ya_cols = []
    for h in range(SGU_HEADS):
        cols = slice(h * SGU_HEAD_DIM, (h + 1) * SGU_HEAD_DIM)
        rhs = jnp.concatenate([vn[n * CHUNK:(n + 1) * CHUNK, cols] for n in range(n_chunks)], axis=1)
        sg = _dot(sgu_w[h], rhs) + sb_ref[:, h:h + 1]
        s_h = jnp.concatenate([sg[:, n * SGU_HEAD_DIM:(n + 1) * SGU_HEAD_DIM] for n in range(n_chunks)],
                              axis=0)
        ya_cols.append((u[:, cols] * s_h).astype(_bf16))
    ya = jnp.concatenate(ya_cols, axis=1)
    merged = gate_a * _dot(ya, wa_ref[...])

    pos1 = (seq_pos0 + 1 + lax.broadcasted_iota(jnp.int32, (ts, 1), 0)).astype(_f32)
    g = POOL_GROUP_DIM
    lo, n = e0 - HALO, HALO + ts
    sum2 = ext_ref[lo:lo + n, :] + ext_ref[lo - 1:lo - 1 + n, :]
    sum2_ref[POOL_PAD:POOL_PAD + n, :] = sum2[:, g:]
    sum4 = sum2[:, g:] + sum2_ref[POOL_PAD - 2:POOL_PAD - 2 + n, :]
    sum4_ref[POOL_PAD:POOL_PAD + n, :] = sum4[:, g:]
    sum8 = sum4[:, g:] + sum4_ref[POOL_PAD - 4:POOL_PAD - 4 + n, :]
    sum16 = sum8[8:, g:] + sum8[:n - 8, g:]
    window_sums = (sum2[HALO:, 0:g], sum4[HALO:, 0:g], sum8[HALO:, 0:g], sum16[HALO - 8:, :])
    yb_cols = []
    for gi, w in enumerate(POOL_WINDOWS):
        cols = slice(gi * g, (gi + 1) * g)
        count = jnp.minimum(pos1, float(w))
        pooled = (window_sums[gi] / count - p[:, cols]).astype(_bf16)
        yb_cols.append(_dot(pooled, pw_ref[gi]))
    yb = (jnp.concatenate(yb_cols, axis=1) * ps_ref[...]).astype(_bf16)
    merged = merged + gate_b * _dot(yb, wb_ref[...])

    x_new = x + g_m * _dot(merged.astype(_bf16), wo_ref[...])
    xo_ref[rows, :] = x_new

    hf = _rms_modulate(x_new, gffn_ref[...], sh_f, sc_f)
    hf_ref[rows] = _pack_bf16_pairs(hf)
    bucket, rank = _route(hf, rw_ref, rb_ref, carry_ref)
    route_ref[0:1, rows] = bucket
    route_ref[1:2, rows] = rank


def _mixer(x, y_prev, gate_prev, mod_l, gmix, gffn, w_in, v_g, sgu_w, sgu_bt, pool_w, pool_scale,
           w_a, w_b, w_o, router_w_pad, router_bias, layer, expert_stacks):
    batch, seq, d = x.shape
    has_prev = y_prev is not None
    ts = SEQ_TILE
    n_tiles = batch * (seq // ts)
    tiles_per_seq = seq // ts
    const = lambda *shape: pl.BlockSpec(shape, lambda b, s: (0,) * len(shape),
                                        pipeline_mode=pl.Buffered(1))
    row_spec = pl.BlockSpec((None, ts, d), lambda b, s: (b, s, 0))
    slab_spec = pl.BlockSpec((None, ts) + PACKED_SLAB, lambda b, s: (b, s, 0, 0))
    prev_specs = [slab_spec, pl.BlockSpec((None, 1, d), lambda b, s: (b, 0, 0))] if has_prev else []
    prev_args = (y_prev, gate_prev) if has_prev else ()
    n_steps = batch * tiles_per_seq
    parts = n_steps // N_EXPERTS
    assert parts * N_EXPERTS == n_steps

    def part_index(b, s):
        step = b * tiles_per_seq + s
        return step // parts, step % parts

    def part_specs(w, with_layer):
        rows, cols = w.shape[-2] // parts, w.shape[-1]
        if with_layer:
            return pl.BlockSpec((None, None, rows, cols), lambda b, s: (layer, *part_index(b, s), 0))
        return pl.BlockSpec((None, rows, cols), lambda b, s: (*part_index(b, s), 0))

    return pl.pallas_call(
        functools.partial(_mixer_kernel, has_prev=has_prev),
        grid=(batch, tiles_per_seq),
        in_specs=[row_spec] + prev_specs + [
            pl.BlockSpec((None, 1, N_MOD * d), lambda b, s: (b, 0, 0)),
            const(1, d), const(1, d),
            const(*w_in.shape),
            const(1, SGU_WIDTH),
            const(*sgu_w.shape),
            const(*sgu_bt.shape),
            const(*pool_w.shape),
            const(1, POOL_WIDTH),
            const(d, d), const(d, d), const(d, d),
            const(*router_w_pad.shape),
            pl.BlockSpec(memory_space=pltpu.SMEM),
        ] + [part_specs(w, True) for w in expert_stacks],
        out_specs=[
            pl.BlockSpec((None, ts, d), lambda b, s: (b, s, 0)),
            pl.BlockSpec((None, ts) + PACKED_SLAB, lambda b, s: (b, s, 0, 0)),
            pl.BlockSpec((None, 8, ts), lambda b, s: (b * tiles_per_seq + s, 0, 0)),
            pl.BlockSpec((BUCKET_ROWS, 128), lambda b, s: (0, 0)),
        ] + [part_specs(w, False) for w in expert_stacks],
        out_shape=[
            jax.ShapeDtypeStruct((batch, seq, d), _f32),
            jax.ShapeDtypeStruct((batch, seq) + PACKED_SLAB, jnp.uint32),
            jax.ShapeDtypeStruct((n_tiles, 8, ts), _f32),
            jax.ShapeDtypeStruct((BUCKET_ROWS, 128), _f32),
        ] + [jax.ShapeDtypeStruct(w.shape[1:], _bf16) for w in expert_stacks],
        scratch_shapes=[
            pltpu.VMEM((POOL_PAD + HALO + ts, POOL_WIDTH), _f32),
            pltpu.VMEM((POOL_PAD + HALO + ts, POOL_WIDTH - POOL_GROUP_DIM), _f32),
            pltpu.VMEM((POOL_PAD + HALO + ts, POOL_WIDTH - 2 * POOL_GROUP_DIM), _f32),
            pltpu.VMEM((BUCKET_ROWS, 128), _f32),
        ],
        compiler_params=pltpu.CompilerParams(
            dimension_semantics=("arbitrary", "arbitrary"), vmem_limit_bytes=VMEM_LIMIT_BYTES),
        name="mixer_router",
    )(x, *prev_args, mod_l, gmix, gffn, w_in, v_g, sgu_w, sgu_bt, pool_w, pool_scale, w_a, w_b, w_o,
      router_w_pad, router_bias, *expert_stacks)


def _sc_mesh():
    return plsc.VectorSubcoreMesh(core_axis_name="c", subcore_axis_name="s")


def _sc_worker():
    return lax.axis_index("s") * SC_CORES + lax.axis_index("c")


def _sc_gather_chunks(table_hbm, idx_v, out_hbm, out_row0, n_chunks, bufs, sems):
    depth = len(bufs)

    def gather(j, slot):
        return pltpu.make_async_copy(table_hbm.at[idx_v.at[j]], bufs[slot], sems[slot])

    for j in range(depth - 1):
        @pl.when(j < n_chunks)
        def _():
            gather(j, j).start()

    @pl.loop(0, n_chunks, step=depth)
    def _(j0):
        for slot in range(depth):
            j = j0 + slot

            @pl.when(j < n_chunks)
            def _():
                @pl.when(j + depth - 1 < n_chunks)
                def _():
                    gather(j + depth - 1, (slot + depth - 1) % depth).start()

                gather(j, slot).wait()
                pltpu.sync_copy(bufs[slot], out_hbm.at[pl.ds(out_row0 + j * SC_CHUNK, SC_CHUNK)])


def _sc_chunks_per_worker(n_rows):
    n_chunks = n_rows // (SC_CORES * SC_SUBCORES * SC_CHUNK)
    assert n_chunks * SC_CORES * SC_SUBCORES * SC_CHUNK == n_rows
    return n_chunks


def _sc_gather_rows(idx, table):
    n_out = idx.shape[0]
    n_chunks = _sc_chunks_per_worker(n_out)
    slab = table.shape[1:]
    buf = pltpu.VMEM((SC_CHUNK,) + slab, table.dtype)

    @functools.partial(
        pl.kernel, mesh=_sc_mesh(),
        out_type=jax.ShapeDtypeStruct((n_out,) + slab, table.dtype),
        scratch_types=[pltpu.VMEM((n_chunks, SC_CHUNK), jnp.int32)]
        + [buf] * SC_RING + [pltpu.SemaphoreType.DMA] * SC_RING,
    )
    def gather_kernel(table_hbm, idx_hbm, out_hbm, idx_v, *ring):
        chunk0 = _sc_worker() * n_chunks
        pltpu.sync_copy(idx_hbm.at[pl.ds(chunk0, n_chunks)], idx_v)
        _sc_gather_chunks(table_hbm, idx_v, out_hbm, chunk0 * SC_CHUNK, n_chunks,
                          ring[:SC_RING], ring[SC_RING:])

    return gather_kernel(table, idx.reshape(n_out // SC_CHUNK, SC_CHUNK))


def _sc_gather_by_slot(pos, table, n_slots, n_live_slots):
    n_tok = pos.shape[0]
    per_worker = n_slots // (SC_CORES * SC_SUBCORES)
    n_chunks = _sc_chunks_per_worker(n_slots)
    slab = table.shape[1:]
    buf = pltpu.VMEM((SC_CHUNK,) + slab, table.dtype)

    @functools.partial(
        pl.kernel, mesh=_sc_mesh(),
        compiler_params=pltpu.CompilerParams(needs_layout_passes=False),
        out_type=jax.ShapeDtypeStruct((n_slots,) + slab, table.dtype),
        scratch_types=[pltpu.VMEM((n_tok,), jnp.int32), pltpu.VMEM((n_chunks, SC_CHUNK), jnp.int32),
                       pltpu.VMEM((SC_LANES,), jnp.int32)]
        + [buf] * SC_RING + [pltpu.SemaphoreType.DMA] * SC_RING,
    )
    def slot_kernel(table_hbm, pos_hbm, live_hbm, out_hbm, pos_v, inv_v, live_v, *ring):
        slot0 = _sc_worker() * per_worker
        pltpu.sync_copy(pos_hbm, pos_v)
        pltpu.sync_copy(live_hbm, live_v)
        n_live_chunks = jnp.clip((jnp.max(live_v[...]) - slot0) // SC_CHUNK, 0, n_chunks)
        lane = lax.iota(jnp.int32, SC_LANES)

        @plsc.parallel_loop(0, n_chunks)
        def _(j):
            for h in range(SC_CHUNK // SC_LANES):
                s = slot0 + j * SC_CHUNK + h * SC_LANES + lane
                inv_v[j, pl.ds(h * SC_LANES, SC_LANES)] = jnp.where(s >= n_tok, s - n_tok, s)

        @plsc.parallel_loop(0, n_tok // SC_LANES, unroll=4)
        def _(i):
            local = pos_v[pl.ds(i * SC_LANES, SC_LANES)] - slot0
            mine = (local >= 0) & (local < per_worker)
            local = jnp.where(mine, local, 0)
            plsc.store_scatter(inv_v, [local // SC_CHUNK, local % SC_CHUNK], i * SC_LANES + lane, mask=mine)

        _sc_gather_chunks(table_hbm, inv_v, out_hbm, slot0, n_live_chunks, ring[:SC_RING], ring[SC_RING:])

    return slot_kernel(table, pos, jnp.broadcast_to(n_live_slots, (SC_LANES,)))


def _final_kernel(x_ref, y_ref, g_ref, fg_ref, o_ref):
    out = x_ref[...] + g_ref[...] * _unpack_bf16_pairs(y_ref[...])
    o_ref[...] = out * lax.rsqrt(jnp.mean(out * out, axis=-1, keepdims=True) + EPS) * fg_ref[...]


def _final_norm(x, y_tok, gate_f, final_g):
    batch, seq, d = x.shape
    row_spec = pl.BlockSpec((None, ROW_TILE, d), lambda b, s: (b, s, 0))
    return pl.pallas_call(
        _final_kernel,
        grid=(batch, seq // ROW_TILE),
        in_specs=[row_spec, pl.BlockSpec((None, ROW_TILE) + PACKED_SLAB, lambda b, s: (b, s, 0, 0)),
                  pl.BlockSpec((None, 1, d), lambda b, s: (b, 0, 0)),
                  pl.BlockSpec((1, d), lambda b, s: (0, 0))],
        out_specs=row_spec,
        out_shape=jax.ShapeDtypeStruct((batch, seq, d), _f32),
        compiler_params=pltpu.CompilerParams(dimension_semantics=("arbitrary", "arbitrary")),
        name="residual_final_norm",
    )(x, y_tok, gate_f, final_g)


def _experts_kernel(ea_ref, eb_ref, nused_ref, xs_ref, rwa_ref, rwb_ref,
                    wga_ref, wgb_ref, wua_ref, wub_ref, wda_ref, wdb_ref, *rest):
    n_cast = (len(rest) - 1) // 2
    cast_in, ys_ref, cast_out = rest[:n_cast], rest[n_cast], rest[n_cast + 1:]
    i = pl.program_id(0)

    def cast_share():
        for w32_ref, w16_ref in zip(cast_in, cast_out):
            w16_ref[...] = w32_ref[...].astype(_bf16)

    @pl.when(i < nused_ref[0])
    def _():
        cast_share()
        x = _unpack_bf16_pairs(xs_ref[...])
        la = jnp.sum(x * rwa_ref[...], axis=-1, keepdims=True)
        lb = jnp.sum(x * rwb_ref[...], axis=-1, keepdims=True)
        wa = jax.nn.sigmoid(la - lb)
        wb = jax.nn.sigmoid(lb - la)
        xb = x.astype(_bf16)
        act_a = (jax.nn.silu(_dot(xb, wga_ref[...])) * _dot(xb, wua_ref[...]) * wa).astype(_bf16)
        act_b = (jax.nn.silu(_dot(xb, wgb_ref[...])) * _dot(xb, wub_ref[...]) * wb).astype(_bf16)
        y = _dot(act_a, wda_ref[...]) + _dot(act_b, wdb_ref[...])
        ys_ref[...] = _pack_bf16_pairs(y)

    @pl.when(i >= nused_ref[0])
    def _():
        cast_share()
        ys_ref[...] = jnp.zeros_like(ys_ref)


def _experts(tile_ea, tile_eb, n_used, xs, router_wt, w_gate, w_up, w_down, cast_stacks=(), cast_layer=0):
    n_slots = xs.shape[0]
    d = D_MODEL
    n_tiles = n_slots // EXPERT_TILE
    f = D_EXPERT
    n_cast_steps = n_tiles - N_BUCKETS

    def share(i):
        return jnp.minimum(i, n_cast_steps - 1)

    def cast_rows(w):
        rows = w.shape[1] // n_cast_steps
        assert rows * n_cast_steps == w.shape[1] and rows % 16 == 0
        return rows

    cast_in_specs = [pl.BlockSpec((None, cast_rows(w), w.shape[2]),
                                  lambda i, ea, eb, nu: (cast_layer, share(i), 0)) for w in cast_stacks]
    cast_out_specs = [pl.BlockSpec((cast_rows(w), w.shape[2]), lambda i, ea, eb, nu: (share(i), 0))
                      for w in cast_stacks]
    cast_out_shapes = [jax.ShapeDtypeStruct(w.shape[1:], _bf16) for w in cast_stacks]

    def row(i, ea, eb, nu):
        return (jnp.maximum(jnp.minimum(i, nu[0] - 1), 0), 0, 0)

    grid_spec = pltpu.PrefetchScalarGridSpec(
        num_scalar_prefetch=3,
        grid=(n_tiles,),
        in_specs=[
            pl.BlockSpec((EXPERT_TILE,) + PACKED_SLAB, row),
            pl.BlockSpec((None, 1, d), lambda i, ea, eb, nu: (ea[i], 0, 0)),
            pl.BlockSpec((None, 1, d), lambda i, ea, eb, nu: (eb[i], 0, 0)),
            pl.BlockSpec((None, d, f), lambda i, ea, eb, nu: (ea[i], 0, 0)),
            pl.BlockSpec((None, d, f), lambda i, ea, eb, nu: (eb[i], 0, 0)),
            pl.BlockSpec((None, d, f), lambda i, ea, eb, nu: (ea[i], 0, 0)),
            pl.BlockSpec((None, d, f), lambda i, ea, eb, nu: (eb[i], 0, 0)),
            pl.BlockSpec((None, f, d), lambda i, ea, eb, nu: (ea[i], 0, 0)),
            pl.BlockSpec((None, f, d), lambda i, ea, eb, nu: (eb[i], 0, 0)),
        ] + cast_in_specs,
        out_specs=[pl.BlockSpec((EXPERT_TILE,) + PACKED_SLAB, lambda i, ea, eb, nu: (i, 0, 0))]
        + cast_out_specs,
    )
    return pl.pallas_call(
        _experts_kernel,
        grid_spec=grid_spec,
        out_shape=[jax.ShapeDtypeStruct((n_slots,) + PACKED_SLAB, jnp.uint32)] + cast_out_shapes,
        compiler_params=pltpu.CompilerParams(
            dimension_semantics=("arbitrary",), vmem_limit_bytes=VMEM_LIMIT_BYTES),
        name="grouped_experts",
    )(tile_ea, tile_eb, n_used, xs, router_wt, router_wt, w_gate, w_gate, w_up, w_up, w_down, w_down,
      *cast_stacks)


def _routing_tables(route, counts, n_tiles_max):
    bucket = route[:, 0, :].reshape(-1).astype(jnp.int32)
    rank = route[:, 1, :].reshape(-1).astype(jnp.int32)
    cnt = counts[:N_BUCKETS, 0].astype(jnp.int32)
    tiles_b = (cnt + EXPERT_TILE - 1) // EXPERT_TILE
    tile_end = jnp.cumsum(tiles_b)
    tile_start = tile_end - tiles_b
    n_used = tile_end[-1]
    onehot = bucket[:, None] == jnp.arange(N_BUCKETS, dtype=jnp.int32)[None, :]
    pos = jnp.sum(jnp.where(onehot, (tile_start * EXPERT_TILE)[None, :], 0), axis=1) + rank
    tile_ids = jnp.minimum(jnp.arange(n_tiles_max, dtype=jnp.int32), n_used - 1)
    tile_bucket = jnp.sum((tile_ids[:, None] >= tile_end[None, :]).astype(jnp.int32), axis=1)
    group = tile_bucket // PAIRS_PER_GROUP
    pair = tile_bucket % PAIRS_PER_GROUP
    tile_ea = group * EXPERTS_PER_GROUP + jnp.asarray(_PAIR_A, jnp.int32)[pair]
    tile_eb = group * EXPERTS_PER_GROUP + jnp.asarray(_PAIR_B, jnp.int32)[pair]
    return pos, tile_ea, tile_eb, n_used.reshape(1)


def kernel(x, c, w_ada, b_ada, norm_mix_g, w_in, v_norm_g, sgu_w, sgu_b, pool_w, pool_scale,
           w_branch_a, w_branch_b, w_out, norm_ffn_g, router_w, router_bias,
           w_exp_gate, w_exp_up, w_exp_down, final_norm_g):
    batch, seq, d = x.shape
    depth = w_ada.shape[0]
    t = batch * seq
    n_tiles_max = t // EXPERT_TILE + N_BUCKETS
    n_slots = n_tiles_max * EXPERT_TILE

    mod = _ada(c, w_ada, b_ada)
    rw_pad = jnp.pad(router_w, ((0, 0), (0, 128 - N_EXPERTS)))
    rw_hi = rw_pad.astype(_bf16)
    rw_lo = (rw_pad - rw_hi.astype(_f32)).astype(_bf16)
    router_w_pad = jnp.concatenate([rw_hi, rw_lo], axis=1)
    router_wt = router_w.T.reshape(N_EXPERTS, 1, d)
    final_g = final_norm_g.reshape(1, d)

    def layer_bf16(stacks, l):
        return tuple(w[l].astype(_bf16) for w in stacks)

    mixer_stacks = (w_in, sgu_w.reshape(depth, d, SGU_HEAD_DIM), pool_w.reshape(depth, d, POOL_GROUP_DIM),
                    w_branch_a, w_branch_b, w_out)
    expert_stacks = (w_exp_gate, w_exp_up, w_exp_down)
    y_tok, gate_prev = None, None
    mixer_w = layer_bf16(mixer_stacks, 0)
    for l in range(depth):
        mod_l = mod[l].reshape(batch, 1, N_MOD * d)
        w_in_l, sgu_w_l, pool_w_l, w_a_l, w_b_l, w_o_l = mixer_w
        sgu_w_l = sgu_w_l.reshape(SGU_HEADS, CHUNK, CHUNK)
        pool_w_l = pool_w_l.reshape(len(POOL_WINDOWS), POOL_GROUP_DIM, POOL_GROUP_DIM)
        x, hf, route, counts, *expert_w = _mixer(
            x, y_tok, gate_prev, mod_l, norm_mix_g[l].reshape(1, d), norm_ffn_g[l].reshape(1, d),
            w_in_l, v_norm_g[l].reshape(1, SGU_WIDTH), sgu_w_l, sgu_b[l].T,
            pool_w_l, pool_scale[l].reshape(1, POOL_WIDTH), w_a_l, w_b_l, w_o_l,
            router_w_pad, router_bias, l, expert_stacks)
        pos, tile_ea, tile_eb, n_used = _routing_tables(route, counts, n_tiles_max)
        xs = _sc_gather_by_slot(pos, hf.reshape((t,) + PACKED_SLAB), n_slots, n_used * EXPERT_TILE)
        ys, *mixer_w = _experts(tile_ea, tile_eb, n_used, xs, router_wt, *expert_w,
                                cast_stacks=mixer_stacks if l + 1 < depth else (), cast_layer=l + 1)
        y_tok = _sc_gather_rows(pos, ys).reshape((batch, seq) + PACKED_SLAB)
        gate_prev = mod_l[:, :, 5 * d:6 * d]
    return _final_norm(x, y_tok, gate_prev, final_g)
```

```python
import functools

import jax
import jax.numpy as jnp
from jax import lax
from jax.experimental import pallas as pl
from jax.experimental.pallas import tpu as pltpu
from jax.experimental.pallas import tpu_sc as plsc

D_MODEL = 1024
CHUNK = 128
SGU_HEADS = 8
SGU_HEAD_DIM = 128
SGU_WIDTH = 1024
POOL_WINDOWS = (2, 4, 8, 16)
POOL_GROUP_DIM = 256
POOL_WIDTH = 1024
HALO = 16
POOL_PAD = 8
N_EXPERTS = 16
N_EXPERT_GROUPS = 4
EXPERTS_PER_GROUP = 4
PAIRS_PER_GROUP = 6
N_BUCKETS = N_EXPERT_GROUPS * PAIRS_PER_GROUP
BUCKET_ROWS = 32
D_EXPERT = 512
N_MOD = 6
EPS = 1e-6

SEQ_TILE = 512
ROW_TILE = 512
EXPERT_TILE = 256
VMEM_LIMIT_BYTES = 58 * 1024 * 1024
PACKED_SLAB = (4, 128)
SC_CORES = 2
SC_SUBCORES = 16
SC_CHUNK = 64
SC_LANES = 16
SC_RING = 3

_PAIR_A = (0, 0, 0, 1, 1, 3)
_PAIR_B = (1, 2, 3, 3, 2, 2)

_bf16 = jnp.bfloat16
_f32 = jnp.float32


def _dot(a, b):
    return jnp.dot(a, b, preferred_element_type=_f32)


def _pack_bf16_pairs(x):
    rows, d = x.shape
    hi = lax.bitcast_convert_type(x[:, :d // 2].astype(_bf16).astype(_f32), jnp.uint32)
    lo = lax.bitcast_convert_type(x[:, d // 2:].astype(_bf16).astype(_f32), jnp.uint32)
    return (hi | (lo >> 16)).reshape((rows,) + PACKED_SLAB)


def _unpack_bf16_pairs(words):
    rows = words.shape[0]
    words = words.reshape(rows, PACKED_SLAB[0] * PACKED_SLAB[1])
    hi = lax.bitcast_convert_type(words & jnp.uint32(0xFFFF0000), _f32)
    lo = lax.bitcast_convert_type(words << 16, _f32)
    return jnp.concatenate([hi, lo], axis=1)


def _rms_modulate(x, g, shift, scale):
    y = x * lax.rsqrt(jnp.mean(x * x, axis=-1, keepdims=True) + EPS)
    return (y * g) * (1.0 + scale) + shift


def _ada_kernel(c_ref, w_ref, b_ref, o_ref):
    c = c_ref[...]
    c_act = (c * jax.nn.sigmoid(c)).astype(_bf16)
    o_ref[...] = _dot(c_act, w_ref[...].astype(_bf16)) + b_ref[...]


def _ada(c, w_ada, b_ada):
    depth, d, n = w_ada.shape
    batch = c.shape[0]
    tn = 2048
    return pl.pallas_call(
        _ada_kernel,
        grid=(depth, n // tn),
        in_specs=[
            pl.BlockSpec((batch, d), lambda l, j: (0, 0)),
            pl.BlockSpec((None, d, tn), lambda l, j: (l, 0, j)),
            pl.BlockSpec((None, 1, tn), lambda l, j: (l, 0, j)),
        ],
        out_specs=pl.BlockSpec((None, batch, tn), lambda l, j: (l, 0, j)),
        out_shape=jax.ShapeDtypeStruct((depth, batch, n), _f32),
        compiler_params=pltpu.CompilerParams(
            dimension_semantics=("arbitrary", "arbitrary"), vmem_limit_bytes=VMEM_LIMIT_BYTES),
        name="ada_modulation",
    )(c, w_ada, b_ada.reshape(depth, 1, n))


def _route(hf, rw_ref, rb_ref, carry_ref):
    ts = hf.shape[0]
    hf_hi = hf.astype(_bf16)
    hf_lo = (hf - hf_hi.astype(_f32)).astype(_bf16)
    both = _dot(hf_hi, rw_ref[...])
    logits = both[:, 0:128] + both[:, 128:256] + _dot(hf_lo, rw_ref[:, 0:128])
    lt = logits.T
    rows = [lt[e:e + 1, :] for e in range(N_EXPERTS)]
    m = functools.reduce(jnp.maximum, rows)
    ex = [jnp.exp(r - m) for r in rows]
    den = functools.reduce(lambda a, b: a + b, ex)
    probs = [e / den for e in ex]
    sel = [probs[e] + rb_ref[e] for e in range(N_EXPERTS)]

    def top2_sum(v):
        pairs = [v[i] + v[j] for i, j in zip(_PAIR_A, _PAIR_B)]
        return functools.reduce(jnp.maximum, pairs)

    gscore = [top2_sum(sel[4 * g:4 * g + 4]) for g in range(N_EXPERT_GROUPS)]
    best = gscore[0]
    gidx = jnp.zeros_like(best, dtype=jnp.int32)
    for g in range(1, N_EXPERT_GROUPS):
        better = gscore[g] > best
        best = jnp.where(better, gscore[g], best)
        gidx = jnp.where(better, g, gidx)
    ing = []
    for k in range(EXPERTS_PER_GROUP):
        v = sel[k]
        for g in range(1, N_EXPERT_GROUPS):
            v = jnp.where(gidx == g, sel[4 * g + k], v)
        ing.append(v)
    chosen = []
    for k in range(EXPERTS_PER_GROUP):
        r = jnp.zeros_like(gidx)
        for j in range(EXPERTS_PER_GROUP):
            if j == k:
                continue
            beats = (ing[j] >= ing[k]) if j < k else (ing[j] > ing[k])
            r = r + beats.astype(jnp.int32)
        chosen.append(r < 2)
    lo = jnp.where(chosen[0], 0, jnp.where(chosen[1], 1, 2))
    hi = jnp.where(chosen[3], 3, jnp.where(chosen[2], 2, 1))
    pair = jnp.where(lo == 0, hi - 1, jnp.where(lo == 2, 5, jnp.where(hi == 3, 3, 4)))
    bucket = gidx * PAIRS_PER_GROUP + pair

    brow = lax.broadcasted_iota(jnp.int32, (BUCKET_ROWS, ts), 0)
    onehot = (brow == bucket).astype(_f32)
    jj = lax.broadcasted_iota(jnp.int32, (ts, ts), 0)
    tt = lax.broadcasted_iota(jnp.int32, (ts, ts), 1)
    upper = (jj <= tt).astype(_bf16)
    cum = _dot(onehot.astype(_bf16), upper)
    carry = carry_ref[...][:, 0:1]
    rank = jnp.sum(onehot * (cum - 1.0 + carry), axis=0, keepdims=True)
    carry_ref[...] = carry_ref[...] + jnp.sum(onehot, axis=1, keepdims=True)
    return bucket.astype(_f32), rank


def _mixer_kernel(*refs, has_prev):
    if has_prev:
        x_ref, y_ref, gprev_ref = refs[:3]
        refs = refs[3:]
        x_in = lambda rows: x_ref[rows, :] + gprev_ref[...] * _unpack_bf16_pairs(y_ref[rows])
    else:
        x_ref = refs[0]
        refs = refs[1:]
        x_in = lambda rows: x_ref[rows, :]
    (mod_ref, gmix_ref, gffn_ref, win_ref, vg_ref, sw_ref, sb_ref, pw_ref, ps_ref, wa_ref, wb_ref,
     wo_ref, rw_ref, rb_ref, eg_ref, eu_ref, ed_ref,
     xo_ref, hf_ref, route_ref, counts_ref, eg16_ref, eu16_ref, ed16_ref,
     ext_ref, sum2_ref, sum4_ref, carry_ref) = refs
    b = pl.program_id(0)
    s = pl.program_id(1)
    ts = x_ref.shape[0]
    d = D_MODEL

    @pl.when((b == 0) & (s == 0))
    def _():
        carry_ref[...] = jnp.zeros_like(carry_ref)
        ext_ref[0:POOL_PAD, :] = jnp.zeros((POOL_PAD, POOL_WIDTH), _f32)
        sum2_ref[0:POOL_PAD, :] = jnp.zeros((POOL_PAD, sum2_ref.shape[1]), _f32)
        sum4_ref[0:POOL_PAD, :] = jnp.zeros((POOL_PAD, sum4_ref.shape[1]), _f32)

    @pl.when(s == 0)
    def _():
        ext_ref[POOL_PAD:POOL_PAD + HALO, :] = jnp.zeros((HALO, POOL_WIDTH), _f32)

    for w32_ref, w16_ref in ((eg_ref, eg16_ref), (eu_ref, eu16_ref), (ed_ref, ed16_ref)):
        w16_ref[...] = w32_ref[...].astype(_bf16)

    mod = mod_ref[...]
    ci = lax.broadcasted_iota(jnp.int32, (CHUNK, CHUNK), 0)
    cj = lax.broadcasted_iota(jnp.int32, (CHUNK, CHUNK), 1)
    sgu_w = [jnp.where(ci >= cj, sw_ref[h], jnp.zeros((), _bf16)) for h in range(SGU_HEADS)]
    _mixer_rows(s * ts, mod, sgu_w, x_in, gmix_ref, gffn_ref, win_ref, vg_ref, sb_ref, pw_ref, ps_ref,
                wa_ref, wb_ref, wo_ref, rw_ref, rb_ref, xo_ref, hf_ref, route_ref, ext_ref, sum2_ref,
                sum4_ref, carry_ref)
    ext_ref[POOL_PAD:POOL_PAD + HALO, :] = ext_ref[POOL_PAD + ts:POOL_PAD + ts + HALO, :]
    route_ref[2:8, :] = jnp.zeros((6, ts), _f32)
    counts_ref[...] = carry_ref[...]


def _mixer_rows(seq_pos0, mod, sgu_w, x_in, gmix_ref, gffn_ref, win_ref, vg_ref, sb_ref,
                pw_ref, ps_ref, wa_ref, wb_ref, wo_ref, rw_ref, rb_ref, xo_ref, hf_ref, route_ref,
                ext_ref, sum2_ref, sum4_ref, carry_ref):
    d = D_MODEL
    ts = SEQ_TILE
    rows = slice(0, ts)
    sh_m, sc_m, g_m = mod[:, 0:d], mod[:, d:2 * d], mod[:, 2 * d:3 * d]
    sh_f, sc_f = mod[:, 3 * d:4 * d], mod[:, 4 * d:5 * d]
    x = x_in(rows)

    hb = _rms_modulate(x, gmix_ref[...], sh_m, sc_m).astype(_bf16)

    v = jax.nn.gelu(_dot(hb, win_ref[:, SGU_WIDTH:2 * SGU_WIDTH]))
    p = _dot(hb, win_ref[:, 2 * SGU_WIDTH:2 * SGU_WIDTH + POOL_WIDTH])
    u = jax.nn.gelu(_dot(hb, win_ref[:, 0:SGU_WIDTH]))
    gate_a = jax.nn.sigmoid(_dot(hb, win_ref[:, 3 * d:4 * d]))
    gate_b = jax.nn.sigmoid(_dot(hb, win_ref[:, 4 * d:5 * d]))
    e0 = POOL_PAD + HALO
    ext_ref[e0:e0 + ts, :] = p

    vc = v - jnp.mean(v, axis=-1, keepdims=True)
    vn = (vc * lax.rsqrt(jnp.mean(vc * vc, axis=-1, keepdims=True) + EPS) * vg_ref[...]).astype(_bf16)
    n_chunks = ts // CHUNK
    ya_cols = []
    for h in range(SGU_HEADS):
        cols = slice(h * SGU_HEAD_DIM, (h + 1) * SGU_HEAD_DIM)
        rhs = jnp.concatenate([vn[n * CHUNK:(n + 1) * CHUNK, cols] for n in range(n_chunks)], axis=1)
        sg = _dot(sgu_w[h], rhs) + sb_ref[:, h:h + 1]
        s_h = jnp.concatenate([sg[:, n * SGU_HEAD_DIM:(n + 1) * SGU_HEAD_DIM] for n in range(n_chunks)],
                              axis=0)
        ya_cols.append((u[:, cols] * s_h).astype(_bf16))
    ya = jnp.concatenate(ya_cols, axis=1)
    merged = gate_a * _dot(ya, wa_ref[...])

    pos1 = (seq_pos0 + 1 + lax.broadcasted_iota(jnp.int32, (ts, 1), 0)).astype(_f32)
    g = POOL_GROUP_DIM
    lo, n = e0 - HALO, HALO + ts
    sum2 = ext_ref[lo:lo + n, :] + ext_ref[lo - 1:lo - 1 + n, :]
    sum2_ref[POOL_PAD:POOL_PAD + n, :] = sum2[:, g:]
    sum4 = sum2[:, g:] + sum2_ref[POOL_PAD - 2:POOL_PAD - 2 + n, :]
    sum4_ref[POOL_PAD:POOL_PAD + n, :] = sum4[:, g:]
    sum8 = sum4[:, g:] + sum4_ref[POOL_PAD - 4:POOL_PAD - 4 + n, :]
    sum16 = sum8[8:, g:] + sum8[:n - 8, g:]
    window_sums = (sum2[HALO:, 0:g], sum4[HALO:, 0:g], sum8[HALO:, 0:g], sum16[HALO - 8:, :])
    yb_cols = []
    for gi, w in enumerate(POOL_WINDOWS):
        cols = slice(gi * g, (gi + 1) * g)
        count = jnp.minimum(pos1, float(w))
        pooled = (window_sums[gi] / count - p[:, cols]).astype(_bf16)
        yb_cols.append(_dot(pooled, pw_ref[gi]))
    yb = (jnp.concatenate(yb_cols, axis=1) * ps_ref[...]).astype(_bf16)
    merged = merged + gate_b * _dot(yb, wb_ref[...])

    x_new = x + g_m * _dot(merged.astype(_bf16), wo_ref[...])
    xo_ref[rows, :] = x_new

    hf = _rms_modulate(x_new, gffn_ref[...], sh_f, sc_f)
    hf_ref[rows] = _pack_bf16_pairs(hf)
    bucket, rank = _route(hf, rw_ref, rb_ref, carry_ref)
    route_ref[0:1, rows] = bucket
    route_ref[1:2, rows] = rank


def _mixer(x, y_prev, gate_prev, mod_l, gmix, gffn, w_in, v_g, sgu_w, sgu_bt, pool_w, pool_scale,
           w_a, w_b, w_o, router_w_pad, router_bias, layer, expert_stacks):
    batch, seq, d = x.shape
    has_prev = y_prev is not None
    ts = SEQ_TILE
    n_tiles = batch * (seq // ts)
    tiles_per_seq = seq // ts
    const = lambda *shape: pl.BlockSpec(shape, lambda b, s: (0,) * len(shape),
                                        pipeline_mode=pl.Buffered(1))
    row_spec = pl.BlockSpec((None, ts, d), lambda b, s: (b, s, 0))
    slab_spec = pl.BlockSpec((None, ts) + PACKED_SLAB, lambda b, s: (b, s, 0, 0))
    prev_specs = [slab_spec, pl.BlockSpec((None, 1, d), lambda b, s: (b, 0, 0))] if has_prev else []
    prev_args = (y_prev, gate_prev) if has_prev else ()
    n_steps = batch * tiles_per_seq
    parts = n_steps // N_EXPERTS
    assert parts * N_EXPERTS == n_steps

    def part_index(b, s):
        step = b * tiles_per_seq + s
        return step // parts, step % parts

    def part_specs(w, with_layer):
        rows, cols = w.shape[-2] // parts, w.shape[-1]
        if with_layer:
            return pl.BlockSpec((None, None, rows, cols), lambda b, s: (layer, *part_index(b, s), 0))
        return pl.BlockSpec((None, rows, cols), lambda b, s: (*part_index(b, s), 0))

    return pl.pallas_call(
        functools.partial(_mixer_kernel, has_prev=has_prev),
        grid=(batch, tiles_per_seq),
        in_specs=[row_spec] + prev_specs + [
            pl.BlockSpec((None, 1, N_MOD * d), lambda b, s: (b, 0, 0)),
            const(1, d), const(1, d),
            const(*w_in.shape),
            const(1, SGU_WIDTH),
            const(*sgu_w.shape),
            const(*sgu_bt.shape),
            const(*pool_w.shape),
            const(1, POOL_WIDTH),
            const(d, d), const(d, d), const(d, d),
            const(*router_w_pad.shape),
            pl.BlockSpec(memory_space=pltpu.SMEM),
        ] + [part_specs(w, True) for w in expert_stacks],
        out_specs=[
            pl.BlockSpec((None, ts, d), lambda b, s: (b, s, 0)),
            pl.BlockSpec((None, ts) + PACKED_SLAB, lambda b, s: (b, s, 0, 0)),
            pl.BlockSpec((None, 8, ts), lambda b, s: (b * tiles_per_seq + s, 0, 0)),
            pl.BlockSpec((BUCKET_ROWS, 128), lambda b, s: (0, 0)),
        ] + [part_specs(w, False) for w in expert_stacks],
        out_shape=[
            jax.ShapeDtypeStruct((batch, seq, d), _f32),
            jax.ShapeDtypeStruct((batch, seq) + PACKED_SLAB, jnp.uint32),
            jax.ShapeDtypeStruct((n_tiles, 8, ts), _f32),
            jax.ShapeDtypeStruct((BUCKET_ROWS, 128), _f32),
        ] + [jax.ShapeDtypeStruct(w.shape[1:], _bf16) for w in expert_stacks],
        scratch_shapes=[
            pltpu.VMEM((POOL_PAD + HALO + ts, POOL_WIDTH), _f32),
            pltpu.VMEM((POOL_PAD + HALO + ts, POOL_WIDTH - POOL_GROUP_DIM), _f32),
            pltpu.VMEM((POOL_PAD + HALO + ts, POOL_WIDTH - 2 * POOL_GROUP_DIM), _f32),
            pltpu.VMEM((BUCKET_ROWS, 128), _f32),
        ],
        compiler_params=pltpu.CompilerParams(
            dimension_semantics=("arbitrary", "arbitrary"), vmem_limit_bytes=VMEM_LIMIT_BYTES),
        name="mixer_router",
    )(x, *prev_args, mod_l, gmix, gffn, w_in, v_g, sgu_w, sgu_bt, pool_w, pool_scale, w_a, w_b, w_o,
      router_w_pad, router_bias, *expert_stacks)


def _sc_mesh():
    return plsc.VectorSubcoreMesh(core_axis_name="c", subcore_axis_name="s")


def _sc_worker():
    return lax.axis_index("s") * SC_CORES + lax.axis_index("c")


def _sc_gather_chunks(table_hbm, idx_v, out_hbm, out_row0, n_chunks, bufs, sems):
    depth = len(bufs)

    def gather(j, slot):
        return pltpu.make_async_copy(table_hbm.at[idx_v.at[j]], bufs[slot], sems[slot])

    for j in range(depth - 1):
        @pl.when(j < n_chunks)
        def _():
            gather(j, j).start()

    @pl.loop(0, n_chunks, step=depth)
    def _(j0):
        for slot in range(depth):
            j = j0 + slot

            @pl.when(j < n_chunks)
            def _():
                @pl.when(j + depth - 1 < n_chunks)
                def _():
                    gather(j + depth - 1, (slot + depth - 1) % depth).start()

                gather(j, slot).wait()
                pltpu.sync_copy(bufs[slot], out_hbm.at[pl.ds(out_row0 + j * SC_CHUNK, SC_CHUNK)])


def _sc_chunks_per_worker(n_rows):
    n_chunks = n_rows // (SC_CORES * SC_SUBCORES * SC_CHUNK)
    assert n_chunks * SC_CORES * SC_SUBCORES * SC_CHUNK == n_rows
    return n_chunks


def _sc_gather_rows(idx, table):
    n_out = idx.shape[0]
    n_chunks = _sc_chunks_per_worker(n_out)
    slab = table.shape[1:]
    buf = pltpu.VMEM((SC_CHUNK,) + slab, table.dtype)

    @functools.partial(
        pl.kernel, mesh=_sc_mesh(),
        out_type=jax.ShapeDtypeStruct((n_out,) + slab, table.dtype),
        scratch_types=[pltpu.VMEM((n_chunks, SC_CHUNK), jnp.int32)]
        + [buf] * SC_RING + [pltpu.SemaphoreType.DMA] * SC_RING,
    )
    def gather_kernel(table_hbm, idx_hbm, out_hbm, idx_v, *ring):
        chunk0 = _sc_worker() * n_chunks
        pltpu.sync_copy(idx_hbm.at[pl.ds(chunk0, n_chunks)], idx_v)
        _sc_gather_chunks(table_hbm, idx_v, out_hbm, chunk0 * SC_CHUNK, n_chunks,
                          ring[:SC_RING], ring[SC_RING:])

    return gather_kernel(table, idx.reshape(n_out // SC_CHUNK, SC_CHUNK))


def _sc_gather_by_slot(pos, table, n_slots, n_live_slots):
    n_tok = pos.shape[0]
    per_worker = n_slots // (SC_CORES * SC_SUBCORES)
    n_chunks = _sc_chunks_per_worker(n_slots)
    slab = table.shape[1:]
    buf = pltpu.VMEM((SC_CHUNK,) + slab, table.dtype)

    @functools.partial(
        pl.kernel, mesh=_sc_mesh(),
        compiler_params=pltpu.CompilerParams(needs_layout_passes=False),
        out_type=jax.ShapeDtypeStruct((n_slots,) + slab, table.dtype),
        scratch_types=[pltpu.VMEM((n_tok,), jnp.int32), pltpu.VMEM((n_chunks, SC_CHUNK), jnp.int32),
                       pltpu.VMEM((SC_LANES,), jnp.int32)]
        + [buf] * SC_RING + [pltpu.SemaphoreType.DMA] * SC_RING,
    )
    def slot_kernel(table_hbm, pos_hbm, live_hbm, out_hbm, pos_v, inv_v, live_v, *ring):
        slot0 = _sc_worker() * per_worker
        pltpu.sync_copy(pos_hbm, pos_v)
        pltpu.sync_copy(live_hbm, live_v)
        n_live_chunks = jnp.clip((jnp.max(live_v[...]) - slot0) // SC_CHUNK, 0, n_chunks)
        lane = lax.iota(jnp.int32, SC_LANES)

        @plsc.parallel_loop(0, n_chunks)
        def _(j):
            for h in range(SC_CHUNK // SC_LANES):
                s = slot0 + j * SC_CHUNK + h * SC_LANES + lane
                inv_v[j, pl.ds(h * SC_LANES, SC_LANES)] = jnp.where(s >= n_tok, s - n_tok, s)

        @plsc.parallel_loop(0, n_tok // SC_LANES, unroll=4)
        def _(i):
            local = pos_v[pl.ds(i * SC_LANES, SC_LANES)] - slot0
            mine = (local >= 0) & (local < per_worker)
            local = jnp.where(mine, local, 0)
            plsc.store_scatter(inv_v, [local // SC_CHUNK, local % SC_CHUNK], i * SC_LANES + lane, mask=mine)

        _sc_gather_chunks(table_hbm, inv_v, out_hbm, slot0, n_live_chunks, ring[:SC_RING], ring[SC_RING:])

    return slot_kernel(table, pos, jnp.broadcast_to(n_live_slots, (SC_LANES,)))


def _final_kernel(x_ref, y_ref, g_ref, fg_ref, o_ref):
    out = x_ref[...] + g_ref[...] * _unpack_bf16_pairs(y_ref[...])
    o_ref[...] = out * lax.rsqrt(jnp.mean(out * out, axis=-1, keepdims=True) + EPS) * fg_ref[...]


def _final_norm(x, y_tok, gate_f, final_g):
    batch, seq, d = x.shape
    row_spec = pl.BlockSpec((None, ROW_TILE, d), lambda b, s: (b, s, 0))
    return pl.pallas_call(
        _final_kernel,
        grid=(batch, seq // ROW_TILE),
        in_specs=[row_spec, pl.BlockSpec((None, ROW_TILE) + PACKED_SLAB, lambda b, s: (b, s, 0, 0)),
                  pl.BlockSpec((None, 1, d), lambda b, s: (b, 0, 0)),
                  pl.BlockSpec((1, d), lambda b, s: (0, 0))],
        out_specs=row_spec,
        out_shape=jax.ShapeDtypeStruct((batch, seq, d), _f32),
        compiler_params=pltpu.CompilerParams(dimension_semantics=("arbitrary", "arbitrary")),
        name="residual_final_norm",
    )(x, y_tok, gate_f, final_g)


def _experts_kernel(ea_ref, eb_ref, nused_ref, xs_ref, rwa_ref, rwb_ref,
                    wga_ref, wgb_ref, wua_ref, wub_ref, wda_ref, wdb_ref, *rest):
    n_cast = (len(rest) - 1) // 2
    cast_in, ys_ref, cast_out = rest[:n_cast], rest[n_cast], rest[n_cast + 1:]
    i = pl.program_id(0)

    def cast_share():
        for w32_ref, w16_ref in zip(cast_in, cast_out):
            w16_ref[...] = w32_ref[...].astype(_bf16)

    @pl.when(i < nused_ref[0])
    def _():
        cast_share()
        x = _unpack_bf16_pairs(xs_ref[...])
        la = jnp.sum(x * rwa_ref[...], axis=-1, keepdims=True)
        lb = jnp.sum(x * rwb_ref[...], axis=-1, keepdims=True)
        wa = jax.nn.sigmoid(la - lb)
        wb = jax.nn.sigmoid(lb - la)
        xb = x.astype(_bf16)
        act_a = (jax.nn.silu(_dot(xb, wga_ref[...])) * _dot(xb, wua_ref[...]) * wa).astype(_bf16)
        act_b = (jax.nn.silu(_dot(xb, wgb_ref[...])) * _dot(xb, wub_ref[...]) * wb).astype(_bf16)
        y = _dot(act_a, wda_ref[...]) + _dot(act_b, wdb_ref[...])
        ys_ref[...] = _pack_bf16_pairs(y)

    @pl.when(i >= nused_ref[0])
    def _():
        cast_share()
        ys_ref[...] = jnp.zeros_like(ys_ref)


def _experts(tile_ea, tile_eb, n_used, xs, router_wt, w_gate, w_up, w_down, cast_stacks=(), cast_layer=0):
    n_slots = xs.shape[0]
    d = D_MODEL
    n_tiles = n_slots // EXPERT_TILE
    f = D_EXPERT
    n_cast_steps = n_tiles - N_BUCKETS

    def share(i):
        return jnp.minimum(i, n_cast_steps - 1)

    def cast_rows(w):
        rows = w.shape[1] // n_cast_steps
        assert rows * n_cast_steps == w.shape[1] and rows % 16 == 0
        return rows

    cast_in_specs = [pl.BlockSpec((None, cast_rows(w), w.shape[2]),
                                  lambda i, ea, eb, nu: (cast_layer, share(i), 0)) for w in cast_stacks]
    cast_out_specs = [pl.BlockSpec((cast_rows(w), w.shape[2]), lambda i, ea, eb, nu: (share(i), 0))
                      for w in cast_stacks]
    cast_out_shapes = [jax.ShapeDtypeStruct(w.shape[1:], _bf16) for w in cast_stacks]

    def row(i, ea, eb, nu):
        return (jnp.maximum(jnp.minimum(i, nu[0] - 1), 0), 0, 0)

    grid_spec = pltpu.PrefetchScalarGridSpec(
        num_scalar_prefetch=3,
        grid=(n_tiles,),
        in_specs=[
            pl.BlockSpec((EXPERT_TILE,) + PACKED_SLAB, row),
            pl.BlockSpec((None, 1, d), lambda i, ea, eb, nu: (ea[i], 0, 0)),
            pl.BlockSpec((None, 1, d), lambda i, ea, eb, nu: (eb[i], 0, 0)),
            pl.BlockSpec((None, d, f), lambda i, ea, eb, nu: (ea[i], 0, 0)),
            pl.BlockSpec((None, d, f), lambda i, ea, eb, nu: (eb[i], 0, 0)),
            pl.BlockSpec((None, d, f), lambda i, ea, eb, nu: (ea[i], 0, 0)),
            pl.BlockSpec((None, d, f), lambda i, ea, eb, nu: (eb[i], 0, 0)),
            pl.BlockSpec((None, f, d), lambda i, ea, eb, nu: (ea[i], 0, 0)),
            pl.BlockSpec((None, f, d), lambda i, ea, eb, nu: (eb[i], 0, 0)),
        ] + cast_in_specs,
        out_specs=[pl.BlockSpec((EXPERT_TILE,) + PACKED_SLAB, lambda i, ea, eb, nu: (i, 0, 0))]
        + cast_out_specs,
    )
    return pl.pallas_call(
        _experts_kernel,
        grid_spec=grid_spec,
        out_shape=[jax.ShapeDtypeStruct((n_slots,) + PACKED_SLAB, jnp.uint32)] + cast_out_shapes,
        compiler_params=pltpu.CompilerParams(
            dimension_semantics=("arbitrary",), vmem_limit_bytes=VMEM_LIMIT_BYTES),
        name="grouped_experts",
    )(tile_ea, tile_eb, n_used, xs, router_wt, router_wt, w_gate, w_gate, w_up, w_up, w_down, w_down,
      *cast_stacks)


def _routing_tables(route, counts, n_tiles_max):
    bucket = route[:, 0, :].reshape(-1).astype(jnp.int32)
    rank = route[:, 1, :].reshape(-1).astype(jnp.int32)
    cnt = counts[:N_BUCKETS, 0].astype(jnp.int32)
    tiles_b = (cnt + EXPERT_TILE - 1) // EXPERT_TILE
    tile_end = jnp.cumsum(tiles_b)
    tile_start = tile_end - tiles_b
    n_used = tile_end[-1]
    onehot = bucket[:, None] == jnp.arange(N_BUCKETS, dtype=jnp.int32)[None, :]
    pos = jnp.sum(jnp.where(onehot, (tile_start * EXPERT_TILE)[None, :], 0), axis=1) + rank
    tile_ids = jnp.minimum(jnp.arange(n_tiles_max, dtype=jnp.int32), n_used - 1)
    tile_bucket = jnp.sum((tile_ids[:, None] >= tile_end[None, :]).astype(jnp.int32), axis=1)
    group = tile_bucket // PAIRS_PER_GROUP
    pair = tile_bucket % PAIRS_PER_GROUP
    tile_ea = group * EXPERTS_PER_GROUP + jnp.asarray(_PAIR_A, jnp.int32)[pair]
    tile_eb = group * EXPERTS_PER_GROUP + jnp.asarray(_PAIR_B, jnp.int32)[pair]
    return pos, tile_ea, tile_eb, n_used.reshape(1)


def kernel(x, c, w_ada, b_ada, norm_mix_g, w_in, v_norm_g, sgu_w, sgu_b, pool_w, pool_scale,
           w_branch_a, w_branch_b, w_out, norm_ffn_g, router_w, router_bias,
           w_exp_gate, w_exp_up, w_exp_down, final_norm_g):
    batch, seq, d = x.shape
    depth = w_ada.shape[0]
    t = batch * seq
    n_tiles_max = t // EXPERT_TILE + N_BUCKETS
    n_slots = n_tiles_max * EXPERT_TILE

    mod = _ada(c, w_ada, b_ada)
    rw_pad = jnp.pad(router_w, ((0, 0), (0, 128 - N_EXPERTS)))
    rw_hi = rw_pad.astype(_bf16)
    rw_lo = (rw_pad - rw_hi.astype(_f32)).astype(_bf16)
    router_w_pad = jnp.concatenate([rw_hi, rw_lo], axis=1)
    router_wt = router_w.T.reshape(N_EXPERTS, 1, d)
    final_g = final_norm_g.reshape(1, d)

    def layer_bf16(stacks, l):
        return tuple(w[l].astype(_bf16) for w in stacks)

    mixer_stacks = (w_in, sgu_w.reshape(depth, d, SGU_HEAD_DIM), pool_w.reshape(depth, d, POOL_GROUP_DIM),
                    w_branch_a, w_branch_b, w_out)
    expert_stacks = (w_exp_gate, w_exp_up, w_exp_down)
    y_tok, gate_prev = None, None
    mixer_w = layer_bf16(mixer_stacks, 0)
    for l in range(depth):
        mod_l = mod[l].reshape(batch, 1, N_MOD * d)
        w_in_l, sgu_w_l, pool_w_l, w_a_l, w_b_l, w_o_l = mixer_w
        sgu_w_l = sgu_w_l.reshape(SGU_HEADS, CHUNK, CHUNK)
        pool_w_l = pool_w_l.reshape(len(POOL_WINDOWS), POOL_GROUP_DIM, POOL_GROUP_DIM)
        x, hf, route, counts, *expert_w = _mixer(
            x, y_tok, gate_prev, mod_l, norm_mix_g[l].reshape(1, d), norm_ffn_g[l].reshape(1, d),
            w_in_l, v_norm_g[l].reshape(1, SGU_WIDTH), sgu_w_l, sgu_b[l].T,
            pool_w_l, pool_scale[l].reshape(1, POOL_WIDTH), w_a_l, w_b_l, w_o_l,
            router_w_pad, router_bias, l, expert_stacks)
        pos, tile_ea, tile_eb, n_used = _routing_tables(route, counts, n_tiles_max)
        xs = _sc_gather_by_slot(pos, hf.reshape((t,) + PACKED_SLAB), n_slots, n_used * EXPERT_TILE)
        ys, *mixer_w = _experts(tile_ea, tile_eb, n_used, xs, router_wt, *expert_w,
                                cast_stacks=mixer_stacks if l + 1 < depth else (), cast_layer=l + 1)
        y_tok = _sc_gather_rows(pos, ys).reshape((batch, seq) + PACKED_SLAB)
        gate_prev = mod_l[:, :, 5 * d:6 * d]
    return _final_norm(x, y_tok, gate_prev, final_g)
```

```python
import functools

import jax
import jax.numpy as jnp
from jax import lax
from jax.experimental import pallas as pl
from jax.experimental.pallas import tpu as pltpu
from jax.experimental.pallas import tpu_sc as plsc

D_MODEL = 1024
CHUNK = 128
SGU_HEADS = 8
SGU_HEAD_DIM = 128
SGU_WIDTH = 1024
POOL_WINDOWS = (2, 4, 8, 16)
POOL_GROUP_DIM = 256
POOL_WIDTH = 1024
HALO = 16
POOL_PAD = 8
N_EXPERTS = 16
N_EXPERT_GROUPS = 4
EXPERTS_PER_GROUP = 4
PAIRS_PER_GROUP = 6
N_BUCKETS = N_EXPERT_GROUPS * PAIRS_PER_GROUP
BUCKET_ROWS = 32
D_EXPERT = 512
N_MOD = 6
EPS = 1e-6

SEQ_TILE = 512
ROW_TILE = 512
EXPERT_TILE = 256
VMEM_LIMIT_BYTES = 58 * 1024 * 1024
PACKED_SLAB = (4, 128)
SC_CORES = 2
SC_SUBCORES = 16
SC_CHUNK = 32
SC_LANES = 16
SC_RING = 3

_PAIR_A = (0, 0, 0, 1, 1, 3)
_PAIR_B = (1, 2, 3, 3, 2, 2)

_bf16 = jnp.bfloat16
_f32 = jnp.float32


def _dot(a, b):
    return jnp.dot(a, b, preferred_element_type=_f32)


def _pack_bf16_pairs(x):
    rows, d = x.shape
    hi = lax.bitcast_convert_type(x[:, :d // 2].astype(_bf16).astype(_f32), jnp.uint32)
    lo = lax.bitcast_convert_type(x[:, d // 2:].astype(_bf16).astype(_f32), jnp.uint32)
    return (hi | (lo >> 16)).reshape((rows,) + PACKED_SLAB)


def _unpack_bf16_pairs(words):
    rows = words.shape[0]
    words = words.reshape(rows, PACKED_SLAB[0] * PACKED_SLAB[1])
    hi = lax.bitcast_convert_type(words & jnp.uint32(0xFFFF0000), _f32)
    lo = lax.bitcast_convert_type(words << 16, _f32)
    return jnp.concatenate([hi, lo], axis=1)


def _rms_modulate(x, g, shift, scale):
    y = x * lax.rsqrt(jnp.mean(x * x, axis=-1, keepdims=True) + EPS)
    return (y * g) * (1.0 + scale) + shift


def _ada_kernel(c_ref, w_ref, b_ref, o_ref):
    c = c_ref[...]
    c_act = (c * jax.nn.sigmoid(c)).astype(_bf16)
    o_ref[...] = _dot(c_act, w_ref[...].astype(_bf16)) + b_ref[...]


def _ada(c, w_ada, b_ada):
    depth, d, n = w_ada.shape
    batch = c.shape[0]
    tn = 2048
    return pl.pallas_call(
        _ada_kernel,
        grid=(depth, n // tn),
        in_specs=[
            pl.BlockSpec((batch, d), lambda l, j: (0, 0)),
            pl.BlockSpec((None, d, tn), lambda l, j: (l, 0, j)),
            pl.BlockSpec((None, 1, tn), lambda l, j: (l, 0, j)),
        ],
        out_specs=pl.BlockSpec((None, batch, tn), lambda l, j: (l, 0, j)),
        out_shape=jax.ShapeDtypeStruct((depth, batch, n), _f32),
        compiler_params=pltpu.CompilerParams(
            dimension_semantics=("arbitrary", "arbitrary"), vmem_limit_bytes=VMEM_LIMIT_BYTES),
        name="ada_modulation",
    )(c, w_ada, b_ada.reshape(depth, 1, n))


def _route(hf, rw_ref, rb_ref, carry_ref):
    ts = hf.shape[0]
    hf_hi = hf.astype(_bf16)
    hf_lo = (hf - hf_hi.astype(_f32)).astype(_bf16)
    both = _dot(hf_hi, rw_ref[...])
    logits = both[:, 0:128] + both[:, 128:256] + _dot(hf_lo, rw_ref[:, 0:128])
    lt = logits.T
    rows = [lt[e:e + 1, :] for e in range(N_EXPERTS)]
    m = functools.reduce(jnp.maximum, rows)
    ex = [jnp.exp(r - m) for r in rows]
    den = functools.reduce(lambda a, b: a + b, ex)
    probs = [e / den for e in ex]
    sel = [probs[e] + rb_ref[e] for e in range(N_EXPERTS)]

    def top2_sum(v):
        pairs = [v[i] + v[j] for i, j in zip(_PAIR_A, _PAIR_B)]
        return functools.reduce(jnp.maximum, pairs)

    gscore = [top2_sum(sel[4 * g:4 * g + 4]) for g in range(N_EXPERT_GROUPS)]
    best = gscore[0]
    gidx = jnp.zeros_like(best, dtype=jnp.int32)
    for g in range(1, N_EXPERT_GROUPS):
        better = gscore[g] > best
        best = jnp.where(better, gscore[g], best)
        gidx = jnp.where(better, g, gidx)
    ing = []
    for k in range(EXPERTS_PER_GROUP):
        v = sel[k]
        for g in range(1, N_EXPERT_GROUPS):
            v = jnp.where(gidx == g, sel[4 * g + k], v)
        ing.append(v)
    chosen = []
    for k in range(EXPERTS_PER_GROUP):
        r = jnp.zeros_like(gidx)
        for j in range(EXPERTS_PER_GROUP):
            if j == k:
                continue
            beats = (ing[j] >= ing[k]) if j < k else (ing[j] > ing[k])
            r = r + beats.astype(jnp.int32)
        chosen.append(r < 2)
    lo = jnp.where(chosen[0], 0, jnp.where(chosen[1], 1, 2))
    hi = jnp.where(chosen[3], 3, jnp.where(chosen[2], 2, 1))
    pair = jnp.where(lo == 0, hi - 1, jnp.where(lo == 2, 5, jnp.where(hi == 3, 3, 4)))
    bucket = gidx * PAIRS_PER_GROUP + pair

    brow = lax.broadcasted_iota(jnp.int32, (BUCKET_ROWS, ts), 0)
    onehot = (brow == bucket).astype(_f32)
    jj = lax.broadcasted_iota(jnp.int32, (ts, ts), 0)
    tt = lax.broadcasted_iota(jnp.int32, (ts, ts), 1)
    upper = (jj <= tt).astype(_bf16)
    cum = _dot(onehot.astype(_bf16), upper)
    carry = carry_ref[...][:, 0:1]
    rank = jnp.sum(onehot * (cum - 1.0 + carry), axis=0, keepdims=True)
    carry_ref[...] = carry_ref[...] + jnp.sum(onehot, axis=1, keepdims=True)
    return bucket.astype(_f32), rank


def _mixer_kernel(*refs, has_prev):
    if has_prev:
        x_ref, y_ref, gprev_ref = refs[:3]
        refs = refs[3:]
        x_in = lambda rows: x_ref[rows, :] + gprev_ref[...] * _unpack_bf16_pairs(y_ref[rows])
    else:
        x_ref = refs[0]
        refs = refs[1:]
        x_in = lambda rows: x_ref[rows, :]
    (mod_ref, gmix_ref, gffn_ref, win_ref, vg_ref, sw_ref, sb_ref, pw_ref, ps_ref, wa_ref, wb_ref,
     wo_ref, rw_ref, rb_ref, eg_ref, eu_ref, ed_ref,
     xo_ref, hf_ref, route_ref, counts_ref, eg16_ref, eu16_ref, ed16_ref,
     ext_ref, sum2_ref, sum4_ref, carry_ref) = refs
    b = pl.program_id(0)
    s = pl.program_id(1)
    ts = x_ref.shape[0]
    d = D_MODEL

    @pl.when((b == 0) & (s == 0))
    def _():
        carry_ref[...] = jnp.zeros_like(carry_ref)
        ext_ref[0:POOL_PAD, :] = jnp.zeros((POOL_PAD, POOL_WIDTH), _f32)
        sum2_ref[0:POOL_PAD, :] = jnp.zeros((POOL_PAD, sum2_ref.shape[1]), _f32)
        sum4_ref[0:POOL_PAD, :] = jnp.zeros((POOL_PAD, sum4_ref.shape[1]), _f32)

    @pl.when(s == 0)
    def _():
        ext_ref[POOL_PAD:POOL_PAD + HALO, :] = jnp.zeros((HALO, POOL_WIDTH), _f32)

    for w32_ref, w16_ref in ((eg_ref, eg16_ref), (eu_ref, eu16_ref), (ed_ref, ed16_ref)):
        w16_ref[...] = w32_ref[...].astype(_bf16)

    mod = mod_ref[...]
    ci = lax.broadcasted_iota(jnp.int32, (CHUNK, CHUNK), 0)
    cj = lax.broadcasted_iota(jnp.int32, (CHUNK, CHUNK), 1)
    sgu_w = [jnp.where(ci >= cj, sw_ref[h], jnp.zeros((), _bf16)) for h in range(SGU_HEADS)]
    _mixer_rows(s * ts, mod, sgu_w, x_in, gmix_ref, gffn_ref, win_ref, vg_ref, sb_ref, pw_ref, ps_ref,
                wa_ref, wb_ref, wo_ref, rw_ref, rb_ref, xo_ref, hf_ref, route_ref, ext_ref, sum2_ref,
                sum4_ref, carry_ref)
    ext_ref[POOL_PAD:POOL_PAD + HALO, :] = ext_ref[POOL_PAD + ts:POOL_PAD + ts + HALO, :]
    route_ref[2:8, :] = jnp.zeros((6, ts), _f32)
    counts_ref[...] = carry_ref[...]


def _mixer_rows(seq_pos0, mod, sgu_w, x_in, gmix_ref, gffn_ref, win_ref, vg_ref, sb_ref,
                pw_ref, ps_ref, wa_ref, wb_ref, wo_ref, rw_ref, rb_ref, xo_ref, hf_ref, route_ref,
                ext_ref, sum2_ref, sum4_ref, carry_ref):
    d = D_MODEL
    ts = SEQ_TILE
    rows = slice(0, ts)
    sh_m, sc_m, g_m = mod[:, 0:d], mod[:, d:2 * d], mod[:, 2 * d:3 * d]
    sh_f, sc_f = mod[:, 3 * d:4 * d], mod[:, 4 * d:5 * d]
    x = x_in(rows)

    hb = _rms_modulate(x, gmix_ref[...], sh_m, sc_m).astype(_bf16)

    v = jax.nn.gelu(_dot(hb, win_ref[:, SGU_WIDTH:2 * SGU_WIDTH]))
    p = _dot(hb, win_ref[:, 2 * SGU_WIDTH:2 * SGU_WIDTH + POOL_WIDTH])
    u = jax.nn.gelu(_dot(hb, win_ref[:, 0:SGU_WIDTH]))
    gate_a = jax.nn.sigmoid(_dot(hb, win_ref[:, 3 * d:4 * d]))
    gate_b = jax.nn.sigmoid(_dot(hb, win_ref[:, 4 * d:5 * d]))
    e0 = POOL_PAD + HALO
    ext_ref[e0:e0 + ts, :] = p

    vc = v - jnp.mean(v, axis=-1, keepdims=True)
    vn = (vc * lax.rsqrt(jnp.mean(vc * vc, axis=-1, keepdims=True) + EPS) * vg_ref[...]).astype(_bf16)
    n_chunks = ts // CHUNK
    ya_cols = []
    for h in range(SGU_HEADS):
        cols = slice(h * SGU_HEAD_DIM, (h + 1) * SGU_HEAD_DIM)
        rhs = jnp.concatenate([vn[n * CHUNK:(n + 1) * CHUNK, cols] for n in range(n_chunks)], axis=1)
        sg = _dot(sgu_w[h], rhs) + sb_ref[:, h:h + 1]
        s_h = jnp.concatenate([sg[:, n * SGU_HEAD_DIM:(n + 1) * SGU_HEAD_DIM] for n in range(n_chunks)],
                              axis=0)
        ya_cols.append((u[:, cols] * s_h).astype(_bf16))
    ya = jnp.concatenate(ya_cols, axis=1)
    merged = gate_a * _dot(ya, wa_ref[...])

    pos1 = (seq_pos0 + 1 + lax.broadcasted_iota(jnp.int32, (ts, 1), 0)).astype(_f32)
    g = POOL_GROUP_DIM
    lo, n = e0 - HALO, HALO + ts
    sum2 = ext_ref[lo:lo + n, :] + ext_ref[lo - 1:lo - 1 + n, :]
    sum2_ref[POOL_PAD:POOL_PAD + n, :] = sum2[:, g:]
    sum4 = sum2[:, g:] + sum2_ref[POOL_PAD - 2:POOL_PAD - 2 + n, :]
    sum4_ref[POOL_PAD:POOL_PAD + n, :] = sum4[:, g:]
    sum8 = sum4[:, g:] + sum4_ref[POOL_PAD - 4:POOL_PAD - 4 + n, :]
    sum16 = sum8[8:, g:] + sum8[:n - 8, g:]
    window_sums = (sum2[HALO:, 0:g], sum4[HALO:, 0:g], sum8[HALO:, 0:g], sum16[HALO - 8:, :])
    yb_cols = []
    for gi, w in enumerate(POOL_WINDOWS):
        cols = slice(gi * g, (gi + 1) * g)
        count = jnp.minimum(pos1, float(w))
        pooled = (window_sums[gi] / count - p[:, cols]).astype(_bf16)
        yb_cols.append(_dot(pooled, pw_ref[gi]))
    yb = (jnp.concatenate(yb_cols, axis=1) * ps_ref[...]).astype(_bf16)
    merged = merged + gate_b * _dot(yb, wb_ref[...])

    x_new = x + g_m * _dot(merged.astype(_bf16), wo_ref[...])
    xo_ref[rows, :] = x_new

    hf = _rms_modulate(x_new, gffn_ref[...], sh_f, sc_f)
    hf_ref[rows] = _pack_bf16_pairs(hf)
    bucket, rank = _route(hf, rw_ref, rb_ref, carry_ref)
    route_ref[0:1, rows] = bucket
    route_ref[1:2, rows] = rank


def _mixer(x, y_prev, gate_prev, mod_l, gmix, gffn, w_in, v_g, sgu_w, sgu_bt, pool_w, pool_scale,
           w_a, w_b, w_o, router_w_pad, router_bias, layer, expert_stacks):
    batch, seq, d = x.shape
    has_prev = y_prev is not None
    ts = SEQ_TILE
    n_tiles = batch * (seq // ts)
    tiles_per_seq = seq // ts
    const = lambda *shape: pl.BlockSpec(shape, lambda b, s: (0,) * len(shape),
                                        pipeline_mode=pl.Buffered(1))
    row_spec = pl.BlockSpec((None, ts, d), lambda b, s: (b, s, 0))
    slab_spec = pl.BlockSpec((None, ts) + PACKED_SLAB, lambda b, s: (b, s, 0, 0))
    prev_specs = [slab_spec, pl.BlockSpec((None, 1, d), lambda b, s: (b, 0, 0))] if has_prev else []
    prev_args = (y_prev, gate_prev) if has_prev else ()
    n_steps = batch * tiles_per_seq
    parts = n_steps // N_EXPERTS
    assert parts * N_EXPERTS == n_steps

    def part_index(b, s):
        step = b * tiles_per_seq + s
        return step // parts, step % parts

    def part_specs(w, with_layer):
        rows, cols = w.shape[-2] // parts, w.shape[-1]
        if with_layer:
            return pl.BlockSpec((None, None, rows, cols), lambda b, s: (layer, *part_index(b, s), 0))
        return pl.BlockSpec((None, rows, cols), lambda b, s: (*part_index(b, s), 0))

    return pl.pallas_call(
        functools.partial(_mixer_kernel, has_prev=has_prev),
        grid=(batch, tiles_per_seq),
        in_specs=[row_spec] + prev_specs + [
            pl.BlockSpec((None, 1, N_MOD * d), lambda b, s: (b, 0, 0)),
            const(1, d), const(1, d),
            const(*w_in.shape),
            const(1, SGU_WIDTH),
            const(*sgu_w.shape),
            const(*sgu_bt.shape),
            const(*pool_w.shape),
            const(1, POOL_WIDTH),
            const(d, d), const(d, d), const(d, d),
            const(*router_w_pad.shape),
            pl.BlockSpec(memory_space=pltpu.SMEM),
        ] + [part_specs(w, True) for w in expert_stacks],
        out_specs=[
            pl.BlockSpec((None, ts, d), lambda b, s: (b, s, 0)),
            pl.BlockSpec((None, ts) + PACKED_SLAB, lambda b, s: (b, s, 0, 0)),
            pl.BlockSpec((None, 8, ts), lambda b, s: (b * tiles_per_seq + s, 0, 0)),
            pl.BlockSpec((BUCKET_ROWS, 128), lambda b, s: (0, 0)),
        ] + [part_specs(w, False) for w in expert_stacks],
        out_shape=[
            jax.ShapeDtypeStruct((batch, seq, d), _f32),
            jax.ShapeDtypeStruct((batch, seq) + PACKED_SLAB, jnp.uint32),
            jax.ShapeDtypeStruct((n_tiles, 8, ts), _f32),
            jax.ShapeDtypeStruct((BUCKET_ROWS, 128), _f32),
        ] + [jax.ShapeDtypeStruct(w.shape[1:], _bf16) for w in expert_stacks],
        scratch_shapes=[
            pltpu.VMEM((POOL_PAD + HALO + ts, POOL_WIDTH), _f32),
            pltpu.VMEM((POOL_PAD + HALO + ts, POOL_WIDTH - POOL_GROUP_DIM), _f32),
            pltpu.VMEM((POOL_PAD + HALO + ts, POOL_WIDTH - 2 * POOL_GROUP_DIM), _f32),
            pltpu.VMEM((BUCKET_ROWS, 128), _f32),
        ],
        compiler_params=pltpu.CompilerParams(
            dimension_semantics=("arbitrary", "arbitrary"), vmem_limit_bytes=VMEM_LIMIT_BYTES),
        name="mixer_router",
    )(x, *prev_args, mod_l, gmix, gffn, w_in, v_g, sgu_w, sgu_bt, pool_w, pool_scale, w_a, w_b, w_o,
      router_w_pad, router_bias, *expert_stacks)


def _sc_mesh():
    return plsc.VectorSubcoreMesh(core_axis_name="c", subcore_axis_name="s")


def _sc_worker():
    return lax.axis_index("s") * SC_CORES + lax.axis_index("c")


def _sc_gather_chunks(table_hbm, idx_v, out_hbm, out_row0, n_chunks, bufs, sems):
    depth = len(bufs)

    def gather(j, slot):
        return pltpu.make_async_copy(table_hbm.at[idx_v.at[j]], bufs[slot], sems[slot])

    for j in range(depth - 1):
        @pl.when(j < n_chunks)
        def _():
            gather(j, j).start()

    @pl.loop(0, n_chunks, step=depth)
    def _(j0):
        for slot in range(depth):
            j = j0 + slot

            @pl.when(j < n_chunks)
            def _():
                @pl.when(j + depth - 1 < n_chunks)
                def _():
                    gather(j + depth - 1, (slot + depth - 1) % depth).start()

                gather(j, slot).wait()
                pltpu.sync_copy(bufs[slot], out_hbm.at[pl.ds(out_row0 + j * SC_CHUNK, SC_CHUNK)])


def _sc_chunks_per_worker(n_rows):
    n_chunks = n_rows // (SC_CORES * SC_SUBCORES * SC_CHUNK)
    assert n_chunks * SC_CORES * SC_SUBCORES * SC_CHUNK == n_rows
    return n_chunks


def _sc_gather_rows(idx, table):
    n_out = idx.shape[0]
    n_chunks = _sc_chunks_per_worker(n_out)
    slab = table.shape[1:]
    buf = pltpu.VMEM((SC_CHUNK,) + slab, table.dtype)

    @functools.partial(
        pl.kernel, mesh=_sc_mesh(),
        out_type=jax.ShapeDtypeStruct((n_out,) + slab, table.dtype),
        scratch_types=[pltpu.VMEM((n_chunks, SC_CHUNK), jnp.int32)]
        + [buf] * SC_RING + [pltpu.SemaphoreType.DMA] * SC_RING,
    )
    def gather_kernel(table_hbm, idx_hbm, out_hbm, idx_v, *ring):
        chunk0 = _sc_worker() * n_chunks
        pltpu.sync_copy(idx_hbm.at[pl.ds(chunk0, n_chunks)], idx_v)
        _sc_gather_chunks(table_hbm, idx_v, out_hbm, chunk0 * SC_CHUNK, n_chunks,
                          ring[:SC_RING], ring[SC_RING:])

    return gather_kernel(table, idx.reshape(n_out // SC_CHUNK, SC_CHUNK))


def _sc_gather_by_slot(pos, table, n_slots, n_live_slots):
    n_tok = pos.shape[0]
    per_worker = n_slots // (SC_CORES * SC_SUBCORES)
    n_chunks = _sc_chunks_per_worker(n_slots)
    slab = table.shape[1:]
    buf = pltpu.VMEM((SC_CHUNK,) + slab, table.dtype)

    @functools.partial(
        pl.kernel, mesh=_sc_mesh(),
        compiler_params=pltpu.CompilerParams(needs_layout_passes=False),
        out_type=jax.ShapeDtypeStruct((n_slots,) + slab, table.dtype),
        scratch_types=[pltpu.VMEM((n_tok,), jnp.int32), pltpu.VMEM((n_chunks, SC_CHUNK), jnp.int32),
                       pltpu.VMEM((SC_LANES,), jnp.int32)]
        + [buf] * SC_RING + [pltpu.SemaphoreType.DMA] * SC_RING,
    )
    def slot_kernel(table_hbm, pos_hbm, live_hbm, out_hbm, pos_v, inv_v, live_v, *ring):
        slot0 = _sc_worker() * per_worker
        pltpu.sync_copy(pos_hbm, pos_v)
        pltpu.sync_copy(live_hbm, live_v)
        n_live_chunks = jnp.clip((jnp.max(live_v[...]) - slot0) // SC_CHUNK, 0, n_chunks)
        lane = lax.iota(jnp.int32, SC_LANES)

        @plsc.parallel_loop(0, n_chunks)
        def _(j):
            for h in range(SC_CHUNK // SC_LANES):
                s = slot0 + j * SC_CHUNK + h * SC_LANES + lane
                inv_v[j, pl.ds(h * SC_LANES, SC_LANES)] = jnp.where(s >= n_tok, s - n_tok, s)

        @plsc.parallel_loop(0, n_tok // SC_LANES, unroll=4)
        def _(i):
            local = pos_v[pl.ds(i * SC_LANES, SC_LANES)] - slot0
            mine = (local >= 0) & (local < per_worker)
            local = jnp.where(mine, local, 0)
            plsc.store_scatter(inv_v, [local // SC_CHUNK, local % SC_CHUNK], i * SC_LANES + lane, mask=mine)

        _sc_gather_chunks(table_hbm, inv_v, out_hbm, slot0, n_live_chunks, ring[:SC_RING], ring[SC_RING:])

    return slot_kernel(table, pos, jnp.broadcast_to(n_live_slots, (SC_LANES,)))


def _final_kernel(x_ref, y_ref, g_ref, fg_ref, o_ref):
    out = x_ref[...] + g_ref[...] * _unpack_bf16_pairs(y_ref[...])
    o_ref[...] = out * lax.rsqrt(jnp.mean(out * out, axis=-1, keepdims=True) + EPS) * fg_ref[...]


def _final_norm(x, y_tok, gate_f, final_g):
    batch, seq, d = x.shape
    row_spec = pl.BlockSpec((None, ROW_TILE, d), lambda b, s: (b, s, 0))
    return pl.pallas_call(
        _final_kernel,
        grid=(batch, seq // ROW_TILE),
        in_specs=[row_spec, pl.BlockSpec((None, ROW_TILE) + PACKED_SLAB, lambda b, s: (b, s, 0, 0)),
                  pl.BlockSpec((None, 1, d), lambda b, s: (b, 0, 0)),
                  pl.BlockSpec((1, d), lambda b, s: (0, 0))],
        out_specs=row_spec,
        out_shape=jax.ShapeDtypeStruct((batch, seq, d), _f32),
        compiler_params=pltpu.CompilerParams(dimension_semantics=("arbitrary", "arbitrary")),
        name="residual_final_norm",
    )(x, y_tok, gate_f, final_g)


def _experts_kernel(ea_ref, eb_ref, nused_ref, xs_ref, rwa_ref, rwb_ref,
                    wga_ref, wgb_ref, wua_ref, wub_ref, wda_ref, wdb_ref, *rest):
    n_cast = (len(rest) - 1) // 2
    cast_in, ys_ref, cast_out = rest[:n_cast], rest[n_cast], rest[n_cast + 1:]
    i = pl.program_id(0)

    def cast_share():
        for w32_ref, w16_ref in zip(cast_in, cast_out):
            w16_ref[...] = w32_ref[...].astype(_bf16)

    @pl.when(i < nused_ref[0])
    def _():
        cast_share()
        x = _unpack_bf16_pairs(xs_ref[...])
        la = jnp.sum(x * rwa_ref[...], axis=-1, keepdims=True)
        lb = jnp.sum(x * rwb_ref[...], axis=-1, keepdims=True)
        wa = jax.nn.sigmoid(la - lb)
        wb = jax.nn.sigmoid(lb - la)
        xb = x.astype(_bf16)
        act_a = (jax.nn.silu(_dot(xb, wga_ref[...])) * _dot(xb, wua_ref[...]) * wa).astype(_bf16)
        act_b = (jax.nn.silu(_dot(xb, wgb_ref[...])) * _dot(xb, wub_ref[...]) * wb).astype(_bf16)
        y = _dot(act_a, wda_ref[...]) + _dot(act_b, wdb_ref[...])
        ys_ref[...] = _pack_bf16_pairs(y)

    @pl.when(i >= nused_ref[0])
    def _():
        cast_share()
        ys_ref[...] = jnp.zeros_like(ys_ref)


def _experts(tile_ea, tile_eb, n_used, xs, router_wt, w_gate, w_up, w_down, cast_stacks=(), cast_layer=0):
    n_slots = xs.shape[0]
    d = D_MODEL
    n_tiles = n_slots // EXPERT_TILE
    f = D_EXPERT
    n_cast_steps = n_tiles - N_BUCKETS

    def share(i):
        return jnp.minimum(i, n_cast_steps - 1)

    def cast_rows(w):
        rows = w.shape[1] // n_cast_steps
        assert rows * n_cast_steps == w.shape[1] and rows % 16 == 0
        return rows

    cast_in_specs = [pl.BlockSpec((None, cast_rows(w), w.shape[2]),
                                  lambda i, ea, eb, nu: (cast_layer, share(i), 0)) for w in cast_stacks]
    cast_out_specs = [pl.BlockSpec((cast_rows(w), w.shape[2]), lambda i, ea, eb, nu: (share(i), 0))
                      for w in cast_stacks]
    cast_out_shapes = [jax.ShapeDtypeStruct(w.shape[1:], _bf16) for w in cast_stacks]

    def row(i, ea, eb, nu):
        return (jnp.maximum(jnp.minimum(i, nu[0] - 1), 0), 0, 0)

    grid_spec = pltpu.PrefetchScalarGridSpec(
        num_scalar_prefetch=3,
        grid=(n_tiles,),
        in_specs=[
            pl.BlockSpec((EXPERT_TILE,) + PACKED_SLAB, row),
            pl.BlockSpec((None, 1, d), lambda i, ea, eb, nu: (ea[i], 0, 0)),
            pl.BlockSpec((None, 1, d), lambda i, ea, eb, nu: (eb[i], 0, 0)),
            pl.BlockSpec((None, d, f), lambda i, ea, eb, nu: (ea[i], 0, 0)),
            pl.BlockSpec((None, d, f), lambda i, ea, eb, nu: (eb[i], 0, 0)),
            pl.BlockSpec((None, d, f), lambda i, ea, eb, nu: (ea[i], 0, 0)),
            pl.BlockSpec((None, d, f), lambda i, ea, eb, nu: (eb[i], 0, 0)),
            pl.BlockSpec((None, f, d), lambda i, ea, eb, nu: (ea[i], 0, 0)),
            pl.BlockSpec((None, f, d), lambda i, ea, eb, nu: (eb[i], 0, 0)),
        ] + cast_in_specs,
        out_specs=[pl.BlockSpec((EXPERT_TILE,) + PACKED_SLAB, lambda i, ea, eb, nu: (i, 0, 0))]
        + cast_out_specs,
    )
    return pl.pallas_call(
        _experts_kernel,
        grid_spec=grid_spec,
        out_shape=[jax.ShapeDtypeStruct((n_slots,) + PACKED_SLAB, jnp.uint32)] + cast_out_shapes,
        compiler_params=pltpu.CompilerParams(
            dimension_semantics=("arbitrary",), vmem_limit_bytes=VMEM_LIMIT_BYTES),
        name="grouped_experts",
    )(tile_ea, tile_eb, n_used, xs, router_wt, router_wt, w_gate, w_gate, w_up, w_up, w_down, w_down,
      *cast_stacks)


def _tables_kernel(cnt_ref, bucket_ref, rank_ref, pos_ref, ea_ref, eb_ref, nused_ref):
    n_tiles_max = ea_ref.shape[0]
    tile = jnp.int32(0)
    last_a, last_b = jnp.int32(0), jnp.int32(0)
    pos = rank_ref[...]
    bucket = bucket_ref[...]
    for b in range(N_BUCKETS):
        group, pair = divmod(b, PAIRS_PER_GROUP)
        ea = group * EXPERTS_PER_GROUP + _PAIR_A[pair]
        eb = group * EXPERTS_PER_GROUP + _PAIR_B[pair]
        n_b = lax.shift_right_logical(cnt_ref[b] + (EXPERT_TILE - 1), EXPERT_TILE.bit_length() - 1)

        def fill(i, c, ea=ea, eb=eb):
            ea_ref[i] = ea
            eb_ref[i] = eb
            return c

        lax.fori_loop(tile, tile + n_b, fill, 0)
        pos = pos + jnp.where(bucket == b, tile * EXPERT_TILE, 0)
        last_a = jnp.where(n_b > 0, ea, last_a)
        last_b = jnp.where(n_b > 0, eb, last_b)
        tile = tile + n_b
    nused_ref[0] = tile

    def fill_rest(i, c):
        ea_ref[i] = last_a
        eb_ref[i] = last_b
        return c

    lax.fori_loop(tile, n_tiles_max, fill_rest, 0)
    pos_ref[...] = pos


def _routing_tables(route, counts, n_tiles_max):
    n_tiles, _, ts = route.shape
    smem = pl.BlockSpec(memory_space=pltpu.SMEM)
    pos, tile_ea, tile_eb, n_used = pl.pallas_call(
        _tables_kernel,
        in_specs=[smem, pl.BlockSpec((n_tiles, ts), lambda: (0, 0)), pl.BlockSpec((n_tiles, ts), lambda: (0, 0))],
        out_specs=[pl.BlockSpec((n_tiles, ts), lambda: (0, 0)), smem, smem, smem],
        out_shape=[jax.ShapeDtypeStruct((n_tiles, ts), jnp.int32),
                   jax.ShapeDtypeStruct((n_tiles_max,), jnp.int32),
                   jax.ShapeDtypeStruct((n_tiles_max,), jnp.int32),
                   jax.ShapeDtypeStruct((1,), jnp.int32)],
        name="routing_tables",
    )(counts[:N_BUCKETS, 0].astype(jnp.int32), route[:, 0, :].astype(jnp.int32),
      route[:, 1, :].astype(jnp.int32))
    return pos.reshape(-1), tile_ea, tile_eb, n_used


def kernel(x, c, w_ada, b_ada, norm_mix_g, w_in, v_norm_g, sgu_w, sgu_b, pool_w, pool_scale,
           w_branch_a, w_branch_b, w_out, norm_ffn_g, router_w, router_bias,
           w_exp_gate, w_exp_up, w_exp_down, final_norm_g):
    batch, seq, d = x.shape
    depth = w_ada.shape[0]
    t = batch * seq
    n_tiles_max = t // EXPERT_TILE + N_BUCKETS
    n_slots = n_tiles_max * EXPERT_TILE

    mod = _ada(c, w_ada, b_ada)
    rw_pad = jnp.pad(router_w, ((0, 0), (0, 128 - N_EXPERTS)))
    rw_hi = rw_pad.astype(_bf16)
    rw_lo = (rw_pad - rw_hi.astype(_f32)).astype(_bf16)
    router_w_pad = jnp.concatenate([rw_hi, rw_lo], axis=1)
    router_wt = router_w.T.reshape(N_EXPERTS, 1, d)
    final_g = final_norm_g.reshape(1, d)

    def layer_bf16(stacks, l):
        return tuple(w[l].astype(_bf16) for w in stacks)

    mixer_stacks = (w_in, sgu_w.reshape(depth, d, SGU_HEAD_DIM), pool_w.reshape(depth, d, POOL_GROUP_DIM),
                    w_branch_a, w_branch_b, w_out)
    expert_stacks = (w_exp_gate, w_exp_up, w_exp_down)
    y_tok, gate_prev = None, None
    mixer_w = layer_bf16(mixer_stacks, 0)
    for l in range(depth):
        mod_l = mod[l].reshape(batch, 1, N_MOD * d)
        w_in_l, sgu_w_l, pool_w_l, w_a_l, w_b_l, w_o_l = mixer_w
        sgu_w_l = sgu_w_l.reshape(SGU_HEADS, CHUNK, CHUNK)
        pool_w_l = pool_w_l.reshape(len(POOL_WINDOWS), POOL_GROUP_DIM, POOL_GROUP_DIM)
        x, hf, route, counts, *expert_w = _mixer(
            x, y_tok, gate_prev, mod_l, norm_mix_g[l].reshape(1, d), norm_ffn_g[l].reshape(1, d),
            w_in_l, v_norm_g[l].reshape(1, SGU_WIDTH), sgu_w_l, sgu_b[l].T,
            pool_w_l, pool_scale[l].reshape(1, POOL_WIDTH), w_a_l, w_b_l, w_o_l,
            router_w_pad, router_bias, l, expert_stacks)
        pos, tile_ea, tile_eb, n_used = _routing_tables(route, counts, n_tiles_max)
        xs = _sc_gather_by_slot(pos, hf.reshape((t,) + PACKED_SLAB), n_slots, n_used * EXPERT_TILE)
        ys, *mixer_w = _experts(tile_ea, tile_eb, n_used, xs, router_wt, *expert_w,
                                cast_stacks=mixer_stacks if l + 1 < depth else (), cast_layer=l + 1)
        y_tok = _sc_gather_rows(pos, ys).reshape((batch, seq) + PACKED_SLAB)
        gate_prev = mod_l[:, :, 5 * d:6 * d]
    return _final_norm(x, y_tok, gate_prev, final_g)
```

```python
import functools

import jax
import jax.numpy as jnp
from jax import lax
from jax.experimental import pallas as pl
from jax.experimental.pallas import tpu as pltpu
from jax.experimental.pallas import tpu_sc as plsc

D_MODEL = 1024
CHUNK = 128
SGU_HEADS = 8
SGU_HEAD_DIM = 128
SGU_WIDTH = 1024
POOL_WINDOWS = (2, 4, 8, 16)
POOL_GROUP_DIM = 256
POOL_WIDTH = 1024
HALO = 16
POOL_PAD = 8
N_EXPERTS = 16
N_EXPERT_GROUPS = 4
EXPERTS_PER_GROUP = 4
PAIRS_PER_GROUP = 6
N_BUCKETS = N_EXPERT_GROUPS * PAIRS_PER_GROUP
BUCKET_ROWS = 32
D_EXPERT = 512
N_MOD = 6
EPS = 1e-6

SEQ_TILE = 512
ROW_TILE = 512
EXPERT_TILE = 256
VMEM_LIMIT_BYTES = 58 * 1024 * 1024
PACKED_SLAB = (4, 128)
SC_CORES = 2
SC_SUBCORES = 16
SC_CHUNK = 32
SC_LANES = 16
SC_RING = 3

_PAIR_A = (0, 0, 0, 1, 1, 3)
_PAIR_B = (1, 2, 3, 3, 2, 2)

_bf16 = jnp.bfloat16
_f32 = jnp.float32


def _dot(a, b):
    return jnp.dot(a, b, preferred_element_type=_f32)


def _pack_bf16_pairs(x):
    rows, d = x.shape
    hi = lax.bitcast_convert_type(x[:, :d // 2].astype(_bf16).astype(_f32), jnp.uint32)
    lo = lax.bitcast_convert_type(x[:, d // 2:].astype(_bf16).astype(_f32), jnp.uint32)
    return (hi | (lo >> 16)).reshape((rows,) + PACKED_SLAB)


def _unpack_bf16_pairs(words):
    rows = words.shape[0]
    words = words.reshape(rows, PACKED_SLAB[0] * PACKED_SLAB[1])
    hi = lax.bitcast_convert_type(words & jnp.uint32(0xFFFF0000), _f32)
    lo = lax.bitcast_convert_type(words << 16, _f32)
    return jnp.concatenate([hi, lo], axis=1)


def _rms_modulate(x, g, shift, scale):
    y = x * lax.rsqrt(jnp.mean(x * x, axis=-1, keepdims=True) + EPS)
    return (y * g) * (1.0 + scale) + shift


def _ada_kernel(c_ref, w_ref, b_ref, o_ref):
    c = c_ref[...]
    c_act = (c * jax.nn.sigmoid(c)).astype(_bf16)
    o_ref[...] = _dot(c_act, w_ref[...].astype(_bf16)) + b_ref[...]


def _ada(c, w_ada, b_ada):
    depth, d, n = w_ada.shape
    batch = c.shape[0]
    tn = 2048
    return pl.pallas_call(
        _ada_kernel,
        grid=(depth, n // tn),
        in_specs=[
            pl.BlockSpec((batch, d), lambda l, j: (0, 0)),
            pl.BlockSpec((None, d, tn), lambda l, j: (l, 0, j)),
            pl.BlockSpec((None, 1, tn), lambda l, j: (l, 0, j)),
        ],
        out_specs=pl.BlockSpec((None, batch, tn), lambda l, j: (l, 0, j)),
        out_shape=jax.ShapeDtypeStruct((depth, batch, n), _f32),
        compiler_params=pltpu.CompilerParams(
            dimension_semantics=("arbitrary", "arbitrary"), vmem_limit_bytes=VMEM_LIMIT_BYTES),
        name="ada_modulation",
    )(c, w_ada, b_ada.reshape(depth, 1, n))


def _route(hf, rw_ref, rb_ref, carry_ref):
    ts = hf.shape[0]
    hf_hi = hf.astype(_bf16)
    hf_lo = (hf - hf_hi.astype(_f32)).astype(_bf16)
    both = _dot(hf_hi, rw_ref[...])
    logits = both[:, 0:128] + both[:, 128:256] + _dot(hf_lo, rw_ref[:, 0:128])
    lt = logits.T
    rows = [lt[e:e + 1, :] for e in range(N_EXPERTS)]
    m = functools.reduce(jnp.maximum, rows)
    ex = [jnp.exp(r - m) for r in rows]
    den = functools.reduce(lambda a, b: a + b, ex)
    probs = [e / den for e in ex]
    sel = [probs[e] + rb_ref[e] for e in range(N_EXPERTS)]

    def top2_sum(v):
        pairs = [v[i] + v[j] for i, j in zip(_PAIR_A, _PAIR_B)]
        return functools.reduce(jnp.maximum, pairs)

    gscore = [top2_sum(sel[4 * g:4 * g + 4]) for g in range(N_EXPERT_GROUPS)]
    best = gscore[0]
    gidx = jnp.zeros_like(best, dtype=jnp.int32)
    for g in range(1, N_EXPERT_GROUPS):
        better = gscore[g] > best
        best = jnp.where(better, gscore[g], best)
        gidx = jnp.where(better, g, gidx)
    ing = []
    for k in range(EXPERTS_PER_GROUP):
        v = sel[k]
        for g in range(1, N_EXPERT_GROUPS):
            v = jnp.where(gidx == g, sel[4 * g + k], v)
        ing.append(v)
    chosen = []
    for k in range(EXPERTS_PER_GROUP):
        r = jnp.zeros_like(gidx)
        for j in range(EXPERTS_PER_GROUP):
            if j == k:
                continue
            beats = (ing[j] >= ing[k]) if j < k else (ing[j] > ing[k])
            r = r + beats.astype(jnp.int32)
        chosen.append(r < 2)
    lo = jnp.where(chosen[0], 0, jnp.where(chosen[1], 1, 2))
    hi = jnp.where(chosen[3], 3, jnp.where(chosen[2], 2, 1))
    pair = jnp.where(lo == 0, hi - 1, jnp.where(lo == 2, 5, jnp.where(hi == 3, 3, 4)))
    bucket = gidx * PAIRS_PER_GROUP + pair

    brow = lax.broadcasted_iota(jnp.int32, (BUCKET_ROWS, ts), 0)
    onehot = (brow == bucket).astype(_f32)
    jj = lax.broadcasted_iota(jnp.int32, (ts, ts), 0)
    tt = lax.broadcasted_iota(jnp.int32, (ts, ts), 1)
    upper = (jj <= tt).astype(_bf16)
    cum = _dot(onehot.astype(_bf16), upper)
    carry = carry_ref[...][:, 0:1]
    rank = jnp.sum(onehot * (cum - 1.0 + carry), axis=0, keepdims=True)
    carry_ref[...] = carry_ref[...] + jnp.sum(onehot, axis=1, keepdims=True)
    return bucket.astype(_f32), rank


def _mixer_kernel(*refs, has_prev):
    if has_prev:
        x_ref, y_ref, gprev_ref = refs[:3]
        refs = refs[3:]
        x_in = lambda rows: x_ref[rows, :] + gprev_ref[...] * _unpack_bf16_pairs(y_ref[rows])
    else:
        x_ref = refs[0]
        refs = refs[1:]
        x_in = lambda rows: x_ref[rows, :]
    (mod_ref, gmix_ref, gffn_ref, win_ref, vg_ref, sw_ref, sb_ref, pw_ref, ps_ref, wa_ref, wb_ref,
     wo_ref, rw_ref, rb_ref) = refs[:14]
    n_cast = (len(refs) - 14 - 4 - 4) // 2
    cast_in = refs[14:14 + n_cast]
    xo_ref, hf_ref, route_ref, counts_ref = refs[14 + n_cast:18 + n_cast]
    cast_out = refs[18 + n_cast:18 + 2 * n_cast]
    ext_ref, sum2_ref, sum4_ref, carry_ref = refs[18 + 2 * n_cast:]
    b = pl.program_id(0)
    s = pl.program_id(1)
    ts = x_ref.shape[0]
    d = D_MODEL

    @pl.when((b == 0) & (s == 0))
    def _():
        carry_ref[...] = jnp.zeros_like(carry_ref)
        ext_ref[0:POOL_PAD, :] = jnp.zeros((POOL_PAD, POOL_WIDTH), _f32)
        sum2_ref[0:POOL_PAD, :] = jnp.zeros((POOL_PAD, sum2_ref.shape[1]), _f32)
        sum4_ref[0:POOL_PAD, :] = jnp.zeros((POOL_PAD, sum4_ref.shape[1]), _f32)

    @pl.when(s == 0)
    def _():
        ext_ref[POOL_PAD:POOL_PAD + HALO, :] = jnp.zeros((HALO, POOL_WIDTH), _f32)

    for w32_ref, w16_ref in zip(cast_in, cast_out):
        w16_ref[...] = w32_ref[...].astype(_bf16)

    mod = mod_ref[...]
    ci = lax.broadcasted_iota(jnp.int32, (CHUNK, CHUNK), 0)
    cj = lax.broadcasted_iota(jnp.int32, (CHUNK, CHUNK), 1)
    sgu_w = [jnp.where(ci >= cj, sw_ref[h], jnp.zeros((), _bf16)) for h in range(SGU_HEADS)]
    _mixer_rows(s * ts, mod, sgu_w, x_in, gmix_ref, gffn_ref, win_ref, vg_ref, sb_ref, pw_ref, ps_ref,
                wa_ref, wb_ref, wo_ref, rw_ref, rb_ref, xo_ref, hf_ref, route_ref, ext_ref, sum2_ref,
                sum4_ref, carry_ref)
    ext_ref[POOL_PAD:POOL_PAD + HALO, :] = ext_ref[POOL_PAD + ts:POOL_PAD + ts + HALO, :]
    route_ref[2:8, :] = jnp.zeros((6, ts), _f32)
    counts_ref[...] = carry_ref[...]


def _mixer_rows(seq_pos0, mod, sgu_w, x_in, gmix_ref, gffn_ref, win_ref, vg_ref, sb_ref,
                pw_ref, ps_ref, wa_ref, wb_ref, wo_ref, rw_ref, rb_ref, xo_ref, hf_ref, route_ref,
                ext_ref, sum2_ref, sum4_ref, carry_ref):
    d = D_MODEL
    ts = SEQ_TILE
    rows = slice(0, ts)
    sh_m, sc_m, g_m = mod[:, 0:d], mod[:, d:2 * d], mod[:, 2 * d:3 * d]
    sh_f, sc_f = mod[:, 3 * d:4 * d], mod[:, 4 * d:5 * d]
    x = x_in(rows)

    hb = _rms_modulate(x, gmix_ref[...], sh_m, sc_m).astype(_bf16)

    v = jax.nn.gelu(_dot(hb, win_ref[:, SGU_WIDTH:2 * SGU_WIDTH]))
    p = _dot(hb, win_ref[:, 2 * SGU_WIDTH:2 * SGU_WIDTH + POOL_WIDTH])
    u = jax.nn.gelu(_dot(hb, win_ref[:, 0:SGU_WIDTH]))
    gate_a = jax.nn.sigmoid(_dot(hb, win_ref[:, 3 * d:4 * d]))
    gate_b = jax.nn.sigmoid(_dot(hb, win_ref[:, 4 * d:5 * d]))
    e0 = POOL_PAD + HALO
    ext_ref[e0:e0 + ts, :] = p

    vc = v - jnp.mean(v, axis=-1, keepdims=True)
    vn = (vc * lax.rsqrt(jnp.mean(vc * vc, axis=-1, keepdims=True) + EPS) * vg_ref[...]).astype(_bf16)
    n_chunks = ts // CHUNK
    ya_cols = []
    for h in range(SGU_HEADS):
        cols = slice(h * SGU_HEAD_DIM, (h + 1) * SGU_HEAD_DIM)
        rhs = jnp.concatenate([vn[n * CHUNK:(n + 1) * CHUNK, cols] for n in range(n_chunks)], axis=1)
        sg = _dot(sgu_w[h], rhs) + sb_ref[:, h:h + 1]
        s_h = jnp.concatenate([sg[:, n * SGU_HEAD_DIM:(n + 1) * SGU_HEAD_DIM] for n in range(n_chunks)],
                              axis=0)
        ya_cols.append((u[:, cols] * s_h).astype(_bf16))
    ya = jnp.concatenate(ya_cols, axis=1)
    merged = gate_a * _dot(ya, wa_ref[...])

    pos1 = (seq_pos0 + 1 + lax.broadcasted_iota(jnp.int32, (ts, 1), 0)).astype(_f32)
    g = POOL_GROUP_DIM
    lo, n = e0 - HALO, HALO + ts
    sum2 = ext_ref[lo:lo + n, :] + ext_ref[lo - 1:lo - 1 + n, :]
    sum2_ref[POOL_PAD:POOL_PAD + n, :] = sum2[:, g:]
    sum4 = sum2[:, g:] + sum2_ref[POOL_PAD - 2:POOL_PAD - 2 + n, :]
    sum4_ref[POOL_PAD:POOL_PAD + n, :] = sum4[:, g:]
    sum8 = sum4[:, g:] + sum4_ref[POOL_PAD - 4:POOL_PAD - 4 + n, :]
    sum16 = sum8[8:, g:] + sum8[:n - 8, g:]
    window_sums = (sum2[HALO:, 0:g], sum4[HALO:, 0:g], sum8[HALO:, 0:g], sum16[HALO - 8:, :])
    yb_cols = []
    for gi, w in enumerate(POOL_WINDOWS):
        cols = slice(gi * g, (gi + 1) * g)
        count = jnp.minimum(pos1, float(w))
        pooled = (window_sums[gi] / count - p[:, cols]).astype(_bf16)
        yb_cols.append(_dot(pooled, pw_ref[gi]))
    yb = (jnp.concatenate(yb_cols, axis=1) * ps_ref[...]).astype(_bf16)
    merged = merged + gate_b * _dot(yb, wb_ref[...])

    x_new = x + g_m * _dot(merged.astype(_bf16), wo_ref[...])
    xo_ref[rows, :] = x_new

    hf = _rms_modulate(x_new, gffn_ref[...], sh_f, sc_f)
    hf_ref[rows] = _pack_bf16_pairs(hf)
    bucket, rank = _route(hf, rw_ref, rb_ref, carry_ref)
    route_ref[0:1, rows] = bucket
    route_ref[1:2, rows] = rank


def _mixer(x, y_prev, gate_prev, mod_l, gmix, gffn, w_in, v_g, sgu_w, sgu_bt, pool_w, pool_scale,
           w_a, w_b, w_o, router_w_pad, router_bias, layer, expert_stacks, next_stacks=()):
    batch, seq, d = x.shape
    has_prev = y_prev is not None
    ts = SEQ_TILE
    n_tiles = batch * (seq // ts)
    tiles_per_seq = seq // ts
    const = lambda *shape: pl.BlockSpec(shape, lambda b, s: (0,) * len(shape),
                                        pipeline_mode=pl.Buffered(1))
    row_spec = pl.BlockSpec((None, ts, d), lambda b, s: (b, s, 0))
    slab_spec = pl.BlockSpec((None, ts) + PACKED_SLAB, lambda b, s: (b, s, 0, 0))
    prev_specs = [slab_spec, pl.BlockSpec((None, 1, d), lambda b, s: (b, 0, 0))] if has_prev else []
    prev_args = (y_prev, gate_prev) if has_prev else ()
    n_steps = batch * tiles_per_seq
    parts = n_steps // N_EXPERTS
    assert parts * N_EXPERTS == n_steps

    def part_index(b, s):
        step = b * tiles_per_seq + s
        return step // parts, step % parts

    def part_specs(w, with_layer):
        rows, cols = w.shape[-2] // parts, w.shape[-1]
        if with_layer:
            return pl.BlockSpec((None, None, rows, cols), lambda b, s: (layer, *part_index(b, s), 0))
        return pl.BlockSpec((None, rows, cols), lambda b, s: (*part_index(b, s), 0))

    def next_specs(w, with_layer):
        rows, cols = w.shape[1] // n_steps, w.shape[2]
        assert rows * n_steps == w.shape[1] and rows % 16 == 0
        if with_layer:
            return pl.BlockSpec((None, rows, cols), lambda b, s: (layer + 1, b * tiles_per_seq + s, 0))
        return pl.BlockSpec((rows, cols), lambda b, s: (b * tiles_per_seq + s, 0))

    return pl.pallas_call(
        functools.partial(_mixer_kernel, has_prev=has_prev),
        grid=(batch, tiles_per_seq),
        in_specs=[row_spec] + prev_specs + [
            pl.BlockSpec((None, 1, N_MOD * d), lambda b, s: (b, 0, 0)),
            const(1, d), const(1, d),
            const(*w_in.shape),
            const(1, SGU_WIDTH),
            const(*sgu_w.shape),
            const(*sgu_bt.shape),
            const(*pool_w.shape),
            const(1, POOL_WIDTH),
            const(d, d), const(d, d), const(d, d),
            const(*router_w_pad.shape),
            pl.BlockSpec(memory_space=pltpu.SMEM),
        ] + [part_specs(w, True) for w in expert_stacks] + [next_specs(w, True) for w in next_stacks],
        out_specs=[
            pl.BlockSpec((None, ts, d), lambda b, s: (b, s, 0)),
            pl.BlockSpec((None, ts) + PACKED_SLAB, lambda b, s: (b, s, 0, 0)),
            pl.BlockSpec((None, 8, ts), lambda b, s: (b * tiles_per_seq + s, 0, 0)),
            pl.BlockSpec((BUCKET_ROWS, 128), lambda b, s: (0, 0)),
        ] + [part_specs(w, False) for w in expert_stacks] + [next_specs(w, False) for w in next_stacks],
        out_shape=[
            jax.ShapeDtypeStruct((batch, seq, d), _f32),
            jax.ShapeDtypeStruct((batch, seq) + PACKED_SLAB, jnp.uint32),
            jax.ShapeDtypeStruct((n_tiles, 8, ts), _f32),
            jax.ShapeDtypeStruct((BUCKET_ROWS, 128), _f32),
        ] + [jax.ShapeDtypeStruct(w.shape[1:], _bf16) for w in expert_stacks + tuple(next_stacks)],
        scratch_shapes=[
            pltpu.VMEM((POOL_PAD + HALO + ts, POOL_WIDTH), _f32),
            pltpu.VMEM((POOL_PAD + HALO + ts, POOL_WIDTH - POOL_GROUP_DIM), _f32),
            pltpu.VMEM((POOL_PAD + HALO + ts, POOL_WIDTH - 2 * POOL_GROUP_DIM), _f32),
            pltpu.VMEM((BUCKET_ROWS, 128), _f32),
        ],
        compiler_params=pltpu.CompilerParams(
            dimension_semantics=("arbitrary", "arbitrary"), vmem_limit_bytes=VMEM_LIMIT_BYTES),
        name="mixer_router",
    )(x, *prev_args, mod_l, gmix, gffn, w_in, v_g, sgu_w, sgu_bt, pool_w, pool_scale, w_a, w_b, w_o,
      router_w_pad, router_bias, *expert_stacks, *next_stacks)


def _sc_mesh():
    return plsc.VectorSubcoreMesh(core_axis_name="c", subcore_axis_name="s")


def _sc_worker():
    return lax.axis_index("s") * SC_CORES + lax.axis_index("c")


def _sc_gather_chunks(table_hbm, idx_v, out_hbm, out_row0, n_chunks, bufs, sems):
    depth = len(bufs)

    def gather(j, slot):
        return pltpu.make_async_copy(table_hbm.at[idx_v.at[j]], bufs[slot], sems[slot])

    for j in range(depth - 1):
        @pl.when(j < n_chunks)
        def _():
            gather(j, j).start()

    @pl.loop(0, n_chunks, step=depth)
    def _(j0):
        for slot in range(depth):
            j = j0 + slot

            @pl.when(j < n_chunks)
            def _():
                @pl.when(j + depth - 1 < n_chunks)
                def _():
                    gather(j + depth - 1, (slot + depth - 1) % depth).start()

                gather(j, slot).wait()
                pltpu.sync_copy(bufs[slot], out_hbm.at[pl.ds(out_row0 + j * SC_CHUNK, SC_CHUNK)])


def _sc_chunks_per_worker(n_rows):
    n_chunks = n_rows // (SC_CORES * SC_SUBCORES * SC_CHUNK)
    assert n_chunks * SC_CORES * SC_SUBCORES * SC_CHUNK == n_rows
    return n_chunks


def _sc_gather_rows(idx, table):
    n_out = idx.shape[0]
    n_chunks = _sc_chunks_per_worker(n_out)
    slab = table.shape[1:]
    buf = pltpu.VMEM((SC_CHUNK,) + slab, table.dtype)

    @functools.partial(
        pl.kernel, mesh=_sc_mesh(),
        out_type=jax.ShapeDtypeStruct((n_out,) + slab, table.dtype),
        scratch_types=[pltpu.VMEM((n_chunks, SC_CHUNK), jnp.int32)]
        + [buf] * SC_RING + [pltpu.SemaphoreType.DMA] * SC_RING,
    )
    def gather_kernel(table_hbm, idx_hbm, out_hbm, idx_v, *ring):
        chunk0 = _sc_worker() * n_chunks
        pltpu.sync_copy(idx_hbm.at[pl.ds(chunk0, n_chunks)], idx_v)
        _sc_gather_chunks(table_hbm, idx_v, out_hbm, chunk0 * SC_CHUNK, n_chunks,
                          ring[:SC_RING], ring[SC_RING:])

    return gather_kernel(table, idx.reshape(n_out // SC_CHUNK, SC_CHUNK))


def _sc_gather_by_slot(pos, table, n_slots, n_live_slots):
    n_tok = pos.shape[0]
    per_worker = n_slots // (SC_CORES * SC_SUBCORES)
    n_chunks = _sc_chunks_per_worker(n_slots)
    slab = table.shape[1:]
    buf = pltpu.VMEM((SC_CHUNK,) + slab, table.dtype)

    @functools.partial(
        pl.kernel, mesh=_sc_mesh(),
        compiler_params=pltpu.CompilerParams(needs_layout_passes=False),
        out_type=jax.ShapeDtypeStruct((n_slots,) + slab, table.dtype),
        scratch_types=[pltpu.VMEM((n_tok,), jnp.int32), pltpu.VMEM((n_chunks, SC_CHUNK), jnp.int32),
                       pltpu.VMEM((SC_LANES,), jnp.int32)]
        + [buf] * SC_RING + [pltpu.SemaphoreType.DMA] * SC_RING,
    )
    def slot_kernel(table_hbm, pos_hbm, live_hbm, out_hbm, pos_v, inv_v, live_v, *ring):
        slot0 = _sc_worker() * per_worker
        pltpu.sync_copy(pos_hbm, pos_v)
        pltpu.sync_copy(live_hbm, live_v)
        n_live_chunks = jnp.clip((jnp.max(live_v[...]) - slot0) // SC_CHUNK, 0, n_chunks)
        lane = lax.iota(jnp.int32, SC_LANES)

        @plsc.parallel_loop(0, n_chunks)
        def _(j):
            for h in range(SC_CHUNK // SC_LANES):
                s = slot0 + j * SC_CHUNK + h * SC_LANES + lane
                inv_v[j, pl.ds(h * SC_LANES, SC_LANES)] = jnp.where(s >= n_tok, s - n_tok, s)

        @plsc.parallel_loop(0, n_tok // SC_LANES, unroll=4)
        def _(i):
            local = pos_v[pl.ds(i * SC_LANES, SC_LANES)] - slot0
            mine = (local >= 0) & (local < per_worker)
            local = jnp.where(mine, local, 0)
            plsc.store_scatter(inv_v, [local // SC_CHUNK, local % SC_CHUNK], i * SC_LANES + lane, mask=mine)

        _sc_gather_chunks(table_hbm, inv_v, out_hbm, slot0, n_live_chunks, ring[:SC_RING], ring[SC_RING:])

    return slot_kernel(table, pos, jnp.broadcast_to(n_live_slots, (SC_LANES,)))


def _final_kernel(x_ref, y_ref, g_ref, fg_ref, o_ref):
    out = x_ref[...] + g_ref[...] * _unpack_bf16_pairs(y_ref[...])
    o_ref[...] = out * lax.rsqrt(jnp.mean(out * out, axis=-1, keepdims=True) + EPS) * fg_ref[...]


def _final_norm(x, y_tok, gate_f, final_g):
    batch, seq, d = x.shape
    row_spec = pl.BlockSpec((None, ROW_TILE, d), lambda b, s: (b, s, 0))
    return pl.pallas_call(
        _final_kernel,
        grid=(batch, seq // ROW_TILE),
        in_specs=[row_spec, pl.BlockSpec((None, ROW_TILE) + PACKED_SLAB, lambda b, s: (b, s, 0, 0)),
                  pl.BlockSpec((None, 1, d), lambda b, s: (b, 0, 0)),
                  pl.BlockSpec((1, d), lambda b, s: (0, 0))],
        out_specs=row_spec,
        out_shape=jax.ShapeDtypeStruct((batch, seq, d), _f32),
        compiler_params=pltpu.CompilerParams(dimension_semantics=("arbitrary", "arbitrary")),
        name="residual_final_norm",
    )(x, y_tok, gate_f, final_g)


def _experts_kernel(ea_ref, eb_ref, nused_ref, xs_ref, rwa_ref, rwb_ref,
                    wga_ref, wgb_ref, wua_ref, wub_ref, wda_ref, wdb_ref, *rest):
    n_cast = (len(rest) - 1) // 2
    cast_in, ys_ref, cast_out = rest[:n_cast], rest[n_cast], rest[n_cast + 1:]
    i = pl.program_id(0)

    def cast_share():
        for w32_ref, w16_ref in zip(cast_in, cast_out):
            w16_ref[...] = w32_ref[...].astype(_bf16)

    @pl.when(i < nused_ref[0])
    def _():
        cast_share()
        x = _unpack_bf16_pairs(xs_ref[...])
        la = jnp.sum(x * rwa_ref[...], axis=-1, keepdims=True)
        lb = jnp.sum(x * rwb_ref[...], axis=-1, keepdims=True)
        wa = jax.nn.sigmoid(la - lb)
        wb = jax.nn.sigmoid(lb - la)
        xb = x.astype(_bf16)
        act_a = (jax.nn.silu(_dot(xb, wga_ref[...])) * _dot(xb, wua_ref[...]) * wa).astype(_bf16)
        act_b = (jax.nn.silu(_dot(xb, wgb_ref[...])) * _dot(xb, wub_ref[...]) * wb).astype(_bf16)
        y = _dot(act_a, wda_ref[...]) + _dot(act_b, wdb_ref[...])
        ys_ref[...] = _pack_bf16_pairs(y)

    @pl.when(i >= nused_ref[0])
    def _():
        cast_share()
        ys_ref[...] = jnp.zeros_like(ys_ref)


def _experts(tile_ea, tile_eb, n_used, xs, router_wt, w_gate, w_up, w_down, cast_stacks=(), cast_layer=0):
    n_slots = xs.shape[0]
    d = D_MODEL
    n_tiles = n_slots // EXPERT_TILE
    f = D_EXPERT
    n_cast_steps = n_tiles - N_BUCKETS

    def share(i):
        return jnp.minimum(i, n_cast_steps - 1)

    def cast_rows(w):
        rows = w.shape[1] // n_cast_steps
        assert rows * n_cast_steps == w.shape[1] and rows % 16 == 0
        return rows

    cast_in_specs = [pl.BlockSpec((None, cast_rows(w), w.shape[2]),
                                  lambda i, ea, eb, nu: (cast_layer, share(i), 0)) for w in cast_stacks]
    cast_out_specs = [pl.BlockSpec((cast_rows(w), w.shape[2]), lambda i, ea, eb, nu: (share(i), 0))
                      for w in cast_stacks]
    cast_out_shapes = [jax.ShapeDtypeStruct(w.shape[1:], _bf16) for w in cast_stacks]

    def row(i, ea, eb, nu):
        return (jnp.maximum(jnp.minimum(i, nu[0] - 1), 0), 0, 0)

    grid_spec = pltpu.PrefetchScalarGridSpec(
        num_scalar_prefetch=3,
        grid=(n_tiles,),
        in_specs=[
            pl.BlockSpec((EXPERT_TILE,) + PACKED_SLAB, row),
            pl.BlockSpec((None, 1, d), lambda i, ea, eb, nu: (ea[i], 0, 0)),
            pl.BlockSpec((None, 1, d), lambda i, ea, eb, nu: (eb[i], 0, 0)),
            pl.BlockSpec((None, d, f), lambda i, ea, eb, nu: (ea[i], 0, 0)),
            pl.BlockSpec((None, d, f), lambda i, ea, eb, nu: (eb[i], 0, 0)),
            pl.BlockSpec((None, d, f), lambda i, ea, eb, nu: (ea[i], 0, 0)),
            pl.BlockSpec((None, d, f), lambda i, ea, eb, nu: (eb[i], 0, 0)),
            pl.BlockSpec((None, f, d), lambda i, ea, eb, nu: (ea[i], 0, 0)),
            pl.BlockSpec((None, f, d), lambda i, ea, eb, nu: (eb[i], 0, 0)),
        ] + cast_in_specs,
        out_specs=[pl.BlockSpec((EXPERT_TILE,) + PACKED_SLAB, lambda i, ea, eb, nu: (i, 0, 0))]
        + cast_out_specs,
    )
    return pl.pallas_call(
        _experts_kernel,
        grid_spec=grid_spec,
        out_shape=[jax.ShapeDtypeStruct((n_slots,) + PACKED_SLAB, jnp.uint32)] + cast_out_shapes,
        compiler_params=pltpu.CompilerParams(
            dimension_semantics=("arbitrary",), vmem_limit_bytes=VMEM_LIMIT_BYTES),
        name="grouped_experts",
    )(tile_ea, tile_eb, n_used, xs, router_wt, router_wt, w_gate, w_gate, w_up, w_up, w_down, w_down,
      *cast_stacks)


def _routing_tables(route, counts, n_tiles_max):
    bucket = route[:, 0, :].reshape(-1).astype(jnp.int32)
    rank = route[:, 1, :].reshape(-1).astype(jnp.int32)
    cnt = counts[:N_BUCKETS, 0].astype(jnp.int32)
    tiles_b = (cnt + EXPERT_TILE - 1) // EXPERT_TILE
    tile_end = jnp.cumsum(tiles_b)
    tile_start = tile_end - tiles_b
    n_used = tile_end[-1]
    onehot = bucket[:, None] == jnp.arange(N_BUCKETS, dtype=jnp.int32)[None, :]
    pos = jnp.sum(jnp.where(onehot, (tile_start * EXPERT_TILE)[None, :], 0), axis=1) + rank
    tile_ids = jnp.minimum(jnp.arange(n_tiles_max, dtype=jnp.int32), n_used - 1)
    tile_bucket = jnp.sum((tile_ids[:, None] >= tile_end[None, :]).astype(jnp.int32), axis=1)
    group = tile_bucket // PAIRS_PER_GROUP
    pair = tile_bucket % PAIRS_PER_GROUP
    tile_ea = group * EXPERTS_PER_GROUP + jnp.asarray(_PAIR_A, jnp.int32)[pair]
    tile_eb = group * EXPERTS_PER_GROUP + jnp.asarray(_PAIR_B, jnp.int32)[pair]
    return pos, tile_ea, tile_eb, n_used.reshape(1)


def kernel(x, c, w_ada, b_ada, norm_mix_g, w_in, v_norm_g, sgu_w, sgu_b, pool_w, pool_scale,
           w_branch_a, w_branch_b, w_out, norm_ffn_g, router_w, router_bias,
           w_exp_gate, w_exp_up, w_exp_down, final_norm_g):
    batch, seq, d = x.shape
    depth = w_ada.shape[0]
    t = batch * seq
    n_tiles_max = t // EXPERT_TILE + N_BUCKETS
    n_slots = n_tiles_max * EXPERT_TILE

    mod = _ada(c, w_ada, b_ada)
    rw_pad = jnp.pad(router_w, ((0, 0), (0, 128 - N_EXPERTS)))
    rw_hi = rw_pad.astype(_bf16)
    rw_lo = (rw_pad - rw_hi.astype(_f32)).astype(_bf16)
    router_w_pad = jnp.concatenate([rw_hi, rw_lo], axis=1)
    router_wt = router_w.T.reshape(N_EXPERTS, 1, d)
    final_g = final_norm_g.reshape(1, d)

    def layer_bf16(stacks, l):
        return tuple(w[l].astype(_bf16) for w in stacks)

    mixer_stacks = (w_in, sgu_w.reshape(depth, d, SGU_HEAD_DIM), pool_w.reshape(depth, d, POOL_GROUP_DIM),
                    w_branch_a, w_branch_b, w_out)
    expert_stacks = (w_exp_gate, w_exp_up, w_exp_down)
    y_tok, gate_prev = None, None
    mixer_w = layer_bf16(mixer_stacks, 0)
    for l in range(depth):
        mod_l = mod[l].reshape(batch, 1, N_MOD * d)
        w_in_l, sgu_w_l, pool_w_l, w_a_l, w_b_l, w_o_l = mixer_w
        sgu_w_l = sgu_w_l.reshape(SGU_HEADS, CHUNK, CHUNK)
        pool_w_l = pool_w_l.reshape(len(POOL_WINDOWS), POOL_GROUP_DIM, POOL_GROUP_DIM)
        x, hf, route, counts, *cast_w = _mixer(
            x, y_tok, gate_prev, mod_l, norm_mix_g[l].reshape(1, d), norm_ffn_g[l].reshape(1, d),
            w_in_l, v_norm_g[l].reshape(1, SGU_WIDTH), sgu_w_l, sgu_b[l].T,
            pool_w_l, pool_scale[l].reshape(1, POOL_WIDTH), w_a_l, w_b_l, w_o_l,
            router_w_pad, router_bias, l, expert_stacks, mixer_stacks if l + 1 < depth else ())
        expert_w, mixer_w = cast_w[:3], cast_w[3:]
        pos, tile_ea, tile_eb, n_used = _routing_tables(route, counts, n_tiles_max)
        xs = _sc_gather_by_slot(pos, hf.reshape((t,) + PACKED_SLAB), n_slots, n_used * EXPERT_TILE)
        ys, = _experts(tile_ea, tile_eb, n_used, xs, router_wt, *expert_w)
        y_tok = _sc_gather_rows(pos, ys).reshape((batch, seq) + PACKED_SLAB)
        gate_prev = mod_l[:, :, 5 * d:6 * d]
    return _final_norm(x, y_tok, gate_prev, final_g)
```

```python
import functools

import jax
import jax.numpy as jnp
from jax import lax
from jax.experimental import pallas as pl
from jax.experimental.pallas import tpu as pltpu
from jax.experimental.pallas import tpu_sc as plsc

D_MODEL = 1024
CHUNK = 128
SGU_HEADS = 8
SGU_HEAD_DIM = 128
SGU_WIDTH = 1024
POOL_WINDOWS = (2, 4, 8, 16)
POOL_GROUP_DIM = 256
POOL_WIDTH = 1024
HALO = 16
POOL_PAD = 8
N_EXPERTS = 16
N_EXPERT_GROUPS = 4
EXPERTS_PER_GROUP = 4
PAIRS_PER_GROUP = 6
N_BUCKETS = N_EXPERT_GROUPS * PAIRS_PER_GROUP
BUCKET_ROWS = 32
D_EXPERT = 512
N_MOD = 6
EPS = 1e-6

SEQ_TILE = 512
ROW_TILE = 512
EXPERT_TILE = 256
VMEM_LIMIT_BYTES = 58 * 1024 * 1024
PACKED_SLAB = (4, 128)
SC_CORES = 2
SC_SUBCORES = 16
SC_CHUNK = 32
SC_LANES = 16
SC_RING = 3

_PAIR_A = (0, 0, 0, 1, 1, 3)
_PAIR_B = (1, 2, 3, 3, 2, 2)

_bf16 = jnp.bfloat16
_f32 = jnp.float32


def _dot(a, b):
    return jnp.dot(a, b, preferred_element_type=_f32)


def _pack_bf16_pairs(x):
    rows, d = x.shape
    hi = lax.bitcast_convert_type(x[:, :d // 2].astype(_bf16).astype(_f32), jnp.uint32)
    lo = lax.bitcast_convert_type(x[:, d // 2:].astype(_bf16).astype(_f32), jnp.uint32)
    return (hi | (lo >> 16)).reshape((rows,) + PACKED_SLAB)


def _unpack_bf16_pairs(words):
    rows = words.shape[0]
    words = words.reshape(rows, PACKED_SLAB[0] * PACKED_SLAB[1])
    hi = lax.bitcast_convert_type(words & jnp.uint32(0xFFFF0000), _f32)
    lo = lax.bitcast_convert_type(words << 16, _f32)
    return jnp.concatenate([hi, lo], axis=1)


def _rms_modulate(x, g, shift, scale):
    y = x * lax.rsqrt(jnp.mean(x * x, axis=-1, keepdims=True) + EPS)
    return (y * g) * (1.0 + scale) + shift


def _ada_kernel(c_ref, w_ref, b_ref, o_ref):
    c = c_ref[...]
    c_act = (c * jax.nn.sigmoid(c)).astype(_bf16)
    o_ref[...] = _dot(c_act, w_ref[...].astype(_bf16)) + b_ref[...]


def _ada(c, w_ada, b_ada):
    depth, d, n = w_ada.shape
    batch = c.shape[0]
    tn = 2048
    return pl.pallas_call(
        _ada_kernel,
        grid=(depth, n // tn),
        in_specs=[
            pl.BlockSpec((batch, d), lambda l, j: (0, 0)),
            pl.BlockSpec((None, d, tn), lambda l, j: (l, 0, j)),
            pl.BlockSpec((None, 1, tn), lambda l, j: (l, 0, j)),
        ],
        out_specs=pl.BlockSpec((None, batch, tn), lambda l, j: (l, 0, j)),
        out_shape=jax.ShapeDtypeStruct((depth, batch, n), _f32),
        compiler_params=pltpu.CompilerParams(
            dimension_semantics=("arbitrary", "arbitrary"), vmem_limit_bytes=VMEM_LIMIT_BYTES),
        name="ada_modulation",
    )(c, w_ada, b_ada.reshape(depth, 1, n))


def _route(hf, rw_ref, rb_ref, carry_ref):
    ts = hf.shape[0]
    hf_hi = hf.astype(_bf16)
    hf_lo = (hf - hf_hi.astype(_f32)).astype(_bf16)
    both = _dot(hf_hi, rw_ref[...])
    logits = both[:, 0:128] + both[:, 128:256] + _dot(hf_lo, rw_ref[:, 0:128])
    lt = logits.T
    rows = [lt[e:e + 1, :] for e in range(N_EXPERTS)]
    m = functools.reduce(jnp.maximum, rows)
    ex = [jnp.exp(r - m) for r in rows]
    den = functools.reduce(lambda a, b: a + b, ex)
    probs = [e / den for e in ex]
    sel = [probs[e] + rb_ref[e] for e in range(N_EXPERTS)]

    def top2_sum(v):
        pairs = [v[i] + v[j] for i, j in zip(_PAIR_A, _PAIR_B)]
        return functools.reduce(jnp.maximum, pairs)

    gscore = [top2_sum(sel[4 * g:4 * g + 4]) for g in range(N_EXPERT_GROUPS)]
    best = gscore[0]
    gidx = jnp.zeros_like(best, dtype=jnp.int32)
    for g in range(1, N_EXPERT_GROUPS):
        better = gscore[g] > best
        best = jnp.where(better, gscore[g], best)
        gidx = jnp.where(better, g, gidx)
    ing = []
    for k in range(EXPERTS_PER_GROUP):
        v = sel[k]
        for g in range(1, N_EXPERT_GROUPS):
            v = jnp.where(gidx == g, sel[4 * g + k], v)
        ing.append(v)
    chosen = []
    for k in range(EXPERTS_PER_GROUP):
        r = jnp.zeros_like(gidx)
        for j in range(EXPERTS_PER_GROUP):
            if j == k:
                continue
            beats = (ing[j] >= ing[k]) if j < k else (ing[j] > ing[k])
            r = r + beats.astype(jnp.int32)
        chosen.append(r < 2)
    lo = jnp.where(chosen[0], 0, jnp.where(chosen[1], 1, 2))
    hi = jnp.where(chosen[3], 3, jnp.where(chosen[2], 2, 1))
    pair = jnp.where(lo == 0, hi - 1, jnp.where(lo == 2, 5, jnp.where(hi == 3, 3, 4)))
    bucket = gidx * PAIRS_PER_GROUP + pair

    brow = lax.broadcasted_iota(jnp.int32, (BUCKET_ROWS, ts), 0)
    onehot = (brow == bucket).astype(_f32)
    jj = lax.broadcasted_iota(jnp.int32, (ts, ts), 0)
    tt = lax.broadcasted_iota(jnp.int32, (ts, ts), 1)
    upper = (jj <= tt).astype(_bf16)
    cum = _dot(onehot.astype(_bf16), upper)
    carry = carry_ref[...][:, 0:1]
    rank = jnp.sum(onehot * (cum - 1.0 + carry), axis=0, keepdims=True)
    carry_ref[...] = carry_ref[...] + jnp.sum(onehot, axis=1, keepdims=True)
    return bucket.astype(_f32), rank


def _mixer_kernel(*refs, has_prev):
    if has_prev:
        x_ref, y_ref, gprev_ref = refs[:3]
        refs = refs[3:]
        x_in = lambda rows: x_ref[rows, :] + gprev_ref[...] * _unpack_bf16_pairs(y_ref[rows])
    else:
        x_ref = refs[0]
        refs = refs[1:]
        x_in = lambda rows: x_ref[rows, :]
    (mod_ref, gmix_ref, gffn_ref, win_ref, vg_ref, sw_ref, sb_ref, pw_ref, ps_ref, wa_ref, wb_ref,
     wo_ref, rw_ref, rb_ref) = refs[:14]
    n_cast = (len(refs) - 14 - 4 - 4) // 2
    cast_in = refs[14:14 + n_cast]
    xo_ref, hf_ref, route_ref, counts_ref = refs[14 + n_cast:18 + n_cast]
    cast_out = refs[18 + n_cast:18 + 2 * n_cast]
    ext_ref, sum2_ref, sum4_ref, carry_ref = refs[18 + 2 * n_cast:]
    b = pl.program_id(0)
    s = pl.program_id(1)
    ts = x_ref.shape[0]
    d = D_MODEL

    @pl.when((b == 0) & (s == 0))
    def _():
        carry_ref[...] = jnp.zeros_like(carry_ref)
        ext_ref[0:POOL_PAD, :] = jnp.zeros((POOL_PAD, POOL_WIDTH), _f32)
        sum2_ref[0:POOL_PAD, :] = jnp.zeros((POOL_PAD, sum2_ref.shape[1]), _f32)
        sum4_ref[0:POOL_PAD, :] = jnp.zeros((POOL_PAD, sum4_ref.shape[1]), _f32)

    @pl.when(s == 0)
    def _():
        ext_ref[POOL_PAD:POOL_PAD + HALO, :] = jnp.zeros((HALO, POOL_WIDTH), _f32)

    for w32_ref, w16_ref in zip(cast_in, cast_out):
        w16_ref[...] = w32_ref[...].astype(_bf16)

    mod = mod_ref[...]
    ci = lax.broadcasted_iota(jnp.int32, (CHUNK, CHUNK), 0)
    cj = lax.broadcasted_iota(jnp.int32, (CHUNK, CHUNK), 1)
    sgu_w = [jnp.where(ci >= cj, sw_ref[h], jnp.zeros((), _bf16)) for h in range(SGU_HEADS)]
    _mixer_rows(s * ts, mod, sgu_w, x_in, gmix_ref, gffn_ref, win_ref, vg_ref, sb_ref, pw_ref, ps_ref,
                wa_ref, wb_ref, wo_ref, rw_ref, rb_ref, xo_ref, hf_ref, route_ref, ext_ref, sum2_ref,
                sum4_ref, carry_ref)
    ext_ref[POOL_PAD:POOL_PAD + HALO, :] = ext_ref[POOL_PAD + ts:POOL_PAD + ts + HALO, :]
    route_ref[2:8, :] = jnp.zeros((6, ts), _f32)
    counts_ref[...] = carry_ref[...]


def _mixer_rows(seq_pos0, mod, sgu_w, x_in, gmix_ref, gffn_ref, win_ref, vg_ref, sb_ref,
                pw_ref, ps_ref, wa_ref, wb_ref, wo_ref, rw_ref, rb_ref, xo_ref, hf_ref, route_ref,
                ext_ref, sum2_ref, sum4_ref, carry_ref):
    d = D_MODEL
    ts = SEQ_TILE
    rows = slice(0, ts)
    sh_m, sc_m, g_m = mod[:, 0:d], mod[:, d:2 * d], mod[:, 2 * d:3 * d]
    sh_f, sc_f = mod[:, 3 * d:4 * d], mod[:, 4 * d:5 * d]
    x = x_in(rows)

    hb = _rms_modulate(x, gmix_ref[...], sh_m, sc_m).astype(_bf16)

    v = jax.nn.gelu(_dot(hb, win_ref[:, SGU_WIDTH:2 * SGU_WIDTH]))
    p = _dot(hb, win_ref[:, 2 * SGU_WIDTH:2 * SGU_WIDTH + POOL_WIDTH])
    u = jax.nn.gelu(_dot(hb, win_ref[:, 0:SGU_WIDTH]))
    gate_a = jax.nn.sigmoid(_dot(hb, win_ref[:, 3 * d:4 * d]))
    gate_b = jax.nn.sigmoid(_dot(hb, win_ref[:, 4 * d:5 * d]))
    e0 = POOL_PAD + HALO
    ext_ref[e0:e0 + ts, :] = p

    vc = v - jnp.mean(v, axis=-1, keepdims=True)
    vn = (vc * lax.rsqrt(jnp.mean(vc * vc, axis=-1, keepdims=True) + EPS) * vg_ref[...]).astype(_bf16)
    n_chunks = ts // CHUNK
    ya_cols = []
    for h in range(SGU_HEADS):
        cols = slice(h * SGU_HEAD_DIM, (h + 1) * SGU_HEAD_DIM)
        rhs = jnp.concatenate([vn[n * CHUNK:(n + 1) * CHUNK, cols] for n in range(n_chunks)], axis=1)
        sg = _dot(sgu_w[h], rhs) + sb_ref[:, h:h + 1]
        s_h = jnp.concatenate([sg[:, n * SGU_HEAD_DIM:(n + 1) * SGU_HEAD_DIM] for n in range(n_chunks)],
                              axis=0)
        ya_cols.append((u[:, cols] * s_h).astype(_bf16))
    ya = jnp.concatenate(ya_cols, axis=1)
    merged = gate_a * _dot(ya, wa_ref[...])

    pos1 = (seq_pos0 + 1 + lax.broadcasted_iota(jnp.int32, (ts, 1), 0)).astype(_f32)
    g = POOL_GROUP_DIM
    lo, n = e0 - HALO, HALO + ts
    sum2 = ext_ref[lo:lo + n, :] + ext_ref[lo - 1:lo - 1 + n, :]
    sum2_ref[POOL_PAD:POOL_PAD + n, :] = sum2[:, g:]
    sum4 = sum2[:, g:] + sum2_ref[POOL_PAD - 2:POOL_PAD - 2 + n, :]
    sum4_ref[POOL_PAD:POOL_PAD + n, :] = sum4[:, g:]
    sum8 = sum4[:, g:] + sum4_ref[POOL_PAD - 4:POOL_PAD - 4 + n, :]
    sum16 = sum8[8:, g:] + sum8[:n - 8, g:]
    window_sums = (sum2[HALO:, 0:g], sum4[HALO:, 0:g], sum8[HALO:, 0:g], sum16[HALO - 8:, :])
    yb_cols = []
    for gi, w in enumerate(POOL_WINDOWS):
        cols = slice(gi * g, (gi + 1) * g)
        count = jnp.minimum(pos1, float(w))
        pooled = (window_sums[gi] / count - p[:, cols]).astype(_bf16)
        yb_cols.append(_dot(pooled, pw_ref[gi]))
    yb = (jnp.concatenate(yb_cols, axis=1) * ps_ref[...]).astype(_bf16)
    merged = merged + gate_b * _dot(yb, wb_ref[...])

    x_new = x + g_m * _dot(merged.astype(_bf16), wo_ref[...])
    xo_ref[rows, :] = x_new

    hf = _rms_modulate(x_new, gffn_ref[...], sh_f, sc_f)
    hf_ref[rows] = _pack_bf16_pairs(hf)
    bucket, rank = _route(hf, rw_ref, rb_ref, carry_ref)
    route_ref[0:1, rows] = bucket
    route_ref[1:2, rows] = rank


def _mixer(x, y_prev, gate_prev, mod_l, gmix, gffn, w_in, v_g, sgu_w, sgu_bt, pool_w, pool_scale,
           w_a, w_b, w_o, router_w_pad, router_bias, layer, expert_stacks, next_stacks=()):
    batch, seq, d = x.shape
    has_prev = y_prev is not None
    ts = SEQ_TILE
    n_tiles = batch * (seq // ts)
    tiles_per_seq = seq // ts
    const = lambda *shape: pl.BlockSpec(shape, lambda b, s: (0,) * len(shape),
                                        pipeline_mode=pl.Buffered(1))
    row_spec = pl.BlockSpec((None, ts, d), lambda b, s: (b, s, 0))
    slab_spec = pl.BlockSpec((None, ts) + PACKED_SLAB, lambda b, s: (b, s, 0, 0))
    prev_specs = [slab_spec, pl.BlockSpec((None, 1, d), lambda b, s: (b, 0, 0))] if has_prev else []
    prev_args = (y_prev, gate_prev) if has_prev else ()
    n_steps = batch * tiles_per_seq
    parts = n_steps // N_EXPERTS
    assert parts * N_EXPERTS == n_steps

    def part_index(b, s):
        step = b * tiles_per_seq + s
        return step // parts, step % parts

    def part_specs(w, with_layer):
        rows, cols = w.shape[-2] // parts, w.shape[-1]
        if with_layer:
            return pl.BlockSpec((None, None, rows, cols), lambda b, s: (layer, *part_index(b, s), 0))
        return pl.BlockSpec((None, rows, cols), lambda b, s: (*part_index(b, s), 0))

    def next_specs(w, with_layer):
        rows, cols = w.shape[1] // n_steps, w.shape[2]
        assert rows * n_steps == w.shape[1] and rows % 16 == 0
        if with_layer:
            return pl.BlockSpec((None, rows, cols), lambda b, s: (layer + 1, b * tiles_per_seq + s, 0))
        return pl.BlockSpec((rows, cols), lambda b, s: (b * tiles_per_seq + s, 0))

    return pl.pallas_call(
        functools.partial(_mixer_kernel, has_prev=has_prev),
        grid=(batch, tiles_per_seq),
        in_specs=[row_spec] + prev_specs + [
            pl.BlockSpec((None, 1, N_MOD * d), lambda b, s: (b, 0, 0)),
            const(1, d), const(1, d),
            const(*w_in.shape),
            const(1, SGU_WIDTH),
            const(*sgu_w.shape),
            const(*sgu_bt.shape),
            const(*pool_w.shape),
            const(1, POOL_WIDTH),
            const(d, d), const(d, d), const(d, d),
            const(*router_w_pad.shape),
            pl.BlockSpec(memory_space=pltpu.SMEM),
        ] + [part_specs(w, True) for w in expert_stacks] + [next_specs(w, True) for w in next_stacks],
        out_specs=[
            pl.BlockSpec((None, ts, d), lambda b, s: (b, s, 0)),
            pl.BlockSpec((None, ts) + PACKED_SLAB, lambda b, s: (b, s, 0, 0)),
            pl.BlockSpec((None, 8, ts), lambda b, s: (b * tiles_per_seq + s, 0, 0)),
            pl.BlockSpec((BUCKET_ROWS, 128), lambda b, s: (0, 0)),
        ] + [part_specs(w, False) for w in expert_stacks] + [next_specs(w, False) for w in next_stacks],
        out_shape=[
            jax.ShapeDtypeStruct((batch, seq, d), _f32),
            jax.ShapeDtypeStruct((batch, seq) + PACKED_SLAB, jnp.uint32),
            jax.ShapeDtypeStruct((n_tiles, 8, ts), _f32),
            jax.ShapeDtypeStruct((BUCKET_ROWS, 128), _f32),
        ] + [jax.ShapeDtypeStruct(w.shape[1:], _bf16) for w in expert_stacks + tuple(next_stacks)],
        scratch_shapes=[
            pltpu.VMEM((POOL_PAD + HALO + ts, POOL_WIDTH), _f32),
            pltpu.VMEM((POOL_PAD + HALO + ts, POOL_WIDTH - POOL_GROUP_DIM), _f32),
            pltpu.VMEM((POOL_PAD + HALO + ts, POOL_WIDTH - 2 * POOL_GROUP_DIM), _f32),
            pltpu.VMEM((BUCKET_ROWS, 128), _f32),
        ],
        compiler_params=pltpu.CompilerParams(
            dimension_semantics=("arbitrary", "arbitrary"), vmem_limit_bytes=VMEM_LIMIT_BYTES),
        name="mixer_router",
    )(x, *prev_args, mod_l, gmix, gffn, w_in, v_g, sgu_w, sgu_bt, pool_w, pool_scale, w_a, w_b, w_o,
      router_w_pad, router_bias, *expert_stacks, *next_stacks)


def _sc_mesh():
    return plsc.VectorSubcoreMesh(core_axis_name="c", subcore_axis_name="s")


def _sc_worker():
    return lax.axis_index("s") * SC_CORES + lax.axis_index("c")


def _sc_gather_chunks(table_hbm, idx_v, out_hbm, out_row0, n_chunks, bufs, sems):
    depth = len(bufs)

    def gather(j, slot):
        return pltpu.make_async_copy(table_hbm.at[idx_v.at[j]], bufs[slot], sems[slot])

    for j in range(depth - 1):
        @pl.when(j < n_chunks)
        def _():
            gather(j, j).start()

    @pl.loop(0, n_chunks, step=depth)
    def _(j0):
        for slot in range(depth):
            j = j0 + slot

            @pl.when(j < n_chunks)
            def _():
                @pl.when(j + depth - 1 < n_chunks)
                def _():
                    gather(j + depth - 1, (slot + depth - 1) % depth).start()

                gather(j, slot).wait()
                pltpu.sync_copy(bufs[slot], out_hbm.at[pl.ds(out_row0 + j * SC_CHUNK, SC_CHUNK)])


def _sc_chunks_per_worker(n_rows):
    n_chunks = n_rows // (SC_CORES * SC_SUBCORES * SC_CHUNK)
    assert n_chunks * SC_CORES * SC_SUBCORES * SC_CHUNK == n_rows
    return n_chunks


def _sc_gather_rows(idx, table):
    n_out = idx.shape[0]
    n_chunks = _sc_chunks_per_worker(n_out)
    slab = table.shape[1:]
    buf = pltpu.VMEM((SC_CHUNK,) + slab, table.dtype)

    @functools.partial(
        pl.kernel, mesh=_sc_mesh(),
        out_type=jax.ShapeDtypeStruct((n_out,) + slab, table.dtype),
        scratch_types=[pltpu.VMEM((n_chunks, SC_CHUNK), jnp.int32)]
        + [buf] * SC_RING + [pltpu.SemaphoreType.DMA] * SC_RING,
    )
    def gather_kernel(table_hbm, idx_hbm, out_hbm, idx_v, *ring):
        chunk0 = _sc_worker() * n_chunks
        pltpu.sync_copy(idx_hbm.at[pl.ds(chunk0, n_chunks)], idx_v)
        _sc_gather_chunks(table_hbm, idx_v, out_hbm, chunk0 * SC_CHUNK, n_chunks,
                          ring[:SC_RING], ring[SC_RING:])

    return gather_kernel(table, idx.reshape(n_out // SC_CHUNK, SC_CHUNK))


def _sc_gather_by_slot(pos, table, n_slots, n_live_slots):
    n_tok = pos.shape[0]
    per_worker = n_slots // (SC_CORES * SC_SUBCORES)
    n_chunks = _sc_chunks_per_worker(n_slots)
    slab = table.shape[1:]
    buf = pltpu.VMEM((SC_CHUNK,) + slab, table.dtype)

    @functools.partial(
        pl.kernel, mesh=_sc_mesh(),
        compiler_params=pltpu.CompilerParams(needs_layout_passes=False),
        out_type=jax.ShapeDtypeStruct((n_slots,) + slab, table.dtype),
        scratch_types=[pltpu.VMEM((n_tok,), jnp.int32), pltpu.VMEM((n_chunks, SC_CHUNK), jnp.int32),
                       pltpu.VMEM((SC_LANES,), jnp.int32)]
        + [buf] * SC_RING + [pltpu.SemaphoreType.DMA] * SC_RING,
    )
    def slot_kernel(table_hbm, pos_hbm, live_hbm, out_hbm, pos_v, inv_v, live_v, *ring):
        slot0 = _sc_worker() * per_worker
        pltpu.sync_copy(pos_hbm, pos_v)
        pltpu.sync_copy(live_hbm, live_v)
        n_live_chunks = jnp.clip((jnp.max(live_v[...]) - slot0) // SC_CHUNK, 0, n_chunks)
        lane = lax.iota(jnp.int32, SC_LANES)

        @plsc.parallel_loop(0, n_chunks)
        def _(j):
            for h in range(SC_CHUNK // SC_LANES):
                s = slot0 + j * SC_CHUNK + h * SC_LANES + lane
                inv_v[j, pl.ds(h * SC_LANES, SC_LANES)] = jnp.where(s >= n_tok, s - n_tok, s)

        @plsc.parallel_loop(0, n_tok // SC_LANES, unroll=4)
        def _(i):
            local = pos_v[pl.ds(i * SC_LANES, SC_LANES)] - slot0
            mine = (local >= 0) & (local < per_worker)
            local = jnp.where(mine, local, 0)
            plsc.store_scatter(inv_v, [local // SC_CHUNK, local % SC_CHUNK], i * SC_LANES + lane, mask=mine)

        _sc_gather_chunks(table_hbm, inv_v, out_hbm, slot0, n_live_chunks, ring[:SC_RING], ring[SC_RING:])

    return slot_kernel(table, pos, jnp.broadcast_to(n_live_slots, (SC_LANES,)))


def _final_kernel(x_ref, y_ref, g_ref, fg_ref, o_ref):
    out = x_ref[...] + g_ref[...] * _unpack_bf16_pairs(y_ref[...])
    o_ref[...] = out * lax.rsqrt(jnp.mean(out * out, axis=-1, keepdims=True) + EPS) * fg_ref[...]


def _final_norm(x, y_tok, gate_f, final_g):
    batch, seq, d = x.shape
    row_spec = pl.BlockSpec((None, ROW_TILE, d), lambda b, s: (b, s, 0))
    return pl.pallas_call(
        _final_kernel,
        grid=(batch, seq // ROW_TILE),
        in_specs=[row_spec, pl.BlockSpec((None, ROW_TILE) + PACKED_SLAB, lambda b, s: (b, s, 0, 0)),
                  pl.BlockSpec((None, 1, d), lambda b, s: (b, 0, 0)),
                  pl.BlockSpec((1, d), lambda b, s: (0, 0))],
        out_specs=row_spec,
        out_shape=jax.ShapeDtypeStruct((batch, seq, d), _f32),
        compiler_params=pltpu.CompilerParams(dimension_semantics=("arbitrary", "arbitrary")),
        name="residual_final_norm",
    )(x, y_tok, gate_f, final_g)


def _experts_kernel(ea_ref, eb_ref, nused_ref, xs_ref, rwa_ref, rwb_ref,
                    wga_ref, wgb_ref, wua_ref, wub_ref, wda_ref, wdb_ref, ys_ref):
    i = pl.program_id(0)

    @pl.when(i < nused_ref[0])
    def _():
        x = _unpack_bf16_pairs(xs_ref[...])
        la = jnp.sum(x * rwa_ref[...], axis=-1, keepdims=True)
        lb = jnp.sum(x * rwb_ref[...], axis=-1, keepdims=True)
        wa = jax.nn.sigmoid(la - lb)
        wb = jax.nn.sigmoid(lb - la)
        xb = x.astype(_bf16)
        act_a = (jax.nn.silu(_dot(xb, wga_ref[...])) * _dot(xb, wua_ref[...]) * wa).astype(_bf16)
        act_b = (jax.nn.silu(_dot(xb, wgb_ref[...])) * _dot(xb, wub_ref[...]) * wb).astype(_bf16)
        y = _dot(act_a, wda_ref[...]) + _dot(act_b, wdb_ref[...])
        ys_ref[...] = _pack_bf16_pairs(y)

    @pl.when(i >= nused_ref[0])
    def _():
        ys_ref[...] = jnp.zeros_like(ys_ref)


def _experts(tile_ea, tile_eb, n_used, xs, router_wt, w_gate, w_up, w_down):
    n_slots = xs.shape[0]
    d = D_MODEL
    n_tiles = n_slots // EXPERT_TILE
    f = D_EXPERT

    def row(i, ea, eb, nu):
        return (jnp.maximum(jnp.minimum(i, nu[0] - 1), 0), 0, 0)

    grid_spec = pltpu.PrefetchScalarGridSpec(
        num_scalar_prefetch=3,
        grid=(n_tiles,),
        in_specs=[
            pl.BlockSpec((EXPERT_TILE,) + PACKED_SLAB, row),
            pl.BlockSpec((None, 1, d), lambda i, ea, eb, nu: (ea[i], 0, 0)),
            pl.BlockSpec((None, 1, d), lambda i, ea, eb, nu: (eb[i], 0, 0)),
            pl.BlockSpec((None, d, f), lambda i, ea, eb, nu: (ea[i], 0, 0)),
            pl.BlockSpec((None, d, f), lambda i, ea, eb, nu: (eb[i], 0, 0)),
            pl.BlockSpec((None, d, f), lambda i, ea, eb, nu: (ea[i], 0, 0)),
            pl.BlockSpec((None, d, f), lambda i, ea, eb, nu: (eb[i], 0, 0)),
            pl.BlockSpec((None, f, d), lambda i, ea, eb, nu: (ea[i], 0, 0)),
            pl.BlockSpec((None, f, d), lambda i, ea, eb, nu: (eb[i], 0, 0)),
        ],
        out_specs=pl.BlockSpec((EXPERT_TILE,) + PACKED_SLAB, lambda i, ea, eb, nu: (i, 0, 0)),
    )
    return pl.pallas_call(
        _experts_kernel,
        grid_spec=grid_spec,
        out_shape=jax.ShapeDtypeStruct((n_slots,) + PACKED_SLAB, jnp.uint32),
        compiler_params=pltpu.CompilerParams(
            dimension_semantics=("arbitrary",), vmem_limit_bytes=VMEM_LIMIT_BYTES),
        name="grouped_experts",
    )(tile_ea, tile_eb, n_used, xs, router_wt, router_wt, w_gate, w_gate, w_up, w_up, w_down, w_down)


def _routing_tables(route, counts, n_tiles_max):
    bucket = route[:, 0, :].reshape(-1).astype(jnp.int32)
    rank = route[:, 1, :].reshape(-1).astype(jnp.int32)
    cnt = counts[:N_BUCKETS, 0].astype(jnp.int32)
    tiles_b = (cnt + EXPERT_TILE - 1) // EXPERT_TILE
    tile_end = jnp.cumsum(tiles_b)
    tile_start = tile_end - tiles_b
    n_used = tile_end[-1]
    onehot = bucket[:, None] == jnp.arange(N_BUCKETS, dtype=jnp.int32)[None, :]
    pos = jnp.sum(jnp.where(onehot, (tile_start * EXPERT_TILE)[None, :], 0), axis=1) + rank
    tile_ids = jnp.minimum(jnp.arange(n_tiles_max, dtype=jnp.int32), n_used - 1)
    tile_bucket = jnp.sum((tile_ids[:, None] >= tile_end[None, :]).astype(jnp.int32), axis=1)
    group = tile_bucket // PAIRS_PER_GROUP
    pair = tile_bucket % PAIRS_PER_GROUP
    tile_ea = group * EXPERTS_PER_GROUP + jnp.asarray(_PAIR_A, jnp.int32)[pair]
    tile_eb = group * EXPERTS_PER_GROUP + jnp.asarray(_PAIR_B, jnp.int32)[pair]
    return pos, tile_ea, tile_eb, n_used.reshape(1)


def kernel(x, c, w_ada, b_ada, norm_mix_g, w_in, v_norm_g, sgu_w, sgu_b, pool_w, pool_scale,
           w_branch_a, w_branch_b, w_out, norm_ffn_g, router_w, router_bias,
           w_exp_gate, w_exp_up, w_exp_down, final_norm_g):
    batch, seq, d = x.shape
    depth = w_ada.shape[0]
    t = batch * seq
    n_tiles_max = t // EXPERT_TILE + N_BUCKETS
    n_slots = n_tiles_max * EXPERT_TILE

    mod = _ada(c, w_ada, b_ada)
    rw_pad = jnp.pad(router_w, ((0, 0), (0, 128 - N_EXPERTS)))
    rw_hi = rw_pad.astype(_bf16)
    rw_lo = (rw_pad - rw_hi.astype(_f32)).astype(_bf16)
    router_w_pad = jnp.concatenate([rw_hi, rw_lo], axis=1)
    router_wt = router_w.T.reshape(N_EXPERTS, 1, d)
    final_g = final_norm_g.reshape(1, d)

    def layer_bf16(stacks, l):
        return tuple(w[l].astype(_bf16) for w in stacks)

    mixer_stacks = (w_in, sgu_w.reshape(depth, d, SGU_HEAD_DIM), pool_w.reshape(depth, d, POOL_GROUP_DIM),
                    w_branch_a, w_branch_b, w_out)
    expert_stacks = (w_exp_gate, w_exp_up, w_exp_down)
    y_tok, gate_prev = None, None
    mixer_w = layer_bf16(mixer_stacks, 0)
    for l in range(depth):
        mod_l = mod[l].reshape(batch, 1, N_MOD * d)
        w_in_l, sgu_w_l, pool_w_l, w_a_l, w_b_l, w_o_l = mixer_w
        sgu_w_l = sgu_w_l.reshape(SGU_HEADS, CHUNK, CHUNK)
        pool_w_l = pool_w_l.reshape(len(POOL_WINDOWS), POOL_GROUP_DIM, POOL_GROUP_DIM)
        x, hf, route, counts, *cast_w = _mixer(
            x, y_tok, gate_prev, mod_l, norm_mix_g[l].reshape(1, d), norm_ffn_g[l].reshape(1, d),
            w_in_l, v_norm_g[l].reshape(1, SGU_WIDTH), sgu_w_l, sgu_b[l].T,
            pool_w_l, pool_scale[l].reshape(1, POOL_WIDTH), w_a_l, w_b_l, w_o_l,
            router_w_pad, router_bias, l, expert_stacks, mixer_stacks if l + 1 < depth else ())
        expert_w, mixer_w = cast_w[:3], cast_w[3:]
        pos, tile_ea, tile_eb, n_used = _routing_tables(route, counts, n_tiles_max)
        xs = _sc_gather_by_slot(pos, hf.reshape((t,) + PACKED_SLAB), n_slots, n_used * EXPERT_TILE)
        ys = _experts(tile_ea, tile_eb, n_used, xs, router_wt, *expert_w)
        y_tok = _sc_gather_rows(pos, ys).reshape((batch, seq) + PACKED_SLAB)
        gate_prev = mod_l[:, :, 5 * d:6 * d]
    return _final_norm(x, y_tok, gate_prev, final_g)
```
